```python
import math
import jax, jax.numpy as jnp
from jax import lax
import numpy as np

D_MODEL = 2048
BATCH = 4
SEQ = 2048
DEPTH = 1
DEC_BATCH = 8
DEC_SEQ = 4
PAST_LEN = 16384
PAGE_SIZE = 128

N_HEADS = 8
HEAD_DIM = 128
ATTN_WIDTH = N_HEADS * HEAD_DIM
IDX_HEADS = 16
IDX_DIM = 64
TOPK_MAX = 256
CONV_WIDTH = D_MODEL // 2
CONV_K = 3
D_FF = 4 * D_MODEL
N_BUCKETS = 32
MAX_DISTANCE = 128
Q_BLOCK = 64
EPS = 1e-6
SPLITS = (ATTN_WIDTH, ATTN_WIDTH, ATTN_WIDTH,
          IDX_HEADS * IDX_DIM, IDX_DIM, IDX_HEADS,
          CONV_WIDTH, CONV_WIDTH, CONV_WIDTH,
          D_MODEL, D_MODEL)
N_IN = sum(SPLITS)

kernel_name = "hybrid_dsa_shortconv_gated_step"


def rmsnorm(x, g):
    xf = x.astype(jnp.float32)
    y = xf * lax.rsqrt(jnp.mean(xf * xf, axis=-1, keepdims=True) + EPS)
    return (y * g.astype(jnp.float32)).astype(x.dtype)


def rel_bucket(n):
    n = jnp.maximum(n, 0)
    max_exact = N_BUCKETS // 2
    nf = jnp.maximum(n, max_exact).astype(jnp.float32)
    large = max_exact + (jnp.log(nf / max_exact) / math.log(MAX_DISTANCE / max_exact)
                         * (N_BUCKETS - max_exact)).astype(jnp.int32)
    large = jnp.minimum(large, N_BUCKETS - 1)
    return jnp.where(n < max_exact, n, large)


def split_proj(h, w_in):
    B, T = h.shape[:2]
    p = h @ w_in
    pieces, off = [], 0
    for w in SPLITS:
        pieces.append(p[..., off:off + w])
        off += w
    q, k, v, qi, ki, wi, cx, cb, cc, ga, gb = pieces
    q = q.reshape(B, T, N_HEADS, HEAD_DIM)
    k = k.reshape(B, T, N_HEADS, HEAD_DIM)
    v = v.reshape(B, T, N_HEADS, HEAD_DIM)
    qi = qi.reshape(B, T, IDX_HEADS, IDX_DIM)
    return q, k, v, qi, ki, wi, cx, cb, cc, ga, gb


def indexer_topk(qi, wi, ki, q_pos, topk):
    s = jnp.einsum('bthd,bsd->bths', qi.astype(jnp.float32), ki.astype(jnp.float32)) * (IDX_DIM ** -0.5)
    score = jnp.einsum('bths,bth->bts', jax.nn.relu(s), wi.astype(jnp.float32) * (IDX_HEADS ** -0.5))
    key_pos = jnp.arange(ki.shape[1])
    adm = key_pos[None, None, :] <= q_pos[None, :, None]
    score = jnp.where(adm, score, -jnp.inf)
    _, idx = lax.top_k(score, topk)
    valid = idx <= q_pos[None, :, None]
    return idx, valid


def sparse_attend(q, k_sel, v_sel, idx, valid, q_pos, rel_bias):
    logits = jnp.einsum('bthd,btkhd->bthk', q.astype(jnp.float32), k_sel.astype(jnp.float32)) * (HEAD_DIM ** -0.5)
    bias = rel_bias.astype(jnp.float32)[rel_bucket(q_pos[None, :, None] - idx)]
    logits = logits + jnp.moveaxis(bias, -1, 2)
    logits = jnp.where(valid[:, :, None, :], logits, -jnp.inf)
    p = jax.nn.softmax(logits, axis=-1)
    out = jnp.einsum('bthk,btkhd->bthd', p, v_sel.astype(jnp.float32))
    return out.astype(q.dtype)


def prompt_attention(q, k, v, qi, ki, wi, rel_bias):
    B, T = q.shape[:2]
    topk = min(TOPK_MAX, T // 4)
    nb = T // Q_BLOCK
    bidx = jnp.arange(B)[:, None, None]

    def block(args):
        qb, qib, wib, pos = args
        idx, valid = indexer_topk(qib, wib, ki, pos, topk)
        ks = k[bidx, idx]
        vs = v[bidx, idx]
        return sparse_attend(qb, ks, vs, idx, valid, pos, rel_bias)

    def to_blocks(a):
        return jnp.moveaxis(a.reshape((B, nb, Q_BLOCK) + a.shape[2:]), 1, 0)

    pos = jnp.arange(T).reshape(nb, Q_BLOCK)
    out = lax.map(block, (to_blocks(q), to_blocks(qi), to_blocks(wi), pos))
    return jnp.moveaxis(out, 0, 1).reshape(B, T, N_HEADS, HEAD_DIM)


def sample_attention(q, k_new, v_new, qi, ki_new, wi, cache_k, cache_v, cache_kidx, page_table, rel_bias):
    DB, T = q.shape[:2]
    n_pages = page_table.shape[1]
    ps = cache_k.shape[1]
    past = n_pages * ps
    topk = min(TOPK_MAX, (past + T) // 4)
    ki_past = cache_kidx[page_table].reshape(DB, past, IDX_DIM)
    ki_all = jnp.concatenate([ki_past.astype(ki_new.dtype), ki_new], axis=1)
    q_pos = past + jnp.arange(T)
    idx, valid = indexer_topk(qi, wi, ki_all, q_pos, topk)
    bidx = jnp.arange(DB)[:, None, None]
    in_past = (idx < past)[..., None, None]
    pidx = jnp.minimum(idx, past - 1)
    phys = page_table[bidx, pidx // ps]
    off = pidx % ps
    nidx = jnp.clip(idx - past, 0, T - 1)
    ks = jnp.where(in_past, cache_k[phys, off].astype(k_new.dtype), k_new[bidx, nidx])
    vs = jnp.where(in_past, cache_v[phys, off].astype(v_new.dtype), v_new[bidx, nidx])
    return sparse_attend(q, ks, vs, idx, valid, q_pos, rel_bias)


def causal_conv(u, prev, conv_w):
    T = u.shape[1]
    full = jnp.concatenate([prev.astype(u.dtype), u], axis=1)
    y = sum(conv_w[j] * full[:, j:j + T] for j in range(CONV_K))
    return y, full[:, -(CONV_K - 1):]


def mix_out(attn, conv_y, cb, ga, gb, w_pa, w_pb, w_o):
    B, T = attn.shape[:2]
    a = attn.reshape(B, T, ATTN_WIDTH) @ w_pa
    c = (cb * conv_y) @ w_pb
    m = jax.nn.sigmoid(ga) * a + jax.nn.sigmoid(gb) * c
    return m @ w_o


def mlp(h, w1, w2):
    return jnp.square(jax.nn.relu(h @ w1)) @ w2


def setup_inputs(seed: int = 0) -> dict:
    key = jax.random.key(seed)
    ks = jax.random.split(key, 20)
    n_pages = PAST_LEN // PAGE_SIZE
    n_pool = (DEC_BATCH * n_pages * 5) // 4
    nrm = lambda k, shape, scale: jax.random.normal(k, shape, jnp.float32) * scale
    page_table = jax.random.permutation(ks[0], n_pool)[:DEC_BATCH * n_pages].reshape(DEC_BATCH, n_pages).astype(jnp.int32)
    return {
        "x_prompt": nrm(ks[1], (BATCH, SEQ, D_MODEL), 1.0),
        "x_sample": nrm(ks[2], (DEC_BATCH, DEC_SEQ, D_MODEL), 1.0),
        "cache_k": nrm(ks[3], (DEPTH, n_pool, PAGE_SIZE, N_HEADS, HEAD_DIM), 1.0),
        "cache_v": nrm(ks[4], (DEPTH, n_pool, PAGE_SIZE, N_HEADS, HEAD_DIM), 1.0),
        "cache_kidx": nrm(ks[5], (DEPTH, n_pool, PAGE_SIZE, IDX_DIM), 1.0),
        "state_conv": nrm(ks[6], (DEPTH, DEC_BATCH, CONV_K - 1, CONV_WIDTH), 1.0),
        "page_table": page_table,
        "rel_bias": nrm(ks[7], (N_BUCKETS, N_HEADS), 0.1),
        "norm_mix_g": 1.0 + nrm(ks[8], (DEPTH, D_MODEL), 0.01),
        "w_in": nrm(ks[9], (DEPTH, D_MODEL, N_IN), D_MODEL ** -0.5),
        "conv_w": nrm(ks[10], (DEPTH, CONV_K, CONV_WIDTH), CONV_K ** -0.5),
        "w_pa": nrm(ks[11], (DEPTH, ATTN_WIDTH, D_MODEL), ATTN_WIDTH ** -0.5),
        "w_pb": nrm(ks[12], (DEPTH, CONV_WIDTH, D_MODEL), CONV_WIDTH ** -0.5),
        "w_o": nrm(ks[13], (DEPTH, D_MODEL, D_MODEL), D_MODEL ** -0.5),
        "norm_mlp_g": 1.0 + nrm(ks[14], (DEPTH, D_MODEL), 0.01),
        "w_mlp_in": nrm(ks[15], (DEPTH, D_MODEL, D_FF), D_MODEL ** -0.5),
        "w_mlp_out": nrm(ks[16], (DEPTH, D_FF, D_MODEL), D_FF ** -0.5),
        "norm_final_g": 1.0 + nrm(ks[17], (D_MODEL,), 0.01),
    }


def reference(x_prompt, x_sample, cache_k, cache_v, cache_kidx, state_conv, page_table, rel_bias,
              norm_mix_g, w_in, conv_w, w_pa, w_pb, w_o, norm_mlp_g, w_mlp_in, w_mlp_out, norm_final_g):
    xp, xs = x_prompt, x_sample
    kp_l, vp_l, kip_l, sp_l, ks_l, vs_l, kis_l, ss_l = [], [], [], [], [], [], [], []
    for l in range(DEPTH):
        hp = rmsnorm(xp, norm_mix_g[l])
        qp, kp, vp, qip, kip, wip, cxp, cbp, ccp, gap, gbp = split_proj(hp, w_in[l])
        ap = prompt_attention(qp, kp, vp, qip, kip, wip, rel_bias)
        up = ccp * cxp
        zero_prev = jnp.zeros((up.shape[0], CONV_K - 1, CONV_WIDTH), up.dtype)
        yconv_p, sp = causal_conv(up, zero_prev, conv_w[l])
        xp = xp + mix_out(ap, yconv_p, cbp, gap, gbp, w_pa[l], w_pb[l], w_o[l])
        xp = xp + mlp(rmsnorm(xp, norm_mlp_g[l]), w_mlp_in[l], w_mlp_out[l])
        hs = rmsnorm(xs, norm_mix_g[l])
        qs, kn, vn, qis, kin, wis, cxs, cbs, ccs, gas, gbs = split_proj(hs, w_in[l])
        a_s = sample_attention(qs, kn, vn, qis, kin, wis, cache_k[l], cache_v[l], cache_kidx[l], page_table, rel_bias)
        us = ccs * cxs
        yconv_s, ss = causal_conv(us, state_conv[l], conv_w[l])
        xs = xs + mix_out(a_s, yconv_s, cbs, gas, gbs, w_pa[l], w_pb[l], w_o[l])
        xs = xs + mlp(rmsnorm(xs, norm_mlp_g[l]), w_mlp_in[l], w_mlp_out[l])
        kp_l.append(kp); vp_l.append(vp); kip_l.append(kip); sp_l.append(sp)
        ks_l.append(kn); vs_l.append(vn); kis_l.append(kin); ss_l.append(ss)
    y_prompt = rmsnorm(xp, norm_final_g)
    y_sample = rmsnorm(xs, norm_final_g)
    k_prompt = jnp.stack(kp_l)
    v_prompt = jnp.stack(vp_l)
    kidx_prompt = jnp.stack(kip_l)
    conv_prompt = jnp.stack(sp_l)
    k_sample = jnp.stack(ks_l)
    v_sample = jnp.stack(vs_l)
    kidx_sample = jnp.stack(kis_l)
    conv_sample = jnp.stack(ss_l)
    return (y_prompt, y_sample, k_prompt, v_prompt, kidx_prompt, conv_prompt,
            k_sample, v_sample, kidx_sample, conv_sample)
```

```python
import functools
import math

import jax
import jax.numpy as jnp
import numpy as np
from jax import lax
from jax.experimental import pallas as pl
from jax.experimental.pallas import tpu as pltpu

F32 = jnp.float32
BF16 = jnp.bfloat16
I32 = jnp.int32

TOPK_MAX = 256
MAX_DISTANCE = 128
EPS = 1e-6

LANES = 128
SUBLANES = 8
V7X_SCOPED_VMEM_BYTES = 60000 * 1024

INT_MIN = np.int32(-2 ** 31)
NEG_BIG = -1e30


def _cparams(*sem):
    return pltpu.CompilerParams(dimension_semantics=sem, vmem_limit_bytes=V7X_SCOPED_VMEM_BYTES)


def _resident(shape):
    nd = len(shape)
    return pl.BlockSpec(shape, lambda *_: (0,) * nd, pipeline_mode=pl.Buffered(1))


def _rms_bf16(x, g):
    y = x * lax.rsqrt(jnp.mean(x * x, axis=-1, keepdims=True) + EPS)
    return (y * g).astype(BF16)


def _dot(a, b):
    return jnp.dot(a, b, preferred_element_type=F32)


def _dot_nt(a, b):
    return lax.dot_general(a, b, (((1,), (1,)), ((), ())), preferred_element_type=F32)


def _ordered_key(x):
    b = lax.bitcast_convert_type(x, I32)
    return b ^ ((b >> 31) & np.int32(0x7FFFFFFF))


def _bucket(n, n_buckets):
    n = jnp.maximum(n, 0)
    me = n_buckets // 2
    nf = jnp.maximum(n, me).astype(F32)
    large = me + (jnp.log(nf / me) / math.log(MAX_DISTANCE / me) * (n_buckets - me)).astype(I32)
    large = jnp.minimum(large, n_buckets - 1)
    return jnp.where(n < me, n, large)


def _bias_of_bucket(bucket, relb_ref, h, n_buckets):
    acc = jnp.zeros(bucket.shape, F32)
    for bkt in range(n_buckets):
        acc = jnp.where(bucket == bkt, relb_ref[bkt, h], acc)
    return acc


def _kth_threshold(count_ge, rows, k, trips):
    def bit_body(t, uthr):
        cand = uthr | jnp.left_shift(np.int32(1), 31 - t)
        cnt = count_ge(cand ^ INT_MIN)
        return jnp.where(cnt >= k, cand, uthr)

    uthr = lax.fori_loop(0, trips, bit_body, jnp.zeros((rows, LANES), I32))
    return jnp.maximum(uthr ^ INT_MIN, INT_MIN + 1)


def _flash_update(lg, mask, v, m_ref, l_ref, acc_ref, h, hd):
    reps = lg.shape[1] // LANES
    m_prev = m_ref[h]
    m_cur = jnp.max(jnp.where(mask, lg, NEG_BIG), axis=1, keepdims=True)
    m_new = jnp.maximum(m_prev, m_cur)
    p = jnp.where(mask, jnp.exp(lg - jnp.tile(m_new, (1, reps))), 0.0)
    alpha = jnp.exp(m_prev - m_new)
    l_ref[h] = alpha * l_ref[h] + jnp.sum(p, axis=1, keepdims=True)
    m_ref[h] = m_new
    hs = slice(h * hd, (h + 1) * hd)
    acc_ref[:, hs] = acc_ref[:, hs] * jnp.tile(alpha, (1, hd // LANES)) + _dot(p.astype(BF16), v)


def _proj_attn_kernel(x_ref, g_ref, w_ref, ws_ref, q_ref, k_ref, v_ref, kb_ref, vb_ref, qi_ref,
                      sm_ref, smb_ref, *, aw, iw, qscale):
    hb = _rms_bf16(x_ref[...], g_ref[...])
    q_ref[...] = (_dot(hb, w_ref[:, 0:aw]) * qscale).astype(BF16)
    k = _dot(hb, w_ref[:, aw:2 * aw])
    k_ref[...] = k
    kb_ref[...] = k.astype(BF16)
    v = _dot(hb, w_ref[:, 2 * aw:3 * aw])
    v_ref[...] = v
    vb_ref[...] = v.astype(BF16)
    qi_ref[...] = _dot(hb, w_ref[:, 3 * aw:3 * aw + iw]).astype(BF16)
    sm = _dot(hb, ws_ref[...])
    sm_ref[...] = sm
    smb_ref[...] = sm.astype(BF16)


def _proj_attn(x2d, g, w_attn, w_small, *, aw, iw, head_dim, tm):
    m, d = x2d.shape
    wt = w_attn.shape[1]
    row = lambda width: pl.BlockSpec((tm, width), lambda i: (i, 0))
    outs = [(aw, BF16), (aw, F32), (aw, F32), (aw, BF16), (aw, BF16), (iw, BF16), (LANES, F32), (LANES, BF16)]
    return pl.pallas_call(
        functools.partial(_proj_attn_kernel, aw=aw, iw=iw, qscale=head_dim ** -0.5),
        grid=(m // tm,),
        in_specs=[row(d), _resident((1, d)), _resident((d, wt)), _resident((d, LANES))],
        out_specs=[row(w) for w, _ in outs],
        out_shape=[jax.ShapeDtypeStruct((m, w), dt) for w, dt in outs],
        compiler_params=_cparams("arbitrary"),
        name="proj_attn",
    )(x2d, g, w_attn, w_small)


def _proj_conv_kernel(*refs, cw, tm, seq, tiles_per_seq, tail):
    if tiles_per_seq:
        x_ref, g_ref, w_ref, cwt_ref, c_ref, ut_ref, carry_ref = refs
    else:
        x_ref, g_ref, w_ref, cwt_ref, p1_ref, p2_ref, c_ref, ut_ref = refs
    hb = _rms_bf16(x_ref[...], g_ref[...])
    cx = _dot(hb, w_ref[:, 0:cw])
    cb = _dot(hb, w_ref[:, cw:2 * cw])
    cc = _dot(hb, w_ref[:, 2 * cw:3 * cw])
    u = cc * cx
    r = lax.broadcasted_iota(I32, (tm, cw), 0)
    um1 = pltpu.roll(u, 1, 0)
    um2 = pltpu.roll(u, 2, 0)
    if tiles_per_seq:
        @pl.when(pl.program_id(0) % tiles_per_seq == 0)
        def _():
            carry_ref[...] = jnp.zeros_like(carry_ref)
        prev1 = jnp.broadcast_to(carry_ref[SUBLANES - 1:SUBLANES, :], (tm, cw))
        prev2 = jnp.broadcast_to(carry_ref[SUBLANES - 2:SUBLANES - 1, :], (tm, cw))
        um1 = jnp.where(r == 0, prev1, um1)
        um2 = jnp.where(r == 0, prev2, jnp.where(r == 1, prev1, um2))
        carry_ref[...] = u[tm - SUBLANES:tm, :]
    else:
        assert seq & (seq - 1) == 0
        t = r & (seq - 1)
        um1 = jnp.where(t >= 1, um1, p1_ref[...])
        um2 = jnp.where(t >= 2, um2, p2_ref[...])
    y = cwt_ref[0:1, :] * um2 + cwt_ref[1:2, :] * um1 + cwt_ref[2:3, :] * u
    c_ref[...] = (cb * y).astype(BF16)
    ut_ref[...] = u[tm - tail:tm, :]


def _proj_conv(x2d, g, w_conv, conv_w, prev, *, seq, tm):
    m, d = x2d.shape
    cw = conv_w.shape[1]
    assert conv_w.shape[0] == 3
    row = lambda width: pl.BlockSpec((tm, width), lambda i: (i, 0))
    in_specs = [row(d), _resident((1, d)), _resident((d, 3 * cw)), _resident((3, cw))]
    args = [x2d, g, w_conv, conv_w]
    if prev is None:
        assert seq % tm == 0
        tiles_per_seq, tail = seq // tm, SUBLANES
        scratch = [pltpu.VMEM((SUBLANES, cw), F32)]
    else:
        assert tm % seq == 0 and seq >= 2
        tiles_per_seq, tail = 0, tm
        scratch = []
        in_specs += [row(cw), row(cw)]
        args += list(prev)
    return pl.pallas_call(
        functools.partial(_proj_conv_kernel, cw=cw, tm=tm, seq=seq, tiles_per_seq=tiles_per_seq, tail=tail),
        grid=(m // tm,),
        in_specs=in_specs,
        out_specs=[row(cw), pl.BlockSpec((tail, cw), lambda i: (i, 0))],
        out_shape=[jax.ShapeDtypeStruct((m, cw), BF16), jax.ShapeDtypeStruct((m // tm * tail, cw), F32)],
        scratch_shapes=scratch,
        compiler_params=_cparams("arbitrary"),
        name="proj_conv",
    )(*args)


def _bias_tiles_kernel(relb_ref, o_ref, *, tb, n_buckets):
    kind = pl.program_id(0)
    h = pl.program_id(1)
    r = lax.broadcasted_iota(I32, (tb, tb), 0)
    c = lax.broadcasted_iota(I32, (tb, tb), 1)
    o_ref[...] = _bias_of_bucket(_bucket(kind * tb + r - c, n_buckets), relb_ref, h, n_buckets)


def _bias_tiles(rel_bias, *, tb):
    n_buckets, heads = rel_bias.shape
    return pl.pallas_call(
        functools.partial(_bias_tiles_kernel, tb=tb, n_buckets=n_buckets),
        grid=(2, heads),
        in_specs=[pl.BlockSpec(memory_space=pltpu.SMEM)],
        out_specs=pl.BlockSpec((None, None, tb, tb), lambda a, h: (a, h, 0, 0)),
        out_shape=jax.ShapeDtypeStruct((2, heads, tb, tb), F32),
        compiler_params=_cparams("arbitrary", "arbitrary"),
        name="bias_tiles",
    )(rel_bias)


def _attn_prompt_kernel(relb_ref, qi_ref, sm_ref, smb_ref, q_ref, kb_ref, vb_ref, bias_ref, o_ref,
                        skey_ref, thr_ref, wb_ref, m_ref, l_ref, acc_ref,
                        *, tb, heads, hd, iheads, idim, topk, n_buckets):
    i = pl.program_id(1)
    reps = tb // LANES
    wscale = idim ** -0.5 * iheads ** -0.5

    m_ref[...] = jnp.full(m_ref.shape, NEG_BIG, F32)
    l_ref[...] = jnp.zeros(l_ref.shape, F32)
    acc_ref[...] = jnp.zeros(acc_ref.shape, F32)
    for h in range(iheads):
        wb_ref[h] = jnp.broadcast_to(sm_ref[:, idim + h:idim + h + 1] * wscale, (tb, LANES))

    def chunk(j):
        return pl.ds(pl.multiple_of(j * tb, tb), tb)

    def score_chunk(j, diag):
        kic = smb_ref[chunk(j), 0:idim]
        acc = jnp.zeros((tb, tb), F32)
        for h in range(iheads):
            s = _dot_nt(qi_ref[:, h * idim:(h + 1) * idim], kic)
            acc = acc + jnp.maximum(s, 0.0) * jnp.tile(wb_ref[h], (1, reps))
        key = _ordered_key(acc)
        if diag:
            r = lax.broadcasted_iota(I32, (tb, tb), 0)
            c = lax.broadcasted_iota(I32, (tb, tb), 1)
            key = jnp.where(c > r, INT_MIN, key)
        skey_ref[:, chunk(j)] = key

    def score_body(j, carry):
        score_chunk(j, False)
        return carry

    lax.fori_loop(0, i, score_body, 0)
    score_chunk(i, True)

    def count_ge(scand):
        rep = jnp.tile(scand, (1, reps))

        def body(j, cnt):
            return cnt + jnp.sum(jnp.where(skey_ref[:, chunk(j)] >= rep, 1.0, 0.0), axis=1, keepdims=True)

        return lax.fori_loop(0, i + 1, body, jnp.zeros((tb, LANES), F32))

    trips = jnp.where((i + 1) * tb <= topk, 0, 32)
    thr_ref[...] = _kth_threshold(count_ge, tb, float(topk), trips)

    def attend_chunk(j, kind):
        mask = skey_ref[:, chunk(j)] >= jnp.tile(thr_ref[...], (1, reps))
        for h in range(heads):
            hs = slice(h * hd, (h + 1) * hd)
            lg = _dot_nt(q_ref[:, hs], kb_ref[chunk(j), hs])
            if kind is None:
                lg = lg + relb_ref[n_buckets - 1, h]
            else:
                lg = lg + bias_ref[kind, h]
            _flash_update(lg, mask, vb_ref[chunk(j), hs], m_ref, l_ref, acc_ref, h, hd)

    def attend_body(j, carry):
        attend_chunk(j, None)
        return carry

    lax.fori_loop(0, jnp.maximum(i - 1, 0), attend_body, 0)

    @pl.when(i >= 1)
    def _():
        attend_chunk(i - 1, 1)

    attend_chunk(i, 0)

    for h in range(heads):
        hs = slice(h * hd, (h + 1) * hd)
        o_ref[:, hs] = (acc_ref[:, hs] / jnp.tile(l_ref[h], (1, hd // LANES))).astype(BF16)


def _attn_prompt(rel_bias, bias_tiles, qi, sm, smb, q, kb, vb, *, batch, seq, tb, heads, hd, iheads, idim, topk):
    m, aw = q.shape
    iw = qi.shape[1]
    nq = seq // tb
    n_buckets = rel_bias.shape[0]
    assert tb >= MAX_DISTANCE and tb % LANES == 0 and hd % LANES == 0
    qrow = lambda width: pl.BlockSpec((tb, width), lambda b, i: (b * nq + i, 0))
    seqblk = lambda width: pl.BlockSpec((seq, width), lambda b, i: (b, 0))
    return pl.pallas_call(
        functools.partial(_attn_prompt_kernel, tb=tb, heads=heads, hd=hd, iheads=iheads, idim=idim,
                          topk=topk, n_buckets=n_buckets),
        grid=(batch, nq),
        in_specs=[pl.BlockSpec(memory_space=pltpu.SMEM), qrow(iw), qrow(LANES), seqblk(LANES), qrow(aw),
                  seqblk(aw), seqblk(aw), _resident(bias_tiles.shape)],
        out_specs=qrow(aw),
        out_shape=jax.ShapeDtypeStruct((m, aw), BF16),
        scratch_shapes=[pltpu.VMEM((tb, seq), I32), pltpu.VMEM((tb, LANES), I32),
                        pltpu.VMEM((iheads, tb, LANES), F32), pltpu.VMEM((heads, tb, LANES), F32),
                        pltpu.VMEM((heads, tb, LANES), F32), pltpu.VMEM((tb, aw), F32)],
        compiler_params=_cparams("arbitrary", "arbitrary"),
        name="attn_prompt",
    )(rel_bias, qi, sm, smb, q, kb, vb, bias_tiles)


def _sample_score_kernel(pt_ref, qi_ref, w_ref, kin_ref, *rest, pages, ps, nc, iheads, idim, tq, rq):
    page_refs, o_ref = rest[:pages], rest[pages]
    c = pl.program_id(1)
    wscale = idim ** -0.5 * iheads ** -0.5

    def score(keys):
        s = _dot_nt(qi_ref[...], keys)
        t = jnp.maximum(s, 0.0) * (w_ref[:, 0:1] * wscale)
        acc = t[0:rq]
        for h in range(1, iheads):
            acc = acc + t[h * rq:(h + 1) * rq]
        return _ordered_key(acc)

    @pl.when(c < nc)
    def _():
        for p in range(pages):
            o_ref[:, p * ps:(p + 1) * ps] = score(page_refs[p][...].astype(BF16))

    @pl.when(c == nc)
    def _():
        key = score(kin_ref[...])
        j = lax.broadcasted_iota(I32, (rq, ps), 0)
        n = lax.broadcasted_iota(I32, (rq, ps), 1)
        o_ref[:, 0:ps] = jnp.where((n <= j) & (n < tq), key, INT_MIN)
        o_ref[:, ps:] = jnp.full((rq, (pages - 1) * ps), INT_MIN, I32)


def _sample_scores(page_table, qi_hm, w_hm, kin, cache_kidx, *, pages, tq, rq, iheads):
    db, n_pages = page_table.shape
    _, ps, idim = cache_kidx.shape
    nc = n_pages // pages
    ch = pages * ps
    page_spec = lambda p: pl.BlockSpec(
        (None, ps, idim), lambda b, c, pt: (pt[b, jnp.minimum(c, nc - 1) * pages + p], 0, 0))
    per_b = lambda shape: pl.BlockSpec((None,) + shape, lambda b, c, pt: (b, 0, 0))
    return pl.pallas_call(
        functools.partial(_sample_score_kernel, pages=pages, ps=ps, nc=nc, iheads=iheads, idim=idim, tq=tq, rq=rq),
        grid_spec=pltpu.PrefetchScalarGridSpec(
            num_scalar_prefetch=1,
            grid=(db, nc + 1),
            in_specs=[per_b((iheads * rq, idim)), per_b((iheads * rq, LANES)), per_b((ps, idim))]
            + [page_spec(p) for p in range(pages)],
            out_specs=pl.BlockSpec((None, rq, ch), lambda b, c, pt: (b, 0, c)),
        ),
        out_shape=jax.ShapeDtypeStruct((db, rq, (nc + 1) * ch), I32),
        compiler_params=_cparams("arbitrary", "arbitrary"),
        name="sample_scores",
    )(page_table, qi_hm, w_hm, kin, *([cache_kidx] * pages))


def _sample_attn_kernel(pt_ref, relb_ref, skey_ref, q_ref, kn_ref, vn_ref, *rest,
                        pages, ps, nc, heads, hd, topk, tq, rq, n_buckets):
    k_refs, v_refs = rest[:pages], rest[pages:2 * pages]
    o_ref, thr_ref, bias_ref, kbf_ref, vbf_ref, m_ref, l_ref, acc_ref = rest[2 * pages:]
    c = pl.program_id(1)
    ch = pages * ps
    past = nc * ch

    @pl.when(c == 0)
    def _():
        m_ref[...] = jnp.full(m_ref.shape, NEG_BIG, F32)
        l_ref[...] = jnp.zeros(l_ref.shape, F32)
        acc_ref[...] = jnp.zeros(acc_ref.shape, F32)

        def count_ge(scand):
            sk = skey_ref[...]
            return jnp.sum(jnp.where(sk >= jnp.tile(scand, (1, sk.shape[1] // LANES)), 1.0, 0.0),
                           axis=1, keepdims=True)

        thr_ref[...] = _kth_threshold(count_ge, rq, float(topk), 32)
        for h in range(heads):
            bias_ref[h] = jnp.full((rq, ch), relb_ref[n_buckets - 1, h], F32)

    @pl.when(c == nc - 1)
    def _():
        j = lax.broadcasted_iota(I32, (rq, ch), 0)
        s = lax.broadcasted_iota(I32, (rq, ch), 1)
        bucket = _bucket(ch + j - s, n_buckets)
        for h in range(heads):
            bias_ref[h] = _bias_of_bucket(bucket, relb_ref, h, n_buckets)

    @pl.when(c < nc)
    def _():
        for p in range(pages):
            kbf_ref[p * ps:(p + 1) * ps, :] = k_refs[p][...].astype(BF16)
            vbf_ref[p * ps:(p + 1) * ps, :] = v_refs[p][...].astype(BF16)
        mask = skey_ref[:, pl.ds(pl.multiple_of(c * ch, ch), ch)] >= jnp.tile(thr_ref[...], (1, ch // LANES))
        for h in range(heads):
            hs = slice(h * hd, (h + 1) * hd)
            lg = _dot_nt(q_ref[:, hs], kbf_ref[:, hs]) + bias_ref[h]
            _flash_update(lg, mask, vbf_ref[:, hs], m_ref, l_ref, acc_ref, h, hd)

    @pl.when(c == nc)
    def _():
        mask = skey_ref[:, past:past + ps] >= jnp.tile(thr_ref[...], (1, ps // LANES))
        j = lax.broadcasted_iota(I32, (rq, ps), 0)
        n = lax.broadcasted_iota(I32, (rq, ps), 1)
        bucket = _bucket(j - n, n_buckets)
        for h in range(heads):
            hs = slice(h * hd, (h + 1) * hd)
            lg = _dot_nt(q_ref[:, hs], kn_ref[:, hs]) + _bias_of_bucket(bucket, relb_ref, h, n_buckets)
            _flash_update(lg, mask, vn_ref[:, hs], m_ref, l_ref, acc_ref, h, hd)
        for h in range(heads):
            hs = slice(h * hd, (h + 1) * hd)
            o_ref[:, hs] = (acc_ref[:, hs] / jnp.tile(l_ref[h], (1, hd // LANES))).astype(BF16)


def _sample_attn(page_table, rel_bias, skey, q8, kn, vn, cache_k, cache_v, *, pages, heads, hd, topk, tq, rq):
    db, n_pages = page_table.shape
    _, ps, aw = cache_k.shape
    nc = n_pages // pages
    ch = pages * ps
    n_buckets = rel_bias.shape[0]
    assert ch >= MAX_DISTANCE and ps >= tq and ps % LANES == 0
    page_spec = lambda p: pl.BlockSpec(
        (None, ps, aw), lambda b, c, pt: (pt[b, jnp.minimum(c, nc - 1) * pages + p], 0, 0))
    per_b = lambda shape: pl.BlockSpec((None,) + shape, lambda b, c, pt: (b, 0, 0))
    return pl.pallas_call(
        functools.partial(_sample_attn_kernel, pages=pages, ps=ps, nc=nc, heads=heads, hd=hd, topk=topk,
                          tq=tq, rq=rq, n_buckets=n_buckets),
        grid_spec=pltpu.PrefetchScalarGridSpec(
            num_scalar_prefetch=1,
            grid=(db, nc + 1),
            in_specs=[pl.BlockSpec(memory_space=pltpu.SMEM), per_b((rq, skey.shape[2])), per_b((rq, aw)),
                      per_b((ps, aw)), per_b((ps, aw))]
            + [page_spec(p) for p in range(pages)] * 2,
            out_specs=per_b((rq, aw)),
            scratch_shapes=[pltpu.VMEM((rq, LANES), I32), pltpu.VMEM((heads, rq, ch), F32),
                            pltpu.VMEM((ch, aw), BF16), pltpu.VMEM((ch, aw), BF16),
                            pltpu.VMEM((heads, rq, LANES), F32), pltpu.VMEM((heads, rq, LANES), F32),
                            pltpu.VMEM((rq, aw), F32)],
        ),
        out_shape=jax.ShapeDtypeStruct((db, rq, aw), BF16),
        compiler_params=_cparams("arbitrary", "arbitrary"),
        name="sample_attn",
    )(page_table, rel_bias, skey, q8, kn, vn, *([cache_k] * pages), *([cache_v] * pages))


def _mix_kernel(x_ref, g_ref, a_ref, c_ref, wg_ref, wpa_ref, wpb_ref, wo_ref, o_ref, *, d):
    x = x_ref[...]
    hb = _rms_bf16(x, g_ref[...])
    a = _dot(a_ref[...], wpa_ref[...])
    m = jax.nn.sigmoid(_dot(hb, wg_ref[:, 0:d])) * a
    c = _dot(c_ref[...], wpb_ref[...])
    m = m + jax.nn.sigmoid(_dot(hb, wg_ref[:, d:2 * d])) * c
    o_ref[...] = x + _dot(m.astype(BF16), wo_ref[...])


def _mix(x2d, g, attn, c_in, w_gate, w_pa, w_pb, w_o, *, tm):
    m, d = x2d.shape
    row = lambda width: pl.BlockSpec((tm, width), lambda i: (i, 0))
    return pl.pallas_call(
        functools.partial(_mix_kernel, d=d),
        grid=(m // tm,),
        in_specs=[row(d), _resident((1, d)), row(attn.shape[1]), row(c_in.shape[1]), _resident(w_gate.shape),
                  _resident(w_pa.shape), _resident(w_pb.shape), _resident(w_o.shape)],
        out_specs=row(d),
        out_shape=jax.ShapeDtypeStruct((m, d), F32),
        compiler_params=_cparams("arbitrary"),
        name="mix_out",
    )(x2d, g, attn, c_in, w_gate, w_pa, w_pb, w_o)


def _mlp_kernel(x_ref, g_ref, gf_ref, w1_ref, w2_ref, y_ref, h_ref, acc_ref, *, final):
    f = pl.program_id(1)

    @pl.when(f == 0)
    def _():
        h_ref[...] = _rms_bf16(x_ref[...], g_ref[...])
        acc_ref[...] = jnp.zeros(acc_ref.shape, F32)

    t = jnp.square(jnp.maximum(_dot(h_ref[...], w1_ref[...]), 0.0))
    acc_ref[...] += _dot(t.astype(BF16), w2_ref[...])

    @pl.when(f == pl.num_programs(1) - 1)
    def _():
        x2 = x_ref[...] + acc_ref[...]
        if final:
            x2 = x2 * lax.rsqrt(jnp.mean(x2 * x2, axis=-1, keepdims=True) + EPS) * gf_ref[...]
        y_ref[...] = x2


def _mlp(x2d, g, gf, w1, w2, *, tm, tf, final):
    m, d = x2d.shape
    ff = w1.shape[1]
    return pl.pallas_call(
        functools.partial(_mlp_kernel, final=final),
        grid=(m // tm, ff // tf),
        in_specs=[pl.BlockSpec((tm, d), lambda i, f: (i, 0)), _resident((1, d)), _resident((1, d)),
                  pl.BlockSpec((d, tf), lambda i, f: (0, f)), pl.BlockSpec((tf, d), lambda i, f: (f, 0))],
        out_specs=pl.BlockSpec((tm, d), lambda i, f: (i, 0)),
        out_shape=jax.ShapeDtypeStruct((m, d), F32),
        scratch_shapes=[pltpu.VMEM((tm, d), BF16), pltpu.VMEM((tm, d), F32)],
        compiler_params=_cparams("arbitrary", "arbitrary"),
        name="mlp",
    )(x2d, g, gf, w1, w2)


def _tile(m, cap):
    return min(m, cap)


def kernel(x_prompt, x_sample, cache_k, cache_v, cache_kidx, state_conv, page_table, rel_bias, norm_mix_g, w_in,
           conv_w, w_pa, w_pb, w_o, norm_mlp_g, w_mlp_in, w_mlp_out, norm_final_g):
    batch, seq, d = x_prompt.shape
    db, tq, _ = x_sample.shape
    depth, n_pool, ps, heads, hd = cache_k.shape
    idim = cache_kidx.shape[-1]
    cw = conv_w.shape[-1]
    aw = heads * hd
    n_in = w_in.shape[-1]
    iheads = (n_in - 3 * aw - idim - 3 * cw - 2 * d) // (idim + 1)
    iw = iheads * idim
    assert 3 * aw + iw + idim + iheads + 3 * cw + 2 * d == n_in and idim + iheads <= LANES
    n_pages = page_table.shape[1]
    past = n_pages * ps
    rq = SUBLANES
    assert tq <= rq

    mp, ms = batch * seq, db * tq
    xp = x_prompt.reshape(mp, d)
    xs = x_sample.reshape(ms, d)
    tb = _tile(seq, 256)
    pages = math.gcd(n_pages, 8)
    o_small = 3 * aw + iw
    o_conv = o_small + idim + iheads
    o_gate = o_conv + 3 * cw
    gf = norm_final_g.reshape(1, d)
    bias_tiles = _bias_tiles(rel_bias, tb=tb)

    outs = {k: [] for k in ("kp", "vp", "kip", "sp", "ks", "vs", "kis", "ss")}
    for l in range(depth):
        wl = w_in[l]
        w_attn = wl[:, :o_small].astype(BF16)
        w_small = jnp.pad(wl[:, o_small:o_conv], ((0, 0), (0, LANES - idim - iheads))).astype(BF16)
        w_conv = wl[:, o_conv:o_gate].astype(BF16)
        w_gate = wl[:, o_gate:].astype(BF16)
        wpa, wpb, wo = w_pa[l].astype(BF16), w_pb[l].astype(BF16), w_o[l].astype(BF16)
        w1, w2 = w_mlp_in[l].astype(BF16), w_mlp_out[l].astype(BF16)
        g_mix = norm_mix_g[l].reshape(1, d)
        g_mlp = norm_mlp_g[l].reshape(1, d)

        tm = _tile(seq, 512)
        q, k, v, kb, vb, qi, sm, smb = _proj_attn(xp, g_mix, w_attn, w_small, aw=aw, iw=iw, head_dim=hd, tm=tm)
        c_in, u_tail = _proj_conv(xp, g_mix, w_conv, conv_w[l], None, seq=seq, tm=_tile(seq, 256))
        attn = _attn_prompt(rel_bias, bias_tiles, qi, sm, smb, q, kb, vb, batch=batch, seq=seq, tb=tb,
                            heads=heads, hd=hd, iheads=iheads, idim=idim, topk=min(TOPK_MAX, seq // 4))
        x1 = _mix(xp, g_mix, attn, c_in, w_gate, wpa, wpb, wo, tm=_tile(mp, 256))
        xp_next = _mlp(x1, g_mlp, gf, w1, w2, tm=_tile(mp, 512), tf=_tile(w1.shape[1], 1024), final=l == depth - 1)
        outs["kp"].append(k.reshape(batch, seq, heads, hd))
        outs["vp"].append(v.reshape(batch, seq, heads, hd))
        outs["kip"].append(sm[:, :idim].reshape(batch, seq, idim))
        outs["sp"].append(u_tail.reshape(batch, -1, SUBLANES, cw)[:, -1, SUBLANES - 2:])

        q, k, v, kb, vb, qi, sm, smb = _proj_attn(xs, g_mix, w_attn, w_small, aw=aw, iw=iw, head_dim=hd, tm=ms)
        st = state_conv[l]
        zero = jnp.zeros((db, tq - 1, cw), F32)
        prev1 = jnp.concatenate([st[:, 1:2], zero], axis=1).reshape(ms, cw)
        prev2 = jnp.concatenate([st, zero[:, 1:]], axis=1).reshape(ms, cw)
        c_in, u_all = _proj_conv(xs, g_mix, w_conv, conv_w[l], (prev1, prev2), seq=tq, tm=ms)

        def pad_rows(a, n):
            return jnp.pad(a, ((0, 0), (0, n - a.shape[1])) + ((0, 0),) * (a.ndim - 2))

        qi_hm = pad_rows(qi.reshape(db, tq, iheads, idim).transpose(0, 2, 1, 3).reshape(db * iheads, tq, idim), rq)
        qi_hm = qi_hm.reshape(db, iheads * rq, idim)
        w_hm = pad_rows(sm[:, idim:idim + iheads].reshape(db, tq, iheads).transpose(0, 2, 1).reshape(db * iheads, tq), rq)
        w_hm = jnp.broadcast_to(w_hm.reshape(db, iheads * rq, 1), (db, iheads * rq, LANES))
        kin = pad_rows(smb[:, :idim].reshape(db, tq, idim), ps)
        skey = _sample_scores(page_table, qi_hm, w_hm, kin, cache_kidx[l], pages=pages, tq=tq, rq=rq, iheads=iheads)
        attn8 = _sample_attn(page_table, rel_bias, skey, pad_rows(q.reshape(db, tq, aw), rq),
                             pad_rows(kb.reshape(db, tq, aw), ps), pad_rows(vb.reshape(db, tq, aw), ps),
                             cache_k[l].reshape(n_pool, ps, aw), cache_v[l].reshape(n_pool, ps, aw),
                             pages=pages, heads=heads, hd=hd, topk=min(TOPK_MAX, (past + tq) // 4), tq=tq, rq=rq)
        attn = attn8[:, :tq].reshape(ms, aw)
        x1 = _mix(xs, g_mix, attn, c_in, w_gate, wpa, wpb, wo, tm=ms)
        xs_next = _mlp(x1, g_mlp, gf, w1, w2, tm=ms, tf=_tile(w1.shape[1], 1024), final=l == depth - 1)
        outs["ks"].append(k.reshape(db, tq, heads, hd))
        outs["vs"].append(v.reshape(db, tq, heads, hd))
        outs["kis"].append(sm[:, :idim].reshape(db, tq, idim))
        outs["ss"].append(u_all.reshape(db, tq, cw)[:, tq - 2:])
        xp, xs = xp_next, xs_next

    st = {k: jnp.stack(v) for k, v in outs.items()}
    return (xp.reshape(batch, seq, d), xs.reshape(db, tq, d), st["kp"], st["vp"], st["kip"], st["sp"],
            st["ks"], st["vs"], st["kis"], st["ss"])
```

```python
import functools
import math

import jax
import jax.numpy as jnp
import numpy as np
from jax import lax
from jax.experimental import pallas as pl
from jax.experimental.pallas import tpu as pltpu

F32 = jnp.float32
BF16 = jnp.bfloat16
I32 = jnp.int32

TOPK_MAX = 256
MAX_DISTANCE = 128
EPS = 1e-6

LANES = 128
SUBLANES = 8
V7X_SCOPED_VMEM_BYTES = 60000 * 1024

INT_MIN = np.int32(-2 ** 31)
NEG_BIG = -1e30


def _cparams(*sem):
    return pltpu.CompilerParams(dimension_semantics=sem, vmem_limit_bytes=V7X_SCOPED_VMEM_BYTES)


def _resident(shape):
    nd = len(shape)
    return pl.BlockSpec(shape, lambda *_: (0,) * nd, pipeline_mode=pl.Buffered(1))


def _rms_bf16(x, g):
    y = x * lax.rsqrt(jnp.mean(x * x, axis=-1, keepdims=True) + EPS)
    return (y * g).astype(BF16)


def _dot(a, b):
    return jnp.dot(a, b, preferred_element_type=F32)


def _dot_nt(a, b):
    return lax.dot_general(a, b, (((1,), (1,)), ((), ())), preferred_element_type=F32)


def _ordered_key(x):
    b = lax.bitcast_convert_type(x, I32)
    return b ^ ((b >> 31) & np.int32(0x7FFFFFFF))


def _bucket(n, n_buckets):
    n = jnp.maximum(n, 0)
    me = n_buckets // 2
    nf = jnp.maximum(n, me).astype(F32)
    large = me + (jnp.log(nf / me) / math.log(MAX_DISTANCE / me) * (n_buckets - me)).astype(I32)
    large = jnp.minimum(large, n_buckets - 1)
    return jnp.where(n < me, n, large)


def _bias_of_bucket(bucket, relb_ref, h, n_buckets):
    acc = jnp.zeros(bucket.shape, F32)
    for bkt in range(n_buckets):
        acc = jnp.where(bucket == bkt, relb_ref[bkt, h], acc)
    return acc


def _fold_lanes(x):
    acc = x[:, 0:LANES]
    for t in range(1, x.shape[1] // LANES):
        acc = acc + x[:, t * LANES:(t + 1) * LANES]
    return acc


def _kth_threshold(count_ge, rows, k, trips):
    def bit_body(t, uthr):
        cand = uthr | jnp.left_shift(np.int32(1), 31 - t)
        cnt = jnp.sum(count_ge(cand ^ INT_MIN), axis=1, keepdims=True)
        return jnp.where(cnt >= k, cand, uthr)

    uthr = lax.fori_loop(0, trips, bit_body, jnp.zeros((rows, LANES), I32))
    return jnp.maximum(uthr ^ INT_MIN, INT_MIN + 1)


def _flash_update(lg, mask, v, m_ref, l_ref, acc_ref, h, hd):
    reps = lg.shape[1] // LANES
    m_prev = m_ref[h]
    m_cur = jnp.max(jnp.where(mask, lg, NEG_BIG), axis=1, keepdims=True)
    m_new = jnp.maximum(m_prev, m_cur)
    p = jnp.where(mask, jnp.exp(lg - jnp.tile(m_new, (1, reps))), 0.0)
    alpha = jnp.exp(m_prev - m_new)
    l_ref[h] = alpha * l_ref[h] + jnp.sum(p, axis=1, keepdims=True)
    m_ref[h] = m_new
    hs = slice(h * hd, (h + 1) * hd)
    acc_ref[:, hs] = acc_ref[:, hs] * jnp.tile(alpha, (1, hd // LANES)) + _dot(p.astype(BF16), v)


def _proj_attn_kernel(x_ref, g_ref, w_ref, ws_ref, q_ref, k_ref, v_ref, kb_ref, vb_ref, qi_ref,
                      sm_ref, smb_ref, *, aw, iw, hd, tm, qscale):
    heads = aw // hd
    hb = _rms_bf16(x_ref[...], g_ref[...])
    q_ref[...] = (_dot(hb, w_ref[:, 0:aw]) * qscale).astype(BF16)
    for o_ref, ob_ref, c0 in ((k_ref, kb_ref, aw), (v_ref, vb_ref, 2 * aw)):
        kv = _dot(hb, w_ref[:, c0:c0 + aw])
        ob_ref[...] = kv.astype(BF16)
        for h in range(heads):
            o_ref[pl.ds(h, tm, stride=heads), :] = kv[:, h * hd:(h + 1) * hd]
    qi_ref[...] = _dot(hb, w_ref[:, 3 * aw:3 * aw + iw]).astype(BF16)
    sm = _dot(hb, ws_ref[...])
    sm_ref[...] = sm
    smb_ref[...] = sm.astype(BF16)


def _proj_attn(x2d, g, w_attn, w_small, *, aw, iw, hd, tm):
    m, d = x2d.shape
    wt = w_attn.shape[1]
    heads = aw // hd
    assert hd == LANES
    row = lambda width: pl.BlockSpec((tm, width), lambda i: (i, 0))
    kv_spec = pl.BlockSpec((tm * heads, hd), lambda i: (i, 0))
    kv_shape = jax.ShapeDtypeStruct((m * heads, hd), F32)
    outs = [(aw, BF16), None, None, (aw, BF16), (aw, BF16), (iw, BF16), (LANES, F32), (LANES, BF16)]
    return pl.pallas_call(
        functools.partial(_proj_attn_kernel, aw=aw, iw=iw, hd=hd, tm=tm, qscale=hd ** -0.5),
        grid=(m // tm,),
        in_specs=[row(d), _resident((1, d)), _resident((d, wt)), _resident((d, LANES))],
        out_specs=[kv_spec if o is None else row(o[0]) for o in outs],
        out_shape=[kv_shape if o is None else jax.ShapeDtypeStruct((m, o[0]), o[1]) for o in outs],
        compiler_params=_cparams("arbitrary"),
        name="proj_attn",
    )(x2d, g, w_attn, w_small)


def _proj_conv_kernel(*refs, cw, tm, seq, tiles_per_seq, tail):
    if tiles_per_seq:
        x_ref, g_ref, w_ref, cwt_ref, c_ref, ut_ref, carry_ref = refs
    else:
        x_ref, g_ref, w_ref, cwt_ref, p1_ref, p2_ref, c_ref, ut_ref = refs
    hb = _rms_bf16(x_ref[...], g_ref[...])
    cx = _dot(hb, w_ref[:, 0:cw])
    cb = _dot(hb, w_ref[:, cw:2 * cw])
    cc = _dot(hb, w_ref[:, 2 * cw:3 * cw])
    u = cc * cx
    r = lax.broadcasted_iota(I32, (tm, cw), 0)
    um1 = pltpu.roll(u, 1, 0)
    um2 = pltpu.roll(u, 2, 0)
    if tiles_per_seq:
        @pl.when(pl.program_id(0) % tiles_per_seq == 0)
        def _():
            carry_ref[...] = jnp.zeros_like(carry_ref)
        prev1 = jnp.broadcast_to(carry_ref[SUBLANES - 1:SUBLANES, :], (tm, cw))
        prev2 = jnp.broadcast_to(carry_ref[SUBLANES - 2:SUBLANES - 1, :], (tm, cw))
        um1 = jnp.where(r == 0, prev1, um1)
        um2 = jnp.where(r == 0, prev2, jnp.where(r == 1, prev1, um2))
        carry_ref[...] = u[tm - SUBLANES:tm, :]
    else:
        assert seq & (seq - 1) == 0
        t = r & (seq - 1)
        um1 = jnp.where(t >= 1, um1, p1_ref[...])
        um2 = jnp.where(t >= 2, um2, p2_ref[...])
    y = cwt_ref[0:1, :] * um2 + cwt_ref[1:2, :] * um1 + cwt_ref[2:3, :] * u
    c_ref[...] = (cb * y).astype(BF16)
    ut_ref[...] = u[tm - tail:tm, :]


def _proj_conv(x2d, g, w_conv, conv_w, prev, *, seq, tm):
    m, d = x2d.shape
    cw = conv_w.shape[1]
    assert conv_w.shape[0] == 3
    row = lambda width: pl.BlockSpec((tm, width), lambda i: (i, 0))
    in_specs = [row(d), _resident((1, d)), _resident((d, 3 * cw)), _resident((3, cw))]
    args = [x2d, g, w_conv, conv_w]
    if prev is None:
        assert seq % tm == 0
        tiles_per_seq, tail = seq // tm, SUBLANES
        scratch = [pltpu.VMEM((SUBLANES, cw), F32)]
    else:
        assert tm % seq == 0 and seq >= 2
        tiles_per_seq, tail = 0, tm
        scratch = []
        in_specs += [row(cw), row(cw)]
        args += list(prev)
    return pl.pallas_call(
        functools.partial(_proj_conv_kernel, cw=cw, tm=tm, seq=seq, tiles_per_seq=tiles_per_seq, tail=tail),
        grid=(m // tm,),
        in_specs=in_specs,
        out_specs=[row(cw), pl.BlockSpec((tail, cw), lambda i: (i, 0))],
        out_shape=[jax.ShapeDtypeStruct((m, cw), BF16), jax.ShapeDtypeStruct((m // tm * tail, cw), F32)],
        scratch_shapes=scratch,
        compiler_params=_cparams("arbitrary"),
        name="proj_conv",
    )(*args)


def _bias_tiles_kernel(relb_ref, o_ref, *, tb, n_buckets):
    kind = pl.program_id(0)
    h = pl.program_id(1)
    r = lax.broadcasted_iota(I32, (tb, tb), 0)
    c = lax.broadcasted_iota(I32, (tb, tb), 1)
    o_ref[...] = _bias_of_bucket(_bucket(kind * tb + r - c, n_buckets), relb_ref, h, n_buckets)


def _bias_tiles(rel_bias, *, tb):
    n_buckets, heads = rel_bias.shape
    return pl.pallas_call(
        functools.partial(_bias_tiles_kernel, tb=tb, n_buckets=n_buckets),
        grid=(2, heads),
        in_specs=[pl.BlockSpec(memory_space=pltpu.SMEM)],
        out_specs=pl.BlockSpec((None, None, tb, tb), lambda a, h: (a, h, 0, 0)),
        out_shape=jax.ShapeDtypeStruct((2, heads, tb, tb), F32),
        compiler_params=_cparams("arbitrary", "arbitrary"),
        name="bias_tiles",
    )(rel_bias)


def _attn_prompt_kernel(relb_ref, qi_ref, sm_ref, smb_ref, q_ref, kb_ref, vb_ref, bias_ref, o_ref,
                        skey_ref, thr_ref, wb_ref, m_ref, l_ref, acc_ref,
                        *, tb, heads, hd, iheads, idim, topk, n_buckets):
    i = pl.program_id(1)
    reps = tb // LANES
    wscale = idim ** -0.5 * iheads ** -0.5

    m_ref[...] = jnp.full(m_ref.shape, NEG_BIG, F32)
    l_ref[...] = jnp.zeros(l_ref.shape, F32)
    acc_ref[...] = jnp.zeros(acc_ref.shape, F32)
    for h in range(iheads):
        wb_ref[h] = jnp.broadcast_to(sm_ref[:, idim + h:idim + h + 1] * wscale, (tb, LANES))

    def chunk(j):
        return pl.ds(pl.multiple_of(j * tb, tb), tb)

    def score_chunk(j, diag):
        kic = smb_ref[chunk(j), 0:idim]
        acc = jnp.zeros((tb, tb), F32)
        for h in range(iheads):
            s = _dot_nt(qi_ref[:, h * idim:(h + 1) * idim], kic)
            acc = acc + jnp.maximum(s, 0.0) * jnp.tile(wb_ref[h], (1, reps))
        key = _ordered_key(acc)
        if diag:
            r = lax.broadcasted_iota(I32, (tb, tb), 0)
            c = lax.broadcasted_iota(I32, (tb, tb), 1)
            key = jnp.where(c > r, INT_MIN, key)
        skey_ref[:, chunk(j)] = key

    def score_body(j, carry):
        score_chunk(j, False)
        return carry

    lax.fori_loop(0, i, score_body, 0)
    score_chunk(i, True)

    def count_ge(scand):
        rep = jnp.tile(scand, (1, reps))

        def body(j, cnt):
            return cnt + _fold_lanes(jnp.where(skey_ref[:, chunk(j)] >= rep, 1.0, 0.0))

        return lax.fori_loop(0, i + 1, body, jnp.zeros((tb, LANES), F32))

    trips = jnp.where((i + 1) * tb <= topk, 0, 32)
    thr_ref[...] = _kth_threshold(count_ge, tb, float(topk), trips)

    def attend_chunk(j, kind):
        mask = skey_ref[:, chunk(j)] >= jnp.tile(thr_ref[...], (1, reps))
        for h in range(heads):
            hs = slice(h * hd, (h + 1) * hd)
            lg = _dot_nt(q_ref[:, hs], kb_ref[chunk(j), hs])
            if kind is None:
                lg = lg + relb_ref[n_buckets - 1, h]
            else:
                lg = lg + bias_ref[kind, h]
            _flash_update(lg, mask, vb_ref[chunk(j), hs], m_ref, l_ref, acc_ref, h, hd)

    def attend_body(j, carry):
        attend_chunk(j, None)
        return carry

    lax.fori_loop(0, jnp.maximum(i - 1, 0), attend_body, 0)

    @pl.when(i >= 1)
    def _():
        attend_chunk(i - 1, 1)

    attend_chunk(i, 0)

    for h in range(heads):
        hs = slice(h * hd, (h + 1) * hd)
        o_ref[:, hs] = (acc_ref[:, hs] / jnp.tile(l_ref[h], (1, hd // LANES))).astype(BF16)


def _attn_prompt(rel_bias, bias_tiles, qi, sm, smb, q, kb, vb, *, batch, seq, tb, heads, hd, iheads, idim, topk):
    m, aw = q.shape
    iw = qi.shape[1]
    nq = seq // tb
    n_buckets = rel_bias.shape[0]
    assert tb >= MAX_DISTANCE and tb % LANES == 0 and hd % LANES == 0
    qrow = lambda width: pl.BlockSpec((tb, width), lambda b, i: (b * nq + i, 0))
    seqblk = lambda width: pl.BlockSpec((seq, width), lambda b, i: (b, 0))
    return pl.pallas_call(
        functools.partial(_attn_prompt_kernel, tb=tb, heads=heads, hd=hd, iheads=iheads, idim=idim,
                          topk=topk, n_buckets=n_buckets),
        grid=(batch, nq),
        in_specs=[pl.BlockSpec(memory_space=pltpu.SMEM), qrow(iw), qrow(LANES), seqblk(LANES), qrow(aw),
                  seqblk(aw), seqblk(aw), _resident(bias_tiles.shape)],
        out_specs=qrow(aw),
        out_shape=jax.ShapeDtypeStruct((m, aw), BF16),
        scratch_shapes=[pltpu.VMEM((tb, seq), I32), pltpu.VMEM((tb, LANES), I32),
                        pltpu.VMEM((iheads, tb, LANES), F32), pltpu.VMEM((heads, tb, LANES), F32),
                        pltpu.VMEM((heads, tb, LANES), F32), pltpu.VMEM((tb, aw), F32)],
        compiler_params=_cparams("arbitrary", "arbitrary"),
        name="attn_prompt",
    )(rel_bias, qi, sm, smb, q, kb, vb, bias_tiles)


def _sample_score_kernel(pt_ref, qi_ref, w_ref, kin_ref, *rest, pages, ps, nc, iheads, idim, tq, rq):
    page_refs, o_ref = rest[:pages], rest[pages]
    c = pl.program_id(1)
    wscale = idim ** -0.5 * iheads ** -0.5

    def score(keys):
        s = _dot_nt(qi_ref[...], keys)
        t = jnp.maximum(s, 0.0) * (w_ref[:, 0:1] * wscale)
        acc = t[0:rq]
        for h in range(1, iheads):
            acc = acc + t[h * rq:(h + 1) * rq]
        return _ordered_key(acc)

    @pl.when(c < nc)
    def _():
        for p in range(pages):
            o_ref[:, p * ps:(p + 1) * ps] = score(page_refs[p][...].astype(BF16))

    @pl.when(c == nc)
    def _():
        key = score(kin_ref[...])
        j = lax.broadcasted_iota(I32, (rq, ps), 0)
        n = lax.broadcasted_iota(I32, (rq, ps), 1)
        o_ref[:, 0:ps] = jnp.where((n <= j) & (n < tq), key, INT_MIN)
        o_ref[:, ps:] = jnp.full((rq, (pages - 1) * ps), INT_MIN, I32)


def _sample_scores(page_table, qi_hm, w_hm, kin, cache_kidx, *, pages, tq, rq, iheads):
    db, n_pages = page_table.shape
    _, ps, idim = cache_kidx.shape
    nc = n_pages // pages
    ch = pages * ps
    page_spec = lambda p: pl.BlockSpec(
        (None, ps, idim), lambda b, c, pt: (pt[b, jnp.minimum(c, nc - 1) * pages + p], 0, 0))
    per_b = lambda shape: pl.BlockSpec((None,) + shape, lambda b, c, pt: (b, 0, 0))
    return pl.pallas_call(
        functools.partial(_sample_score_kernel, pages=pages, ps=ps, nc=nc, iheads=iheads, idim=idim, tq=tq, rq=rq),
        grid_spec=pltpu.PrefetchScalarGridSpec(
            num_scalar_prefetch=1,
            grid=(db, nc + 1),
            in_specs=[per_b((iheads * rq, idim)), per_b((iheads * rq, LANES)), per_b((ps, idim))]
            + [page_spec(p) for p in range(pages)],
            out_specs=pl.BlockSpec((None, rq, ch), lambda b, c, pt: (b, 0, c)),
        ),
        out_shape=jax.ShapeDtypeStruct((db, rq, (nc + 1) * ch), I32),
        compiler_params=_cparams("arbitrary", "arbitrary"),
        name="sample_scores",
    )(page_table, qi_hm, w_hm, kin, *([cache_kidx] * pages))


def _sample_attn_kernel(pt_ref, relb_ref, skey_ref, q_ref, kn_ref, vn_ref, *rest,
                        pages, ps, nc, heads, hd, topk, tq, rq, n_buckets):
    k_refs, v_refs = rest[:pages], rest[pages:2 * pages]
    o_ref, thr_ref, bias_ref, kbf_ref, vbf_ref, m_ref, l_ref, acc_ref = rest[2 * pages:]
    c = pl.program_id(1)
    ch = pages * ps
    past = nc * ch

    @pl.when(c == 0)
    def _():
        m_ref[...] = jnp.full(m_ref.shape, NEG_BIG, F32)
        l_ref[...] = jnp.zeros(l_ref.shape, F32)
        acc_ref[...] = jnp.zeros(acc_ref.shape, F32)

        def count_ge(scand):
            sk = skey_ref[...]
            return _fold_lanes(jnp.where(sk >= jnp.tile(scand, (1, sk.shape[1] // LANES)), 1.0, 0.0))

        thr_ref[...] = _kth_threshold(count_ge, rq, float(topk), 32)
        for h in range(heads):
            bias_ref[h] = jnp.full((rq, ch), relb_ref[n_buckets - 1, h], F32)

    @pl.when(c == nc - 1)
    def _():
        j = lax.broadcasted_iota(I32, (rq, ch), 0)
        s = lax.broadcasted_iota(I32, (rq, ch), 1)
        bucket = _bucket(ch + j - s, n_buckets)
        for h in range(heads):
            bias_ref[h] = _bias_of_bucket(bucket, relb_ref, h, n_buckets)

    @pl.when(c < nc)
    def _():
        for p in range(pages):
            for h in range(heads):
                kbf_ref[h, p * ps:(p + 1) * ps, :] = k_refs[p][pl.ds(h, ps, stride=heads), :].astype(BF16)
                vbf_ref[h, p * ps:(p + 1) * ps, :] = v_refs[p][pl.ds(h, ps, stride=heads), :].astype(BF16)
        mask = skey_ref[:, pl.ds(pl.multiple_of(c * ch, ch), ch)] >= jnp.tile(thr_ref[...], (1, ch // LANES))
        for h in range(heads):
            hs = slice(h * hd, (h + 1) * hd)
            lg = _dot_nt(q_ref[:, hs], kbf_ref[h]) + bias_ref[h]
            _flash_update(lg, mask, vbf_ref[h], m_ref, l_ref, acc_ref, h, hd)

    @pl.when(c == nc)
    def _():
        mask = skey_ref[:, past:past + ps] >= jnp.tile(thr_ref[...], (1, ps // LANES))
        j = lax.broadcasted_iota(I32, (rq, ps), 0)
        n = lax.broadcasted_iota(I32, (rq, ps), 1)
        bucket = _bucket(j - n, n_buckets)
        for h in range(heads):
            hs = slice(h * hd, (h + 1) * hd)
            lg = _dot_nt(q_ref[:, hs], kn_ref[:, hs]) + _bias_of_bucket(bucket, relb_ref, h, n_buckets)
            _flash_update(lg, mask, vn_ref[:, hs], m_ref, l_ref, acc_ref, h, hd)
        for h in range(heads):
            hs = slice(h * hd, (h + 1) * hd)
            o_ref[:, hs] = (acc_ref[:, hs] / jnp.tile(l_ref[h], (1, hd // LANES))).astype(BF16)


def _sample_attn(page_table, rel_bias, skey, q8, kn, vn, cache_k, cache_v, *, pages, heads, hd, topk, tq, rq):
    db, n_pages = page_table.shape
    ps = cache_k.shape[1] // heads
    aw = heads * hd
    nc = n_pages // pages
    ch = pages * ps
    n_buckets = rel_bias.shape[0]
    assert ch >= MAX_DISTANCE and ps >= tq and ps % LANES == 0 and hd == LANES
    page_spec = lambda p: pl.BlockSpec(
        (None, ps * heads, hd), lambda b, c, pt: (pt[b, jnp.minimum(c, nc - 1) * pages + p], 0, 0))
    per_b = lambda shape: pl.BlockSpec((None,) + shape, lambda b, c, pt: (b, 0, 0))
    return pl.pallas_call(
        functools.partial(_sample_attn_kernel, pages=pages, ps=ps, nc=nc, heads=heads, hd=hd, topk=topk,
                          tq=tq, rq=rq, n_buckets=n_buckets),
        grid_spec=pltpu.PrefetchScalarGridSpec(
            num_scalar_prefetch=1,
            grid=(db, nc + 1),
            in_specs=[pl.BlockSpec(memory_space=pltpu.SMEM), per_b((rq, skey.shape[2])), per_b((rq, aw)),
                      per_b((ps, aw)), per_b((ps, aw))]
            + [page_spec(p) for p in range(pages)] * 2,
            out_specs=per_b((rq, aw)),
            scratch_shapes=[pltpu.VMEM((rq, LANES), I32), pltpu.VMEM((heads, rq, ch), F32),
                            pltpu.VMEM((heads, ch, hd), BF16), pltpu.VMEM((heads, ch, hd), BF16),
                            pltpu.VMEM((heads, rq, LANES), F32), pltpu.VMEM((heads, rq, LANES), F32),
                            pltpu.VMEM((rq, aw), F32)],
        ),
        out_shape=jax.ShapeDtypeStruct((db, rq, aw), BF16),
        compiler_params=_cparams("arbitrary", "arbitrary"),
        name="sample_attn",
    )(page_table, rel_bias, skey, q8, kn, vn, *([cache_k] * pages), *([cache_v] * pages))


def _mix_kernel(x_ref, g_ref, a_ref, c_ref, wg_ref, wpa_ref, wpb_ref, wo_ref, o_ref, *, d):
    x = x_ref[...]
    hb = _rms_bf16(x, g_ref[...])
    a = _dot(a_ref[...], wpa_ref[...])
    m = jax.nn.sigmoid(_dot(hb, wg_ref[:, 0:d])) * a
    c = _dot(c_ref[...], wpb_ref[...])
    m = m + jax.nn.sigmoid(_dot(hb, wg_ref[:, d:2 * d])) * c
    o_ref[...] = x + _dot(m.astype(BF16), wo_ref[...])


def _mix(x2d, g, attn, c_in, w_gate, w_pa, w_pb, w_o, *, tm):
    m, d = x2d.shape
    row = lambda width: pl.BlockSpec((tm, width), lambda i: (i, 0))
    return pl.pallas_call(
        functools.partial(_mix_kernel, d=d),
        grid=(m // tm,),
        in_specs=[row(d), _resident((1, d)), row(attn.shape[1]), row(c_in.shape[1]), _resident(w_gate.shape),
                  _resident(w_pa.shape), _resident(w_pb.shape), _resident(w_o.shape)],
        out_specs=row(d),
        out_shape=jax.ShapeDtypeStruct((m, d), F32),
        compiler_params=_cparams("arbitrary"),
        name="mix_out",
    )(x2d, g, attn, c_in, w_gate, w_pa, w_pb, w_o)


def _mlp_kernel(x_ref, g_ref, gf_ref, w1_ref, w2_ref, y_ref, h_ref, acc_ref, *, final):
    f = pl.program_id(1)

    @pl.when(f == 0)
    def _():
        h_ref[...] = _rms_bf16(x_ref[...], g_ref[...])
        acc_ref[...] = jnp.zeros(acc_ref.shape, F32)

    t = jnp.square(jnp.maximum(_dot(h_ref[...], w1_ref[...]), 0.0))
    acc_ref[...] += _dot(t.astype(BF16), w2_ref[...])

    @pl.when(f == pl.num_programs(1) - 1)
    def _():
        x2 = x_ref[...] + acc_ref[...]
        if final:
            x2 = x2 * lax.rsqrt(jnp.mean(x2 * x2, axis=-1, keepdims=True) + EPS) * gf_ref[...]
        y_ref[...] = x2


def _mlp(x2d, g, gf, w1, w2, *, tm, tf, final):
    m, d = x2d.shape
    ff = w1.shape[1]
    return pl.pallas_call(
        functools.partial(_mlp_kernel, final=final),
        grid=(m // tm, ff // tf),
        in_specs=[pl.BlockSpec((tm, d), lambda i, f: (i, 0)), _resident((1, d)), _resident((1, d)),
                  pl.BlockSpec((d, tf), lambda i, f: (0, f)), pl.BlockSpec((tf, d), lambda i, f: (f, 0))],
        out_specs=pl.BlockSpec((tm, d), lambda i, f: (i, 0)),
        out_shape=jax.ShapeDtypeStruct((m, d), F32),
        scratch_shapes=[pltpu.VMEM((tm, d), BF16), pltpu.VMEM((tm, d), F32)],
        compiler_params=_cparams("arbitrary", "arbitrary"),
        name="mlp",
    )(x2d, g, gf, w1, w2)


def _tile(m, cap):
    return min(m, cap)


def kernel(x_prompt, x_sample, cache_k, cache_v, cache_kidx, state_conv, page_table, rel_bias, norm_mix_g, w_in,
           conv_w, w_pa, w_pb, w_o, norm_mlp_g, w_mlp_in, w_mlp_out, norm_final_g):
    batch, seq, d = x_prompt.shape
    db, tq, _ = x_sample.shape
    depth, n_pool, ps, heads, hd = cache_k.shape
    idim = cache_kidx.shape[-1]
    cw = conv_w.shape[-1]
    aw = heads * hd
    n_in = w_in.shape[-1]
    iheads = (n_in - 3 * aw - idim - 3 * cw - 2 * d) // (idim + 1)
    iw = iheads * idim
    assert 3 * aw + iw + idim + iheads + 3 * cw + 2 * d == n_in and idim + iheads <= LANES
    n_pages = page_table.shape[1]
    past = n_pages * ps
    rq = SUBLANES
    assert tq <= rq

    mp, ms = batch * seq, db * tq
    xp = x_prompt.reshape(mp, d)
    xs = x_sample.reshape(ms, d)
    tb = _tile(seq, 256)
    pages = math.gcd(n_pages, 8)
    o_small = 3 * aw + iw
    o_conv = o_small + idim + iheads
    o_gate = o_conv + 3 * cw
    gf = norm_final_g.reshape(1, d)
    bias_tiles = _bias_tiles(rel_bias, tb=tb)

    outs = {k: [] for k in ("kp", "vp", "kip", "sp", "ks", "vs", "kis", "ss")}
    for l in range(depth):
        wl = w_in[l]
        w_attn = wl[:, :o_small].astype(BF16)
        w_small = jnp.pad(wl[:, o_small:o_conv], ((0, 0), (0, LANES - idim - iheads))).astype(BF16)
        w_conv = wl[:, o_conv:o_gate].astype(BF16)
        w_gate = wl[:, o_gate:].astype(BF16)
        wpa, wpb, wo = w_pa[l].astype(BF16), w_pb[l].astype(BF16), w_o[l].astype(BF16)
        w1, w2 = w_mlp_in[l].astype(BF16), w_mlp_out[l].astype(BF16)
        g_mix = norm_mix_g[l].reshape(1, d)
        g_mlp = norm_mlp_g[l].reshape(1, d)

        tm = _tile(seq, 512)
        q, k, v, kb, vb, qi, sm, smb = _proj_attn(xp, g_mix, w_attn, w_small, aw=aw, iw=iw, hd=hd, tm=tm)
        c_in, u_tail = _proj_conv(xp, g_mix, w_conv, conv_w[l], None, seq=seq, tm=_tile(seq, 256))
        attn = _attn_prompt(rel_bias, bias_tiles, qi, sm, smb, q, kb, vb, batch=batch, seq=seq, tb=tb,
                            heads=heads, hd=hd, iheads=iheads, idim=idim, topk=min(TOPK_MAX, seq // 4))
        x1 = _mix(xp, g_mix, attn, c_in, w_gate, wpa, wpb, wo, tm=_tile(mp, 256))
        xp_next = _mlp(x1, g_mlp, gf, w1, w2, tm=_tile(mp, 512), tf=_tile(w1.shape[1], 1024), final=l == depth - 1)
        outs["kp"].append(k.reshape(batch, seq, heads, hd))
        outs["vp"].append(v.reshape(batch, seq, heads, hd))
        outs["kip"].append(sm[:, :idim].reshape(batch, seq, idim))
        outs["sp"].append(u_tail.reshape(batch, -1, SUBLANES, cw)[:, -1, SUBLANES - 2:])

        q, k, v, kb, vb, qi, sm, smb = _proj_attn(xs, g_mix, w_attn, w_small, aw=aw, iw=iw, hd=hd, tm=ms)
        st = state_conv[l]
        zero = jnp.zeros((db, tq - 1, cw), F32)
        prev1 = jnp.concatenate([st[:, 1:2], zero], axis=1).reshape(ms, cw)
        prev2 = jnp.concatenate([st, zero[:, 1:]], axis=1).reshape(ms, cw)
        c_in, u_all = _proj_conv(xs, g_mix, w_conv, conv_w[l], (prev1, prev2), seq=tq, tm=ms)

        def pad_rows(a, n):
            return jnp.pad(a, ((0, 0), (0, n - a.shape[1])) + ((0, 0),) * (a.ndim - 2))

        qi_hm = pad_rows(qi.reshape(db, tq, iheads, idim).transpose(0, 2, 1, 3).reshape(db * iheads, tq, idim), rq)
        qi_hm = qi_hm.reshape(db, iheads * rq, idim)
        w_hm = pad_rows(sm[:, idim:idim + iheads].reshape(db, tq, iheads).transpose(0, 2, 1).reshape(db * iheads, tq), rq)
        w_hm = jnp.broadcast_to(w_hm.reshape(db, iheads * rq, 1), (db, iheads * rq, LANES))
        kin = pad_rows(smb[:, :idim].reshape(db, tq, idim), ps)
        skey = _sample_scores(page_table, qi_hm, w_hm, kin, cache_kidx[l], pages=pages, tq=tq, rq=rq, iheads=iheads)
        attn8 = _sample_attn(page_table, rel_bias, skey, pad_rows(q.reshape(db, tq, aw), rq),
                             pad_rows(kb.reshape(db, tq, aw), ps), pad_rows(vb.reshape(db, tq, aw), ps),
                             cache_k[l].reshape(n_pool, ps * heads, hd), cache_v[l].reshape(n_pool, ps * heads, hd),
                             pages=pages, heads=heads, hd=hd, topk=min(TOPK_MAX, (past + tq) // 4), tq=tq, rq=rq)
        attn = attn8[:, :tq].reshape(ms, aw)
        x1 = _mix(xs, g_mix, attn, c_in, w_gate, wpa, wpb, wo, tm=ms)
        xs_next = _mlp(x1, g_mlp, gf, w1, w2, tm=ms, tf=_tile(w1.shape[1], 1024), final=l == depth - 1)
        outs["ks"].append(k.reshape(db, tq, heads, hd))
        outs["vs"].append(v.reshape(db, tq, heads, hd))
        outs["kis"].append(sm[:, :idim].reshape(db, tq, idim))
        outs["ss"].append(u_all.reshape(db, tq, cw)[:, tq - 2:])
        xp, xs = xp_next, xs_next

    st = {k: jnp.stack(v) for k, v in outs.items()}
    return (xp.reshape(batch, seq, d), xs.reshape(db, tq, d), st["kp"], st["vp"], st["kip"], st["sp"],
            st["ks"], st["vs"], st["kis"], st["ss"])
```

```python
import functools
import math

import jax
import jax.numpy as jnp
import numpy as np
from jax import lax
from jax.experimental import pallas as pl
from jax.experimental.pallas import tpu as pltpu

F32 = jnp.float32
BF16 = jnp.bfloat16
I32 = jnp.int32

TOPK_MAX = 256
MAX_DISTANCE = 128
EPS = 1e-6

LANES = 128
SUBLANES = 8
V7X_SCOPED_VMEM_BYTES = 60000 * 1024

INT_MIN = np.int32(-2 ** 31)
NEG_BIG = -1e30


def _cparams(*sem):
    return pltpu.CompilerParams(dimension_semantics=sem, vmem_limit_bytes=V7X_SCOPED_VMEM_BYTES)


def _resident(shape):
    nd = len(shape)
    return pl.BlockSpec(shape, lambda *_: (0,) * nd, pipeline_mode=pl.Buffered(1))


def _rms_bf16(x, g):
    y = x * lax.rsqrt(jnp.mean(x * x, axis=-1, keepdims=True) + EPS)
    return (y * g).astype(BF16)


def _dot(a, b):
    return jnp.dot(a, b, preferred_element_type=F32)


def _dot_nt(a, b):
    return lax.dot_general(a, b, (((1,), (1,)), ((), ())), preferred_element_type=F32)


def _ordered_key(x):
    b = lax.bitcast_convert_type(x, I32)
    return b ^ ((b >> 31) & np.int32(0x7FFFFFFF))


def _bucket(n, n_buckets):
    n = jnp.maximum(n, 0)
    me = n_buckets // 2
    nf = jnp.maximum(n, me).astype(F32)
    large = me + (jnp.log(nf / me) / math.log(MAX_DISTANCE / me) * (n_buckets - me)).astype(I32)
    large = jnp.minimum(large, n_buckets - 1)
    return jnp.where(n < me, n, large)


def _bias_of_bucket(bucket, relb_ref, h, n_buckets):
    acc = jnp.zeros(bucket.shape, F32)
    for bkt in range(n_buckets):
        acc = jnp.where(bucket == bkt, relb_ref[bkt, h], acc)
    return acc


def _fold_lanes(x):
    acc = x[:, 0:LANES]
    for t in range(1, x.shape[1] // LANES):
        acc = acc + x[:, t * LANES:(t + 1) * LANES]
    return acc


def _fold_rows(x, op=jnp.add):
    parts = [x[t * SUBLANES:(t + 1) * SUBLANES, :] for t in range(x.shape[0] // SUBLANES)]
    while len(parts) > 1:
        parts = [op(parts[t], parts[t + 1]) for t in range(0, len(parts) - 1, 2)] + parts[len(parts) & ~1:]
    return parts[0]


def _kth_threshold(count_ge, shape, axis, k, trips):
    def bit_body(t, uthr):
        cand = uthr | jnp.left_shift(np.int32(1), 31 - t)
        cnt = jnp.sum(count_ge(cand ^ INT_MIN), axis=axis, keepdims=True)
        return jnp.where(cnt >= k, cand, uthr)

    uthr = lax.fori_loop(0, trips, bit_body, jnp.zeros(shape, I32))
    return jnp.maximum(uthr ^ INT_MIN, INT_MIN + 1)


def _flash_update(lg, v, m_ref, l_ref, acc_ref, h, hd):
    reps = lg.shape[1] // LANES
    m_prev = m_ref[h]
    m_new = jnp.maximum(m_prev, jnp.max(lg, axis=1, keepdims=True))
    p = jnp.exp(lg - jnp.tile(m_new, (1, reps)))
    alpha = jnp.exp(m_prev - m_new)
    l_ref[h] = alpha * l_ref[h] + jnp.sum(p, axis=1, keepdims=True)
    m_ref[h] = m_new
    hs = slice(h * hd, (h + 1) * hd)
    acc_ref[:, hs] = acc_ref[:, hs] * jnp.tile(alpha, (1, hd // LANES)) + _dot(p.astype(BF16), v)


def _proj_attn_kernel(x_ref, g_ref, w_ref, ws_ref, q_ref, k_ref, v_ref, kb_ref, vb_ref, qi_ref,
                      sm_ref, smb_ref, *, aw, iw, hd, tm, qscale):
    heads = aw // hd
    hb = _rms_bf16(x_ref[...], g_ref[...])
    q_ref[...] = (_dot(hb, w_ref[:, 0:aw]) * qscale).astype(BF16)
    for o_ref, ob_ref, c0 in ((k_ref, kb_ref, aw), (v_ref, vb_ref, 2 * aw)):
        kv = _dot(hb, w_ref[:, c0:c0 + aw])
        ob_ref[...] = kv.astype(BF16)
        for h in range(heads):
            o_ref[pl.ds(h, tm, stride=heads), :] = kv[:, h * hd:(h + 1) * hd]
    qi_ref[...] = _dot(hb, w_ref[:, 3 * aw:3 * aw + iw]).astype(BF16)
    sm = _dot(hb, ws_ref[...])
    sm_ref[...] = sm
    smb_ref[...] = sm.astype(BF16)


def _proj_attn(x2d, g, w_attn, w_small, *, aw, iw, hd, tm):
    m, d = x2d.shape
    wt = w_attn.shape[1]
    heads = aw // hd
    assert hd == LANES
    row = lambda width: pl.BlockSpec((tm, width), lambda i: (i, 0))
    kv_spec = pl.BlockSpec((tm * heads, hd), lambda i: (i, 0))
    kv_shape = jax.ShapeDtypeStruct((m * heads, hd), F32)
    outs = [(aw, BF16), None, None, (aw, BF16), (aw, BF16), (iw, BF16), (LANES, F32), (LANES, BF16)]
    return pl.pallas_call(
        functools.partial(_proj_attn_kernel, aw=aw, iw=iw, hd=hd, tm=tm, qscale=hd ** -0.5),
        grid=(m // tm,),
        in_specs=[row(d), _resident((1, d)), _resident((d, wt)), _resident((d, LANES))],
        out_specs=[kv_spec if o is None else row(o[0]) for o in outs],
        out_shape=[kv_shape if o is None else jax.ShapeDtypeStruct((m, o[0]), o[1]) for o in outs],
        compiler_params=_cparams("arbitrary"),
        name="proj_attn",
    )(x2d, g, w_attn, w_small)


def _proj_conv_kernel(*refs, cw, tm, seq, tiles_per_seq, tail):
    if tiles_per_seq:
        x_ref, g_ref, w_ref, cwt_ref, c_ref, ut_ref, carry_ref = refs
    else:
        x_ref, g_ref, w_ref, cwt_ref, p1_ref, p2_ref, c_ref, ut_ref = refs
    hb = _rms_bf16(x_ref[...], g_ref[...])
    cx = _dot(hb, w_ref[:, 0:cw])
    cb = _dot(hb, w_ref[:, cw:2 * cw])
    cc = _dot(hb, w_ref[:, 2 * cw:3 * cw])
    u = cc * cx
    r = lax.broadcasted_iota(I32, (tm, cw), 0)
    um1 = pltpu.roll(u, 1, 0)
    um2 = pltpu.roll(u, 2, 0)
    if tiles_per_seq:
        @pl.when(pl.program_id(0) % tiles_per_seq == 0)
        def _():
            carry_ref[...] = jnp.zeros_like(carry_ref)
        prev1 = jnp.broadcast_to(carry_ref[SUBLANES - 1:SUBLANES, :], (tm, cw))
        prev2 = jnp.broadcast_to(carry_ref[SUBLANES - 2:SUBLANES - 1, :], (tm, cw))
        um1 = jnp.where(r == 0, prev1, um1)
        um2 = jnp.where(r == 0, prev2, jnp.where(r == 1, prev1, um2))
        carry_ref[...] = u[tm - SUBLANES:tm, :]
    else:
        assert seq & (seq - 1) == 0
        t = r & (seq - 1)
        um1 = jnp.where(t >= 1, um1, p1_ref[...])
        um2 = jnp.where(t >= 2, um2, p2_ref[...])
    y = cwt_ref[0:1, :] * um2 + cwt_ref[1:2, :] * um1 + cwt_ref[2:3, :] * u
    c_ref[...] = (cb * y).astype(BF16)
    ut_ref[...] = u[tm - tail:tm, :]


def _proj_conv(x2d, g, w_conv, conv_w, prev, *, seq, tm):
    m, d = x2d.shape
    cw = conv_w.shape[1]
    assert conv_w.shape[0] == 3
    row = lambda width: pl.BlockSpec((tm, width), lambda i: (i, 0))
    in_specs = [row(d), _resident((1, d)), _resident((d, 3 * cw)), _resident((3, cw))]
    args = [x2d, g, w_conv, conv_w]
    if prev is None:
        assert seq % tm == 0
        tiles_per_seq, tail = seq // tm, SUBLANES
        scratch = [pltpu.VMEM((SUBLANES, cw), F32)]
    else:
        assert tm % seq == 0 and seq >= 2
        tiles_per_seq, tail = 0, tm
        scratch = []
        in_specs += [row(cw), row(cw)]
        args += list(prev)
    return pl.pallas_call(
        functools.partial(_proj_conv_kernel, cw=cw, tm=tm, seq=seq, tiles_per_seq=tiles_per_seq, tail=tail),
        grid=(m // tm,),
        in_specs=in_specs,
        out_specs=[row(cw), pl.BlockSpec((tail, cw), lambda i: (i, 0))],
        out_shape=[jax.ShapeDtypeStruct((m, cw), BF16), jax.ShapeDtypeStruct((m // tm * tail, cw), F32)],
        scratch_shapes=scratch,
        compiler_params=_cparams("arbitrary"),
        name="proj_conv",
    )(*args)


def _bias_tiles_kernel(relb_ref, o_ref, *, tb, n_buckets):
    kind = pl.program_id(0)
    h = pl.program_id(1)
    qry = lax.broadcasted_iota(I32, (tb, tb), 0)
    key = lax.broadcasted_iota(I32, (tb, tb), 1)
    bias = _bias_of_bucket(_bucket(kind * tb + qry - key, n_buckets), relb_ref, h, n_buckets)
    o_ref[...] = bias - relb_ref[n_buckets - 1, h]


def _bias_tiles(rel_bias, *, tb):
    n_buckets, heads = rel_bias.shape
    return pl.pallas_call(
        functools.partial(_bias_tiles_kernel, tb=tb, n_buckets=n_buckets),
        grid=(2, heads),
        in_specs=[pl.BlockSpec(memory_space=pltpu.SMEM)],
        out_specs=pl.BlockSpec((None, None, tb, tb), lambda a, h: (a, h, 0, 0)),
        out_shape=jax.ShapeDtypeStruct((2, heads, tb, tb), F32),
        compiler_params=_cparams("arbitrary", "arbitrary"),
        name="bias_tiles",
    )(rel_bias)


def _attn_prompt_kernel(qi_ref, sm_ref, smb_ref, q_ref, kb_ref, vb_ref, bias_ref, o_ref,
                        skey_ref, thr_ref, wt_ref, madd_ref, m_ref, l_ref, acc_ref,
                        *, tb, heads, hd, iheads, idim, topk):
    i = pl.program_id(1)
    wscale = idim ** -0.5 * iheads ** -0.5

    m_ref[...] = jnp.full(m_ref.shape, NEG_BIG, F32)
    l_ref[...] = jnp.zeros(l_ref.shape, F32)
    acc_ref[...] = jnp.zeros(acc_ref.shape, F32)
    wt_ref[...] = sm_ref[...].T * wscale

    def chunk(j):
        return pl.ds(pl.multiple_of(j * tb, tb), tb)

    def score_chunk(j, diag):
        kic = smb_ref[chunk(j), 0:idim]
        acc = jnp.zeros((tb, tb), F32)
        for h in range(iheads):
            s = _dot_nt(kic, qi_ref[:, h * idim:(h + 1) * idim])
            acc = acc + jnp.maximum(s, 0.0) * wt_ref[idim + h:idim + h + 1, :]
        key = _ordered_key(acc)
        if diag:
            kpos = lax.broadcasted_iota(I32, (tb, tb), 0)
            qpos = lax.broadcasted_iota(I32, (tb, tb), 1)
            key = jnp.where(kpos > qpos, INT_MIN, key)
        skey_ref[chunk(j), :] = key

    def score_body(j, carry):
        score_chunk(j, False)
        return carry

    lax.fori_loop(0, i, score_body, 0)
    score_chunk(i, True)

    def count_ge(scand):
        def body(j, cnt):
            return cnt + _fold_rows(jnp.where(skey_ref[chunk(j), :] >= scand[0:1, :], 1.0, 0.0))

        return lax.fori_loop(0, i + 1, body, jnp.zeros((SUBLANES, tb), F32))

    trips = jnp.where((i + 1) * tb <= topk, 0, 32)
    thr_ref[...] = _kth_threshold(count_ge, (SUBLANES, tb), 0, float(topk), trips)

    def attend_chunk(j, kind):
        madd_ref[...] = jnp.where(skey_ref[chunk(j), :] >= thr_ref[0:1, :], 0.0, NEG_BIG).T
        for h in range(heads):
            hs = slice(h * hd, (h + 1) * hd)
            lg = _dot_nt(q_ref[:, hs], kb_ref[chunk(j), hs]) + madd_ref[...]
            if kind is not None:
                lg = lg + bias_ref[kind, h]
            _flash_update(lg, vb_ref[chunk(j), hs], m_ref, l_ref, acc_ref, h, hd)

    def attend_body(j, carry):
        attend_chunk(j, None)
        return carry

    lax.fori_loop(0, jnp.maximum(i - 1, 0), attend_body, 0)

    @pl.when(i >= 1)
    def _():
        attend_chunk(i - 1, 1)

    attend_chunk(i, 0)

    for h in range(heads):
        hs = slice(h * hd, (h + 1) * hd)
        o_ref[:, hs] = (acc_ref[:, hs] / jnp.tile(l_ref[h], (1, hd // LANES))).astype(BF16)


def _attn_prompt(bias_tiles, qi, sm, smb, q, kb, vb, *, batch, seq, tb, heads, hd, iheads, idim, topk):
    m, aw = q.shape
    iw = qi.shape[1]
    nq = seq // tb
    assert tb >= MAX_DISTANCE and tb % LANES == 0 and hd % LANES == 0
    qrow = lambda width: pl.BlockSpec((tb, width), lambda b, i: (b * nq + i, 0))
    seqblk = lambda width: pl.BlockSpec((seq, width), lambda b, i: (b, 0))
    return pl.pallas_call(
        functools.partial(_attn_prompt_kernel, tb=tb, heads=heads, hd=hd, iheads=iheads, idim=idim, topk=topk),
        grid=(batch, nq),
        in_specs=[qrow(iw), qrow(LANES), seqblk(LANES), qrow(aw), seqblk(aw), seqblk(aw),
                  _resident(bias_tiles.shape)],
        out_specs=qrow(aw),
        out_shape=jax.ShapeDtypeStruct((m, aw), BF16),
        scratch_shapes=[pltpu.VMEM((seq, tb), I32), pltpu.VMEM((SUBLANES, tb), I32),
                        pltpu.VMEM((LANES, tb), F32), pltpu.VMEM((tb, tb), F32),
                        pltpu.VMEM((heads, tb, LANES), F32), pltpu.VMEM((heads, tb, LANES), F32),
                        pltpu.VMEM((tb, aw), F32)],
        compiler_params=_cparams("arbitrary", "arbitrary"),
        name="attn_prompt",
    )(qi, sm, smb, q, kb, vb, bias_tiles)


def _sample_score_kernel(pt_ref, qi_ref, w_ref, kin_ref, *rest, pages, ps, nc, iheads, idim, tq, rq):
    page_refs, o_ref = rest[:pages], rest[pages]
    c = pl.program_id(1)
    wscale = idim ** -0.5 * iheads ** -0.5

    def score(keys_t):
        s = _dot(qi_ref[...], keys_t)
        t = jnp.maximum(s, 0.0) * (w_ref[:, 0:1] * wscale)
        acc = t[0:rq]
        for h in range(1, iheads):
            acc = acc + t[h * rq:(h + 1) * rq]
        return _ordered_key(acc)

    @pl.when(c < nc)
    def _():
        for p in range(pages):
            o_ref[:, p * ps:(p + 1) * ps] = score(page_refs[p][...].astype(BF16))

    @pl.when(c == nc)
    def _():
        key = score(kin_ref[...])
        j = lax.broadcasted_iota(I32, (rq, ps), 0)
        n = lax.broadcasted_iota(I32, (rq, ps), 1)
        o_ref[:, 0:ps] = jnp.where((n <= j) & (n < tq), key, INT_MIN)
        o_ref[:, ps:] = jnp.full((rq, (pages - 1) * ps), INT_MIN, I32)


def _sample_scores(page_table, qi_hm, w_hm, kin_t, cache_kidx_t, *, pages, tq, rq, iheads):
    db, n_pages = page_table.shape
    _, idim, ps = cache_kidx_t.shape
    nc = n_pages // pages
    ch = pages * ps
    page_spec = lambda p: pl.BlockSpec(
        (None, idim, ps), lambda b, c, pt: (pt[b, jnp.minimum(c, nc - 1) * pages + p], 0, 0))
    per_b = lambda shape: pl.BlockSpec((None,) + shape, lambda b, c, pt: (b, 0, 0))
    return pl.pallas_call(
        functools.partial(_sample_score_kernel, pages=pages, ps=ps, nc=nc, iheads=iheads, idim=idim, tq=tq, rq=rq),
        grid_spec=pltpu.PrefetchScalarGridSpec(
            num_scalar_prefetch=1,
            grid=(db, nc + 1),
            in_specs=[per_b((iheads * rq, idim)), per_b((iheads * rq, LANES)), per_b((idim, ps))]
            + [page_spec(p) for p in range(pages)],
            out_specs=pl.BlockSpec((None, rq, ch), lambda b, c, pt: (b, 0, c)),
        ),
        out_shape=jax.ShapeDtypeStruct((db, rq, (nc + 1) * ch), I32),
        compiler_params=_cparams("arbitrary", "arbitrary"),
        name="sample_scores",
    )(page_table, qi_hm, w_hm, kin_t, *([cache_kidx_t] * pages))


def _sample_attn_kernel(pt_ref, relb_ref, skey_ref, q_ref, kn_ref, vn_ref, *rest,
                        pages, ps, nc, heads, hd, topk, tq, rq, n_buckets):
    k_refs, v_refs = rest[:pages], rest[pages:2 * pages]
    o_ref, thr_ref, bias_ref, kbf_ref, vbf_ref, m_ref, l_ref, acc_ref = rest[2 * pages:]
    c = pl.program_id(1)
    ch = pages * ps
    past = nc * ch

    @pl.when(c == 0)
    def _():
        m_ref[...] = jnp.full(m_ref.shape, NEG_BIG, F32)
        l_ref[...] = jnp.zeros(l_ref.shape, F32)
        acc_ref[...] = jnp.zeros(acc_ref.shape, F32)

        def count_ge(scand):
            sk = skey_ref[...]
            return _fold_lanes(jnp.where(sk >= jnp.tile(scand, (1, sk.shape[1] // LANES)), 1.0, 0.0))

        thr_ref[...] = _kth_threshold(count_ge, (rq, LANES), 1, float(topk), 32)
        for h in range(heads):
            bias_ref[h] = jnp.full((rq, ch), relb_ref[n_buckets - 1, h], F32)

    @pl.when(c == nc - 1)
    def _():
        j = lax.broadcasted_iota(I32, (rq, ch), 0)
        s = lax.broadcasted_iota(I32, (rq, ch), 1)
        bucket = _bucket(ch + j - s, n_buckets)
        for h in range(heads):
            bias_ref[h] = _bias_of_bucket(bucket, relb_ref, h, n_buckets)

    @pl.when(c < nc)
    def _():
        for p in range(pages):
            for h in range(heads):
                kbf_ref[h, p * ps:(p + 1) * ps, :] = k_refs[p][pl.ds(h, ps, stride=heads), :].astype(BF16)
                vbf_ref[h, p * ps:(p + 1) * ps, :] = v_refs[p][pl.ds(h, ps, stride=heads), :].astype(BF16)
        mask = skey_ref[:, pl.ds(pl.multiple_of(c * ch, ch), ch)] >= jnp.tile(thr_ref[...], (1, ch // LANES))
        madd = jnp.where(mask, 0.0, NEG_BIG)
        for h in range(heads):
            hs = slice(h * hd, (h + 1) * hd)
            lg = _dot_nt(q_ref[:, hs], kbf_ref[h]) + bias_ref[h] + madd
            _flash_update(lg, vbf_ref[h], m_ref, l_ref, acc_ref, h, hd)

    @pl.when(c == nc)
    def _():
        mask = skey_ref[:, past:past + ps] >= jnp.tile(thr_ref[...], (1, ps // LANES))
        madd = jnp.where(mask, 0.0, NEG_BIG)
        j = lax.broadcasted_iota(I32, (rq, ps), 0)
        n = lax.broadcasted_iota(I32, (rq, ps), 1)
        bucket = _bucket(j - n, n_buckets)
        for h in range(heads):
            hs = slice(h * hd, (h + 1) * hd)
            lg = _dot_nt(q_ref[:, hs], kn_ref[:, hs]) + _bias_of_bucket(bucket, relb_ref, h, n_buckets) + madd
            _flash_update(lg, vn_ref[:, hs], m_ref, l_ref, acc_ref, h, hd)
        for h in range(heads):
            hs = slice(h * hd, (h + 1) * hd)
            o_ref[:, hs] = (acc_ref[:, hs] / jnp.tile(l_ref[h], (1, hd // LANES))).astype(BF16)


def _sample_attn(page_table, rel_bias, skey, q8, kn, vn, cache_k, cache_v, *, pages, heads, hd, topk, tq, rq):
    db, n_pages = page_table.shape
    ps = cache_k.shape[1] // heads
    aw = heads * hd
    nc = n_pages // pages
    ch = pages * ps
    n_buckets = rel_bias.shape[0]
    assert ch >= MAX_DISTANCE and ps >= tq and ps % LANES == 0 and hd == LANES
    page_spec = lambda p: pl.BlockSpec(
        (None, ps * heads, hd), lambda b, c, pt: (pt[b, jnp.minimum(c, nc - 1) * pages + p], 0, 0))
    per_b = lambda shape: pl.BlockSpec((None,) + shape, lambda b, c, pt: (b, 0, 0))
    return pl.pallas_call(
        functools.partial(_sample_attn_kernel, pages=pages, ps=ps, nc=nc, heads=heads, hd=hd, topk=topk,
                          tq=tq, rq=rq, n_buckets=n_buckets),
        grid_spec=pltpu.PrefetchScalarGridSpec(
            num_scalar_prefetch=1,
            grid=(db, nc + 1),
            in_specs=[pl.BlockSpec(memory_space=pltpu.SMEM), per_b((rq, skey.shape[2])), per_b((rq, aw)),
                      per_b((ps, aw)), per_b((ps, aw))]
            + [page_spec(p) for p in range(pages)] * 2,
            out_specs=per_b((rq, aw)),
            scratch_shapes=[pltpu.VMEM((rq, LANES), I32), pltpu.VMEM((heads, rq, ch), F32),
                            pltpu.VMEM((heads, ch, hd), BF16), pltpu.VMEM((heads, ch, hd), BF16),
                            pltpu.VMEM((heads, rq, LANES), F32), pltpu.VMEM((heads, rq, LANES), F32),
                            pltpu.VMEM((rq, aw), F32)],
        ),
        out_shape=jax.ShapeDtypeStruct((db, rq, aw), BF16),
        compiler_params=_cparams("arbitrary", "arbitrary"),
        name="sample_attn",
    )(page_table, rel_bias, skey, q8, kn, vn, *([cache_k] * pages), *([cache_v] * pages))


def _mix_kernel(x_ref, g_ref, a_ref, c_ref, wg_ref, wpa_ref, wpb_ref, wo_ref, o_ref, *, d):
    x = x_ref[...]
    hb = _rms_bf16(x, g_ref[...])
    a = _dot(a_ref[...], wpa_ref[...])
    m = jax.nn.sigmoid(_dot(hb, wg_ref[:, 0:d])) * a
    c = _dot(c_ref[...], wpb_ref[...])
    m = m + jax.nn.sigmoid(_dot(hb, wg_ref[:, d:2 * d])) * c
    o_ref[...] = x + _dot(m.astype(BF16), wo_ref[...])


def _mix(x2d, g, attn, c_in, w_gate, w_pa, w_pb, w_o, *, tm):
    m, d = x2d.shape
    row = lambda width: pl.BlockSpec((tm, width), lambda i: (i, 0))
    return pl.pallas_call(
        functools.partial(_mix_kernel, d=d),
        grid=(m // tm,),
        in_specs=[row(d), _resident((1, d)), row(attn.shape[1]), row(c_in.shape[1]), _resident(w_gate.shape),
                  _resident(w_pa.shape), _resident(w_pb.shape), _resident(w_o.shape)],
        out_specs=row(d),
        out_shape=jax.ShapeDtypeStruct((m, d), F32),
        compiler_params=_cparams("arbitrary"),
        name="mix_out",
    )(x2d, g, attn, c_in, w_gate, w_pa, w_pb, w_o)


def _mlp_kernel(x_ref, g_ref, gf_ref, w1_ref, w2_ref, y_ref, h_ref, acc_ref, *, final):
    f = pl.program_id(1)

    @pl.when(f == 0)
    def _():
        h_ref[...] = _rms_bf16(x_ref[...], g_ref[...])
        acc_ref[...] = jnp.zeros(acc_ref.shape, F32)

    t = jnp.square(jnp.maximum(_dot(h_ref[...], w1_ref[...]), 0.0))
    acc_ref[...] += _dot(t.astype(BF16), w2_ref[...])

    @pl.when(f == pl.num_programs(1) - 1)
    def _():
        x2 = x_ref[...] + acc_ref[...]
        if final:
            x2 = x2 * lax.rsqrt(jnp.mean(x2 * x2, axis=-1, keepdims=True) + EPS) * gf_ref[...]
        y_ref[...] = x2


def _mlp(x2d, g, gf, w1, w2, *, tm, tf, final):
    m, d = x2d.shape
    ff = w1.shape[1]
    return pl.pallas_call(
        functools.partial(_mlp_kernel, final=final),
        grid=(m // tm, ff // tf),
        in_specs=[pl.BlockSpec((tm, d), lambda i, f: (i, 0)), _resident((1, d)), _resident((1, d)),
                  pl.BlockSpec((d, tf), lambda i, f: (0, f)), pl.BlockSpec((tf, d), lambda i, f: (f, 0))],
        out_specs=pl.BlockSpec((tm, d), lambda i, f: (i, 0)),
        out_shape=jax.ShapeDtypeStruct((m, d), F32),
        scratch_shapes=[pltpu.VMEM((tm, d), BF16), pltpu.VMEM((tm, d), F32)],
        compiler_params=_cparams("arbitrary", "arbitrary"),
        name="mlp",
    )(x2d, g, gf, w1, w2)


def _tile(m, cap):
    return min(m, cap)


def kernel(x_prompt, x_sample, cache_k, cache_v, cache_kidx, state_conv, page_table, rel_bias, norm_mix_g, w_in,
           conv_w, w_pa, w_pb, w_o, norm_mlp_g, w_mlp_in, w_mlp_out, norm_final_g):
    batch, seq, d = x_prompt.shape
    db, tq, _ = x_sample.shape
    depth, n_pool, ps, heads, hd = cache_k.shape
    idim = cache_kidx.shape[-1]
    cw = conv_w.shape[-1]
    aw = heads * hd
    n_in = w_in.shape[-1]
    iheads = (n_in - 3 * aw - idim - 3 * cw - 2 * d) // (idim + 1)
    iw = iheads * idim
    assert 3 * aw + iw + idim + iheads + 3 * cw + 2 * d == n_in and idim + iheads <= LANES
    n_pages = page_table.shape[1]
    past = n_pages * ps
    rq = SUBLANES
    assert tq <= rq

    mp, ms = batch * seq, db * tq
    xp = x_prompt.reshape(mp, d)
    xs = x_sample.reshape(ms, d)
    tb = _tile(seq, 256)
    pages = math.gcd(n_pages, 8)
    o_small = 3 * aw + iw
    o_conv = o_small + idim + iheads
    o_gate = o_conv + 3 * cw
    gf = norm_final_g.reshape(1, d)
    bias_tiles = _bias_tiles(rel_bias, tb=tb)

    outs = {k: [] for k in ("kp", "vp", "kip", "sp", "ks", "vs", "kis", "ss")}
    for l in range(depth):
        wl = w_in[l]
        w_attn = wl[:, :o_small].astype(BF16)
        w_small = jnp.pad(wl[:, o_small:o_conv], ((0, 0), (0, LANES - idim - iheads))).astype(BF16)
        w_conv = wl[:, o_conv:o_gate].astype(BF16)
        w_gate = wl[:, o_gate:].astype(BF16)
        wpa, wpb, wo = w_pa[l].astype(BF16), w_pb[l].astype(BF16), w_o[l].astype(BF16)
        w1, w2 = w_mlp_in[l].astype(BF16), w_mlp_out[l].astype(BF16)
        g_mix = norm_mix_g[l].reshape(1, d)
        g_mlp = norm_mlp_g[l].reshape(1, d)

        tm = _tile(seq, 512)
        q, k, v, kb, vb, qi, sm, smb = _proj_attn(xp, g_mix, w_attn, w_small, aw=aw, iw=iw, hd=hd, tm=tm)
        c_in, u_tail = _proj_conv(xp, g_mix, w_conv, conv_w[l], None, seq=seq, tm=_tile(seq, 256))
        attn = _attn_prompt(bias_tiles, qi, sm, smb, q, kb, vb, batch=batch, seq=seq, tb=tb,
                            heads=heads, hd=hd, iheads=iheads, idim=idim, topk=min(TOPK_MAX, seq // 4))
        x1 = _mix(xp, g_mix, attn, c_in, w_gate, wpa, wpb, wo, tm=_tile(mp, 256))
        xp_next = _mlp(x1, g_mlp, gf, w1, w2, tm=_tile(mp, 512), tf=_tile(w1.shape[1], 1024), final=l == depth - 1)
        outs["kp"].append(k.reshape(batch, seq, heads, hd))
        outs["vp"].append(v.reshape(batch, seq, heads, hd))
        outs["kip"].append(sm[:, :idim].reshape(batch, seq, idim))
        outs["sp"].append(u_tail.reshape(batch, -1, SUBLANES, cw)[:, -1, SUBLANES - 2:])

        q, k, v, kb, vb, qi, sm, smb = _proj_attn(xs, g_mix, w_attn, w_small, aw=aw, iw=iw, hd=hd, tm=ms)
        st = state_conv[l]
        zero = jnp.zeros((db, tq - 1, cw), F32)
        prev1 = jnp.concatenate([st[:, 1:2], zero], axis=1).reshape(ms, cw)
        prev2 = jnp.concatenate([st, zero[:, 1:]], axis=1).reshape(ms, cw)
        c_in, u_all = _proj_conv(xs, g_mix, w_conv, conv_w[l], (prev1, prev2), seq=tq, tm=ms)

        def pad_rows(a, n):
            return jnp.pad(a, ((0, 0), (0, n - a.shape[1])) + ((0, 0),) * (a.ndim - 2))

        qi_hm = pad_rows(qi.reshape(db, tq, iheads, idim).transpose(0, 2, 1, 3).reshape(db * iheads, tq, idim), rq)
        qi_hm = qi_hm.reshape(db, iheads * rq, idim)
        w_hm = pad_rows(sm[:, idim:idim + iheads].reshape(db, tq, iheads).transpose(0, 2, 1).reshape(db * iheads, tq), rq)
        w_hm = jnp.broadcast_to(w_hm.reshape(db, iheads * rq, 1), (db, iheads * rq, LANES))
        kin_t = jnp.swapaxes(pad_rows(smb[:, :idim].reshape(db, tq, idim), ps), 1, 2)
        skey = _sample_scores(page_table, qi_hm, w_hm, kin_t, jnp.swapaxes(cache_kidx[l], 1, 2),
                              pages=math.gcd(n_pages, 16), tq=tq, rq=rq, iheads=iheads)
        attn8 = _sample_attn(page_table, rel_bias, skey, pad_rows(q.reshape(db, tq, aw), rq),
                             pad_rows(kb.reshape(db, tq, aw), ps), pad_rows(vb.reshape(db, tq, aw), ps),
                             cache_k[l].reshape(n_pool, ps * heads, hd), cache_v[l].reshape(n_pool, ps * heads, hd),
                             pages=pages, heads=heads, hd=hd, topk=min(TOPK_MAX, (past + tq) // 4), tq=tq, rq=rq)
        attn = attn8[:, :tq].reshape(ms, aw)
        x1 = _mix(xs, g_mix, attn, c_in, w_gate, wpa, wpb, wo, tm=ms)
        xs_next = _mlp(x1, g_mlp, gf, w1, w2, tm=ms, tf=_tile(w1.shape[1], 1024), final=l == depth - 1)
        outs["ks"].append(k.reshape(db, tq, heads, hd))
        outs["vs"].append(v.reshape(db, tq, heads, hd))
        outs["kis"].append(sm[:, :idim].reshape(db, tq, idim))
        outs["ss"].append(u_all.reshape(db, tq, cw)[:, tq - 2:])
        xp, xs = xp_next, xs_next

    st = {k: jnp.stack(v) for k, v in outs.items()}
    return (xp.reshape(batch, seq, d), xs.reshape(db, tq, d), st["kp"], st["vp"], st["kip"], st["sp"],
            st["ks"], st["vs"], st["kis"], st["ss"])
```

```python
import functools
import math

import jax
import jax.numpy as jnp
import numpy as np
from jax import lax
from jax.experimental import pallas as pl
from jax.experimental.pallas import tpu as pltpu
from jax.experimental.pallas import tpu_sc as plsc

F32 = jnp.float32
BF16 = jnp.bfloat16
I32 = jnp.int32

TOPK_MAX = 256
MAX_DISTANCE = 128
EPS = 1e-6

LANES = 128
SUBLANES = 8
V7X_SCOPED_VMEM_BYTES = 60000 * 1024
SC_LANES = 16
V7X_SC_CORES = 2
V7X_SC_SUBCORES = 16

INT_MIN = np.int32(-2 ** 31)
NEG_BIG = -1e30


def _cparams(*sem):
    return pltpu.CompilerParams(dimension_semantics=sem, vmem_limit_bytes=V7X_SCOPED_VMEM_BYTES)


def _resident(shape):
    nd = len(shape)
    return pl.BlockSpec(shape, lambda *_: (0,) * nd, pipeline_mode=pl.Buffered(1))


def _rms_bf16(x, g):
    y = x * lax.rsqrt(jnp.mean(x * x, axis=-1, keepdims=True) + EPS)
    return (y * g).astype(BF16)


def _dot(a, b):
    return jnp.dot(a, b, preferred_element_type=F32)


def _dot_nt(a, b):
    return lax.dot_general(a, b, (((1,), (1,)), ((), ())), preferred_element_type=F32)


def _ordered_key(x):
    b = lax.bitcast_convert_type(x, I32)
    return b ^ ((b >> 31) & np.int32(0x7FFFFFFF))


def _bucket(n, n_buckets):
    n = jnp.maximum(n, 0)
    me = n_buckets // 2
    nf = jnp.maximum(n, me).astype(F32)
    large = me + (jnp.log(nf / me) / math.log(MAX_DISTANCE / me) * (n_buckets - me)).astype(I32)
    large = jnp.minimum(large, n_buckets - 1)
    return jnp.where(n < me, n, large)


def _bias_of_bucket(bucket, relb_ref, h, n_buckets):
    acc = jnp.zeros(bucket.shape, F32)
    for bkt in range(n_buckets):
        acc = jnp.where(bucket == bkt, relb_ref[bkt, h], acc)
    return acc


def _fold_lanes(x):
    acc = x[:, 0:LANES]
    for t in range(1, x.shape[1] // LANES):
        acc = acc + x[:, t * LANES:(t + 1) * LANES]
    return acc


def _fold_rows(x, op=jnp.add):
    parts = [x[t * SUBLANES:(t + 1) * SUBLANES, :] for t in range(x.shape[0] // SUBLANES)]
    while len(parts) > 1:
        parts = [op(parts[t], parts[t + 1]) for t in range(0, len(parts) - 1, 2)] + parts[len(parts) & ~1:]
    return parts[0]


def _kth_threshold(count_ge, shape, axis, k, trips):
    def bit_body(t, uthr):
        cand = uthr | jnp.left_shift(np.int32(1), 31 - t)
        cnt = jnp.sum(count_ge(cand ^ INT_MIN), axis=axis, keepdims=True)
        return jnp.where(cnt >= k, cand, uthr)

    uthr = lax.fori_loop(0, trips, bit_body, jnp.zeros(shape, I32))
    return jnp.maximum(uthr ^ INT_MIN, INT_MIN + 1)


def _flash_update(lg, v, m_ref, l_ref, acc_ref, h, hd):
    reps = lg.shape[1] // LANES
    m_prev = m_ref[h]
    m_new = jnp.maximum(m_prev, jnp.max(lg, axis=1, keepdims=True))
    p = jnp.exp(lg - jnp.tile(m_new, (1, reps)))
    alpha = jnp.exp(m_prev - m_new)
    l_ref[h] = alpha * l_ref[h] + jnp.sum(p, axis=1, keepdims=True)
    m_ref[h] = m_new
    hs = slice(h * hd, (h + 1) * hd)
    acc_ref[:, hs] = acc_ref[:, hs] * jnp.tile(alpha, (1, hd // LANES)) + _dot(p.astype(BF16), v)


def _proj_attn_kernel(x_ref, g_ref, w_ref, ws_ref, q_ref, k_ref, v_ref, kb_ref, vb_ref, qi_ref,
                      sm_ref, smb_ref, *, aw, iw, hd, tm, qscale):
    heads = aw // hd
    hb = _rms_bf16(x_ref[...], g_ref[...])
    q_ref[...] = (_dot(hb, w_ref[:, 0:aw]) * qscale).astype(BF16)
    for o_ref, ob_ref, c0 in ((k_ref, kb_ref, aw), (v_ref, vb_ref, 2 * aw)):
        kv = _dot(hb, w_ref[:, c0:c0 + aw])
        ob_ref[...] = kv.astype(BF16)
        for h in range(heads):
            o_ref[pl.ds(h, tm, stride=heads), :] = kv[:, h * hd:(h + 1) * hd]
    qi_ref[...] = _dot(hb, w_ref[:, 3 * aw:3 * aw + iw]).astype(BF16)
    sm = _dot(hb, ws_ref[...])
    sm_ref[...] = sm
    smb_ref[...] = sm.astype(BF16)


def _proj_attn(x2d, g, w_attn, w_small, *, aw, iw, hd, tm):
    m, d = x2d.shape
    wt = w_attn.shape[1]
    heads = aw // hd
    assert hd == LANES
    row = lambda width: pl.BlockSpec((tm, width), lambda i: (i, 0))
    kv_spec = pl.BlockSpec((tm * heads, hd), lambda i: (i, 0))
    kv_shape = jax.ShapeDtypeStruct((m * heads, hd), F32)
    outs = [(aw, BF16), None, None, (aw, BF16), (aw, BF16), (iw, BF16), (LANES, F32), (LANES, BF16)]
    return pl.pallas_call(
        functools.partial(_proj_attn_kernel, aw=aw, iw=iw, hd=hd, tm=tm, qscale=hd ** -0.5),
        grid=(m // tm,),
        in_specs=[row(d), _resident((1, d)), _resident((d, wt)), _resident((d, LANES))],
        out_specs=[kv_spec if o is None else row(o[0]) for o in outs],
        out_shape=[kv_shape if o is None else jax.ShapeDtypeStruct((m, o[0]), o[1]) for o in outs],
        compiler_params=_cparams("arbitrary"),
        name="proj_attn",
    )(x2d, g, w_attn, w_small)


def _proj_conv_kernel(*refs, cw, tm, seq, tiles_per_seq, tail):
    if tiles_per_seq:
        x_ref, g_ref, w_ref, cwt_ref, c_ref, ut_ref, carry_ref = refs
    else:
        x_ref, g_ref, w_ref, cwt_ref, p1_ref, p2_ref, c_ref, ut_ref = refs
    hb = _rms_bf16(x_ref[...], g_ref[...])
    cx = _dot(hb, w_ref[:, 0:cw])
    cb = _dot(hb, w_ref[:, cw:2 * cw])
    cc = _dot(hb, w_ref[:, 2 * cw:3 * cw])
    u = cc * cx
    r = lax.broadcasted_iota(I32, (tm, cw), 0)
    um1 = pltpu.roll(u, 1, 0)
    um2 = pltpu.roll(u, 2, 0)
    if tiles_per_seq:
        @pl.when(pl.program_id(0) % tiles_per_seq == 0)
        def _():
            carry_ref[...] = jnp.zeros_like(carry_ref)
        prev1 = jnp.broadcast_to(carry_ref[SUBLANES - 1:SUBLANES, :], (tm, cw))
        prev2 = jnp.broadcast_to(carry_ref[SUBLANES - 2:SUBLANES - 1, :], (tm, cw))
        um1 = jnp.where(r == 0, prev1, um1)
        um2 = jnp.where(r == 0, prev2, jnp.where(r == 1, prev1, um2))
        carry_ref[...] = u[tm - SUBLANES:tm, :]
    else:
        assert seq & (seq - 1) == 0
        t = r & (seq - 1)
        um1 = jnp.where(t >= 1, um1, p1_ref[...])
        um2 = jnp.where(t >= 2, um2, p2_ref[...])
    y = cwt_ref[0:1, :] * um2 + cwt_ref[1:2, :] * um1 + cwt_ref[2:3, :] * u
    c_ref[...] = (cb * y).astype(BF16)
    ut_ref[...] = u[tm - tail:tm, :]


def _proj_conv(x2d, g, w_conv, conv_w, prev, *, seq, tm):
    m, d = x2d.shape
    cw = conv_w.shape[1]
    assert conv_w.shape[0] == 3
    row = lambda width: pl.BlockSpec((tm, width), lambda i: (i, 0))
    in_specs = [row(d), _resident((1, d)), _resident((d, 3 * cw)), _resident((3, cw))]
    args = [x2d, g, w_conv, conv_w]
    if prev is None:
        assert seq % tm == 0
        tiles_per_seq, tail = seq // tm, SUBLANES
        scratch = [pltpu.VMEM((SUBLANES, cw), F32)]
    else:
        assert tm % seq == 0 and seq >= 2
        tiles_per_seq, tail = 0, tm
        scratch = []
        in_specs += [row(cw), row(cw)]
        args += list(prev)
    return pl.pallas_call(
        functools.partial(_proj_conv_kernel, cw=cw, tm=tm, seq=seq, tiles_per_seq=tiles_per_seq, tail=tail),
        grid=(m // tm,),
        in_specs=in_specs,
        out_specs=[row(cw), pl.BlockSpec((tail, cw), lambda i: (i, 0))],
        out_shape=[jax.ShapeDtypeStruct((m, cw), BF16), jax.ShapeDtypeStruct((m // tm * tail, cw), F32)],
        scratch_shapes=scratch,
        compiler_params=_cparams("arbitrary"),
        name="proj_conv",
    )(*args)


def _bias_tiles_kernel(relb_ref, o_ref, *, tb, n_buckets):
    kind = pl.program_id(0)
    h = pl.program_id(1)
    qry = lax.broadcasted_iota(I32, (tb, tb), 0)
    key = lax.broadcasted_iota(I32, (tb, tb), 1)
    bias = _bias_of_bucket(_bucket(kind * tb + qry - key, n_buckets), relb_ref, h, n_buckets)
    o_ref[...] = bias - relb_ref[n_buckets - 1, h]


def _bias_tiles(rel_bias, *, tb):
    n_buckets, heads = rel_bias.shape
    return pl.pallas_call(
        functools.partial(_bias_tiles_kernel, tb=tb, n_buckets=n_buckets),
        grid=(2, heads),
        in_specs=[pl.BlockSpec(memory_space=pltpu.SMEM)],
        out_specs=pl.BlockSpec((None, None, tb, tb), lambda a, h: (a, h, 0, 0)),
        out_shape=jax.ShapeDtypeStruct((2, heads, tb, tb), F32),
        compiler_params=_cparams("arbitrary", "arbitrary"),
        name="bias_tiles",
    )(rel_bias)


def _attn_prompt_kernel(qi_ref, sm_ref, smb_ref, q_ref, kb_ref, vb_ref, bias_ref, o_ref,
                        skey_ref, thr_ref, wt_ref, madd_ref, m_ref, l_ref, acc_ref,
                        *, tb, heads, hd, iheads, idim, topk):
    i = pl.program_id(1)
    wscale = idim ** -0.5 * iheads ** -0.5

    m_ref[...] = jnp.full(m_ref.shape, NEG_BIG, F32)
    l_ref[...] = jnp.zeros(l_ref.shape, F32)
    acc_ref[...] = jnp.zeros(acc_ref.shape, F32)
    wt_ref[...] = sm_ref[...].T * wscale

    def chunk(j):
        return pl.ds(pl.multiple_of(j * tb, tb), tb)

    def score_chunk(j, diag):
        kic = smb_ref[chunk(j), 0:idim]
        acc = jnp.zeros((tb, tb), F32)
        for h in range(iheads):
            s = _dot_nt(kic, qi_ref[:, h * idim:(h + 1) * idim])
            acc = acc + jnp.maximum(s, 0.0) * wt_ref[idim + h:idim + h + 1, :]
        key = _ordered_key(acc)
        if diag:
            kpos = lax.broadcasted_iota(I32, (tb, tb), 0)
            qpos = lax.broadcasted_iota(I32, (tb, tb), 1)
            key = jnp.where(kpos > qpos, INT_MIN, key)
        skey_ref[chunk(j), :] = key

    def score_body(j, carry):
        score_chunk(j, False)
        return carry

    lax.fori_loop(0, i, score_body, 0)
    score_chunk(i, True)

    def count_ge(scand):
        def body(j, cnt):
            return cnt + _fold_rows(jnp.where(skey_ref[chunk(j), :] >= scand[0:1, :], 1.0, 0.0))

        return lax.fori_loop(0, i + 1, body, jnp.zeros((SUBLANES, tb), F32))

    trips = jnp.where((i + 1) * tb <= topk, 0, 32)
    thr_ref[...] = _kth_threshold(count_ge, (SUBLANES, tb), 0, float(topk), trips)

    def attend_chunk(j, kind):
        madd_ref[...] = jnp.where(skey_ref[chunk(j), :] >= thr_ref[0:1, :], 0.0, NEG_BIG).T
        for h in range(heads):
            hs = slice(h * hd, (h + 1) * hd)
            lg = _dot_nt(q_ref[:, hs], kb_ref[chunk(j), hs]) + madd_ref[...]
            if kind is not None:
                lg = lg + bias_ref[kind, h]
            _flash_update(lg, vb_ref[chunk(j), hs], m_ref, l_ref, acc_ref, h, hd)

    def attend_body(j, carry):
        attend_chunk(j, None)
        return carry

    lax.fori_loop(0, jnp.maximum(i - 1, 0), attend_body, 0)

    @pl.when(i >= 1)
    def _():
        attend_chunk(i - 1, 1)

    attend_chunk(i, 0)

    for h in range(heads):
        hs = slice(h * hd, (h + 1) * hd)
        o_ref[:, hs] = (acc_ref[:, hs] / jnp.tile(l_ref[h], (1, hd // LANES))).astype(BF16)


def _attn_prompt(bias_tiles, qi, sm, smb, q, kb, vb, *, batch, seq, tb, heads, hd, iheads, idim, topk):
    m, aw = q.shape
    iw = qi.shape[1]
    nq = seq // tb
    assert tb >= MAX_DISTANCE and tb % LANES == 0 and hd % LANES == 0
    qrow = lambda width: pl.BlockSpec((tb, width), lambda b, i: (b * nq + i, 0))
    seqblk = lambda width: pl.BlockSpec((seq, width), lambda b, i: (b, 0))
    return pl.pallas_call(
        functools.partial(_attn_prompt_kernel, tb=tb, heads=heads, hd=hd, iheads=iheads, idim=idim, topk=topk),
        grid=(batch, nq),
        in_specs=[qrow(iw), qrow(LANES), seqblk(LANES), qrow(aw), seqblk(aw), seqblk(aw),
                  _resident(bias_tiles.shape)],
        out_specs=qrow(aw),
        out_shape=jax.ShapeDtypeStruct((m, aw), BF16),
        scratch_shapes=[pltpu.VMEM((seq, tb), I32), pltpu.VMEM((SUBLANES, tb), I32),
                        pltpu.VMEM((LANES, tb), F32), pltpu.VMEM((tb, tb), F32),
                        pltpu.VMEM((heads, tb, LANES), F32), pltpu.VMEM((heads, tb, LANES), F32),
                        pltpu.VMEM((tb, aw), F32)],
        compiler_params=_cparams("arbitrary", "arbitrary"),
        name="attn_prompt",
    )(qi, sm, smb, q, kb, vb, bias_tiles)


def _sample_select_kernel(pt_ref, qi_ref, w_ref, kin_ref, *rest, pages, ps, nc, iheads, idim, tq, rq, topk):
    page_refs = rest[:pages]
    past_ref, new_ref, thr_ref, row_ref = rest[pages:]
    c = pl.program_id(1)
    ch = pages * ps
    wscale = idim ** -0.5 * iheads ** -0.5

    def score(keys_t):
        s = _dot(qi_ref[...], keys_t)
        t = jnp.maximum(s, 0.0) * (w_ref[:, 0:1] * wscale)
        acc = t[0:rq]
        for h in range(1, iheads):
            acc = acc + t[h * rq:(h + 1) * rq]
        return _ordered_key(acc)

    @pl.when(c < nc)
    def _():
        for p in range(pages):
            key = score(page_refs[p][...].astype(BF16))
            row_ref[:, pl.ds(pl.multiple_of(c * ch + p * ps, LANES), ps)] = key
            for t in range(ps // LANES):
                for j in range(tq):
                    past_ref[j, p * (ps // LANES) + t:p * (ps // LANES) + t + 1, :] = \
                        key[j:j + 1, t * LANES:(t + 1) * LANES]

    @pl.when(c == nc)
    def _():
        key = score(kin_ref[...])
        j = lax.broadcasted_iota(I32, (rq, ps), 0)
        n = lax.broadcasted_iota(I32, (rq, ps), 1)
        key = jnp.where((n <= j) & (n < tq), key, INT_MIN)
        new_ref[...] = key
        row_ref[:, nc * ch:nc * ch + ps] = key

        def count_ge(scand):
            sk = row_ref[...]
            return _fold_lanes(jnp.where(sk >= jnp.tile(scand, (1, sk.shape[1] // LANES)), 1.0, 0.0))

        thr = _kth_threshold(count_ge, (rq, LANES), 1, float(topk), 32)
        gt_new = jnp.sum(jnp.where(key > jnp.tile(thr, (1, ps // LANES)), 1.0, 0.0), axis=1, keepdims=True)
        budget = jnp.broadcast_to(topk - gt_new, (rq, LANES)).astype(I32)
        thr_ref[...] = jnp.zeros(thr_ref.shape, I32)
        for jj in range(tq):
            thr_ref[jj, 0:1, :] = thr[jj:jj + 1, :]
            thr_ref[jj, 1:2, :] = budget[jj:jj + 1, :]


def _sample_select(page_table, qi_hm, w_hm, kin_t, cache_kidx_t, *, pages, tq, rq, iheads, topk):
    db, n_pages = page_table.shape
    _, idim, ps = cache_kidx_t.shape
    nc = n_pages // pages
    ch = pages * ps
    assert ps % LANES == 0 and (ch // LANES) % SUBLANES == 0
    page_spec = lambda p: pl.BlockSpec(
        (None, idim, ps), lambda b, c, pt: (pt[b, jnp.minimum(c, nc - 1) * pages + p], 0, 0))
    per_b = lambda shape: pl.BlockSpec((None,) + shape, lambda b, c, pt: (b, 0, 0))
    return pl.pallas_call(
        functools.partial(_sample_select_kernel, pages=pages, ps=ps, nc=nc, iheads=iheads, idim=idim, tq=tq, rq=rq,
                          topk=topk),
        grid_spec=pltpu.PrefetchScalarGridSpec(
            num_scalar_prefetch=1,
            grid=(db, nc + 1),
            in_specs=[per_b((iheads * rq, idim)), per_b((iheads * rq, LANES)), per_b((idim, ps))]
            + [page_spec(p) for p in range(pages)],
            out_specs=[pl.BlockSpec((tq, ch // LANES, LANES), lambda b, c, pt: (b, jnp.minimum(c, nc - 1), 0)),
                       per_b((rq, ps)),
                       pl.BlockSpec((tq, SUBLANES, LANES), lambda b, c, pt: (b, 0, 0))],
            scratch_shapes=[pltpu.VMEM((rq, nc * ch + ps), I32)],
        ),
        out_shape=[jax.ShapeDtypeStruct((db * tq, n_pages * ps // LANES, LANES), I32),
                   jax.ShapeDtypeStruct((db, rq, ps), I32),
                   jax.ShapeDtypeStruct((db * tq, SUBLANES, LANES), I32)],
        compiler_params=_cparams("arbitrary", "arbitrary"),
        name="sample_select",
    )(page_table, qi_hm, w_hm, kin_t, *([cache_kidx_t] * pages))


def _sc_gather_kernel(past_hbm, thr_hbm, pt_hbm, ck_hbm, cv_hbm, ksel_hbm, vsel_hbm, pos_hbm, cnt_hbm,
                      row_v, thr_v, pt_v, idx_v, phys_v, rows_v, cnt_v, sem,
                      *, nq, tq, topk, ps, n_cores, rows_per_copy):
    wid = lax.axis_index("s") * n_cores + lax.axis_index("c")

    @pl.when(wid < nq)
    def _():
        pltpu.sync_copy(past_hbm.at[wid], row_v)
        pltpu.sync_copy(thr_hbm.at[wid], thr_v)
        pltpu.sync_copy(pt_hbm.at[wid // tq], pt_v)
        thr = thr_v[0, pl.ds(0, SC_LANES)]
        budget = thr_v[1, pl.ds(0, SC_LANES)]
        lane = lax.iota(I32, SC_LANES)
        zero = jnp.zeros((SC_LANES,), I32)
        for t in range(idx_v.shape[0] // SC_LANES):
            idx_v[pl.ds(t * SC_LANES, SC_LANES)] = zero

        def compact(pred):
            def body(r, cnt):
                for t in range(LANES // SC_LANES):
                    x = row_v[r, pl.ds(t * SC_LANES, SC_LANES)]
                    m = pred(x, cnt)
                    rank = plsc.cumsum(jnp.where(m, 1, 0).astype(I32))
                    plsc.store_scatter(idx_v, [cnt + rank - 1], lane + (r * LANES + t * SC_LANES), mask=m)
                    cnt = cnt + plsc.all_reduce_population_count(m)
                return cnt
            return body

        cnt = lax.fori_loop(0, row_v.shape[0], compact(lambda x, cnt: x > thr), zero)
        cnt = lax.fori_loop(0, row_v.shape[0], compact(lambda x, cnt: (x == thr) & (cnt < budget)), cnt)
        cnt_v[...] = jnp.minimum(cnt, budget)
        pltpu.sync_copy(cnt_v, cnt_hbm.at[wid])
        pltpu.sync_copy(idx_v.at[pl.ds(0, topk)], pos_hbm.at[wid])

        shift = ps.bit_length() - 1
        for t in range(topk // SC_LANES):
            pos = idx_v[pl.ds(t * SC_LANES, SC_LANES)]
            page = plsc.load_gather(pt_v, [lax.shift_right_logical(pos, shift)])
            phys_v[pl.ds(t * SC_LANES, SC_LANES)] = page * ps + (pos & (ps - 1))
        for g in range(topk // rows_per_copy):
            sel = phys_v.at[pl.ds(g * rows_per_copy, rows_per_copy)]
            dst = pl.ds(wid * topk + g * rows_per_copy, rows_per_copy)
            for src_hbm, dst_hbm in ((ck_hbm, ksel_hbm), (cv_hbm, vsel_hbm)):
                pltpu.async_copy(src_hbm.at[sel], rows_v, sem).wait()
                pltpu.sync_copy(rows_v, dst_hbm.at[dst])


def _sc_gather(past_keys, thr, page_table, cache_k, cache_v, *, tq, topk, ps):
    nq, key_rows, _ = past_keys.shape
    _, heads, hd = cache_k.shape
    assert ps & (ps - 1) == 0 and topk % SC_LANES == 0
    rows_per_copy = 64
    assert topk % rows_per_copy == 0
    mesh = plsc.VectorSubcoreMesh(core_axis_name="c", subcore_axis_name="s", num_cores=V7X_SC_CORES,
                                  num_subcores=V7X_SC_SUBCORES)
    assert nq <= V7X_SC_CORES * V7X_SC_SUBCORES
    sel_shape = jax.ShapeDtypeStruct((nq * topk, heads, hd), F32)
    return pl.kernel(
        functools.partial(_sc_gather_kernel, nq=nq, tq=tq, topk=topk, ps=ps, n_cores=V7X_SC_CORES,
                          rows_per_copy=rows_per_copy),
        out_type=[sel_shape, sel_shape, jax.ShapeDtypeStruct((nq, topk), I32),
                  jax.ShapeDtypeStruct((nq, SC_LANES), I32)],
        mesh=mesh,
        scratch_types=[pltpu.VMEM((key_rows, LANES), I32), pltpu.VMEM((SUBLANES, LANES), I32),
                       pltpu.VMEM((page_table.shape[1],), I32), pltpu.VMEM((topk + SC_LANES,), I32),
                       pltpu.VMEM((topk,), I32), pltpu.VMEM((rows_per_copy, heads, hd), F32),
                       pltpu.VMEM((SC_LANES,), I32), pltpu.SemaphoreType.DMA],
        compiler_params=pltpu.CompilerParams(needs_layout_passes=False),
        name="sc_select_gather",
    )(past_keys, thr, page_table, cache_k, cache_v)


def _sample_attn_sel_kernel(cnt_ref, relb_ref, q_ref, ksel_ref, vsel_ref, pos_ref, snew_ref, thr_ref, kn_ref, vn_ref,
                            o_ref, kbf_ref, vbf_ref, m_ref, l_ref, acc_ref,
                            *, heads, hd, topk, tq, rq, ps, past, n_buckets):
    w = pl.program_id(0)
    j = w % tq
    m_ref[...] = jnp.full(m_ref.shape, NEG_BIG, F32)
    l_ref[...] = jnp.zeros(l_ref.shape, F32)
    acc_ref[...] = jnp.zeros(acc_ref.shape, F32)
    q = jnp.broadcast_to(q_ref[...], (rq, heads * hd))
    for h in range(heads):
        kbf_ref[h] = ksel_ref[pl.ds(h, topk, stride=heads), :].astype(BF16)
        vbf_ref[h] = vsel_ref[pl.ds(h, topk, stride=heads), :].astype(BF16)

    slot = lax.broadcasted_iota(I32, (1, topk), 1)
    madd = jnp.where(slot < cnt_ref[w, 0], 0.0, NEG_BIG)
    bucket = _bucket(past + j - pos_ref[...], n_buckets)
    for h in range(heads):
        hs = slice(h * hd, (h + 1) * hd)
        lg = _dot_nt(q[:, hs], kbf_ref[h]) + (_bias_of_bucket(bucket, relb_ref, h, n_buckets) + madd)
        _flash_update(lg, vbf_ref[h], m_ref, l_ref, acc_ref, h, hd)

    n = lax.broadcasted_iota(I32, (1, ps), 1)
    madd = jnp.where(snew_ref[pl.ds(j, 1), :] >= thr_ref[0:1, :], 0.0, NEG_BIG)
    bucket = _bucket(j - n, n_buckets)
    for h in range(heads):
        hs = slice(h * hd, (h + 1) * hd)
        lg = _dot_nt(q[:, hs], kn_ref[:, hs]) + (_bias_of_bucket(bucket, relb_ref, h, n_buckets) + madd)
        _flash_update(lg, vn_ref[:, hs], m_ref, l_ref, acc_ref, h, hd)
    for h in range(heads):
        hs = slice(h * hd, (h + 1) * hd)
        o_ref[:, hs] = (acc_ref[:, hs] / jnp.tile(l_ref[h], (1, hd // LANES)))[0:1, :]


def _sample_attn_sel(cnt, rel_bias, q, ksel, vsel, pos, snew, thr, kn, vn, *, heads, hd, topk, tq, rq, past):
    nq, _, aw = q.shape
    ps = snew.shape[2]
    assert ps == LANES and hd == LANES
    per_q = lambda shape: pl.BlockSpec((None,) + shape, lambda w, cnt: (w, 0, 0))
    per_b = lambda shape: pl.BlockSpec((None,) + shape, lambda w, cnt: (w // tq, 0, 0))
    sel_spec = pl.BlockSpec((topk * heads, hd), lambda w, cnt: (w, 0))
    return pl.pallas_call(
        functools.partial(_sample_attn_sel_kernel, heads=heads, hd=hd, topk=topk, tq=tq, rq=rq, ps=ps, past=past,
                          n_buckets=rel_bias.shape[0]),
        grid_spec=pltpu.PrefetchScalarGridSpec(
            num_scalar_prefetch=1,
            grid=(nq,),
            in_specs=[pl.BlockSpec(memory_space=pltpu.SMEM), per_q((1, aw)), sel_spec, sel_spec, per_q((1, topk)),
                      per_b((rq, ps)), per_q((SUBLANES, LANES)), per_b((ps, aw)), per_b((ps, aw))],
            out_specs=per_q((1, aw)),
            scratch_shapes=[pltpu.VMEM((heads, topk, hd), BF16), pltpu.VMEM((heads, topk, hd), BF16),
                            pltpu.VMEM((heads, rq, LANES), F32), pltpu.VMEM((heads, rq, LANES), F32),
                            pltpu.VMEM((rq, aw), F32)],
        ),
        out_shape=jax.ShapeDtypeStruct((nq, 1, aw), F32),
        compiler_params=_cparams("arbitrary"),
        name="sample_attn_sel",
    )(cnt, rel_bias, q, ksel, vsel, pos, snew, thr, kn, vn)


def _sample_score_kernel(pt_ref, qi_ref, w_ref, kin_ref, *rest, pages, ps, nc, iheads, idim, tq, rq):
    page_refs, o_ref = rest[:pages], rest[pages]
    c = pl.program_id(1)
    wscale = idim ** -0.5 * iheads ** -0.5

    def score(keys_t):
        s = _dot(qi_ref[...], keys_t)
        t = jnp.maximum(s, 0.0) * (w_ref[:, 0:1] * wscale)
        acc = t[0:rq]
        for h in range(1, iheads):
            acc = acc + t[h * rq:(h + 1) * rq]
        return _ordered_key(acc)

    @pl.when(c < nc)
    def _():
        for p in range(pages):
            o_ref[:, p * ps:(p + 1) * ps] = score(page_refs[p][...].astype(BF16))

    @pl.when(c == nc)
    def _():
        key = score(kin_ref[...])
        j = lax.broadcasted_iota(I32, (rq, ps), 0)
        n = lax.broadcasted_iota(I32, (rq, ps), 1)
        o_ref[:, 0:ps] = jnp.where((n <= j) & (n < tq), key, INT_MIN)
        o_ref[:, ps:] = jnp.full((rq, (pages - 1) * ps), INT_MIN, I32)


def _sample_scores(page_table, qi_hm, w_hm, kin_t, cache_kidx_t, *, pages, tq, rq, iheads):
    db, n_pages = page_table.shape
    _, idim, ps = cache_kidx_t.shape
    nc = n_pages // pages
    ch = pages * ps
    page_spec = lambda p: pl.BlockSpec(
        (None, idim, ps), lambda b, c, pt: (pt[b, jnp.minimum(c, nc - 1) * pages + p], 0, 0))
    per_b = lambda shape: pl.BlockSpec((None,) + shape, lambda b, c, pt: (b, 0, 0))
    return pl.pallas_call(
        functools.partial(_sample_score_kernel, pages=pages, ps=ps, nc=nc, iheads=iheads, idim=idim, tq=tq, rq=rq),
        grid_spec=pltpu.PrefetchScalarGridSpec(
            num_scalar_prefetch=1,
            grid=(db, nc + 1),
            in_specs=[per_b((iheads * rq, idim)), per_b((iheads * rq, LANES)), per_b((idim, ps))]
            + [page_spec(p) for p in range(pages)],
            out_specs=pl.BlockSpec((None, rq, ch), lambda b, c, pt: (b, 0, c)),
        ),
        out_shape=jax.ShapeDtypeStruct((db, rq, (nc + 1) * ch), I32),
        compiler_params=_cparams("arbitrary", "arbitrary"),
        name="sample_scores",
    )(page_table, qi_hm, w_hm, kin_t, *([cache_kidx_t] * pages))


def _sample_attn_kernel(pt_ref, relb_ref, skey_ref, q_ref, kn_ref, vn_ref, *rest,
                        pages, ps, nc, heads, hd, topk, tq, rq, n_buckets):
    k_refs, v_refs = rest[:pages], rest[pages:2 * pages]
    o_ref, thr_ref, bias_ref, kbf_ref, vbf_ref, m_ref, l_ref, acc_ref = rest[2 * pages:]
    c = pl.program_id(1)
    ch = pages * ps
    past = nc * ch

    @pl.when(c == 0)
    def _():
        m_ref[...] = jnp.full(m_ref.shape, NEG_BIG, F32)
        l_ref[...] = jnp.zeros(l_ref.shape, F32)
        acc_ref[...] = jnp.zeros(acc_ref.shape, F32)

        def count_ge(scand):
            sk = skey_ref[...]
            return _fold_lanes(jnp.where(sk >= jnp.tile(scand, (1, sk.shape[1] // LANES)), 1.0, 0.0))

        thr_ref[...] = _kth_threshold(count_ge, (rq, LANES), 1, float(topk), 32)
        for h in range(heads):
            bias_ref[h] = jnp.full((rq, ch), relb_ref[n_buckets - 1, h], F32)

    @pl.when(c == nc - 1)
    def _():
        j = lax.broadcasted_iota(I32, (rq, ch), 0)
        s = lax.broadcasted_iota(I32, (rq, ch), 1)
        bucket = _bucket(ch + j - s, n_buckets)
        for h in range(heads):
            bias_ref[h] = _bias_of_bucket(bucket, relb_ref, h, n_buckets)

    @pl.when(c < nc)
    def _():
        for p in range(pages):
            for h in range(heads):
                kbf_ref[h, p * ps:(p + 1) * ps, :] = k_refs[p][pl.ds(h, ps, stride=heads), :].astype(BF16)
                vbf_ref[h, p * ps:(p + 1) * ps, :] = v_refs[p][pl.ds(h, ps, stride=heads), :].astype(BF16)
        mask = skey_ref[:, pl.ds(pl.multiple_of(c * ch, ch), ch)] >= jnp.tile(thr_ref[...], (1, ch // LANES))
        madd = jnp.where(mask, 0.0, NEG_BIG)
        for h in range(heads):
            hs = slice(h * hd, (h + 1) * hd)
            lg = _dot_nt(q_ref[:, hs], kbf_ref[h]) + bias_ref[h] + madd
            _flash_update(lg, vbf_ref[h], m_ref, l_ref, acc_ref, h, hd)

    @pl.when(c == nc)
    def _():
        mask = skey_ref[:, past:past + ps] >= jnp.tile(thr_ref[...], (1, ps // LANES))
        madd = jnp.where(mask, 0.0, NEG_BIG)
        j = lax.broadcasted_iota(I32, (rq, ps), 0)
        n = lax.broadcasted_iota(I32, (rq, ps), 1)
        bucket = _bucket(j - n, n_buckets)
        for h in range(heads):
            hs = slice(h * hd, (h + 1) * hd)
            lg = _dot_nt(q_ref[:, hs], kn_ref[:, hs]) + _bias_of_bucket(bucket, relb_ref, h, n_buckets) + madd
            _flash_update(lg, vn_ref[:, hs], m_ref, l_ref, acc_ref, h, hd)
        for h in range(heads):
            hs = slice(h * hd, (h + 1) * hd)
            o_ref[:, hs] = (acc_ref[:, hs] / jnp.tile(l_ref[h], (1, hd // LANES))).astype(BF16)


def _sample_attn(page_table, rel_bias, skey, q8, kn, vn, cache_k, cache_v, *, pages, heads, hd, topk, tq, rq):
    db, n_pages = page_table.shape
    ps = cache_k.shape[1] // heads
    aw = heads * hd
    nc = n_pages // pages
    ch = pages * ps
    n_buckets = rel_bias.shape[0]
    assert ch >= MAX_DISTANCE and ps >= tq and ps % LANES == 0 and hd == LANES
    page_spec = lambda p: pl.BlockSpec(
        (None, ps * heads, hd), lambda b, c, pt: (pt[b, jnp.minimum(c, nc - 1) * pages + p], 0, 0))
    per_b = lambda shape: pl.BlockSpec((None,) + shape, lambda b, c, pt: (b, 0, 0))
    return pl.pallas_call(
        functools.partial(_sample_attn_kernel, pages=pages, ps=ps, nc=nc, heads=heads, hd=hd, topk=topk,
                          tq=tq, rq=rq, n_buckets=n_buckets),
        grid_spec=pltpu.PrefetchScalarGridSpec(
            num_scalar_prefetch=1,
            grid=(db, nc + 1),
            in_specs=[pl.BlockSpec(memory_space=pltpu.SMEM), per_b((rq, skey.shape[2])), per_b((rq, aw)),
                      per_b((ps, aw)), per_b((ps, aw))]
            + [page_spec(p) for p in range(pages)] * 2,
            out_specs=per_b((rq, aw)),
            scratch_shapes=[pltpu.VMEM((rq, LANES), I32), pltpu.VMEM((heads, rq, ch), F32),
                            pltpu.VMEM((heads, ch, hd), BF16), pltpu.VMEM((heads, ch, hd), BF16),
                            pltpu.VMEM((heads, rq, LANES), F32), pltpu.VMEM((heads, rq, LANES), F32),
                            pltpu.VMEM((rq, aw), F32)],
        ),
        out_shape=jax.ShapeDtypeStruct((db, rq, aw), BF16),
        compiler_params=_cparams("arbitrary", "arbitrary"),
        name="sample_attn",
    )(page_table, rel_bias, skey, q8, kn, vn, *([cache_k] * pages), *([cache_v] * pages))


def _mix_kernel(x_ref, g_ref, a_ref, c_ref, wg_ref, wpa_ref, wpb_ref, wo_ref, o_ref, *, d):
    x = x_ref[...]
    hb = _rms_bf16(x, g_ref[...])
    a = _dot(a_ref[...], wpa_ref[...])
    m = jax.nn.sigmoid(_dot(hb, wg_ref[:, 0:d])) * a
    c = _dot(c_ref[...], wpb_ref[...])
    m = m + jax.nn.sigmoid(_dot(hb, wg_ref[:, d:2 * d])) * c
    o_ref[...] = x + _dot(m.astype(BF16), wo_ref[...])


def _mix(x2d, g, attn, c_in, w_gate, w_pa, w_pb, w_o, *, tm):
    m, d = x2d.shape
    row = lambda width: pl.BlockSpec((tm, width), lambda i: (i, 0))
    return pl.pallas_call(
        functools.partial(_mix_kernel, d=d),
        grid=(m // tm,),
        in_specs=[row(d), _resident((1, d)), row(attn.shape[1]), row(c_in.shape[1]), _resident(w_gate.shape),
                  _resident(w_pa.shape), _resident(w_pb.shape), _resident(w_o.shape)],
        out_specs=row(d),
        out_shape=jax.ShapeDtypeStruct((m, d), F32),
        compiler_params=_cparams("arbitrary"),
        name="mix_out",
    )(x2d, g, attn, c_in, w_gate, w_pa, w_pb, w_o)


def _mlp_kernel(x_ref, g_ref, gf_ref, w1_ref, w2_ref, y_ref, h_ref, acc_ref, *, final):
    f = pl.program_id(1)

    @pl.when(f == 0)
    def _():
        h_ref[...] = _rms_bf16(x_ref[...], g_ref[...])
        acc_ref[...] = jnp.zeros(acc_ref.shape, F32)

    t = jnp.square(jnp.maximum(_dot(h_ref[...], w1_ref[...]), 0.0))
    acc_ref[...] += _dot(t.astype(BF16), w2_ref[...])

    @pl.when(f == pl.num_programs(1) - 1)
    def _():
        x2 = x_ref[...] + acc_ref[...]
        if final:
            x2 = x2 * lax.rsqrt(jnp.mean(x2 * x2, axis=-1, keepdims=True) + EPS) * gf_ref[...]
        y_ref[...] = x2


def _mlp(x2d, g, gf, w1, w2, *, tm, tf, final):
    m, d = x2d.shape
    ff = w1.shape[1]
    return pl.pallas_call(
        functools.partial(_mlp_kernel, final=final),
        grid=(m // tm, ff // tf),
        in_specs=[pl.BlockSpec((tm, d), lambda i, f: (i, 0)), _resident((1, d)), _resident((1, d)),
                  pl.BlockSpec((d, tf), lambda i, f: (0, f)), pl.BlockSpec((tf, d), lambda i, f: (f, 0))],
        out_specs=pl.BlockSpec((tm, d), lambda i, f: (i, 0)),
        out_shape=jax.ShapeDtypeStruct((m, d), F32),
        scratch_shapes=[pltpu.VMEM((tm, d), BF16), pltpu.VMEM((tm, d), F32)],
        compiler_params=_cparams("arbitrary", "arbitrary"),
        name="mlp",
    )(x2d, g, gf, w1, w2)


def _tile(m, cap):
    return min(m, cap)


def kernel(x_prompt, x_sample, cache_k, cache_v, cache_kidx, state_conv, page_table, rel_bias, norm_mix_g, w_in,
           conv_w, w_pa, w_pb, w_o, norm_mlp_g, w_mlp_in, w_mlp_out, norm_final_g):
    batch, seq, d = x_prompt.shape
    db, tq, _ = x_sample.shape
    depth, n_pool, ps, heads, hd = cache_k.shape
    idim = cache_kidx.shape[-1]
    cw = conv_w.shape[-1]
    aw = heads * hd
    n_in = w_in.shape[-1]
    iheads = (n_in - 3 * aw - idim - 3 * cw - 2 * d) // (idim + 1)
    iw = iheads * idim
    assert 3 * aw + iw + idim + iheads + 3 * cw + 2 * d == n_in and idim + iheads <= LANES
    n_pages = page_table.shape[1]
    past = n_pages * ps
    rq = SUBLANES
    assert tq <= rq

    mp, ms = batch * seq, db * tq
    xp = x_prompt.reshape(mp, d)
    xs = x_sample.reshape(ms, d)
    tb = _tile(seq, 256)
    pages = math.gcd(n_pages, 8)
    o_small = 3 * aw + iw
    o_conv = o_small + idim + iheads
    o_gate = o_conv + 3 * cw
    gf = norm_final_g.reshape(1, d)
    bias_tiles = _bias_tiles(rel_bias, tb=tb)

    outs = {k: [] for k in ("kp", "vp", "kip", "sp", "ks", "vs", "kis", "ss")}
    for l in range(depth):
        wl = w_in[l]
        w_attn = wl[:, :o_small].astype(BF16)
        w_small = jnp.pad(wl[:, o_small:o_conv], ((0, 0), (0, LANES - idim - iheads))).astype(BF16)
        w_conv = wl[:, o_conv:o_gate].astype(BF16)
        w_gate = wl[:, o_gate:].astype(BF16)
        wpa, wpb, wo = w_pa[l].astype(BF16), w_pb[l].astype(BF16), w_o[l].astype(BF16)
        w1, w2 = w_mlp_in[l].astype(BF16), w_mlp_out[l].astype(BF16)
        g_mix = norm_mix_g[l].reshape(1, d)
        g_mlp = norm_mlp_g[l].reshape(1, d)

        tm = _tile(seq, 512)
        q, k, v, kb, vb, qi, sm, smb = _proj_attn(xp, g_mix, w_attn, w_small, aw=aw, iw=iw, hd=hd, tm=tm)
        c_in, u_tail = _proj_conv(xp, g_mix, w_conv, conv_w[l], None, seq=seq, tm=_tile(seq, 256))
        attn = _attn_prompt(bias_tiles, qi, sm, smb, q, kb, vb, batch=batch, seq=seq, tb=tb,
                            heads=heads, hd=hd, iheads=iheads, idim=idim, topk=min(TOPK_MAX, seq // 4))
        x1 = _mix(xp, g_mix, attn, c_in, w_gate, wpa, wpb, wo, tm=_tile(mp, 256))
        xp_next = _mlp(x1, g_mlp, gf, w1, w2, tm=_tile(mp, 512), tf=_tile(w1.shape[1], 1024), final=l == depth - 1)
        outs["kp"].append(k.reshape(batch, seq, heads, hd))
        outs["vp"].append(v.reshape(batch, seq, heads, hd))
        outs["kip"].append(sm[:, :idim].reshape(batch, seq, idim))
        outs["sp"].append(u_tail.reshape(batch, -1, SUBLANES, cw)[:, -1, SUBLANES - 2:])

        q, k, v, kb, vb, qi, sm, smb = _proj_attn(xs, g_mix, w_attn, w_small, aw=aw, iw=iw, hd=hd, tm=ms)
        st = state_conv[l]
        zero = jnp.zeros((db, tq - 1, cw), F32)
        prev1 = jnp.concatenate([st[:, 1:2], zero], axis=1).reshape(ms, cw)
        prev2 = jnp.concatenate([st, zero[:, 1:]], axis=1).reshape(ms, cw)
        c_in, u_all = _proj_conv(xs, g_mix, w_conv, conv_w[l], (prev1, prev2), seq=tq, tm=ms)

        def pad_rows(a, n):
            return jnp.pad(a, ((0, 0), (0, n - a.shape[1])) + ((0, 0),) * (a.ndim - 2))

        qi_hm = pad_rows(qi.reshape(db, tq, iheads, idim).transpose(0, 2, 1, 3).reshape(db * iheads, tq, idim), rq)
        qi_hm = qi_hm.reshape(db, iheads * rq, idim)
        w_hm = pad_rows(sm[:, idim:idim + iheads].reshape(db, tq, iheads).transpose(0, 2, 1).reshape(db * iheads, tq), rq)
        w_hm = jnp.broadcast_to(w_hm.reshape(db, iheads * rq, 1), (db, iheads * rq, LANES))
        kin_t = jnp.swapaxes(pad_rows(smb[:, :idim].reshape(db, tq, idim), ps), 1, 2)
        topk_s = min(TOPK_MAX, (past + tq) // 4)
        past_keys, snew, thr = _sample_select(page_table, qi_hm, w_hm, kin_t, jnp.swapaxes(cache_kidx[l], 1, 2),
                                              pages=math.gcd(n_pages, 16), tq=tq, rq=rq, iheads=iheads, topk=topk_s)
        ksel, vsel, pos, cnt = _sc_gather(past_keys, thr, page_table, cache_k[l].reshape(n_pool * ps, heads, hd),
                                          cache_v[l].reshape(n_pool * ps, heads, hd), tq=tq, topk=topk_s, ps=ps)
        attn = _sample_attn_sel(cnt, rel_bias, q.reshape(ms, 1, aw), ksel.reshape(ms * topk_s * heads, hd),
                                vsel.reshape(ms * topk_s * heads, hd), pos.reshape(ms, 1, topk_s), snew, thr,
                                pad_rows(kb.reshape(db, tq, aw), ps), pad_rows(vb.reshape(db, tq, aw), ps),
                                heads=heads, hd=hd, topk=topk_s, tq=tq, rq=rq, past=past)
        attn = attn.reshape(ms, aw).astype(BF16)
        x1 = _mix(xs, g_mix, attn, c_in, w_gate, wpa, wpb, wo, tm=ms)
        xs_next = _mlp(x1, g_mlp, gf, w1, w2, tm=ms, tf=_tile(w1.shape[1], 1024), final=l == depth - 1)
        outs["ks"].append(k.reshape(db, tq, heads, hd))
        outs["vs"].append(v.reshape(db, tq, heads, hd))
        outs["kis"].append(sm[:, :idim].reshape(db, tq, idim))
        outs["ss"].append(u_all.reshape(db, tq, cw)[:, tq - 2:])
        xp, xs = xp_next, xs_next

    st = {k: jnp.stack(v) for k, v in outs.items()}
    return (xp.reshape(batch, seq, d), xs.reshape(db, tq, d), st["kp"], st["vp"], st["kip"], st["sp"],
            st["ks"], st["vs"], st["kis"], st["ss"])
```

```python
import functools
import math

import jax
import jax.numpy as jnp
import numpy as np
from jax import lax
from jax.experimental import pallas as pl
from jax.experimental.pallas import tpu as pltpu
from jax.experimental.pallas import tpu_sc as plsc

F32 = jnp.float32
BF16 = jnp.bfloat16
I32 = jnp.int32

TOPK_MAX = 256
MAX_DISTANCE = 128
EPS = 1e-6

LANES = 128
SUBLANES = 8
V7X_SCOPED_VMEM_BYTES = 60000 * 1024
SC_LANES = 16
V7X_SC_CORES = 2
V7X_SC_SUBCORES = 16

INT_MIN = np.int32(-2 ** 31)
NEG_BIG = -1e30


def _cparams(*sem):
    return pltpu.CompilerParams(dimension_semantics=sem, vmem_limit_bytes=V7X_SCOPED_VMEM_BYTES)


def _resident(shape):
    nd = len(shape)
    return pl.BlockSpec(shape, lambda *_: (0,) * nd, pipeline_mode=pl.Buffered(1))


def _rms_bf16(x, g):
    y = x * lax.rsqrt(jnp.mean(x * x, axis=-1, keepdims=True) + EPS)
    return (y * g).astype(BF16)


def _dot(a, b):
    return jnp.dot(a, b, preferred_element_type=F32)


def _dot_nt(a, b):
    return lax.dot_general(a, b, (((1,), (1,)), ((), ())), preferred_element_type=F32)


def _ordered_key(x):
    b = lax.bitcast_convert_type(x, I32)
    return b ^ ((b >> 31) & np.int32(0x7FFFFFFF))


def _bucket(n, n_buckets):
    n = jnp.maximum(n, 0)
    me = n_buckets // 2
    nf = jnp.maximum(n, me).astype(F32)
    large = me + (jnp.log(nf / me) / math.log(MAX_DISTANCE / me) * (n_buckets - me)).astype(I32)
    large = jnp.minimum(large, n_buckets - 1)
    return jnp.where(n < me, n, large)


def _bias_of_bucket(bucket, relb_ref, h, n_buckets):
    acc = jnp.zeros(bucket.shape, F32)
    for bkt in range(n_buckets):
        acc = jnp.where(bucket == bkt, relb_ref[bkt, h], acc)
    return acc


def _fold_lanes(x):
    acc = x[:, 0:LANES]
    for t in range(1, x.shape[1] // LANES):
        acc = acc + x[:, t * LANES:(t + 1) * LANES]
    return acc


def _fold_rows(x, op=jnp.add):
    parts = [x[t * SUBLANES:(t + 1) * SUBLANES, :] for t in range(x.shape[0] // SUBLANES)]
    while len(parts) > 1:
        parts = [op(parts[t], parts[t + 1]) for t in range(0, len(parts) - 1, 2)] + parts[len(parts) & ~1:]
    return parts[0]


def _kth_threshold(count_ge, shape, axis, k, trips):
    def bit_body(t, uthr):
        cand = uthr | jnp.left_shift(np.int32(1), 31 - t)
        cnt = jnp.sum(count_ge(cand ^ INT_MIN), axis=axis, keepdims=True)
        return jnp.where(cnt >= k, cand, uthr)

    uthr = lax.fori_loop(0, trips, bit_body, jnp.zeros(shape, I32))
    return jnp.maximum(uthr ^ INT_MIN, INT_MIN + 1)


def _flash_update(lg, v, m_ref, l_ref, acc_ref, h, hd):
    reps = lg.shape[1] // LANES
    m_prev = m_ref[h]
    m_new = jnp.maximum(m_prev, jnp.max(lg, axis=1, keepdims=True))
    p = jnp.exp(lg - jnp.tile(m_new, (1, reps)))
    alpha = jnp.exp(m_prev - m_new)
    l_ref[h] = alpha * l_ref[h] + jnp.sum(p, axis=1, keepdims=True)
    m_ref[h] = m_new
    hs = slice(h * hd, (h + 1) * hd)
    acc_ref[:, hs] = acc_ref[:, hs] * jnp.tile(alpha, (1, hd // LANES)) + _dot(p.astype(BF16), v)


def _proj_attn_kernel(x_ref, g_ref, w_ref, ws_ref, q_ref, k_ref, v_ref, kb_ref, vb_ref, qi_ref,
                      sm_ref, smb_ref, *, aw, iw, hd, tm, qscale):
    heads = aw // hd
    hb = _rms_bf16(x_ref[...], g_ref[...])
    q_ref[...] = (_dot(hb, w_ref[:, 0:aw]) * qscale).astype(BF16)
    for o_ref, ob_ref, c0 in ((k_ref, kb_ref, aw), (v_ref, vb_ref, 2 * aw)):
        kv = _dot(hb, w_ref[:, c0:c0 + aw])
        ob_ref[...] = kv.astype(BF16)
        for h in range(heads):
            o_ref[pl.ds(h, tm, stride=heads), :] = kv[:, h * hd:(h + 1) * hd]
    qi_ref[...] = _dot(hb, w_ref[:, 3 * aw:3 * aw + iw]).astype(BF16)
    sm = _dot(hb, ws_ref[...])
    sm_ref[...] = sm
    smb_ref[...] = sm.astype(BF16)


def _proj_attn(x2d, g, w_attn, w_small, *, aw, iw, hd, tm):
    m, d = x2d.shape
    wt = w_attn.shape[1]
    heads = aw // hd
    assert hd == LANES
    row = lambda width: pl.BlockSpec((tm, width), lambda i: (i, 0))
    kv_spec = pl.BlockSpec((tm * heads, hd), lambda i: (i, 0))
    kv_shape = jax.ShapeDtypeStruct((m * heads, hd), F32)
    outs = [(aw, BF16), None, None, (aw, BF16), (aw, BF16), (iw, BF16), (LANES, F32), (LANES, BF16)]
    return pl.pallas_call(
        functools.partial(_proj_attn_kernel, aw=aw, iw=iw, hd=hd, tm=tm, qscale=hd ** -0.5),
        grid=(m // tm,),
        in_specs=[row(d), _resident((1, d)), _resident((d, wt)), _resident((d, LANES))],
        out_specs=[kv_spec if o is None else row(o[0]) for o in outs],
        out_shape=[kv_shape if o is None else jax.ShapeDtypeStruct((m, o[0]), o[1]) for o in outs],
        compiler_params=_cparams("arbitrary"),
        name="proj_attn",
    )(x2d, g, w_attn, w_small)


def _proj_conv_kernel(*refs, cw, tm, seq, tiles_per_seq, tail):
    if tiles_per_seq:
        x_ref, g_ref, w_ref, cwt_ref, c_ref, ut_ref, carry_ref = refs
    else:
        x_ref, g_ref, w_ref, cwt_ref, p1_ref, p2_ref, c_ref, ut_ref = refs
    hb = _rms_bf16(x_ref[...], g_ref[...])
    cx = _dot(hb, w_ref[:, 0:cw])
    cb = _dot(hb, w_ref[:, cw:2 * cw])
    cc = _dot(hb, w_ref[:, 2 * cw:3 * cw])
    u = cc * cx
    r = lax.broadcasted_iota(I32, (tm, cw), 0)
    um1 = pltpu.roll(u, 1, 0)
    um2 = pltpu.roll(u, 2, 0)
    if tiles_per_seq:
        @pl.when(pl.program_id(0) % tiles_per_seq == 0)
        def _():
            carry_ref[...] = jnp.zeros_like(carry_ref)
        prev1 = jnp.broadcast_to(carry_ref[SUBLANES - 1:SUBLANES, :], (tm, cw))
        prev2 = jnp.broadcast_to(carry_ref[SUBLANES - 2:SUBLANES - 1, :], (tm, cw))
        um1 = jnp.where(r == 0, prev1, um1)
        um2 = jnp.where(r == 0, prev2, jnp.where(r == 1, prev1, um2))
        carry_ref[...] = u[tm - SUBLANES:tm, :]
    else:
        assert seq & (seq - 1) == 0
        t = r & (seq - 1)
        um1 = jnp.where(t >= 1, um1, p1_ref[...])
        um2 = jnp.where(t >= 2, um2, p2_ref[...])
    y = cwt_ref[0:1, :] * um2 + cwt_ref[1:2, :] * um1 + cwt_ref[2:3, :] * u
    c_ref[...] = (cb * y).astype(BF16)
    ut_ref[...] = u[tm - tail:tm, :]


def _proj_conv(x2d, g, w_conv, conv_w, prev, *, seq, tm):
    m, d = x2d.shape
    cw = conv_w.shape[1]
    assert conv_w.shape[0] == 3
    row = lambda width: pl.BlockSpec((tm, width), lambda i: (i, 0))
    in_specs = [row(d), _resident((1, d)), _resident((d, 3 * cw)), _resident((3, cw))]
    args = [x2d, g, w_conv, conv_w]
    if prev is None:
        assert seq % tm == 0
        tiles_per_seq, tail = seq // tm, SUBLANES
        scratch = [pltpu.VMEM((SUBLANES, cw), F32)]
    else:
        assert tm % seq == 0 and seq >= 2
        tiles_per_seq, tail = 0, tm
        scratch = []
        in_specs += [row(cw), row(cw)]
        args += list(prev)
    return pl.pallas_call(
        functools.partial(_proj_conv_kernel, cw=cw, tm=tm, seq=seq, tiles_per_seq=tiles_per_seq, tail=tail),
        grid=(m // tm,),
        in_specs=in_specs,
        out_specs=[row(cw), pl.BlockSpec((tail, cw), lambda i: (i, 0))],
        out_shape=[jax.ShapeDtypeStruct((m, cw), BF16), jax.ShapeDtypeStruct((m // tm * tail, cw), F32)],
        scratch_shapes=scratch,
        compiler_params=_cparams("arbitrary"),
        name="proj_conv",
    )(*args)


def _bias_tiles_kernel(relb_ref, o_ref, *, tb, n_buckets):
    kind = pl.program_id(0)
    h = pl.program_id(1)
    qry = lax.broadcasted_iota(I32, (tb, tb), 0)
    key = lax.broadcasted_iota(I32, (tb, tb), 1)
    bias = _bias_of_bucket(_bucket(kind * tb + qry - key, n_buckets), relb_ref, h, n_buckets)
    o_ref[...] = bias - relb_ref[n_buckets - 1, h]


def _bias_tiles(rel_bias, *, tb):
    n_buckets, heads = rel_bias.shape
    return pl.pallas_call(
        functools.partial(_bias_tiles_kernel, tb=tb, n_buckets=n_buckets),
        grid=(2, heads),
        in_specs=[pl.BlockSpec(memory_space=pltpu.SMEM)],
        out_specs=pl.BlockSpec((None, None, tb, tb), lambda a, h: (a, h, 0, 0)),
        out_shape=jax.ShapeDtypeStruct((2, heads, tb, tb), F32),
        compiler_params=_cparams("arbitrary", "arbitrary"),
        name="bias_tiles",
    )(rel_bias)


def _attn_prompt_kernel(qi_ref, sm_ref, smb_ref, q_ref, kb_ref, vb_ref, bias_ref, o_ref,
                        skey_ref, thr_ref, wt_ref, madd_ref, m_ref, l_ref, acc_ref,
                        *, tb, heads, hd, iheads, idim, topk):
    i = pl.program_id(1)
    wscale = idim ** -0.5 * iheads ** -0.5

    m_ref[...] = jnp.full(m_ref.shape, NEG_BIG, F32)
    l_ref[...] = jnp.zeros(l_ref.shape, F32)
    acc_ref[...] = jnp.zeros(acc_ref.shape, F32)
    wt_ref[...] = sm_ref[...].T * wscale

    def chunk(j):
        return pl.ds(pl.multiple_of(j * tb, tb), tb)

    def score_chunk(j, diag):
        kic = smb_ref[chunk(j), 0:idim]
        acc = jnp.zeros((tb, tb), F32)
        for h in range(iheads):
            s = _dot_nt(kic, qi_ref[:, h * idim:(h + 1) * idim])
            acc = acc + jnp.maximum(s, 0.0) * wt_ref[idim + h:idim + h + 1, :]
        key = _ordered_key(acc)
        if diag:
            kpos = lax.broadcasted_iota(I32, (tb, tb), 0)
            qpos = lax.broadcasted_iota(I32, (tb, tb), 1)
            key = jnp.where(kpos > qpos, INT_MIN, key)
        skey_ref[chunk(j), :] = key

    def score_body(j, carry):
        score_chunk(j, False)
        return carry

    lax.fori_loop(0, i, score_body, 0)
    score_chunk(i, True)

    def count_ge(scand):
        def body(j, cnt):
            return cnt + _fold_rows(jnp.where(skey_ref[chunk(j), :] >= scand[0:1, :], 1.0, 0.0))

        return lax.fori_loop(0, i + 1, body, jnp.zeros((SUBLANES, tb), F32))

    trips = jnp.where((i + 1) * tb <= topk, 0, 32)
    thr_ref[...] = _kth_threshold(count_ge, (SUBLANES, tb), 0, float(topk), trips)

    def attend_chunk(j, kind):
        madd_ref[...] = jnp.where(skey_ref[chunk(j), :] >= thr_ref[0:1, :], 0.0, NEG_BIG).T
        for h in range(heads):
            hs = slice(h * hd, (h + 1) * hd)
            lg = _dot_nt(q_ref[:, hs], kb_ref[chunk(j), hs]) + madd_ref[...]
            if kind is not None:
                lg = lg + bias_ref[kind, h]
            _flash_update(lg, vb_ref[chunk(j), hs], m_ref, l_ref, acc_ref, h, hd)

    def attend_body(j, carry):
        attend_chunk(j, None)
        return carry

    lax.fori_loop(0, jnp.maximum(i - 1, 0), attend_body, 0)

    @pl.when(i >= 1)
    def _():
        attend_chunk(i - 1, 1)

    attend_chunk(i, 0)

    for h in range(heads):
        hs = slice(h * hd, (h + 1) * hd)
        o_ref[:, hs] = (acc_ref[:, hs] / jnp.tile(l_ref[h], (1, hd // LANES))).astype(BF16)


def _attn_prompt(bias_tiles, qi, sm, smb, q, kb, vb, *, batch, seq, tb, heads, hd, iheads, idim, topk):
    m, aw = q.shape
    iw = qi.shape[1]
    nq = seq // tb
    assert tb >= MAX_DISTANCE and tb % LANES == 0 and hd % LANES == 0
    qrow = lambda width: pl.BlockSpec((tb, width), lambda b, i: (b * nq + i, 0))
    seqblk = lambda width: pl.BlockSpec((seq, width), lambda b, i: (b, 0))
    return pl.pallas_call(
        functools.partial(_attn_prompt_kernel, tb=tb, heads=heads, hd=hd, iheads=iheads, idim=idim, topk=topk),
        grid=(batch, nq),
        in_specs=[qrow(iw), qrow(LANES), seqblk(LANES), qrow(aw), seqblk(aw), seqblk(aw),
                  _resident(bias_tiles.shape)],
        out_specs=qrow(aw),
        out_shape=jax.ShapeDtypeStruct((m, aw), BF16),
        scratch_shapes=[pltpu.VMEM((seq, tb), I32), pltpu.VMEM((SUBLANES, tb), I32),
                        pltpu.VMEM((LANES, tb), F32), pltpu.VMEM((tb, tb), F32),
                        pltpu.VMEM((heads, tb, LANES), F32), pltpu.VMEM((heads, tb, LANES), F32),
                        pltpu.VMEM((tb, aw), F32)],
        compiler_params=_cparams("arbitrary", "arbitrary"),
        name="attn_prompt",
    )(qi, sm, smb, q, kb, vb, bias_tiles)


def _sample_select_kernel(pt_ref, qi_ref, w_ref, kin_ref, *rest, pages, ps, nc, iheads, idim, tq, rq, topk):
    page_refs = rest[:pages]
    past_ref, new_ref, thr_ref, row_ref = rest[pages:]
    b = pl.program_id(0)
    c = pl.program_id(1)
    db = pl.num_programs(0)
    ch = pages * ps
    wscale = idim ** -0.5 * iheads ** -0.5
    rows_b = pl.ds(pl.multiple_of(b * rq, rq), rq)

    def score(keys_t):
        s = _dot(qi_ref[...], keys_t)
        t = jnp.maximum(s, 0.0) * (w_ref[:, 0:1] * wscale)
        acc = t[0:rq]
        for h in range(1, iheads):
            acc = acc + t[h * rq:(h + 1) * rq]
        return _ordered_key(acc)

    @pl.when(c < nc)
    def _():
        for p in range(pages):
            key = score(page_refs[p][...].astype(BF16))
            row_ref[rows_b, pl.ds(pl.multiple_of(c * ch + p * ps, LANES), ps)] = key
            for t in range(ps // LANES):
                for j in range(tq):
                    past_ref[j, p * (ps // LANES) + t:p * (ps // LANES) + t + 1, :] = \
                        key[j:j + 1, t * LANES:(t + 1) * LANES]

    @pl.when(c == nc)
    def _():
        key = score(kin_ref[...])
        j = lax.broadcasted_iota(I32, (rq, ps), 0)
        n = lax.broadcasted_iota(I32, (rq, ps), 1)
        key = jnp.where((n <= j) & (n < tq), key, INT_MIN)
        new_ref[...] = key
        row_ref[rows_b, nc * ch:nc * ch + ps] = key

    @pl.when((c == nc) & (b == db - 1))
    def _():
        n_rows = row_ref.shape[0]

        def count_ge(scand):
            sk = row_ref[...]
            return _fold_lanes(jnp.where(sk >= jnp.tile(scand, (1, sk.shape[1] // LANES)), 1.0, 0.0))

        thr = _kth_threshold(count_ge, (n_rows, LANES), 1, float(topk), 32)
        new_keys = row_ref[:, nc * ch:nc * ch + ps]
        gt_new = jnp.sum(jnp.where(new_keys > jnp.tile(thr, (1, ps // LANES)), 1.0, 0.0), axis=1, keepdims=True)
        budget = jnp.broadcast_to(topk - gt_new, (n_rows, LANES)).astype(I32)
        thr_ref[...] = jnp.zeros(thr_ref.shape, I32)
        for bb in range(n_rows // rq):
            for jj in range(tq):
                thr_ref[bb * tq + jj, 0:1, :] = thr[bb * rq + jj:bb * rq + jj + 1, :]
                thr_ref[bb * tq + jj, 1:2, :] = budget[bb * rq + jj:bb * rq + jj + 1, :]


def _sample_select(page_table, qi_hm, w_hm, kin_t, cache_kidx_t, *, pages, tq, rq, iheads, topk):
    db, n_pages = page_table.shape
    _, idim, ps = cache_kidx_t.shape
    nc = n_pages // pages
    ch = pages * ps
    assert ps % LANES == 0 and (ch // LANES) % SUBLANES == 0
    page_spec = lambda p: pl.BlockSpec(
        (None, idim, ps), lambda b, c, pt: (pt[b, jnp.minimum(c, nc - 1) * pages + p], 0, 0))
    per_b = lambda shape: pl.BlockSpec((None,) + shape, lambda b, c, pt: (b, 0, 0))
    return pl.pallas_call(
        functools.partial(_sample_select_kernel, pages=pages, ps=ps, nc=nc, iheads=iheads, idim=idim, tq=tq, rq=rq,
                          topk=topk),
        grid_spec=pltpu.PrefetchScalarGridSpec(
            num_scalar_prefetch=1,
            grid=(db, nc + 1),
            in_specs=[per_b((iheads * rq, idim)), per_b((iheads * rq, LANES)), per_b((idim, ps))]
            + [page_spec(p) for p in range(pages)],
            out_specs=[pl.BlockSpec((tq, ch // LANES, LANES), lambda b, c, pt: (b, jnp.minimum(c, nc - 1), 0)),
                       per_b((rq, ps)),
                       pl.BlockSpec((db * tq, SUBLANES, LANES), lambda b, c, pt: (0, 0, 0))],
            scratch_shapes=[pltpu.VMEM((db * rq, nc * ch + ps), I32)],
        ),
        out_shape=[jax.ShapeDtypeStruct((db * tq, n_pages * ps // LANES, LANES), I32),
                   jax.ShapeDtypeStruct((db, rq, ps), I32),
                   jax.ShapeDtypeStruct((db * tq, SUBLANES, LANES), I32)],
        compiler_params=_cparams("arbitrary", "arbitrary"),
        name="sample_select",
    )(page_table, qi_hm, w_hm, kin_t, *([cache_kidx_t] * pages))


def _sc_gather_kernel(past_hbm, thr_hbm, pt_hbm, ck_hbm, cv_hbm, ksel_hbm, vsel_hbm, pos_hbm, cnt_hbm,
                      row_v, thr_v, pt_v, idx_v, phys_v, rows_v, cnt_v, sem,
                      *, nq, tq, topk, ps, n_cores, rows_per_copy):
    wid = lax.axis_index("s") * n_cores + lax.axis_index("c")

    @pl.when(wid < nq)
    def _():
        pltpu.sync_copy(past_hbm.at[wid], row_v)
        pltpu.sync_copy(thr_hbm.at[wid], thr_v)
        pltpu.sync_copy(pt_hbm.at[wid // tq], pt_v)
        thr = thr_v[0, pl.ds(0, SC_LANES)]
        budget = thr_v[1, pl.ds(0, SC_LANES)]
        lane = lax.iota(I32, SC_LANES)
        zero = jnp.zeros((SC_LANES,), I32)
        for t in range(idx_v.shape[0] // SC_LANES):
            idx_v[pl.ds(t * SC_LANES, SC_LANES)] = zero

        def compact(pred):
            def body(r, cnt):
                for t in range(LANES // SC_LANES):
                    x = row_v[r, pl.ds(t * SC_LANES, SC_LANES)]
                    m = pred(x, cnt)
                    rank = plsc.cumsum(jnp.where(m, 1, 0).astype(I32))
                    plsc.store_scatter(idx_v, [cnt + rank - 1], lane + (r * LANES + t * SC_LANES), mask=m)
                    cnt = cnt + plsc.all_reduce_population_count(m)
                return cnt
            return body

        cnt = lax.fori_loop(0, row_v.shape[0], compact(lambda x, cnt: x > thr), zero)
        cnt = lax.fori_loop(0, row_v.shape[0], compact(lambda x, cnt: (x == thr) & (cnt < budget)), cnt)
        cnt_v[...] = jnp.minimum(cnt, budget)
        pltpu.sync_copy(cnt_v, cnt_hbm.at[wid])
        pltpu.sync_copy(idx_v.at[pl.ds(0, topk)], pos_hbm.at[wid])

        shift = ps.bit_length() - 1
        for t in range(topk // SC_LANES):
            pos = idx_v[pl.ds(t * SC_LANES, SC_LANES)]
            page = plsc.load_gather(pt_v, [lax.shift_right_logical(pos, shift)])
            phys_v[pl.ds(t * SC_LANES, SC_LANES)] = page * ps + (pos & (ps - 1))
        for g in range(topk // rows_per_copy):
            sel = phys_v.at[pl.ds(g * rows_per_copy, rows_per_copy)]
            dst = pl.ds(wid * topk + g * rows_per_copy, rows_per_copy)
            for src_hbm, dst_hbm in ((ck_hbm, ksel_hbm), (cv_hbm, vsel_hbm)):
                pltpu.async_copy(src_hbm.at[sel], rows_v, sem).wait()
                pltpu.sync_copy(rows_v, dst_hbm.at[dst])


def _sc_gather(past_keys, thr, page_table, cache_k, cache_v, *, tq, topk, ps):
    nq, key_rows, _ = past_keys.shape
    _, heads, hd = cache_k.shape
    assert ps & (ps - 1) == 0 and topk % SC_LANES == 0
    rows_per_copy = 64
    assert topk % rows_per_copy == 0
    mesh = plsc.VectorSubcoreMesh(core_axis_name="c", subcore_axis_name="s", num_cores=V7X_SC_CORES,
                                  num_subcores=V7X_SC_SUBCORES)
    assert nq <= V7X_SC_CORES * V7X_SC_SUBCORES
    sel_shape = jax.ShapeDtypeStruct((nq * topk, heads, hd), F32)
    return pl.kernel(
        functools.partial(_sc_gather_kernel, nq=nq, tq=tq, topk=topk, ps=ps, n_cores=V7X_SC_CORES,
                          rows_per_copy=rows_per_copy),
        out_type=[sel_shape, sel_shape, jax.ShapeDtypeStruct((nq, topk), I32),
                  jax.ShapeDtypeStruct((nq, SC_LANES), I32)],
        mesh=mesh,
        scratch_types=[pltpu.VMEM((key_rows, LANES), I32), pltpu.VMEM((SUBLANES, LANES), I32),
                       pltpu.VMEM((page_table.shape[1],), I32), pltpu.VMEM((topk + SC_LANES,), I32),
                       pltpu.VMEM((topk,), I32), pltpu.VMEM((rows_per_copy, heads, hd), F32),
                       pltpu.VMEM((SC_LANES,), I32), pltpu.SemaphoreType.DMA],
        compiler_params=pltpu.CompilerParams(needs_layout_passes=False),
        name="sc_select_gather",
    )(past_keys, thr, page_table, cache_k, cache_v)


def _sample_attn_sel_kernel(cnt_ref, relbt_ref, q_ref, ksel_ref, vsel_ref, pos_ref, snew_ref, thr_ref, kn_ref, vn_ref,
                            o_ref, *, heads, tq, past, n_buckets):
    w = pl.program_id(0)
    j = w % tq
    q = q_ref[...]

    def head_bias(dist):
        bucket = _bucket(dist, n_buckets)
        acc = jnp.zeros(bucket.shape, F32)
        for bkt in range(n_buckets):
            acc = jnp.where(bucket == bkt, relbt_ref[:, bkt:bkt + 1], acc)
        return acc

    def logits(keys):
        lg = _dot_nt(q, keys)
        head = lax.broadcasted_iota(I32, lg.shape, 0)
        col = lax.broadcasted_iota(I32, lg.shape, 1)
        return lg, (col & (heads - 1)) == head, col

    lg, own, col = logits(ksel_ref[...].astype(BF16))
    keep = own & (col < cnt_ref[w, 0] * heads)
    lg = jnp.where(keep, lg + head_bias(jnp.broadcast_to(past + j - pos_ref[...], lg.shape)), NEG_BIG)
    lgn, own, col = logits(kn_ref[...])
    keep = own & (snew_ref[...] >= thr_ref[0:1, :])
    lgn = jnp.where(keep, lgn + head_bias(j - lax.shift_right_logical(col, heads.bit_length() - 1)), NEG_BIG)

    m = jnp.maximum(jnp.max(lg, axis=1, keepdims=True), jnp.max(lgn, axis=1, keepdims=True))
    p = jnp.exp(lg - m)
    pn = jnp.exp(lgn - m)
    denom = jnp.sum(p, axis=1, keepdims=True) + jnp.sum(pn, axis=1, keepdims=True)
    acc = _dot(p.astype(BF16), vsel_ref[...].astype(BF16)) + _dot(pn.astype(BF16), vn_ref[...])
    o_ref[...] = acc / denom


def _sample_attn_sel(cnt, rel_bias_t, q, ksel, vsel, pos, snew, thr, kn, vn, *, heads, hd, topk, tq, past):
    nq = q.shape[0]
    ps = kn.shape[1]
    assert ps == LANES and hd == LANES and heads & (heads - 1) == 0
    per_q = lambda shape: pl.BlockSpec((None,) + shape, lambda w, cnt: (w, 0, 0))
    per_b = lambda shape: pl.BlockSpec((None,) + shape, lambda w, cnt: (w // tq, 0, 0))
    sel_spec = pl.BlockSpec((topk * heads, hd), lambda w, cnt: (w, 0))
    return pl.pallas_call(
        functools.partial(_sample_attn_sel_kernel, heads=heads, tq=tq, past=past, n_buckets=rel_bias_t.shape[1]),
        grid_spec=pltpu.PrefetchScalarGridSpec(
            num_scalar_prefetch=1,
            grid=(nq,),
            in_specs=[pl.BlockSpec(rel_bias_t.shape, lambda w, cnt: (0, 0)), per_q((heads, hd)), sel_spec, sel_spec,
                      per_q((1, topk * heads)), per_q((1, ps)), per_q((SUBLANES, LANES)), per_b((ps, hd)),
                      per_b((ps, hd))],
            out_specs=per_q((heads, hd)),
        ),
        out_shape=jax.ShapeDtypeStruct((nq, heads, hd), F32),
        compiler_params=_cparams("arbitrary"),
        name="sample_attn_sel",
    )(cnt, rel_bias_t, q, ksel, vsel, pos, snew, thr, kn, vn)


def _sample_score_kernel(pt_ref, qi_ref, w_ref, kin_ref, *rest, pages, ps, nc, iheads, idim, tq, rq):
    page_refs, o_ref = rest[:pages], rest[pages]
    c = pl.program_id(1)
    wscale = idim ** -0.5 * iheads ** -0.5

    def score(keys_t):
        s = _dot(qi_ref[...], keys_t)
        t = jnp.maximum(s, 0.0) * (w_ref[:, 0:1] * wscale)
        acc = t[0:rq]
        for h in range(1, iheads):
            acc = acc + t[h * rq:(h + 1) * rq]
        return _ordered_key(acc)

    @pl.when(c < nc)
    def _():
        for p in range(pages):
            o_ref[:, p * ps:(p + 1) * ps] = score(page_refs[p][...].astype(BF16))

    @pl.when(c == nc)
    def _():
        key = score(kin_ref[...])
        j = lax.broadcasted_iota(I32, (rq, ps), 0)
        n = lax.broadcasted_iota(I32, (rq, ps), 1)
        o_ref[:, 0:ps] = jnp.where((n <= j) & (n < tq), key, INT_MIN)
        o_ref[:, ps:] = jnp.full((rq, (pages - 1) * ps), INT_MIN, I32)


def _sample_scores(page_table, qi_hm, w_hm, kin_t, cache_kidx_t, *, pages, tq, rq, iheads):
    db, n_pages = page_table.shape
    _, idim, ps = cache_kidx_t.shape
    nc = n_pages // pages
    ch = pages * ps
    page_spec = lambda p: pl.BlockSpec(
        (None, idim, ps), lambda b, c, pt: (pt[b, jnp.minimum(c, nc - 1) * pages + p], 0, 0))
    per_b = lambda shape: pl.BlockSpec((None,) + shape, lambda b, c, pt: (b, 0, 0))
    return pl.pallas_call(
        functools.partial(_sample_score_kernel, pages=pages, ps=ps, nc=nc, iheads=iheads, idim=idim, tq=tq, rq=rq),
        grid_spec=pltpu.PrefetchScalarGridSpec(
            num_scalar_prefetch=1,
            grid=(db, nc + 1),
            in_specs=[per_b((iheads * rq, idim)), per_b((iheads * rq, LANES)), per_b((idim, ps))]
            + [page_spec(p) for p in range(pages)],
            out_specs=pl.BlockSpec((None, rq, ch), lambda b, c, pt: (b, 0, c)),
        ),
        out_shape=jax.ShapeDtypeStruct((db, rq, (nc + 1) * ch), I32),
        compiler_params=_cparams("arbitrary", "arbitrary"),
        name="sample_scores",
    )(page_table, qi_hm, w_hm, kin_t, *([cache_kidx_t] * pages))


def _sample_attn_kernel(pt_ref, relb_ref, skey_ref, q_ref, kn_ref, vn_ref, *rest,
                        pages, ps, nc, heads, hd, topk, tq, rq, n_buckets):
    k_refs, v_refs = rest[:pages], rest[pages:2 * pages]
    o_ref, thr_ref, bias_ref, kbf_ref, vbf_ref, m_ref, l_ref, acc_ref = rest[2 * pages:]
    c = pl.program_id(1)
    ch = pages * ps
    past = nc * ch

    @pl.when(c == 0)
    def _():
        m_ref[...] = jnp.full(m_ref.shape, NEG_BIG, F32)
        l_ref[...] = jnp.zeros(l_ref.shape, F32)
        acc_ref[...] = jnp.zeros(acc_ref.shape, F32)

        def count_ge(scand):
            sk = skey_ref[...]
            return _fold_lanes(jnp.where(sk >= jnp.tile(scand, (1, sk.shape[1] // LANES)), 1.0, 0.0))

        thr_ref[...] = _kth_threshold(count_ge, (rq, LANES), 1, float(topk), 32)
        for h in range(heads):
            bias_ref[h] = jnp.full((rq, ch), relb_ref[n_buckets - 1, h], F32)

    @pl.when(c == nc - 1)
    def _():
        j = lax.broadcasted_iota(I32, (rq, ch), 0)
        s = lax.broadcasted_iota(I32, (rq, ch), 1)
        bucket = _bucket(ch + j - s, n_buckets)
        for h in range(heads):
            bias_ref[h] = _bias_of_bucket(bucket, relb_ref, h, n_buckets)

    @pl.when(c < nc)
    def _():
        for p in range(pages):
            for h in range(heads):
                kbf_ref[h, p * ps:(p + 1) * ps, :] = k_refs[p][pl.ds(h, ps, stride=heads), :].astype(BF16)
                vbf_ref[h, p * ps:(p + 1) * ps, :] = v_refs[p][pl.ds(h, ps, stride=heads), :].astype(BF16)
        mask = skey_ref[:, pl.ds(pl.multiple_of(c * ch, ch), ch)] >= jnp.tile(thr_ref[...], (1, ch // LANES))
        madd = jnp.where(mask, 0.0, NEG_BIG)
        for h in range(heads):
            hs = slice(h * hd, (h + 1) * hd)
            lg = _dot_nt(q_ref[:, hs], kbf_ref[h]) + bias_ref[h] + madd
            _flash_update(lg, vbf_ref[h], m_ref, l_ref, acc_ref, h, hd)

    @pl.when(c == nc)
    def _():
        mask = skey_ref[:, past:past + ps] >= jnp.tile(thr_ref[...], (1, ps // LANES))
        madd = jnp.where(mask, 0.0, NEG_BIG)
        j = lax.broadcasted_iota(I32, (rq, ps), 0)
        n = lax.broadcasted_iota(I32, (rq, ps), 1)
        bucket = _bucket(j - n, n_buckets)
        for h in range(heads):
            hs = slice(h * hd, (h + 1) * hd)
            lg = _dot_nt(q_ref[:, hs], kn_ref[:, hs]) + _bias_of_bucket(bucket, relb_ref, h, n_buckets) + madd
            _flash_update(lg, vn_ref[:, hs], m_ref, l_ref, acc_ref, h, hd)
        for h in range(heads):
            hs = slice(h * hd, (h + 1) * hd)
            o_ref[:, hs] = (acc_ref[:, hs] / jnp.tile(l_ref[h], (1, hd // LANES))).astype(BF16)


def _sample_attn(page_table, rel_bias, skey, q8, kn, vn, cache_k, cache_v, *, pages, heads, hd, topk, tq, rq):
    db, n_pages = page_table.shape
    ps = cache_k.shape[1] // heads
    aw = heads * hd
    nc = n_pages // pages
    ch = pages * ps
    n_buckets = rel_bias.shape[0]
    assert ch >= MAX_DISTANCE and ps >= tq and ps % LANES == 0 and hd == LANES
    page_spec = lambda p: pl.BlockSpec(
        (None, ps * heads, hd), lambda b, c, pt: (pt[b, jnp.minimum(c, nc - 1) * pages + p], 0, 0))
    per_b = lambda shape: pl.BlockSpec((None,) + shape, lambda b, c, pt: (b, 0, 0))
    return pl.pallas_call(
        functools.partial(_sample_attn_kernel, pages=pages, ps=ps, nc=nc, heads=heads, hd=hd, topk=topk,
                          tq=tq, rq=rq, n_buckets=n_buckets),
        grid_spec=pltpu.PrefetchScalarGridSpec(
            num_scalar_prefetch=1,
            grid=(db, nc + 1),
            in_specs=[pl.BlockSpec(memory_space=pltpu.SMEM), per_b((rq, skey.shape[2])), per_b((rq, aw)),
                      per_b((ps, aw)), per_b((ps, aw))]
            + [page_spec(p) for p in range(pages)] * 2,
            out_specs=per_b((rq, aw)),
            scratch_shapes=[pltpu.VMEM((rq, LANES), I32), pltpu.VMEM((heads, rq, ch), F32),
                            pltpu.VMEM((heads, ch, hd), BF16), pltpu.VMEM((heads, ch, hd), BF16),
                            pltpu.VMEM((heads, rq, LANES), F32), pltpu.VMEM((heads, rq, LANES), F32),
                            pltpu.VMEM((rq, aw), F32)],
        ),
        out_shape=jax.ShapeDtypeStruct((db, rq, aw), BF16),
        compiler_params=_cparams("arbitrary", "arbitrary"),
        name="sample_attn",
    )(page_table, rel_bias, skey, q8, kn, vn, *([cache_k] * pages), *([cache_v] * pages))


def _mix_kernel(x_ref, g_ref, a_ref, c_ref, wg_ref, wpa_ref, wpb_ref, wo_ref, o_ref, *, d):
    x = x_ref[...]
    hb = _rms_bf16(x, g_ref[...])
    a = _dot(a_ref[...], wpa_ref[...])
    m = jax.nn.sigmoid(_dot(hb, wg_ref[:, 0:d])) * a
    c = _dot(c_ref[...], wpb_ref[...])
    m = m + jax.nn.sigmoid(_dot(hb, wg_ref[:, d:2 * d])) * c
    o_ref[...] = x + _dot(m.astype(BF16), wo_ref[...])


def _mix(x2d, g, attn, c_in, w_gate, w_pa, w_pb, w_o, *, tm):
    m, d = x2d.shape
    row = lambda width: pl.BlockSpec((tm, width), lambda i: (i, 0))
    return pl.pallas_call(
        functools.partial(_mix_kernel, d=d),
        grid=(m // tm,),
        in_specs=[row(d), _resident((1, d)), row(attn.shape[1]), row(c_in.shape[1]), _resident(w_gate.shape),
                  _resident(w_pa.shape), _resident(w_pb.shape), _resident(w_o.shape)],
        out_specs=row(d),
        out_shape=jax.ShapeDtypeStruct((m, d), F32),
        compiler_params=_cparams("arbitrary"),
        name="mix_out",
    )(x2d, g, attn, c_in, w_gate, w_pa, w_pb, w_o)


def _mlp_kernel(x_ref, g_ref, gf_ref, w1_ref, w2_ref, y_ref, h_ref, acc_ref, *, final):
    f = pl.program_id(1)

    @pl.when(f == 0)
    def _():
        h_ref[...] = _rms_bf16(x_ref[...], g_ref[...])
        acc_ref[...] = jnp.zeros(acc_ref.shape, F32)

    t = jnp.square(jnp.maximum(_dot(h_ref[...], w1_ref[...]), 0.0))
    acc_ref[...] += _dot(t.astype(BF16), w2_ref[...])

    @pl.when(f == pl.num_programs(1) - 1)
    def _():
        x2 = x_ref[...] + acc_ref[...]
        if final:
            x2 = x2 * lax.rsqrt(jnp.mean(x2 * x2, axis=-1, keepdims=True) + EPS) * gf_ref[...]
        y_ref[...] = x2


def _mlp(x2d, g, gf, w1, w2, *, tm, tf, final):
    m, d = x2d.shape
    ff = w1.shape[1]
    return pl.pallas_call(
        functools.partial(_mlp_kernel, final=final),
        grid=(m // tm, ff // tf),
        in_specs=[pl.BlockSpec((tm, d), lambda i, f: (i, 0)), _resident((1, d)), _resident((1, d)),
                  pl.BlockSpec((d, tf), lambda i, f: (0, f)), pl.BlockSpec((tf, d), lambda i, f: (f, 0))],
        out_specs=pl.BlockSpec((tm, d), lambda i, f: (i, 0)),
        out_shape=jax.ShapeDtypeStruct((m, d), F32),
        scratch_shapes=[pltpu.VMEM((tm, d), BF16), pltpu.VMEM((tm, d), F32)],
        compiler_params=_cparams("arbitrary", "arbitrary"),
        name="mlp",
    )(x2d, g, gf, w1, w2)


def _tile(m, cap):
    return min(m, cap)


def kernel(x_prompt, x_sample, cache_k, cache_v, cache_kidx, state_conv, page_table, rel_bias, norm_mix_g, w_in,
           conv_w, w_pa, w_pb, w_o, norm_mlp_g, w_mlp_in, w_mlp_out, norm_final_g):
    batch, seq, d = x_prompt.shape
    db, tq, _ = x_sample.shape
    depth, n_pool, ps, heads, hd = cache_k.shape
    idim = cache_kidx.shape[-1]
    cw = conv_w.shape[-1]
    aw = heads * hd
    n_in = w_in.shape[-1]
    iheads = (n_in - 3 * aw - idim - 3 * cw - 2 * d) // (idim + 1)
    iw = iheads * idim
    assert 3 * aw + iw + idim + iheads + 3 * cw + 2 * d == n_in and idim + iheads <= LANES
    n_pages = page_table.shape[1]
    past = n_pages * ps
    rq = SUBLANES
    assert tq <= rq

    mp, ms = batch * seq, db * tq
    xp = x_prompt.reshape(mp, d)
    xs = x_sample.reshape(ms, d)
    tb = _tile(seq, 256)
    pages = math.gcd(n_pages, 8)
    o_small = 3 * aw + iw
    o_conv = o_small + idim + iheads
    o_gate = o_conv + 3 * cw
    gf = norm_final_g.reshape(1, d)
    bias_tiles = _bias_tiles(rel_bias, tb=tb)

    outs = {k: [] for k in ("kp", "vp", "kip", "sp", "ks", "vs", "kis", "ss")}
    for l in range(depth):
        wl = w_in[l]
        w_attn = wl[:, :o_small].astype(BF16)
        w_small = jnp.pad(wl[:, o_small:o_conv], ((0, 0), (0, LANES - idim - iheads))).astype(BF16)
        w_conv = wl[:, o_conv:o_gate].astype(BF16)
        w_gate = wl[:, o_gate:].astype(BF16)
        wpa, wpb, wo = w_pa[l].astype(BF16), w_pb[l].astype(BF16), w_o[l].astype(BF16)
        w1, w2 = w_mlp_in[l].astype(BF16), w_mlp_out[l].astype(BF16)
        g_mix = norm_mix_g[l].reshape(1, d)
        g_mlp = norm_mlp_g[l].reshape(1, d)

        tm = _tile(seq, 512)
        q, k, v, kb, vb, qi, sm, smb = _proj_attn(xp, g_mix, w_attn, w_small, aw=aw, iw=iw, hd=hd, tm=tm)
        c_in, u_tail = _proj_conv(xp, g_mix, w_conv, conv_w[l], None, seq=seq, tm=_tile(seq, 256))
        attn = _attn_prompt(bias_tiles, qi, sm, smb, q, kb, vb, batch=batch, seq=seq, tb=tb,
                            heads=heads, hd=hd, iheads=iheads, idim=idim, topk=min(TOPK_MAX, seq // 4))
        x1 = _mix(xp, g_mix, attn, c_in, w_gate, wpa, wpb, wo, tm=_tile(mp, 256))
        xp_next = _mlp(x1, g_mlp, gf, w1, w2, tm=_tile(mp, 512), tf=_tile(w1.shape[1], 1024), final=l == depth - 1)
        outs["kp"].append(k.reshape(batch, seq, heads, hd))
        outs["vp"].append(v.reshape(batch, seq, heads, hd))
        outs["kip"].append(sm[:, :idim].reshape(batch, seq, idim))
        outs["sp"].append(u_tail.reshape(batch, -1, SUBLANES, cw)[:, -1, SUBLANES - 2:])

        q, k, v, kb, vb, qi, sm, smb = _proj_attn(xs, g_mix, w_attn, w_small, aw=aw, iw=iw, hd=hd, tm=ms)
        st = state_conv[l]
        zero = jnp.zeros((db, tq - 1, cw), F32)
        prev1 = jnp.concatenate([st[:, 1:2], zero], axis=1).reshape(ms, cw)
        prev2 = jnp.concatenate([st, zero[:, 1:]], axis=1).reshape(ms, cw)
        c_in, u_all = _proj_conv(xs, g_mix, w_conv, conv_w[l], (prev1, prev2), seq=tq, tm=ms)

        def pad_rows(a, n):
            return jnp.pad(a, ((0, 0), (0, n - a.shape[1])) + ((0, 0),) * (a.ndim - 2))

        qi_hm = pad_rows(qi.reshape(db, tq, iheads, idim).transpose(0, 2, 1, 3).reshape(db * iheads, tq, idim), rq)
        qi_hm = qi_hm.reshape(db, iheads * rq, idim)
        w_hm = pad_rows(sm[:, idim:idim + iheads].reshape(db, tq, iheads).transpose(0, 2, 1).reshape(db * iheads, tq), rq)
        w_hm = jnp.broadcast_to(w_hm.reshape(db, iheads * rq, 1), (db, iheads * rq, LANES))
        kin_t = jnp.swapaxes(pad_rows(smb[:, :idim].reshape(db, tq, idim), ps), 1, 2)
        topk_s = min(TOPK_MAX, (past + tq) // 4)
        past_keys, snew, thr = _sample_select(page_table, qi_hm, w_hm, kin_t, jnp.swapaxes(cache_kidx[l], 1, 2),
                                              pages=math.gcd(n_pages, 16), tq=tq, rq=rq, iheads=iheads, topk=topk_s)
        ksel, vsel, pos, cnt = _sc_gather(past_keys, thr, page_table, cache_k[l].reshape(n_pool * ps, heads, hd),
                                          cache_v[l].reshape(n_pool * ps, heads, hd), tq=tq, topk=topk_s, ps=ps)
        pos_rows = jnp.repeat(pos, heads, axis=1).reshape(ms, 1, topk_s * heads)
        snew_rows = jnp.repeat(snew[:, :tq, :ps // heads], heads, axis=2).reshape(ms, 1, ps)
        attn = _sample_attn_sel(cnt, rel_bias.T, q.reshape(ms, heads, hd), ksel.reshape(ms * topk_s * heads, hd),
                                vsel.reshape(ms * topk_s * heads, hd), pos_rows, snew_rows, thr,
                                pad_rows(kb.reshape(db, tq * heads, hd), ps), pad_rows(vb.reshape(db, tq * heads, hd), ps),
                                heads=heads, hd=hd, topk=topk_s, tq=tq, past=past)
        attn = attn.reshape(ms, aw).astype(BF16)
        x1 = _mix(xs, g_mix, attn, c_in, w_gate, wpa, wpb, wo, tm=ms)
        xs_next = _mlp(x1, g_mlp, gf, w1, w2, tm=ms, tf=_tile(w1.shape[1], 1024), final=l == depth - 1)
        outs["ks"].append(k.reshape(db, tq, heads, hd))
        outs["vs"].append(v.reshape(db, tq, heads, hd))
        outs["kis"].append(sm[:, :idim].reshape(db, tq, idim))
        outs["ss"].append(u_all.reshape(db, tq, cw)[:, tq - 2:])
        xp, xs = xp_next, xs_next

    st = {k: jnp.stack(v) for k, v in outs.items()}
    return (xp.reshape(batch, seq, d), xs.reshape(db, tq, d), st["kp"], st["vp"], st["kip"], st["sp"],
            st["ks"], st["vs"], st["kis"], st["ss"])
```

```python
import functools
import math

import jax
import jax.numpy as jnp
import numpy as np
from jax import lax
from jax.experimental import pallas as pl
from jax.experimental.pallas import tpu as pltpu
from jax.experimental.pallas import tpu_sc as plsc

F32 = jnp.float32
BF16 = jnp.bfloat16
I32 = jnp.int32

TOPK_MAX = 256
MAX_DISTANCE = 128
EPS = 1e-6

LANES = 128
SUBLANES = 8
V7X_SCOPED_VMEM_BYTES = 60000 * 1024
SC_LANES = 16
V7X_SC_CORES = 2
V7X_SC_SUBCORES = 16

LOG2E = math.log2(math.e)
INT_MIN = np.int32(-2 ** 31)
NEG_BIG = -1e30


def _cparams(*sem):
    return pltpu.CompilerParams(dimension_semantics=sem, vmem_limit_bytes=V7X_SCOPED_VMEM_BYTES)


def _resident(shape):
    nd = len(shape)
    return pl.BlockSpec(shape, lambda *_: (0,) * nd, pipeline_mode=pl.Buffered(1))


def _rms_bf16(x, g):
    y = x * lax.rsqrt(jnp.mean(x * x, axis=-1, keepdims=True) + EPS)
    return (y * g).astype(BF16)


def _dot(a, b):
    return jnp.dot(a, b, preferred_element_type=F32)


def _dot_nt(a, b):
    return lax.dot_general(a, b, (((1,), (1,)), ((), ())), preferred_element_type=F32)


def _ordered_key(x):
    b = lax.bitcast_convert_type(x, I32)
    return b ^ ((b >> 31) & np.int32(0x7FFFFFFF))


def _bucket(n, n_buckets):
    n = jnp.maximum(n, 0)
    me = n_buckets // 2
    nf = jnp.maximum(n, me).astype(F32)
    large = me + (jnp.log(nf / me) / math.log(MAX_DISTANCE / me) * (n_buckets - me)).astype(I32)
    large = jnp.minimum(large, n_buckets - 1)
    return jnp.where(n < me, n, large)


def _bias_of_bucket(bucket, relb_ref, h, n_buckets):
    acc = jnp.zeros(bucket.shape, F32)
    for bkt in range(n_buckets):
        acc = jnp.where(bucket == bkt, relb_ref[bkt, h], acc)
    return acc


def _fold_lanes(x):
    acc = x[:, 0:LANES]
    for t in range(1, x.shape[1] // LANES):
        acc = acc + x[:, t * LANES:(t + 1) * LANES]
    return acc


def _fold_rows(x, op=jnp.add):
    parts = [x[t * SUBLANES:(t + 1) * SUBLANES, :] for t in range(x.shape[0] // SUBLANES)]
    while len(parts) > 1:
        parts = [op(parts[t], parts[t + 1]) for t in range(0, len(parts) - 1, 2)] + parts[len(parts) & ~1:]
    return parts[0]


def _kth_threshold(count_ge, shape, axis, k, trips):
    def bit_body(t, uthr):
        cand = uthr | jnp.left_shift(np.int32(1), 31 - t)
        cnt = jnp.sum(count_ge(cand ^ INT_MIN), axis=axis, keepdims=True)
        return jnp.where(cnt >= k, cand, uthr)

    uthr = lax.fori_loop(0, trips, bit_body, jnp.zeros(shape, I32))
    return jnp.maximum(uthr ^ INT_MIN, INT_MIN + 1)


def _flash_update(lg, v, m_ref, l_ref, acc_ref, h, hd):
    reps = lg.shape[1] // LANES
    m_prev = m_ref[h]
    m_new = jnp.maximum(m_prev, jnp.max(lg, axis=1, keepdims=True))
    p = jnp.exp2(lg - jnp.tile(m_new, (1, reps)))
    alpha = jnp.exp2(m_prev - m_new)
    l_ref[h] = alpha * l_ref[h] + jnp.sum(p, axis=1, keepdims=True)
    m_ref[h] = m_new
    hs = slice(h * hd, (h + 1) * hd)
    acc_ref[:, hs] = acc_ref[:, hs] * jnp.tile(alpha, (1, hd // LANES)) + _dot(p.astype(BF16), v)


def _proj_attn_kernel(x_ref, g_ref, w_ref, ws_ref, q_ref, k_ref, v_ref, kb_ref, vb_ref, qi_ref,
                      sm_ref, smb_ref, *, aw, iw, hd, tm, qscale):
    heads = aw // hd
    hb = _rms_bf16(x_ref[...], g_ref[...])
    q_ref[...] = (_dot_nt(hb, w_ref[0:aw, :]) * qscale).astype(BF16)
    for o_ref, ob_ref, c0 in ((k_ref, kb_ref, aw), (v_ref, vb_ref, 2 * aw)):
        kv = _dot_nt(hb, w_ref[c0:c0 + aw, :])
        ob_ref[...] = kv.astype(BF16)
        for h in range(heads):
            o_ref[pl.ds(h, tm, stride=heads), :] = kv[:, h * hd:(h + 1) * hd]
    qi_ref[...] = _dot_nt(hb, w_ref[3 * aw:3 * aw + iw, :]).astype(BF16)
    sm = _dot_nt(hb, ws_ref[...])
    sm_ref[...] = sm
    smb_ref[...] = sm.astype(BF16)


def _proj_attn(x2d, g, w_attn, w_small, *, aw, iw, hd, tm):
    m, d = x2d.shape
    wt = w_attn.shape[0]
    heads = aw // hd
    assert hd == LANES
    row = lambda width: pl.BlockSpec((tm, width), lambda i: (i, 0))
    kv_spec = pl.BlockSpec((tm * heads, hd), lambda i: (i, 0))
    kv_shape = jax.ShapeDtypeStruct((m * heads, hd), F32)
    outs = [(aw, BF16), None, None, (aw, BF16), (aw, BF16), (iw, BF16), (LANES, F32), (LANES, BF16)]
    return pl.pallas_call(
        functools.partial(_proj_attn_kernel, aw=aw, iw=iw, hd=hd, tm=tm, qscale=hd ** -0.5 * LOG2E),
        grid=(m // tm,),
        in_specs=[row(d), _resident((1, d)), _resident((wt, d)), _resident((LANES, d))],
        out_specs=[kv_spec if o is None else row(o[0]) for o in outs],
        out_shape=[kv_shape if o is None else jax.ShapeDtypeStruct((m, o[0]), o[1]) for o in outs],
        compiler_params=_cparams("arbitrary"),
        name="proj_attn",
    )(x2d, g, w_attn, w_small)


def _proj_conv_kernel(*refs, cw, tm, seq, tiles_per_seq, tail):
    if tiles_per_seq:
        x_ref, g_ref, w_ref, cwt_ref, c_ref, ut_ref, carry_ref = refs
    else:
        x_ref, g_ref, w_ref, cwt_ref, p1_ref, p2_ref, c_ref, ut_ref = refs
    hb = _rms_bf16(x_ref[...], g_ref[...])
    cx = _dot_nt(hb, w_ref[0:cw, :])
    cb = _dot_nt(hb, w_ref[cw:2 * cw, :])
    cc = _dot_nt(hb, w_ref[2 * cw:3 * cw, :])
    u = cc * cx
    r = lax.broadcasted_iota(I32, (tm, cw), 0)
    um1 = pltpu.roll(u, 1, 0)
    um2 = pltpu.roll(u, 2, 0)
    if tiles_per_seq:
        @pl.when(pl.program_id(0) % tiles_per_seq == 0)
        def _():
            carry_ref[...] = jnp.zeros_like(carry_ref)
        prev1 = jnp.broadcast_to(carry_ref[SUBLANES - 1:SUBLANES, :], (tm, cw))
        prev2 = jnp.broadcast_to(carry_ref[SUBLANES - 2:SUBLANES - 1, :], (tm, cw))
        um1 = jnp.where(r == 0, prev1, um1)
        um2 = jnp.where(r == 0, prev2, jnp.where(r == 1, prev1, um2))
        carry_ref[...] = u[tm - SUBLANES:tm, :]
    else:
        assert seq & (seq - 1) == 0
        t = r & (seq - 1)
        um1 = jnp.where(t >= 1, um1, p1_ref[...])
        um2 = jnp.where(t >= 2, um2, p2_ref[...])
    y = cwt_ref[0:1, :] * um2 + cwt_ref[1:2, :] * um1 + cwt_ref[2:3, :] * u
    c_ref[...] = (cb * y).astype(BF16)
    ut_ref[...] = u[tm - tail:tm, :]


def _proj_conv(x2d, g, w_conv, conv_w, prev, *, seq, tm):
    m, d = x2d.shape
    cw = conv_w.shape[1]
    assert conv_w.shape[0] == 3
    row = lambda width: pl.BlockSpec((tm, width), lambda i: (i, 0))
    in_specs = [row(d), _resident((1, d)), _resident((3 * cw, d)), _resident((3, cw))]
    args = [x2d, g, w_conv, conv_w]
    if prev is None:
        assert seq % tm == 0
        tiles_per_seq, tail = seq // tm, SUBLANES
        scratch = [pltpu.VMEM((SUBLANES, cw), F32)]
    else:
        assert tm % seq == 0 and seq >= 2
        tiles_per_seq, tail = 0, tm
        scratch = []
        in_specs += [row(cw), row(cw)]
        args += list(prev)
    return pl.pallas_call(
        functools.partial(_proj_conv_kernel, cw=cw, tm=tm, seq=seq, tiles_per_seq=tiles_per_seq, tail=tail),
        grid=(m // tm,),
        in_specs=in_specs,
        out_specs=[row(cw), pl.BlockSpec((tail, cw), lambda i: (i, 0))],
        out_shape=[jax.ShapeDtypeStruct((m, cw), BF16), jax.ShapeDtypeStruct((m // tm * tail, cw), F32)],
        scratch_shapes=scratch,
        compiler_params=_cparams("arbitrary"),
        name="proj_conv",
    )(*args)


def _bias_tiles_kernel(relb_ref, o_ref, *, tb, n_buckets):
    kind = pl.program_id(0)
    h = pl.program_id(1)
    qry = lax.broadcasted_iota(I32, (tb, tb), 0)
    key = lax.broadcasted_iota(I32, (tb, tb), 1)
    bias = _bias_of_bucket(_bucket(kind * tb + qry - key, n_buckets), relb_ref, h, n_buckets)
    o_ref[...] = (bias - relb_ref[n_buckets - 1, h]) * LOG2E


def _bias_tiles(rel_bias, *, tb):
    n_buckets, heads = rel_bias.shape
    return pl.pallas_call(
        functools.partial(_bias_tiles_kernel, tb=tb, n_buckets=n_buckets),
        grid=(2, heads),
        in_specs=[pl.BlockSpec(memory_space=pltpu.SMEM)],
        out_specs=pl.BlockSpec((None, None, tb, tb), lambda a, h: (a, h, 0, 0)),
        out_shape=jax.ShapeDtypeStruct((2, heads, tb, tb), F32),
        compiler_params=_cparams("arbitrary", "arbitrary"),
        name="bias_tiles",
    )(rel_bias)


def _attn_prompt_kernel(qi_ref, sm_ref, smb_ref, q_ref, kb_ref, vb_ref, bias_ref, o_ref,
                        skey_ref, thr_ref, wt_ref, madd_ref, m_ref, l_ref, acc_ref,
                        *, tb, heads, hd, iheads, idim, topk):
    i = pl.program_id(1)
    wscale = idim ** -0.5 * iheads ** -0.5

    m_ref[...] = jnp.full(m_ref.shape, NEG_BIG, F32)
    l_ref[...] = jnp.zeros(l_ref.shape, F32)
    acc_ref[...] = jnp.zeros(acc_ref.shape, F32)
    wt_ref[...] = sm_ref[...].T * wscale

    def chunk(j):
        return pl.ds(pl.multiple_of(j * tb, tb), tb)

    def score_chunk(j, diag):
        kic = smb_ref[chunk(j), 0:idim]
        acc = jnp.zeros((tb, tb), F32)
        for h in range(iheads):
            s = _dot_nt(kic, qi_ref[:, h * idim:(h + 1) * idim])
            acc = acc + jnp.maximum(s, 0.0) * wt_ref[idim + h:idim + h + 1, :]
        key = _ordered_key(acc)
        if diag:
            kpos = lax.broadcasted_iota(I32, (tb, tb), 0)
            qpos = lax.broadcasted_iota(I32, (tb, tb), 1)
            key = jnp.where(kpos > qpos, INT_MIN, key)
        skey_ref[chunk(j), :] = key

    def score_body(j, carry):
        score_chunk(j, False)
        return carry

    lax.fori_loop(0, i, score_body, 0)
    score_chunk(i, True)

    def count_ge(scand):
        def body(j, cnt):
            return cnt + _fold_rows(jnp.where(skey_ref[chunk(j), :] >= scand[0:1, :], 1.0, 0.0))

        return lax.fori_loop(0, i + 1, body, jnp.zeros((SUBLANES, tb), F32))

    trips = jnp.where((i + 1) * tb <= topk, 0, 32)
    thr_ref[...] = _kth_threshold(count_ge, (SUBLANES, tb), 0, float(topk), trips)

    def attend_chunk(j, kind):
        madd_ref[...] = jnp.where(skey_ref[chunk(j), :] >= thr_ref[0:1, :], 0.0, NEG_BIG).T
        for h in range(heads):
            hs = slice(h * hd, (h + 1) * hd)
            lg = _dot_nt(q_ref[:, hs], kb_ref[chunk(j), hs]) + madd_ref[...]
            if kind is not None:
                lg = lg + bias_ref[kind, h]
            _flash_update(lg, vb_ref[chunk(j), hs], m_ref, l_ref, acc_ref, h, hd)

    def attend_body(j, carry):
        attend_chunk(j, None)
        return carry

    lax.fori_loop(0, jnp.maximum(i - 1, 0), attend_body, 0)

    @pl.when(i >= 1)
    def _():
        attend_chunk(i - 1, 1)

    attend_chunk(i, 0)

    for h in range(heads):
        hs = slice(h * hd, (h + 1) * hd)
        o_ref[:, hs] = (acc_ref[:, hs] / jnp.tile(l_ref[h], (1, hd // LANES))).astype(BF16)


def _attn_prompt(bias_tiles, qi, sm, smb, q, kb, vb, *, batch, seq, tb, heads, hd, iheads, idim, topk):
    m, aw = q.shape
    iw = qi.shape[1]
    nq = seq // tb
    assert tb >= MAX_DISTANCE and tb % LANES == 0 and hd % LANES == 0
    qrow = lambda width: pl.BlockSpec((tb, width), lambda b, i: (b * nq + i, 0))
    seqblk = lambda width: pl.BlockSpec((seq, width), lambda b, i: (b, 0))
    return pl.pallas_call(
        functools.partial(_attn_prompt_kernel, tb=tb, heads=heads, hd=hd, iheads=iheads, idim=idim, topk=topk),
        grid=(batch, nq),
        in_specs=[qrow(iw), qrow(LANES), seqblk(LANES), qrow(aw), seqblk(aw), seqblk(aw),
                  _resident(bias_tiles.shape)],
        out_specs=qrow(aw),
        out_shape=jax.ShapeDtypeStruct((m, aw), BF16),
        scratch_shapes=[pltpu.VMEM((seq, tb), I32), pltpu.VMEM((SUBLANES, tb), I32),
                        pltpu.VMEM((LANES, tb), F32), pltpu.VMEM((tb, tb), F32),
                        pltpu.VMEM((heads, tb, LANES), F32), pltpu.VMEM((heads, tb, LANES), F32),
                        pltpu.VMEM((tb, aw), F32)],
        compiler_params=_cparams("arbitrary", "arbitrary"),
        name="attn_prompt",
    )(qi, sm, smb, q, kb, vb, bias_tiles)


def _sample_select_kernel(pt_ref, qi_ref, w_ref, kin_ref, *rest, pages, ps, nc, iheads, idim, tq, rq, topk):
    page_refs = rest[:pages]
    past_ref, new_ref, thr_ref, row_ref = rest[pages:]
    b = pl.program_id(0)
    c = pl.program_id(1)
    db = pl.num_programs(0)
    ch = pages * ps
    wscale = idim ** -0.5 * iheads ** -0.5
    rows_b = pl.ds(pl.multiple_of(b * rq, rq), rq)

    def score(keys_t):
        s = _dot(qi_ref[...], keys_t)
        t = jnp.maximum(s, 0.0) * (w_ref[:, 0:1] * wscale)
        acc = t[0:rq]
        for h in range(1, iheads):
            acc = acc + t[h * rq:(h + 1) * rq]
        return _ordered_key(acc)

    @pl.when(c < nc)
    def _():
        for p in range(pages):
            key = score(page_refs[p][...].astype(BF16))
            row_ref[rows_b, pl.ds(pl.multiple_of(c * ch + p * ps, LANES), ps)] = key
            for t in range(ps // LANES):
                for j in range(tq):
                    past_ref[j, p * (ps // LANES) + t:p * (ps // LANES) + t + 1, :] = \
                        key[j:j + 1, t * LANES:(t + 1) * LANES]

    @pl.when(c == nc)
    def _():
        key = score(kin_ref[...])
        j = lax.broadcasted_iota(I32, (rq, ps), 0)
        n = lax.broadcasted_iota(I32, (rq, ps), 1)
        key = jnp.where((n <= j) & (n < tq), key, INT_MIN)
        new_ref[...] = key
        row_ref[rows_b, nc * ch:nc * ch + ps] = key

    @pl.when((c == nc) & (b == db - 1))
    def _():
        n_rows = row_ref.shape[0]

        def count_ge(scand):
            sk = row_ref[...]
            return _fold_lanes(jnp.where(sk >= jnp.tile(scand, (1, sk.shape[1] // LANES)), 1.0, 0.0))

        thr = _kth_threshold(count_ge, (n_rows, LANES), 1, float(topk), 32)
        new_keys = row_ref[:, nc * ch:nc * ch + ps]
        gt_new = jnp.sum(jnp.where(new_keys > jnp.tile(thr, (1, ps // LANES)), 1.0, 0.0), axis=1, keepdims=True)
        budget = jnp.broadcast_to(topk - gt_new, (n_rows, LANES)).astype(I32)
        thr_ref[...] = jnp.zeros(thr_ref.shape, I32)
        for bb in range(n_rows // rq):
            for jj in range(tq):
                thr_ref[bb * tq + jj, 0:1, :] = thr[bb * rq + jj:bb * rq + jj + 1, :]
                thr_ref[bb * tq + jj, 1:2, :] = budget[bb * rq + jj:bb * rq + jj + 1, :]


def _sample_select(page_table, qi_hm, w_hm, kin_t, cache_kidx_t, *, pages, tq, rq, iheads, topk):
    db, n_pages = page_table.shape
    _, idim, ps = cache_kidx_t.shape
    nc = n_pages // pages
    ch = pages * ps
    assert ps % LANES == 0 and (ch // LANES) % SUBLANES == 0
    page_spec = lambda p: pl.BlockSpec(
        (None, idim, ps), lambda b, c, pt: (pt[b, jnp.minimum(c, nc - 1) * pages + p], 0, 0))
    per_b = lambda shape: pl.BlockSpec((None,) + shape, lambda b, c, pt: (b, 0, 0))
    return pl.pallas_call(
        functools.partial(_sample_select_kernel, pages=pages, ps=ps, nc=nc, iheads=iheads, idim=idim, tq=tq, rq=rq,
                          topk=topk),
        grid_spec=pltpu.PrefetchScalarGridSpec(
            num_scalar_prefetch=1,
            grid=(db, nc + 1),
            in_specs=[per_b((iheads * rq, idim)), per_b((iheads * rq, LANES)), per_b((idim, ps))]
            + [page_spec(p) for p in range(pages)],
            out_specs=[pl.BlockSpec((tq, ch // LANES, LANES), lambda b, c, pt: (b, jnp.minimum(c, nc - 1), 0)),
                       per_b((rq, ps)),
                       pl.BlockSpec((db * tq, SUBLANES, LANES), lambda b, c, pt: (0, 0, 0))],
            scratch_shapes=[pltpu.VMEM((db * rq, nc * ch + ps), I32)],
        ),
        out_shape=[jax.ShapeDtypeStruct((db * tq, n_pages * ps // LANES, LANES), I32),
                   jax.ShapeDtypeStruct((db, rq, ps), I32),
                   jax.ShapeDtypeStruct((db * tq, SUBLANES, LANES), I32)],
        compiler_params=_cparams("arbitrary", "arbitrary"),
        name="sample_select",
    )(page_table, qi_hm, w_hm, kin_t, *([cache_kidx_t] * pages))


def _sc_gather_kernel(past_hbm, thr_hbm, pt_hbm, ck_hbm, cv_hbm, ksel_hbm, vsel_hbm, pos_hbm, cnt_hbm,
                      row_v, thr_v, pt_v, idx_v, phys_v, rows_v, cnt_v, sem,
                      *, nq, tq, topk, ps, n_cores, rows_per_copy):
    wid = lax.axis_index("s") * n_cores + lax.axis_index("c")

    @pl.when(wid < nq)
    def _():
        pltpu.sync_copy(past_hbm.at[wid], row_v)
        pltpu.sync_copy(thr_hbm.at[wid], thr_v)
        pltpu.sync_copy(pt_hbm.at[wid // tq], pt_v)
        thr = thr_v[0, pl.ds(0, SC_LANES)]
        budget = thr_v[1, pl.ds(0, SC_LANES)]
        lane = lax.iota(I32, SC_LANES)
        zero = jnp.zeros((SC_LANES,), I32)
        for t in range(idx_v.shape[0] // SC_LANES):
            idx_v[pl.ds(t * SC_LANES, SC_LANES)] = zero

        def compact(pred):
            def body(r, cnt):
                for t in range(LANES // SC_LANES):
                    x = row_v[r, pl.ds(t * SC_LANES, SC_LANES)]
                    m = pred(x, cnt)
                    rank = plsc.cumsum(jnp.where(m, 1, 0).astype(I32))
                    plsc.store_scatter(idx_v, [cnt + rank - 1], lane + (r * LANES + t * SC_LANES), mask=m)
                    cnt = cnt + plsc.all_reduce_population_count(m)
                return cnt
            return body

        cnt = lax.fori_loop(0, row_v.shape[0], compact(lambda x, cnt: x > thr), zero)
        cnt = lax.fori_loop(0, row_v.shape[0], compact(lambda x, cnt: (x == thr) & (cnt < budget)), cnt)
        cnt_v[...] = jnp.minimum(cnt, budget)
        pltpu.sync_copy(cnt_v, cnt_hbm.at[wid])
        pltpu.sync_copy(idx_v.at[pl.ds(0, topk)], pos_hbm.at[wid])

        shift = ps.bit_length() - 1
        for t in range(topk // SC_LANES):
            pos = idx_v[pl.ds(t * SC_LANES, SC_LANES)]
            page = plsc.load_gather(pt_v, [lax.shift_right_logical(pos, shift)])
            phys_v[pl.ds(t * SC_LANES, SC_LANES)] = page * ps + (pos & (ps - 1))
        for g in range(topk // rows_per_copy):
            sel = phys_v.at[pl.ds(g * rows_per_copy, rows_per_copy)]
            dst = pl.ds(wid * topk + g * rows_per_copy, rows_per_copy)
            for src_hbm, dst_hbm in ((ck_hbm, ksel_hbm), (cv_hbm, vsel_hbm)):
                pltpu.async_copy(src_hbm.at[sel], rows_v, sem).wait()
                pltpu.sync_copy(rows_v, dst_hbm.at[dst])


def _sc_gather(past_keys, thr, page_table, cache_k, cache_v, *, tq, topk, ps):
    nq, key_rows, _ = past_keys.shape
    _, heads, hd = cache_k.shape
    assert ps & (ps - 1) == 0 and topk % SC_LANES == 0
    rows_per_copy = 64
    assert topk % rows_per_copy == 0
    mesh = plsc.VectorSubcoreMesh(core_axis_name="c", subcore_axis_name="s", num_cores=V7X_SC_CORES,
                                  num_subcores=V7X_SC_SUBCORES)
    assert nq <= V7X_SC_CORES * V7X_SC_SUBCORES
    sel_shape = jax.ShapeDtypeStruct((nq * topk, heads, hd), F32)
    return pl.kernel(
        functools.partial(_sc_gather_kernel, nq=nq, tq=tq, topk=topk, ps=ps, n_cores=V7X_SC_CORES,
                          rows_per_copy=rows_per_copy),
        out_type=[sel_shape, sel_shape, jax.ShapeDtypeStruct((nq, topk), I32),
                  jax.ShapeDtypeStruct((nq, SC_LANES), I32)],
        mesh=mesh,
        scratch_types=[pltpu.VMEM((key_rows, LANES), I32), pltpu.VMEM((SUBLANES, LANES), I32),
                       pltpu.VMEM((page_table.shape[1],), I32), pltpu.VMEM((topk + SC_LANES,), I32),
                       pltpu.VMEM((topk,), I32), pltpu.VMEM((rows_per_copy, heads, hd), F32),
                       pltpu.VMEM((SC_LANES,), I32), pltpu.SemaphoreType.DMA],
        compiler_params=pltpu.CompilerParams(needs_layout_passes=False),
        name="sc_select_gather",
    )(past_keys, thr, page_table, cache_k, cache_v)


def _sample_attn_sel_kernel(cnt_ref, relbt_ref, q_ref, ksel_ref, vsel_ref, pos_ref, snew_ref, thr_ref, kn_ref, vn_ref,
                            o_ref, *, heads, tq, past, n_buckets):
    w = pl.program_id(0)
    j = w % tq
    q = q_ref[...]

    def head_bias(dist):
        bucket = _bucket(dist, n_buckets)
        acc = jnp.zeros(bucket.shape, F32)
        for bkt in range(n_buckets):
            acc = jnp.where(bucket == bkt, relbt_ref[:, bkt:bkt + 1], acc)
        return acc * LOG2E

    def logits(keys):
        lg = _dot_nt(q, keys)
        head = lax.broadcasted_iota(I32, lg.shape, 0)
        col = lax.broadcasted_iota(I32, lg.shape, 1)
        return lg, (col & (heads - 1)) == head, col

    lg, own, col = logits(ksel_ref[...].astype(BF16))
    keep = own & (col < cnt_ref[w, 0] * heads)
    lg = jnp.where(keep, lg + head_bias(jnp.broadcast_to(past + j - pos_ref[...], lg.shape)), NEG_BIG)
    lgn, own, col = logits(kn_ref[...])
    keep = own & (snew_ref[...] >= thr_ref[0:1, :])
    lgn = jnp.where(keep, lgn + head_bias(j - lax.shift_right_logical(col, heads.bit_length() - 1)), NEG_BIG)

    m = jnp.maximum(jnp.max(lg, axis=1, keepdims=True), jnp.max(lgn, axis=1, keepdims=True))
    p = jnp.exp2(lg - m)
    pn = jnp.exp2(lgn - m)
    denom = jnp.sum(p, axis=1, keepdims=True) + jnp.sum(pn, axis=1, keepdims=True)
    acc = _dot(p.astype(BF16), vsel_ref[...].astype(BF16)) + _dot(pn.astype(BF16), vn_ref[...])
    o_ref[...] = acc / denom


def _sample_attn_sel(cnt, rel_bias_t, q, ksel, vsel, pos, snew, thr, kn, vn, *, heads, hd, topk, tq, past):
    nq = q.shape[0]
    ps = kn.shape[1]
    assert ps == LANES and hd == LANES and heads & (heads - 1) == 0
    per_q = lambda shape: pl.BlockSpec((None,) + shape, lambda w, cnt: (w, 0, 0))
    per_b = lambda shape: pl.BlockSpec((None,) + shape, lambda w, cnt: (w // tq, 0, 0))
    sel_spec = pl.BlockSpec((topk * heads, hd), lambda w, cnt: (w, 0))
    return pl.pallas_call(
        functools.partial(_sample_attn_sel_kernel, heads=heads, tq=tq, past=past, n_buckets=rel_bias_t.shape[1]),
        grid_spec=pltpu.PrefetchScalarGridSpec(
            num_scalar_prefetch=1,
            grid=(nq,),
            in_specs=[pl.BlockSpec(rel_bias_t.shape, lambda w, cnt: (0, 0)), per_q((heads, hd)), sel_spec, sel_spec,
                      per_q((1, topk * heads)), per_q((1, ps)), per_q((SUBLANES, LANES)), per_b((ps, hd)),
                      per_b((ps, hd))],
            out_specs=per_q((heads, hd)),
        ),
        out_shape=jax.ShapeDtypeStruct((nq, heads, hd), F32),
        compiler_params=_cparams("arbitrary"),
        name="sample_attn_sel",
    )(cnt, rel_bias_t, q, ksel, vsel, pos, snew, thr, kn, vn)


def _mix_kernel(x_ref, g_ref, a_ref, c_ref, wg_ref, wpa_ref, wpb_ref, wo_ref, o_ref, *, d):
    x = x_ref[...]
    hb = _rms_bf16(x, g_ref[...])
    a = _dot(a_ref[...], wpa_ref[...])
    m = jax.nn.sigmoid(_dot_nt(hb, wg_ref[0:d, :])) * a
    c = _dot(c_ref[...], wpb_ref[...])
    m = m + jax.nn.sigmoid(_dot_nt(hb, wg_ref[d:2 * d, :])) * c
    o_ref[...] = x + _dot(m.astype(BF16), wo_ref[...])


def _mix(x2d, g, attn, c_in, w_gate, w_pa, w_pb, w_o, *, tm):
    m, d = x2d.shape
    row = lambda width: pl.BlockSpec((tm, width), lambda i: (i, 0))
    return pl.pallas_call(
        functools.partial(_mix_kernel, d=d),
        grid=(m // tm,),
        in_specs=[row(d), _resident((1, d)), row(attn.shape[1]), row(c_in.shape[1]), _resident(w_gate.shape),
                  _resident(w_pa.shape), _resident(w_pb.shape), _resident(w_o.shape)],
        out_specs=row(d),
        out_shape=jax.ShapeDtypeStruct((m, d), F32),
        compiler_params=_cparams("arbitrary"),
        name="mix_out",
    )(x2d, g, attn, c_in, w_gate, w_pa, w_pb, w_o)


def _mlp_kernel(x_ref, g_ref, gf_ref, w1_ref, w2_ref, y_ref, h_ref, *, final):
    f = pl.program_id(1)

    @pl.when(f == 0)
    def _():
        h_ref[...] = _rms_bf16(x_ref[...], g_ref[...])
        y_ref[...] = jnp.zeros(y_ref.shape, F32)

    t = jnp.square(jnp.maximum(_dot(h_ref[...], w1_ref[...]), 0.0))
    y_ref[...] += _dot(t.astype(BF16), w2_ref[...])

    @pl.when(f == pl.num_programs(1) - 1)
    def _():
        x2 = x_ref[...] + y_ref[...]
        if final:
            x2 = x2 * lax.rsqrt(jnp.mean(x2 * x2, axis=-1, keepdims=True) + EPS) * gf_ref[...]
        y_ref[...] = x2


def _mlp(x2d, g, gf, w1, w2, *, tm, tf, final):
    m, d = x2d.shape
    ff = w1.shape[1]
    return pl.pallas_call(
        functools.partial(_mlp_kernel, final=final),
        grid=(m // tm, ff // tf),
        in_specs=[pl.BlockSpec((tm, d), lambda i, f: (i, 0)), _resident((1, d)), _resident((1, d)),
                  pl.BlockSpec((d, tf), lambda i, f: (0, f)), pl.BlockSpec((tf, d), lambda i, f: (f, 0))],
        out_specs=pl.BlockSpec((tm, d), lambda i, f: (i, 0)),
        out_shape=jax.ShapeDtypeStruct((m, d), F32),
        scratch_shapes=[pltpu.VMEM((tm, d), BF16)],
        compiler_params=_cparams("arbitrary", "arbitrary"),
        name="mlp",
    )(x2d, g, gf, w1, w2)


def _tile(m, cap):
    return min(m, cap)


def kernel(x_prompt, x_sample, cache_k, cache_v, cache_kidx, state_conv, page_table, rel_bias, norm_mix_g, w_in,
           conv_w, w_pa, w_pb, w_o, norm_mlp_g, w_mlp_in, w_mlp_out, norm_final_g):
    batch, seq, d = x_prompt.shape
    db, tq, _ = x_sample.shape
    depth, n_pool, ps, heads, hd = cache_k.shape
    idim = cache_kidx.shape[-1]
    cw = conv_w.shape[-1]
    aw = heads * hd
    n_in = w_in.shape[-1]
    iheads = (n_in - 3 * aw - idim - 3 * cw - 2 * d) // (idim + 1)
    iw = iheads * idim
    assert 3 * aw + iw + idim + iheads + 3 * cw + 2 * d == n_in and idim + iheads <= LANES
    n_pages = page_table.shape[1]
    past = n_pages * ps
    rq = SUBLANES
    assert tq <= rq

    mp, ms = batch * seq, db * tq
    xp = x_prompt.reshape(mp, d)
    xs = x_sample.reshape(ms, d)
    tb = _tile(seq, 256)
    pages = math.gcd(n_pages, 8)
    o_small = 3 * aw + iw
    o_conv = o_small + idim + iheads
    o_gate = o_conv + 3 * cw
    gf = norm_final_g.reshape(1, d)
    bias_tiles = _bias_tiles(rel_bias, tb=tb)

    outs = {k: [] for k in ("kp", "vp", "kip", "sp", "ks", "vs", "kis", "ss")}
    for l in range(depth):
        wl = jnp.swapaxes(w_in[l], 0, 1)
        w_attn = wl[:o_small].astype(BF16)
        w_small = jnp.pad(wl[o_small:o_conv], ((0, LANES - idim - iheads), (0, 0))).astype(BF16)
        w_conv = wl[o_conv:o_gate].astype(BF16)
        w_gate = wl[o_gate:].astype(BF16)
        wpa, wpb, wo = w_pa[l].astype(BF16), w_pb[l].astype(BF16), w_o[l].astype(BF16)
        w1, w2 = w_mlp_in[l].astype(BF16), w_mlp_out[l].astype(BF16)
        g_mix = norm_mix_g[l].reshape(1, d)
        g_mlp = norm_mlp_g[l].reshape(1, d)

        tm = _tile(seq, 512)
        q, k, v, kb, vb, qi, sm, smb = _proj_attn(xp, g_mix, w_attn, w_small, aw=aw, iw=iw, hd=hd, tm=tm)
        c_in, u_tail = _proj_conv(xp, g_mix, w_conv, conv_w[l], None, seq=seq, tm=_tile(seq, 512))
        attn = _attn_prompt(bias_tiles, qi, sm, smb, q, kb, vb, batch=batch, seq=seq, tb=tb,
                            heads=heads, hd=hd, iheads=iheads, idim=idim, topk=min(TOPK_MAX, seq // 4))
        x1 = _mix(xp, g_mix, attn, c_in, w_gate, wpa, wpb, wo, tm=_tile(mp, 256))
        xp_next = _mlp(x1, g_mlp, gf, w1, w2, tm=_tile(mp, 1024), tf=_tile(w1.shape[1], 512), final=l == depth - 1)
        outs["kp"].append(k.reshape(batch, seq, heads, hd))
        outs["vp"].append(v.reshape(batch, seq, heads, hd))
        outs["kip"].append(sm[:, :idim].reshape(batch, seq, idim))
        outs["sp"].append(u_tail.reshape(batch, -1, SUBLANES, cw)[:, -1, SUBLANES - 2:])

        q, k, v, kb, vb, qi, sm, smb = _proj_attn(xs, g_mix, w_attn, w_small, aw=aw, iw=iw, hd=hd, tm=ms)
        st = state_conv[l]
        zero = jnp.zeros((db, tq - 1, cw), F32)
        prev1 = jnp.concatenate([st[:, 1:2], zero], axis=1).reshape(ms, cw)
        prev2 = jnp.concatenate([st, zero[:, 1:]], axis=1).reshape(ms, cw)
        c_in, u_all = _proj_conv(xs, g_mix, w_conv, conv_w[l], (prev1, prev2), seq=tq, tm=ms)

        def pad_rows(a, n):
            return jnp.pad(a, ((0, 0), (0, n - a.shape[1])) + ((0, 0),) * (a.ndim - 2))

        qi_hm = pad_rows(qi.reshape(db, tq, iheads, idim).transpose(0, 2, 1, 3).reshape(db * iheads, tq, idim), rq)
        qi_hm = qi_hm.reshape(db, iheads * rq, idim)
        w_hm = pad_rows(sm[:, idim:idim + iheads].reshape(db, tq, iheads).transpose(0, 2, 1).reshape(db * iheads, tq), rq)
        w_hm = jnp.broadcast_to(w_hm.reshape(db, iheads * rq, 1), (db, iheads * rq, LANES))
        kin_t = jnp.swapaxes(pad_rows(smb[:, :idim].reshape(db, tq, idim), ps), 1, 2)
        topk_s = min(TOPK_MAX, (past + tq) // 4)
        past_keys, snew, thr = _sample_select(page_table, qi_hm, w_hm, kin_t, jnp.swapaxes(cache_kidx[l], 1, 2),
                                              pages=math.gcd(n_pages, 32), tq=tq, rq=rq, iheads=iheads, topk=topk_s)
        ksel, vsel, pos, cnt = _sc_gather(past_keys, thr, page_table, cache_k[l].reshape(n_pool * ps, heads, hd),
                                          cache_v[l].reshape(n_pool * ps, heads, hd), tq=tq, topk=topk_s, ps=ps)
        pos_rows = jnp.repeat(pos, heads, axis=1).reshape(ms, 1, topk_s * heads)
        snew_rows = jnp.repeat(snew[:, :tq, :ps // heads], heads, axis=2).reshape(ms, 1, ps)
        attn = _sample_attn_sel(cnt, rel_bias.T, q.reshape(ms, heads, hd), ksel.reshape(ms * topk_s * heads, hd),
                                vsel.reshape(ms * topk_s * heads, hd), pos_rows, snew_rows, thr,
                                pad_rows(kb.reshape(db, tq * heads, hd), ps), pad_rows(vb.reshape(db, tq * heads, hd), ps),
                                heads=heads, hd=hd, topk=topk_s, tq=tq, past=past)
        attn = attn.reshape(ms, aw).astype(BF16)
        x1 = _mix(xs, g_mix, attn, c_in, w_gate, wpa, wpb, wo, tm=ms)
        xs_next = _mlp(x1, g_mlp, gf, w1, w2, tm=ms, tf=_tile(w1.shape[1], 1024), final=l == depth - 1)
        outs["ks"].append(k.reshape(db, tq, heads, hd))
        outs["vs"].append(v.reshape(db, tq, heads, hd))
        outs["kis"].append(sm[:, :idim].reshape(db, tq, idim))
        outs["ss"].append(u_all.reshape(db, tq, cw)[:, tq - 2:])
        xp, xs = xp_next, xs_next

    st = {k: jnp.stack(v) for k, v in outs.items()}
    return (xp.reshape(batch, seq, d), xs.reshape(db, tq, d), st["kp"], st["vp"], st["kip"], st["sp"],
            st["ks"], st["vs"], st["kis"], st["ss"])
```

```python
import functools
import math

import jax
import jax.numpy as jnp
import numpy as np
from jax import lax
from jax.experimental import pallas as pl
from jax.experimental.pallas import tpu as pltpu
from jax.experimental.pallas import tpu_sc as plsc

F32 = jnp.float32
BF16 = jnp.bfloat16
I32 = jnp.int32

TOPK_MAX = 256
MAX_DISTANCE = 128
EPS = 1e-6

LANES = 128
SUBLANES = 8
V7X_SCOPED_VMEM_BYTES = 60000 * 1024
SC_LANES = 16
V7X_SC_CORES = 2
V7X_SC_SUBCORES = 16

LOG2E = math.log2(math.e)
INT_MIN = np.int32(-2 ** 31)
NEG_BIG = -1e30


def _cparams(*sem):
    return pltpu.CompilerParams(dimension_semantics=sem, vmem_limit_bytes=V7X_SCOPED_VMEM_BYTES)


def _resident(shape):
    nd = len(shape)
    return pl.BlockSpec(shape, lambda *_: (0,) * nd, pipeline_mode=pl.Buffered(1))


def _rms_bf16(x, g):
    y = x * lax.rsqrt(jnp.mean(x * x, axis=-1, keepdims=True) + EPS)
    return (y * g).astype(BF16)


def _dot(a, b):
    return jnp.dot(a, b, preferred_element_type=F32)


def _dot_nt(a, b):
    return lax.dot_general(a, b, (((1,), (1,)), ((), ())), preferred_element_type=F32)


def _ordered_key(x):
    b = lax.bitcast_convert_type(x, I32)
    return b ^ ((b >> 31) & np.int32(0x7FFFFFFF))


def _bucket(n, n_buckets):
    n = jnp.maximum(n, 0)
    me = n_buckets // 2
    nf = jnp.maximum(n, me).astype(F32)
    large = me + (jnp.log(nf / me) / math.log(MAX_DISTANCE / me) * (n_buckets - me)).astype(I32)
    large = jnp.minimum(large, n_buckets - 1)
    return jnp.where(n < me, n, large)


def _bias_of_bucket(bucket, relb_ref, h, n_buckets):
    acc = jnp.zeros(bucket.shape, F32)
    for bkt in range(n_buckets):
        acc = jnp.where(bucket == bkt, relb_ref[bkt, h], acc)
    return acc


def _fold_lanes(x):
    acc = x[:, 0:LANES]
    for t in range(1, x.shape[1] // LANES):
        acc = acc + x[:, t * LANES:(t + 1) * LANES]
    return acc


def _fold_rows(x, op=jnp.add):
    parts = [x[t * SUBLANES:(t + 1) * SUBLANES, :] for t in range(x.shape[0] // SUBLANES)]
    while len(parts) > 1:
        parts = [op(parts[t], parts[t + 1]) for t in range(0, len(parts) - 1, 2)] + parts[len(parts) & ~1:]
    return parts[0]


def _kth_threshold(count_ge, shape, axis, k, trips):
    def bit_body(t, uthr):
        cand = uthr | jnp.left_shift(np.int32(1), 31 - t)
        cnt = jnp.sum(count_ge(cand ^ INT_MIN), axis=axis, keepdims=True)
        return jnp.where(cnt >= k, cand, uthr)

    uthr = lax.fori_loop(0, trips, bit_body, jnp.zeros(shape, I32))
    return jnp.maximum(uthr ^ INT_MIN, INT_MIN + 1)


def _flash_update(lg, v, m_ref, l_ref, acc_ref, h, hd):
    reps = lg.shape[1] // LANES
    m_prev = m_ref[h]
    m_new = jnp.maximum(m_prev, jnp.max(lg, axis=1, keepdims=True))
    p = jnp.exp2(lg - jnp.tile(m_new, (1, reps)))
    alpha = jnp.exp2(m_prev - m_new)
    pv = _dot(p.astype(BF16), jnp.concatenate([v, jnp.ones((v.shape[0], LANES), BF16)], axis=1))
    l_ref[h] = alpha * l_ref[h] + pv[:, hd:]
    m_ref[h] = m_new
    hs = slice(h * hd, (h + 1) * hd)
    acc_ref[:, hs] = acc_ref[:, hs] * jnp.tile(alpha, (1, hd // LANES)) + pv[:, :hd]


def _stage_rows(w_hbm, w_ref, row0):
    @pl.when(pl.program_id(0) == 0)
    def _():
        pltpu.sync_copy(w_hbm.at[pl.ds(row0, w_ref.shape[0])], w_ref)


def _proj_attn_kernel(x_ref, g_ref, w_hbm, q_ref, k_ref, v_ref, kb_ref, vb_ref, qi_ref,
                      sm_ref, smb_ref, w_ref, ws_ref, *, aw, iw, hd, tm, n_small, qscale):
    heads = aw // hd
    _stage_rows(w_hbm, w_ref, 0)

    @pl.when(pl.program_id(0) == 0)
    def _():
        ws_ref[...] = jnp.zeros(ws_ref.shape, BF16)
        pltpu.sync_copy(w_hbm.at[pl.ds(w_ref.shape[0], n_small)], ws_ref.at[pl.ds(0, n_small)])

    hb = _rms_bf16(x_ref[...], g_ref[...])
    q_ref[...] = (_dot_nt(hb, w_ref[0:aw, :]) * qscale).astype(BF16)
    for o_ref, ob_ref, c0 in ((k_ref, kb_ref, aw), (v_ref, vb_ref, 2 * aw)):
        kv = _dot_nt(hb, w_ref[c0:c0 + aw, :])
        ob_ref[...] = kv.astype(BF16)
        for h in range(heads):
            o_ref[pl.ds(h, tm, stride=heads), :] = kv[:, h * hd:(h + 1) * hd]
    qi_ref[...] = _dot_nt(hb, w_ref[3 * aw:3 * aw + iw, :]).astype(BF16)
    sm = _dot_nt(hb, ws_ref[...])
    sm_ref[...] = sm
    smb_ref[...] = sm.astype(BF16)


def _proj_attn(x2d, g, w_all, *, aw, iw, n_small, hd, tm):
    m, d = x2d.shape
    wt = 3 * aw + iw
    heads = aw // hd
    assert hd == LANES and n_small <= LANES and wt % 16 == 0 and n_small % 16 == 0
    row = lambda width: pl.BlockSpec((tm, width), lambda i: (i, 0))
    kv_spec = pl.BlockSpec((tm * heads, hd), lambda i: (i, 0))
    kv_shape = jax.ShapeDtypeStruct((m * heads, hd), F32)
    outs = [(aw, BF16), None, None, (aw, BF16), (aw, BF16), (iw, BF16), (LANES, F32), (LANES, BF16)]
    return pl.pallas_call(
        functools.partial(_proj_attn_kernel, aw=aw, iw=iw, hd=hd, tm=tm, n_small=n_small,
                          qscale=hd ** -0.5 * LOG2E),
        grid=(m // tm,),
        in_specs=[row(d), _resident((1, d)), pl.BlockSpec(memory_space=pl.ANY)],
        out_specs=[kv_spec if o is None else row(o[0]) for o in outs],
        out_shape=[kv_shape if o is None else jax.ShapeDtypeStruct((m, o[0]), o[1]) for o in outs],
        scratch_shapes=[pltpu.VMEM((wt, d), BF16), pltpu.VMEM((LANES, d), BF16)],
        compiler_params=_cparams("arbitrary"),
        name="proj_attn",
    )(x2d, g, w_all)


def _proj_conv_kernel(*refs, cw, tm, seq, tiles_per_seq, tail, row0):
    if tiles_per_seq:
        x_ref, g_ref, w_hbm, cwt_ref, c_ref, ut_ref, w_ref, carry_ref = refs
    else:
        x_ref, g_ref, w_hbm, cwt_ref, p1_ref, p2_ref, c_ref, ut_ref, w_ref = refs
    _stage_rows(w_hbm, w_ref, row0)
    hb = _rms_bf16(x_ref[...], g_ref[...])
    cx = _dot_nt(hb, w_ref[0:cw, :])
    cb = _dot_nt(hb, w_ref[cw:2 * cw, :])
    cc = _dot_nt(hb, w_ref[2 * cw:3 * cw, :])
    u = cc * cx
    r = lax.broadcasted_iota(I32, (tm, cw), 0)
    um1 = pltpu.roll(u, 1, 0)
    um2 = pltpu.roll(u, 2, 0)
    if tiles_per_seq:
        @pl.when(pl.program_id(0) % tiles_per_seq == 0)
        def _():
            carry_ref[...] = jnp.zeros_like(carry_ref)
        prev1 = jnp.broadcast_to(carry_ref[SUBLANES - 1:SUBLANES, :], (tm, cw))
        prev2 = jnp.broadcast_to(carry_ref[SUBLANES - 2:SUBLANES - 1, :], (tm, cw))
        um1 = jnp.where(r == 0, prev1, um1)
        um2 = jnp.where(r == 0, prev2, jnp.where(r == 1, prev1, um2))
        carry_ref[...] = u[tm - SUBLANES:tm, :]
    else:
        assert seq & (seq - 1) == 0
        t = r & (seq - 1)
        um1 = jnp.where(t >= 1, um1, p1_ref[...])
        um2 = jnp.where(t >= 2, um2, p2_ref[...])
    y = cwt_ref[0:1, :] * um2 + cwt_ref[1:2, :] * um1 + cwt_ref[2:3, :] * u
    c_ref[...] = (cb * y).astype(BF16)
    ut_ref[...] = u[tm - tail:tm, :]


def _proj_conv(x2d, g, w_all, row0, conv_w, prev, *, seq, tm):
    m, d = x2d.shape
    cw = conv_w.shape[1]
    assert conv_w.shape[0] == 3 and row0 % 16 == 0
    row = lambda width: pl.BlockSpec((tm, width), lambda i: (i, 0))
    in_specs = [row(d), _resident((1, d)), pl.BlockSpec(memory_space=pl.ANY), _resident((3, cw))]
    args = [x2d, g, w_all, conv_w]
    scratch = [pltpu.VMEM((3 * cw, d), BF16)]
    if prev is None:
        assert seq % tm == 0
        tiles_per_seq, tail = seq // tm, SUBLANES
        scratch += [pltpu.VMEM((SUBLANES, cw), F32)]
    else:
        assert tm % seq == 0 and seq >= 2
        tiles_per_seq, tail = 0, tm
        in_specs += [row(cw), row(cw)]
        args += list(prev)
    return pl.pallas_call(
        functools.partial(_proj_conv_kernel, cw=cw, tm=tm, seq=seq, tiles_per_seq=tiles_per_seq, tail=tail,
                          row0=row0),
        grid=(m // tm,),
        in_specs=in_specs,
        out_specs=[row(cw), pl.BlockSpec((tail, cw), lambda i: (i, 0))],
        out_shape=[jax.ShapeDtypeStruct((m, cw), BF16), jax.ShapeDtypeStruct((m // tm * tail, cw), F32)],
        scratch_shapes=scratch,
        compiler_params=_cparams("arbitrary"),
        name="proj_conv",
    )(*args)


def _bias_tiles_kernel(relb_ref, o_ref, *, tb, n_buckets):
    kind = pl.program_id(0)
    h = pl.program_id(1)
    qry = lax.broadcasted_iota(I32, (tb, tb), 0)
    key = lax.broadcasted_iota(I32, (tb, tb), 1)
    bias = _bias_of_bucket(_bucket(kind * tb + qry - key, n_buckets), relb_ref, h, n_buckets)
    o_ref[...] = (bias - relb_ref[n_buckets - 1, h]) * LOG2E


def _bias_tiles(rel_bias, *, tb):
    n_buckets, heads = rel_bias.shape
    return pl.pallas_call(
        functools.partial(_bias_tiles_kernel, tb=tb, n_buckets=n_buckets),
        grid=(2, heads),
        in_specs=[pl.BlockSpec(memory_space=pltpu.SMEM)],
        out_specs=pl.BlockSpec((None, None, tb, tb), lambda a, h: (a, h, 0, 0)),
        out_shape=jax.ShapeDtypeStruct((2, heads, tb, tb), F32),
        compiler_params=_cparams("arbitrary", "arbitrary"),
        name="bias_tiles",
    )(rel_bias)


def _attn_prompt_kernel(qi_ref, sm_ref, smb_ref, q_ref, kb_ref, vb_ref, bias_ref, o_ref,
                        skey_ref, thr_ref, wt_ref, madd_ref, m_ref, l_ref, acc_ref,
                        *, tb, heads, hd, iheads, idim, topk):
    i = pl.program_id(1)
    wscale = idim ** -0.5 * iheads ** -0.5

    m_ref[...] = jnp.full(m_ref.shape, NEG_BIG, F32)
    l_ref[...] = jnp.zeros(l_ref.shape, F32)
    acc_ref[...] = jnp.zeros(acc_ref.shape, F32)
    wt_ref[...] = sm_ref[...].T * wscale

    def chunk(j):
        return pl.ds(pl.multiple_of(j * tb, tb), tb)

    def score_chunk(j, diag):
        kic = smb_ref[chunk(j), 0:idim]
        acc = jnp.zeros((tb, tb), F32)
        for h in range(iheads):
            s = _dot_nt(kic, qi_ref[:, h * idim:(h + 1) * idim])
            acc = acc + jnp.maximum(s, 0.0) * wt_ref[idim + h:idim + h + 1, :]
        key = _ordered_key(acc)
        if diag:
            kpos = lax.broadcasted_iota(I32, (tb, tb), 0)
            qpos = lax.broadcasted_iota(I32, (tb, tb), 1)
            key = jnp.where(kpos > qpos, INT_MIN, key)
        skey_ref[chunk(j), :] = key

    def score_body(j, carry):
        score_chunk(j, False)
        return carry

    lax.fori_loop(0, i, score_body, 0)
    score_chunk(i, True)

    def count_ge(scand):
        def body(j, cnt):
            return cnt + _fold_rows(jnp.where(skey_ref[chunk(j), :] >= scand[0:1, :], 1.0, 0.0))

        return lax.fori_loop(0, i + 1, body, jnp.zeros((SUBLANES, tb), F32))

    trips = jnp.where((i + 1) * tb <= topk, 0, 32)
    thr_ref[...] = _kth_threshold(count_ge, (SUBLANES, tb), 0, float(topk), trips)

    def attend_chunk(j, kind):
        madd_ref[...] = jnp.where(skey_ref[chunk(j), :] >= thr_ref[0:1, :], 0.0, NEG_BIG).T
        for h in range(heads):
            hs = slice(h * hd, (h + 1) * hd)
            lg = _dot_nt(q_ref[:, hs], kb_ref[chunk(j), hs]) + madd_ref[...]
            if kind is not None:
                lg = lg + bias_ref[kind, h]
            _flash_update(lg, vb_ref[chunk(j), hs], m_ref, l_ref, acc_ref, h, hd)

    def attend_body(j, carry):
        attend_chunk(j, None)
        return carry

    lax.fori_loop(0, jnp.maximum(i - 1, 0), attend_body, 0)

    @pl.when(i >= 1)
    def _():
        attend_chunk(i - 1, 1)

    attend_chunk(i, 0)

    for h in range(heads):
        hs = slice(h * hd, (h + 1) * hd)
        o_ref[:, hs] = (acc_ref[:, hs] / jnp.tile(l_ref[h], (1, hd // LANES))).astype(BF16)


def _attn_prompt(bias_tiles, qi, sm, smb, q, kb, vb, *, batch, seq, tb, heads, hd, iheads, idim, topk):
    m, aw = q.shape
    iw = qi.shape[1]
    nq = seq // tb
    assert tb >= MAX_DISTANCE and tb % LANES == 0 and hd % LANES == 0
    qrow = lambda width: pl.BlockSpec((tb, width), lambda b, i: (b * nq + i, 0))
    seqblk = lambda width: pl.BlockSpec((seq, width), lambda b, i: (b, 0))
    return pl.pallas_call(
        functools.partial(_attn_prompt_kernel, tb=tb, heads=heads, hd=hd, iheads=iheads, idim=idim, topk=topk),
        grid=(batch, nq),
        in_specs=[qrow(iw), qrow(LANES), seqblk(LANES), qrow(aw), seqblk(aw), seqblk(aw),
                  _resident(bias_tiles.shape)],
        out_specs=qrow(aw),
        out_shape=jax.ShapeDtypeStruct((m, aw), BF16),
        scratch_shapes=[pltpu.VMEM((seq, tb), I32), pltpu.VMEM((SUBLANES, tb), I32),
                        pltpu.VMEM((LANES, tb), F32), pltpu.VMEM((tb, tb), F32),
                        pltpu.VMEM((heads, tb, LANES), F32), pltpu.VMEM((heads, tb, LANES), F32),
                        pltpu.VMEM((tb, aw), F32)],
        compiler_params=_cparams("arbitrary", "arbitrary"),
        name="attn_prompt",
    )(qi, sm, smb, q, kb, vb, bias_tiles)


def _sample_select_kernel(pt_ref, qi_ref, w_ref, kin_ref, *rest, pages, ps, nc, iheads, idim, tq, rq, topk):
    page_refs = rest[:pages]
    past_ref, new_ref, thr_ref, row_ref = rest[pages:]
    b = pl.program_id(0)
    c = pl.program_id(1)
    db = pl.num_programs(0)
    ch = pages * ps
    wscale = idim ** -0.5 * iheads ** -0.5
    rows_b = pl.ds(pl.multiple_of(b * rq, rq), rq)

    def score(keys_t):
        s = _dot(qi_ref[...], keys_t)
        t = jnp.maximum(s, 0.0) * (w_ref[:, 0:1] * wscale)
        acc = t[0:rq]
        for h in range(1, iheads):
            acc = acc + t[h * rq:(h + 1) * rq]
        return _ordered_key(acc)

    @pl.when(c < nc)
    def _():
        for p in range(pages):
            key = score(page_refs[p][...].astype(BF16))
            row_ref[rows_b, pl.ds(pl.multiple_of(c * ch + p * ps, LANES), ps)] = key
            for t in range(ps // LANES):
                for j in range(tq):
                    past_ref[j, p * (ps // LANES) + t:p * (ps // LANES) + t + 1, :] = \
                        key[j:j + 1, t * LANES:(t + 1) * LANES]

    @pl.when(c == nc)
    def _():
        key = score(kin_ref[...])
        j = lax.broadcasted_iota(I32, (rq, ps), 0)
        n = lax.broadcasted_iota(I32, (rq, ps), 1)
        key = jnp.where((n <= j) & (n < tq), key, INT_MIN)
        new_ref[...] = key
        row_ref[rows_b, nc * ch:nc * ch + ps] = key

    @pl.when((c == nc) & (b == db - 1))
    def _():
        n_rows = row_ref.shape[0]

        def count_ge(scand):
            sk = row_ref[...]
            return _fold_lanes(jnp.where(sk >= jnp.tile(scand, (1, sk.shape[1] // LANES)), 1.0, 0.0))

        thr = _kth_threshold(count_ge, (n_rows, LANES), 1, float(topk), 32)
        new_keys = row_ref[:, nc * ch:nc * ch + ps]
        gt_new = jnp.sum(jnp.where(new_keys > jnp.tile(thr, (1, ps // LANES)), 1.0, 0.0), axis=1, keepdims=True)
        budget = jnp.broadcast_to(topk - gt_new, (n_rows, LANES)).astype(I32)
        thr_ref[...] = jnp.zeros(thr_ref.shape, I32)
        for bb in range(n_rows // rq):
            for jj in range(tq):
                thr_ref[bb * tq + jj, 0:1, :] = thr[bb * rq + jj:bb * rq + jj + 1, :]
                thr_ref[bb * tq + jj, 1:2, :] = budget[bb * rq + jj:bb * rq + jj + 1, :]


def _sample_select(page_table, qi_hm, w_hm, kin_t, cache_kidx_t, *, pages, tq, rq, iheads, topk):
    db, n_pages = page_table.shape
    _, idim, ps = cache_kidx_t.shape
    nc = n_pages // pages
    ch = pages * ps
    assert ps % LANES == 0 and (ch // LANES) % SUBLANES == 0
    page_spec = lambda p: pl.BlockSpec(
        (None, idim, ps), lambda b, c, pt: (pt[b, jnp.minimum(c, nc - 1) * pages + p], 0, 0))
    per_b = lambda shape: pl.BlockSpec((None,) + shape, lambda b, c, pt: (b, 0, 0))
    return pl.pallas_call(
        functools.partial(_sample_select_kernel, pages=pages, ps=ps, nc=nc, iheads=iheads, idim=idim, tq=tq, rq=rq,
                          topk=topk),
        grid_spec=pltpu.PrefetchScalarGridSpec(
            num_scalar_prefetch=1,
            grid=(db, nc + 1),
            in_specs=[per_b((iheads * rq, idim)), per_b((iheads * rq, LANES)), per_b((idim, ps))]
            + [page_spec(p) for p in range(pages)],
            out_specs=[pl.BlockSpec((tq, ch // LANES, LANES), lambda b, c, pt: (b, jnp.minimum(c, nc - 1), 0)),
                       per_b((rq, ps)),
                       pl.BlockSpec((db * tq, SUBLANES, LANES), lambda b, c, pt: (0, 0, 0))],
            scratch_shapes=[pltpu.VMEM((db * rq, nc * ch + ps), I32)],
        ),
        out_shape=[jax.ShapeDtypeStruct((db * tq, n_pages * ps // LANES, LANES), I32),
                   jax.ShapeDtypeStruct((db, rq, ps), I32),
                   jax.ShapeDtypeStruct((db * tq, SUBLANES, LANES), I32)],
        compiler_params=_cparams("arbitrary", "arbitrary"),
        name="sample_select",
    )(page_table, qi_hm, w_hm, kin_t, *([cache_kidx_t] * pages))


def _sc_gather_kernel(past_hbm, thr_hbm, pt_hbm, ck_hbm, cv_hbm, ksel_hbm, vsel_hbm, pos_hbm, cnt_hbm,
                      row_v, thr_v, pt_v, idx_v, phys_v, rows_v, cnt_v, sem,
                      *, nq, tq, topk, ps, n_cores, rows_per_copy):
    wid = lax.axis_index("s") * n_cores + lax.axis_index("c")

    @pl.when(wid < nq)
    def _():
        pltpu.sync_copy(past_hbm.at[wid], row_v)
        pltpu.sync_copy(thr_hbm.at[wid], thr_v)
        pltpu.sync_copy(pt_hbm.at[wid // tq], pt_v)
        thr = thr_v[0, pl.ds(0, SC_LANES)]
        budget = thr_v[1, pl.ds(0, SC_LANES)]
        lane = lax.iota(I32, SC_LANES)
        zero = jnp.zeros((SC_LANES,), I32)
        for t in range(idx_v.shape[0] // SC_LANES):
            idx_v[pl.ds(t * SC_LANES, SC_LANES)] = zero

        def compact(pred):
            def body(r, cnt):
                for t in range(LANES // SC_LANES):
                    x = row_v[r, pl.ds(t * SC_LANES, SC_LANES)]
                    m = pred(x, cnt)
                    rank = plsc.cumsum(jnp.where(m, 1, 0).astype(I32))
                    plsc.store_scatter(idx_v, [cnt + rank - 1], lane + (r * LANES + t * SC_LANES), mask=m)
                    cnt = cnt + plsc.all_reduce_population_count(m)
                return cnt
            return body

        cnt = lax.fori_loop(0, row_v.shape[0], compact(lambda x, cnt: x > thr), zero)
        cnt = lax.fori_loop(0, row_v.shape[0], compact(lambda x, cnt: (x == thr) & (cnt < budget)), cnt)
        cnt_v[...] = jnp.minimum(cnt, budget)
        pltpu.sync_copy(cnt_v, cnt_hbm.at[wid])
        pltpu.sync_copy(idx_v.at[pl.ds(0, topk)], pos_hbm.at[wid])

        shift = ps.bit_length() - 1
        for t in range(topk // SC_LANES):
            pos = idx_v[pl.ds(t * SC_LANES, SC_LANES)]
            page = plsc.load_gather(pt_v, [lax.shift_right_logical(pos, shift)])
            phys_v[pl.ds(t * SC_LANES, SC_LANES)] = page * ps + (pos & (ps - 1))
        for g in range(topk // rows_per_copy):
            sel = phys_v.at[pl.ds(g * rows_per_copy, rows_per_copy)]
            dst = pl.ds(wid * topk + g * rows_per_copy, rows_per_copy)
            for src_hbm, dst_hbm in ((ck_hbm, ksel_hbm), (cv_hbm, vsel_hbm)):
                pltpu.async_copy(src_hbm.at[sel], rows_v, sem).wait()
                pltpu.sync_copy(rows_v, dst_hbm.at[dst])


def _sc_gather(past_keys, thr, page_table, cache_k, cache_v, *, tq, topk, ps):
    nq, key_rows, _ = past_keys.shape
    _, heads, hd = cache_k.shape
    assert ps & (ps - 1) == 0 and topk % SC_LANES == 0
    rows_per_copy = 64
    assert topk % rows_per_copy == 0
    mesh = plsc.VectorSubcoreMesh(core_axis_name="c", subcore_axis_name="s", num_cores=V7X_SC_CORES,
                                  num_subcores=V7X_SC_SUBCORES)
    assert nq <= V7X_SC_CORES * V7X_SC_SUBCORES
    sel_shape = jax.ShapeDtypeStruct((nq * topk, heads, hd), F32)
    return pl.kernel(
        functools.partial(_sc_gather_kernel, nq=nq, tq=tq, topk=topk, ps=ps, n_cores=V7X_SC_CORES,
                          rows_per_copy=rows_per_copy),
        out_type=[sel_shape, sel_shape, jax.ShapeDtypeStruct((nq, topk), I32),
                  jax.ShapeDtypeStruct((nq, SC_LANES), I32)],
        mesh=mesh,
        scratch_types=[pltpu.VMEM((key_rows, LANES), I32), pltpu.VMEM((SUBLANES, LANES), I32),
                       pltpu.VMEM((page_table.shape[1],), I32), pltpu.VMEM((topk + SC_LANES,), I32),
                       pltpu.VMEM((topk,), I32), pltpu.VMEM((rows_per_copy, heads, hd), F32),
                       pltpu.VMEM((SC_LANES,), I32), pltpu.SemaphoreType.DMA],
        compiler_params=pltpu.CompilerParams(needs_layout_passes=False),
        name="sc_select_gather",
    )(past_keys, thr, page_table, cache_k, cache_v)


def _sample_attn_sel_kernel(cnt_ref, relbt_ref, q_ref, ksel_ref, vsel_ref, pos_ref, snew_ref, thr_ref, kn_ref, vn_ref,
                            o_ref, *, heads, tq, past, n_buckets):
    w = pl.program_id(0)
    j = w % tq
    q = q_ref[...]

    def head_bias(dist):
        bucket = _bucket(dist, n_buckets)
        acc = jnp.zeros(bucket.shape, F32)
        for bkt in range(n_buckets):
            acc = jnp.where(bucket == bkt, relbt_ref[:, bkt:bkt + 1], acc)
        return acc * LOG2E

    def logits(keys):
        lg = _dot_nt(q, keys)
        head = lax.broadcasted_iota(I32, lg.shape, 0)
        col = lax.broadcasted_iota(I32, lg.shape, 1)
        return lg, (col & (heads - 1)) == head, col

    lg, own, col = logits(ksel_ref[...].astype(BF16))
    keep = own & (col < cnt_ref[w, 0] * heads)
    lg = jnp.where(keep, lg + head_bias(jnp.broadcast_to(past + j - pos_ref[...], lg.shape)), NEG_BIG)
    lgn, own, col = logits(kn_ref[...])
    keep = own & (snew_ref[...] >= thr_ref[0:1, :])
    lgn = jnp.where(keep, lgn + head_bias(j - lax.shift_right_logical(col, heads.bit_length() - 1)), NEG_BIG)

    m = jnp.maximum(jnp.max(lg, axis=1, keepdims=True), jnp.max(lgn, axis=1, keepdims=True))
    p = jnp.exp2(lg - m)
    pn = jnp.exp2(lgn - m)
    denom = jnp.sum(p, axis=1, keepdims=True) + jnp.sum(pn, axis=1, keepdims=True)
    acc = _dot(p.astype(BF16), vsel_ref[...].astype(BF16)) + _dot(pn.astype(BF16), vn_ref[...])
    o_ref[...] = acc / denom


def _sample_attn_sel(cnt, rel_bias_t, q, ksel, vsel, pos, snew, thr, kn, vn, *, heads, hd, topk, tq, past):
    nq = q.shape[0]
    ps = kn.shape[1]
    assert ps == LANES and hd == LANES and heads & (heads - 1) == 0
    per_q = lambda shape: pl.BlockSpec((None,) + shape, lambda w, cnt: (w, 0, 0))
    per_b = lambda shape: pl.BlockSpec((None,) + shape, lambda w, cnt: (w // tq, 0, 0))
    sel_spec = pl.BlockSpec((topk * heads, hd), lambda w, cnt: (w, 0))
    return pl.pallas_call(
        functools.partial(_sample_attn_sel_kernel, heads=heads, tq=tq, past=past, n_buckets=rel_bias_t.shape[1]),
        grid_spec=pltpu.PrefetchScalarGridSpec(
            num_scalar_prefetch=1,
            grid=(nq,),
            in_specs=[pl.BlockSpec(rel_bias_t.shape, lambda w, cnt: (0, 0)), per_q((heads, hd)), sel_spec, sel_spec,
                      per_q((1, topk * heads)), per_q((1, ps)), per_q((SUBLANES, LANES)), per_b((ps, hd)),
                      per_b((ps, hd))],
            out_specs=per_q((heads, hd)),
        ),
        out_shape=jax.ShapeDtypeStruct((nq, heads, hd), F32),
        compiler_params=_cparams("arbitrary"),
        name="sample_attn_sel",
    )(cnt, rel_bias_t, q, ksel, vsel, pos, snew, thr, kn, vn)


def _mix_kernel(x_ref, g_ref, a_ref, c_ref, w_hbm, wpa_ref, wpb_ref, wo_ref, o_ref, wg_ref, *, d, row0):
    _stage_rows(w_hbm, wg_ref, row0)
    x = x_ref[...]
    hb = _rms_bf16(x, g_ref[...])
    a = _dot(a_ref[...], wpa_ref[...])
    m = jax.nn.sigmoid(_dot_nt(hb, wg_ref[0:d, :])) * a
    c = _dot(c_ref[...], wpb_ref[...])
    m = m + jax.nn.sigmoid(_dot_nt(hb, wg_ref[d:2 * d, :])) * c
    o_ref[...] = x + _dot(m.astype(BF16), wo_ref[...])


def _mix(x2d, g, attn, c_in, w_all, row0, w_pa, w_pb, w_o, *, tm):
    m, d = x2d.shape
    assert row0 % 16 == 0
    row = lambda width: pl.BlockSpec((tm, width), lambda i: (i, 0))
    return pl.pallas_call(
        functools.partial(_mix_kernel, d=d, row0=row0),
        grid=(m // tm,),
        in_specs=[row(d), _resident((1, d)), row(attn.shape[1]), row(c_in.shape[1]),
                  pl.BlockSpec(memory_space=pl.ANY), _resident(w_pa.shape), _resident(w_pb.shape),
                  _resident(w_o.shape)],
        out_specs=row(d),
        out_shape=jax.ShapeDtypeStruct((m, d), F32),
        scratch_shapes=[pltpu.VMEM((2 * d, d), BF16)],
        compiler_params=_cparams("arbitrary"),
        name="mix_out",
    )(x2d, g, attn, c_in, w_all, w_pa, w_pb, w_o)


def _mlp_kernel(x_ref, g_ref, gf_ref, w1_ref, w2_ref, y_ref, h_ref, *, final):
    f = pl.program_id(1)

    @pl.when(f == 0)
    def _():
        h_ref[...] = _rms_bf16(x_ref[...], g_ref[...])
        y_ref[...] = jnp.zeros(y_ref.shape, F32)

    t = jnp.square(jnp.maximum(_dot(h_ref[...], w1_ref[...]), 0.0))
    y_ref[...] += _dot(t.astype(BF16), w2_ref[...])

    @pl.when(f == pl.num_programs(1) - 1)
    def _():
        x2 = x_ref[...] + y_ref[...]
        if final:
            x2 = x2 * lax.rsqrt(jnp.mean(x2 * x2, axis=-1, keepdims=True) + EPS) * gf_ref[...]
        y_ref[...] = x2


def _mlp(x2d, g, gf, w1, w2, *, tm, tf, final):
    m, d = x2d.shape
    ff = w1.shape[1]
    return pl.pallas_call(
        functools.partial(_mlp_kernel, final=final),
        grid=(m // tm, ff // tf),
        in_specs=[pl.BlockSpec((tm, d), lambda i, f: (i, 0)), _resident((1, d)), _resident((1, d)),
                  pl.BlockSpec((d, tf), lambda i, f: (0, f)), pl.BlockSpec((tf, d), lambda i, f: (f, 0))],
        out_specs=pl.BlockSpec((tm, d), lambda i, f: (i, 0)),
        out_shape=jax.ShapeDtypeStruct((m, d), F32),
        scratch_shapes=[pltpu.VMEM((tm, d), BF16)],
        compiler_params=_cparams("arbitrary", "arbitrary"),
        name="mlp",
    )(x2d, g, gf, w1, w2)


def _tile(m, cap):
    return min(m, cap)


def kernel(x_prompt, x_sample, cache_k, cache_v, cache_kidx, state_conv, page_table, rel_bias, norm_mix_g, w_in,
           conv_w, w_pa, w_pb, w_o, norm_mlp_g, w_mlp_in, w_mlp_out, norm_final_g):
    batch, seq, d = x_prompt.shape
    db, tq, _ = x_sample.shape
    depth, n_pool, ps, heads, hd = cache_k.shape
    idim = cache_kidx.shape[-1]
    cw = conv_w.shape[-1]
    aw = heads * hd
    n_in = w_in.shape[-1]
    iheads = (n_in - 3 * aw - idim - 3 * cw - 2 * d) // (idim + 1)
    iw = iheads * idim
    assert 3 * aw + iw + idim + iheads + 3 * cw + 2 * d == n_in and idim + iheads <= LANES
    n_pages = page_table.shape[1]
    past = n_pages * ps
    rq = SUBLANES
    assert tq <= rq

    mp, ms = batch * seq, db * tq
    xp = x_prompt.reshape(mp, d)
    xs = x_sample.reshape(ms, d)
    tb = _tile(seq, 256)
    pages = math.gcd(n_pages, 8)
    o_small = 3 * aw + iw
    o_conv = o_small + idim + iheads
    o_gate = o_conv + 3 * cw
    gf = norm_final_g.reshape(1, d)
    bias_tiles = _bias_tiles(rel_bias, tb=tb)

    outs = {k: [] for k in ("kp", "vp", "kip", "sp", "ks", "vs", "kis", "ss")}
    for l in range(depth):
        wl = jnp.swapaxes(w_in[l], 0, 1).astype(BF16)
        wpa, wpb, wo = w_pa[l].astype(BF16), w_pb[l].astype(BF16), w_o[l].astype(BF16)
        w1, w2 = w_mlp_in[l].astype(BF16), w_mlp_out[l].astype(BF16)
        g_mix = norm_mix_g[l].reshape(1, d)
        g_mlp = norm_mlp_g[l].reshape(1, d)

        tm = _tile(seq, 512)
        q, k, v, kb, vb, qi, sm, smb = _proj_attn(xp, g_mix, wl, aw=aw, iw=iw, n_small=idim + iheads, hd=hd, tm=tm)
        c_in, u_tail = _proj_conv(xp, g_mix, wl, o_conv, conv_w[l], None, seq=seq, tm=_tile(seq, 512))
        attn = _attn_prompt(bias_tiles, qi, sm, smb, q, kb, vb, batch=batch, seq=seq, tb=tb,
                            heads=heads, hd=hd, iheads=iheads, idim=idim, topk=min(TOPK_MAX, seq // 4))
        x1 = _mix(xp, g_mix, attn, c_in, wl, o_gate, wpa, wpb, wo, tm=_tile(mp, 256))
        xp_next = _mlp(x1, g_mlp, gf, w1, w2, tm=_tile(mp, 1024), tf=_tile(w1.shape[1], 512), final=l == depth - 1)
        outs["kp"].append(k.reshape(batch, seq, heads, hd))
        outs["vp"].append(v.reshape(batch, seq, heads, hd))
        outs["kip"].append(sm[:, :idim].reshape(batch, seq, idim))
        outs["sp"].append(u_tail.reshape(batch, -1, SUBLANES, cw)[:, -1, SUBLANES - 2:])

        q, k, v, kb, vb, qi, sm, smb = _proj_attn(xs, g_mix, wl, aw=aw, iw=iw, n_small=idim + iheads, hd=hd, tm=ms)
        st = state_conv[l]
        zero = jnp.zeros((db, tq - 1, cw), F32)
        prev1 = jnp.concatenate([st[:, 1:2], zero], axis=1).reshape(ms, cw)
        prev2 = jnp.concatenate([st, zero[:, 1:]], axis=1).reshape(ms, cw)
        c_in, u_all = _proj_conv(xs, g_mix, wl, o_conv, conv_w[l], (prev1, prev2), seq=tq, tm=ms)

        def pad_rows(a, n):
            return jnp.pad(a, ((0, 0), (0, n - a.shape[1])) + ((0, 0),) * (a.ndim - 2))

        qi_hm = pad_rows(qi.reshape(db, tq, iheads, idim).transpose(0, 2, 1, 3).reshape(db * iheads, tq, idim), rq)
        qi_hm = qi_hm.reshape(db, iheads * rq, idim)
        w_hm = pad_rows(sm[:, idim:idim + iheads].reshape(db, tq, iheads).transpose(0, 2, 1).reshape(db * iheads, tq), rq)
        w_hm = jnp.broadcast_to(w_hm.reshape(db, iheads * rq, 1), (db, iheads * rq, LANES))
        kin_t = jnp.swapaxes(pad_rows(smb[:, :idim].reshape(db, tq, idim), ps), 1, 2)
        topk_s = min(TOPK_MAX, (past + tq) // 4)
        past_keys, snew, thr = _sample_select(page_table, qi_hm, w_hm, kin_t, jnp.swapaxes(cache_kidx[l], 1, 2),
                                              pages=math.gcd(n_pages, 32), tq=tq, rq=rq, iheads=iheads, topk=topk_s)
        ksel, vsel, pos, cnt = _sc_gather(past_keys, thr, page_table, cache_k[l].reshape(n_pool * ps, heads, hd),
                                          cache_v[l].reshape(n_pool * ps, heads, hd), tq=tq, topk=topk_s, ps=ps)
        pos_rows = jnp.repeat(pos, heads, axis=1).reshape(ms, 1, topk_s * heads)
        snew_rows = jnp.repeat(snew[:, :tq, :ps // heads], heads, axis=2).reshape(ms, 1, ps)
        attn = _sample_attn_sel(cnt, rel_bias.T, q.reshape(ms, heads, hd), ksel.reshape(ms * topk_s * heads, hd),
                                vsel.reshape(ms * topk_s * heads, hd), pos_rows, snew_rows, thr,
                                pad_rows(kb.reshape(db, tq * heads, hd), ps), pad_rows(vb.reshape(db, tq * heads, hd), ps),
                                heads=heads, hd=hd, topk=topk_s, tq=tq, past=past)
        attn = attn.reshape(ms, aw).astype(BF16)
        x1 = _mix(xs, g_mix, attn, c_in, wl, o_gate, wpa, wpb, wo, tm=ms)
        xs_next = _mlp(x1, g_mlp, gf, w1, w2, tm=ms, tf=_tile(w1.shape[1], 1024), final=l == depth - 1)
        outs["ks"].append(k.reshape(db, tq, heads, hd))
        outs["vs"].append(v.reshape(db, tq, heads, hd))
        outs["kis"].append(sm[:, :idim].reshape(db, tq, idim))
        outs["ss"].append(u_all.reshape(db, tq, cw)[:, tq - 2:])
        xp, xs = xp_next, xs_next

    st = {k: jnp.stack(v) for k, v in outs.items()}
    return (xp.reshape(batch, seq, d), xs.reshape(db, tq, d), st["kp"], st["vp"], st["kip"], st["sp"],
            st["ks"], st["vs"], st["kis"], st["ss"])
```

```python
import functools
import math

import jax
import jax.numpy as jnp
import numpy as np
from jax import lax
from jax.experimental import pallas as pl
from jax.experimental.pallas import tpu as pltpu
from jax.experimental.pallas import tpu_sc as plsc

F32 = jnp.float32
BF16 = jnp.bfloat16
I32 = jnp.int32

TOPK_MAX = 256
MAX_DISTANCE = 128
EPS = 1e-6

LANES = 128
SUBLANES = 8
V7X_SCOPED_VMEM_BYTES = 60000 * 1024
SC_LANES = 16
V7X_SC_CORES = 2
V7X_SC_SUBCORES = 16

LOG2E = math.log2(math.e)
INT_MIN = np.int32(-2 ** 31)
NEG_BIG = -1e30


def _cparams(*sem):
    return pltpu.CompilerParams(dimension_semantics=sem, vmem_limit_bytes=V7X_SCOPED_VMEM_BYTES)


def _resident(shape):
    nd = len(shape)
    return pl.BlockSpec(shape, lambda *_: (0,) * nd, pipeline_mode=pl.Buffered(1))


def _rms_bf16(x, g):
    y = x * lax.rsqrt(jnp.mean(x * x, axis=-1, keepdims=True) + EPS)
    return (y * g).astype(BF16)


def _dot(a, b):
    return jnp.dot(a, b, preferred_element_type=F32)


def _dot_nt(a, b):
    return lax.dot_general(a, b, (((1,), (1,)), ((), ())), preferred_element_type=F32)


def _ordered_key(x):
    b = lax.bitcast_convert_type(x, I32)
    return b ^ ((b >> 31) & np.int32(0x7FFFFFFF))


def _bucket(n, n_buckets):
    n = jnp.maximum(n, 0)
    me = n_buckets // 2
    nf = jnp.maximum(n, me).astype(F32)
    large = me + (jnp.log(nf / me) / math.log(MAX_DISTANCE / me) * (n_buckets - me)).astype(I32)
    large = jnp.minimum(large, n_buckets - 1)
    return jnp.where(n < me, n, large)


def _bias_of_bucket(bucket, relb_ref, h, n_buckets):
    acc = jnp.zeros(bucket.shape, F32)
    for bkt in range(n_buckets):
        acc = jnp.where(bucket == bkt, relb_ref[bkt, h], acc)
    return acc


def _fold_lanes(x):
    acc = x[:, 0:LANES]
    for t in range(1, x.shape[1] // LANES):
        acc = acc + x[:, t * LANES:(t + 1) * LANES]
    return acc


def _fold_rows(x, op=jnp.add):
    parts = [x[t * SUBLANES:(t + 1) * SUBLANES, :] for t in range(x.shape[0] // SUBLANES)]
    while len(parts) > 1:
        parts = [op(parts[t], parts[t + 1]) for t in range(0, len(parts) - 1, 2)] + parts[len(parts) & ~1:]
    return parts[0]


def _kth_threshold(count_ge, shape, axis, k, trips):
    def bit_body(t, uthr):
        cand = uthr | jnp.left_shift(np.int32(1), 31 - t)
        cnt = jnp.sum(count_ge(cand ^ INT_MIN), axis=axis, keepdims=True)
        return jnp.where(cnt >= k, cand, uthr)

    uthr = lax.fori_loop(0, trips, bit_body, jnp.zeros(shape, I32))
    return jnp.maximum(uthr ^ INT_MIN, INT_MIN + 1)


def _flash_update(lg, v, m_ref, l_ref, acc_ref, h, hd):
    reps = lg.shape[1] // LANES
    m_prev = m_ref[h]
    m_new = jnp.maximum(m_prev, jnp.max(lg, axis=1, keepdims=True))
    p = jnp.exp2(lg - jnp.tile(m_new, (1, reps)))
    alpha = jnp.exp2(m_prev - m_new)
    pv = _dot(p.astype(BF16), jnp.concatenate([v, jnp.ones((v.shape[0], LANES), BF16)], axis=1))
    l_ref[h] = alpha * l_ref[h] + pv[:, hd:]
    m_ref[h] = m_new
    hs = slice(h * hd, (h + 1) * hd)
    acc_ref[:, hs] = acc_ref[:, hs] * jnp.tile(alpha, (1, hd // LANES)) + pv[:, :hd]


def _stage_rows(w_hbm, w_ref, row0):
    @pl.when(pl.program_id(0) == 0)
    def _():
        pltpu.sync_copy(w_hbm.at[pl.ds(row0, w_ref.shape[0])], w_ref)


def _proj_attn_kernel(x_ref, g_ref, w_hbm, q_ref, k_ref, v_ref, kb_ref, vb_ref, qi_ref,
                      sm_ref, smb_ref, w_ref, ws_ref, *, aw, iw, hd, tm, n_small, qscale):
    heads = aw // hd
    _stage_rows(w_hbm, w_ref, 0)

    @pl.when(pl.program_id(0) == 0)
    def _():
        ws_ref[...] = jnp.zeros(ws_ref.shape, BF16)
        pltpu.sync_copy(w_hbm.at[pl.ds(w_ref.shape[0], n_small)], ws_ref.at[pl.ds(0, n_small)])

    hb = _rms_bf16(x_ref[...], g_ref[...])
    q_ref[...] = (_dot_nt(hb, w_ref[0:aw, :]) * qscale).astype(BF16)
    for o_ref, ob_ref, c0 in ((k_ref, kb_ref, aw), (v_ref, vb_ref, 2 * aw)):
        kv = _dot_nt(hb, w_ref[c0:c0 + aw, :])
        ob_ref[...] = kv.astype(BF16)
        for h in range(heads):
            o_ref[pl.ds(h, tm, stride=heads), :] = kv[:, h * hd:(h + 1) * hd]
    qi_ref[...] = _dot_nt(hb, w_ref[3 * aw:3 * aw + iw, :]).astype(BF16)
    sm = _dot_nt(hb, ws_ref[...])
    sm_ref[...] = sm
    smb_ref[...] = sm.astype(BF16)


def _proj_attn(x2d, g, w_all, *, aw, iw, n_small, hd, tm):
    m, d = x2d.shape
    wt = 3 * aw + iw
    heads = aw // hd
    assert hd == LANES and n_small <= LANES and wt % 16 == 0 and n_small % 16 == 0
    row = lambda width: pl.BlockSpec((tm, width), lambda i: (i, 0))
    kv_spec = pl.BlockSpec((tm * heads, hd), lambda i: (i, 0))
    kv_shape = jax.ShapeDtypeStruct((m * heads, hd), F32)
    outs = [(aw, BF16), None, None, (aw, BF16), (aw, BF16), (iw, BF16), (LANES, F32), (LANES, BF16)]
    return pl.pallas_call(
        functools.partial(_proj_attn_kernel, aw=aw, iw=iw, hd=hd, tm=tm, n_small=n_small,
                          qscale=hd ** -0.5 * LOG2E),
        grid=(m // tm,),
        in_specs=[row(d), _resident((1, d)), pl.BlockSpec(memory_space=pl.ANY)],
        out_specs=[kv_spec if o is None else row(o[0]) for o in outs],
        out_shape=[kv_shape if o is None else jax.ShapeDtypeStruct((m, o[0]), o[1]) for o in outs],
        scratch_shapes=[pltpu.VMEM((wt, d), BF16), pltpu.VMEM((LANES, d), BF16)],
        compiler_params=_cparams("arbitrary"),
        name="proj_attn",
    )(x2d, g, w_all)


def _proj_conv_kernel(*refs, cw, tm, seq, tiles_per_seq, tail, row0, n_cast):
    if tiles_per_seq:
        x_ref, g_ref, w_hbm, cwt_ref = refs[:4]
        c_ref, ut_ref = refs[4 + n_cast:6 + n_cast]
        w_ref, carry_ref = refs[6 + 2 * n_cast:]
        _cast_slabs(refs[4:4 + n_cast], refs[6 + n_cast:6 + 2 * n_cast])
    else:
        x_ref, g_ref, w_hbm, cwt_ref, p1_ref, p2_ref, c_ref, ut_ref, w_ref = refs
    _stage_rows(w_hbm, w_ref, row0)
    hb = _rms_bf16(x_ref[...], g_ref[...])
    cx = _dot_nt(hb, w_ref[0:cw, :])
    cb = _dot_nt(hb, w_ref[cw:2 * cw, :])
    cc = _dot_nt(hb, w_ref[2 * cw:3 * cw, :])
    u = cc * cx
    r = lax.broadcasted_iota(I32, (tm, cw), 0)
    um1 = pltpu.roll(u, 1, 0)
    um2 = pltpu.roll(u, 2, 0)
    if tiles_per_seq:
        @pl.when(pl.program_id(0) % tiles_per_seq == 0)
        def _():
            carry_ref[...] = jnp.zeros_like(carry_ref)
        prev1 = jnp.broadcast_to(carry_ref[SUBLANES - 1:SUBLANES, :], (tm, cw))
        prev2 = jnp.broadcast_to(carry_ref[SUBLANES - 2:SUBLANES - 1, :], (tm, cw))
        um1 = jnp.where(r == 0, prev1, um1)
        um2 = jnp.where(r == 0, prev2, jnp.where(r == 1, prev1, um2))
        carry_ref[...] = u[tm - SUBLANES:tm, :]
    else:
        assert seq & (seq - 1) == 0
        t = r & (seq - 1)
        um1 = jnp.where(t >= 1, um1, p1_ref[...])
        um2 = jnp.where(t >= 2, um2, p2_ref[...])
    y = cwt_ref[0:1, :] * um2 + cwt_ref[1:2, :] * um1 + cwt_ref[2:3, :] * u
    c_ref[...] = (cb * y).astype(BF16)
    ut_ref[...] = u[tm - tail:tm, :]


def _proj_conv(x2d, g, w_all, row0, conv_w, prev, to_cast=(), *, seq, tm):
    m, d = x2d.shape
    cw = conv_w.shape[1]
    assert conv_w.shape[0] == 3 and row0 % 16 == 0
    row = lambda width: pl.BlockSpec((tm, width), lambda i: (i, 0))
    in_specs = [row(d), _resident((1, d)), pl.BlockSpec(memory_space=pl.ANY), _resident((3, cw))]
    args = [x2d, g, w_all, conv_w]
    scratch = [pltpu.VMEM((3 * cw, d), BF16)]
    if prev is None:
        assert seq % tm == 0
        tiles_per_seq, tail = seq // tm, SUBLANES
        scratch += [pltpu.VMEM((SUBLANES, cw), F32)]
    else:
        assert tm % seq == 0 and seq >= 2
        tiles_per_seq, tail = 0, tm
        in_specs += [row(cw), row(cw)]
        args += list(prev)
    assert not (to_cast and prev is not None)
    c_in, c_out, c_shapes = _cast_specs(to_cast, m // tm, lambda i: i)
    return pl.pallas_call(
        functools.partial(_proj_conv_kernel, cw=cw, tm=tm, seq=seq, tiles_per_seq=tiles_per_seq, tail=tail,
                          row0=row0, n_cast=len(to_cast)),
        grid=(m // tm,),
        in_specs=in_specs + c_in,
        out_specs=[row(cw), pl.BlockSpec((tail, cw), lambda i: (i, 0))] + c_out,
        out_shape=[jax.ShapeDtypeStruct((m, cw), BF16), jax.ShapeDtypeStruct((m // tm * tail, cw), F32)] + c_shapes,
        scratch_shapes=scratch,
        compiler_params=_cparams("arbitrary"),
        name="proj_conv",
    )(*args, *to_cast)


def _bias_tiles_kernel(relb_ref, o_ref, *, tb, n_buckets):
    kind = pl.program_id(0)
    h = pl.program_id(1)
    qry = lax.broadcasted_iota(I32, (tb, tb), 0)
    key = lax.broadcasted_iota(I32, (tb, tb), 1)
    bias = _bias_of_bucket(_bucket(kind * tb + qry - key, n_buckets), relb_ref, h, n_buckets)
    o_ref[...] = (bias - relb_ref[n_buckets - 1, h]) * LOG2E


def _bias_tiles(rel_bias, *, tb):
    n_buckets, heads = rel_bias.shape
    return pl.pallas_call(
        functools.partial(_bias_tiles_kernel, tb=tb, n_buckets=n_buckets),
        grid=(2, heads),
        in_specs=[pl.BlockSpec(memory_space=pltpu.SMEM)],
        out_specs=pl.BlockSpec((None, None, tb, tb), lambda a, h: (a, h, 0, 0)),
        out_shape=jax.ShapeDtypeStruct((2, heads, tb, tb), F32),
        compiler_params=_cparams("arbitrary", "arbitrary"),
        name="bias_tiles",
    )(rel_bias)


def _cast_specs(weights, n_steps, step_of):
    in_specs, out_specs, out_shapes = [], [], []
    for w in weights:
        rows, cols = w.shape
        assert rows % n_steps == 0 and (rows // n_steps) % 16 == 0
        spec = pl.BlockSpec((rows // n_steps, cols), lambda *idx: (step_of(*idx), 0))
        in_specs.append(spec)
        out_specs.append(spec)
        out_shapes.append(jax.ShapeDtypeStruct((rows, cols), BF16))
    return in_specs, out_specs, out_shapes


def _cast_slabs(src_refs, dst_refs):
    for src, dst in zip(src_refs, dst_refs):
        dst[...] = src[...].astype(BF16)


def _attn_prompt_kernel(*refs, n_cast, tb, heads, hd, iheads, idim, topk):
    qi_ref, sm_ref, smb_ref, q_ref, kb_ref, vb_ref, bias_ref = refs[:7]
    o_ref = refs[7 + n_cast]
    skey_ref, thr_ref, wt_ref, madd_ref, m_ref, l_ref, acc_ref = refs[8 + 2 * n_cast:]
    _cast_slabs(refs[7:7 + n_cast], refs[8 + n_cast:8 + 2 * n_cast])
    i = pl.program_id(1)
    wscale = idim ** -0.5 * iheads ** -0.5

    m_ref[...] = jnp.full(m_ref.shape, NEG_BIG, F32)
    l_ref[...] = jnp.zeros(l_ref.shape, F32)
    acc_ref[...] = jnp.zeros(acc_ref.shape, F32)
    wt_ref[...] = sm_ref[...].T * wscale

    def chunk(j):
        return pl.ds(pl.multiple_of(j * tb, tb), tb)

    def score_chunk(j, diag):
        kic = smb_ref[chunk(j), 0:idim]
        acc = jnp.zeros((tb, tb), F32)
        for h in range(iheads):
            s = _dot_nt(kic, qi_ref[:, h * idim:(h + 1) * idim])
            acc = acc + jnp.maximum(s, 0.0) * wt_ref[idim + h:idim + h + 1, :]
        key = _ordered_key(acc)
        if diag:
            kpos = lax.broadcasted_iota(I32, (tb, tb), 0)
            qpos = lax.broadcasted_iota(I32, (tb, tb), 1)
            key = jnp.where(kpos > qpos, INT_MIN, key)
        skey_ref[chunk(j), :] = key

    def score_body(j, carry):
        score_chunk(j, False)
        return carry

    lax.fori_loop(0, i, score_body, 0)
    score_chunk(i, True)

    def count_ge(scand):
        def body(j, cnt):
            return cnt + _fold_rows(jnp.where(skey_ref[chunk(j), :] >= scand[0:1, :], 1.0, 0.0))

        return lax.fori_loop(0, i + 1, body, jnp.zeros((SUBLANES, tb), F32))

    trips = jnp.where((i + 1) * tb <= topk, 0, 32)
    thr_ref[...] = _kth_threshold(count_ge, (SUBLANES, tb), 0, float(topk), trips)

    def attend_chunk(j, kind):
        madd_ref[...] = jnp.where(skey_ref[chunk(j), :] >= thr_ref[0:1, :], 0.0, NEG_BIG).T
        for h in range(heads):
            hs = slice(h * hd, (h + 1) * hd)
            lg = _dot_nt(q_ref[:, hs], kb_ref[chunk(j), hs]) + madd_ref[...]
            if kind is not None:
                lg = lg + bias_ref[kind, h]
            _flash_update(lg, vb_ref[chunk(j), hs], m_ref, l_ref, acc_ref, h, hd)

    def attend_body(j, carry):
        attend_chunk(j, None)
        return carry

    lax.fori_loop(0, jnp.maximum(i - 1, 0), attend_body, 0)

    @pl.when(i >= 1)
    def _():
        attend_chunk(i - 1, 1)

    attend_chunk(i, 0)

    for h in range(heads):
        hs = slice(h * hd, (h + 1) * hd)
        o_ref[:, hs] = (acc_ref[:, hs] / jnp.tile(l_ref[h], (1, hd // LANES))).astype(BF16)


def _attn_prompt(bias_tiles, qi, sm, smb, q, kb, vb, to_cast, *, batch, seq, tb, heads, hd, iheads, idim, topk):
    m, aw = q.shape
    iw = qi.shape[1]
    nq = seq // tb
    assert tb >= MAX_DISTANCE and tb % LANES == 0 and hd % LANES == 0
    qrow = lambda width: pl.BlockSpec((tb, width), lambda b, i: (b * nq + i, 0))
    seqblk = lambda width: pl.BlockSpec((seq, width), lambda b, i: (b, 0))
    c_in, c_out, c_shapes = _cast_specs(to_cast, batch * nq, lambda b, i: b * nq + i)
    return pl.pallas_call(
        functools.partial(_attn_prompt_kernel, n_cast=len(to_cast), tb=tb, heads=heads, hd=hd, iheads=iheads,
                          idim=idim, topk=topk),
        grid=(batch, nq),
        in_specs=[qrow(iw), qrow(LANES), seqblk(LANES), qrow(aw), seqblk(aw), seqblk(aw),
                  _resident(bias_tiles.shape)] + c_in,
        out_specs=[qrow(aw)] + c_out,
        out_shape=[jax.ShapeDtypeStruct((m, aw), BF16)] + c_shapes,
        scratch_shapes=[pltpu.VMEM((seq, tb), I32), pltpu.VMEM((SUBLANES, tb), I32),
                        pltpu.VMEM((LANES, tb), F32), pltpu.VMEM((tb, tb), F32),
                        pltpu.VMEM((heads, tb, LANES), F32), pltpu.VMEM((heads, tb, LANES), F32),
                        pltpu.VMEM((tb, aw), F32)],
        compiler_params=_cparams("arbitrary", "arbitrary"),
        name="attn_prompt",
    )(qi, sm, smb, q, kb, vb, bias_tiles, *to_cast)


def _sample_select_kernel(pt_ref, qi_ref, w_ref, kin_ref, *rest, pages, ps, nc, iheads, idim, tq, rq, topk):
    page_refs = rest[:pages]
    past_ref, new_ref, thr_ref, row_ref = rest[pages:]
    b = pl.program_id(0)
    c = pl.program_id(1)
    db = pl.num_programs(0)
    ch = pages * ps
    wscale = idim ** -0.5 * iheads ** -0.5
    rows_b = pl.ds(pl.multiple_of(b * rq, rq), rq)

    def score(keys_t):
        s = _dot(qi_ref[...], keys_t)
        t = jnp.maximum(s, 0.0) * (w_ref[:, 0:1] * wscale)
        acc = t[0:rq]
        for h in range(1, iheads):
            acc = acc + t[h * rq:(h + 1) * rq]
        return _ordered_key(acc)

    @pl.when(c < nc)
    def _():
        for p in range(pages):
            key = score(page_refs[p][...].astype(BF16))
            row_ref[rows_b, pl.ds(pl.multiple_of(c * ch + p * ps, LANES), ps)] = key
            for t in range(ps // LANES):
                for j in range(tq):
                    past_ref[j, p * (ps // LANES) + t:p * (ps // LANES) + t + 1, :] = \
                        key[j:j + 1, t * LANES:(t + 1) * LANES]

    @pl.when(c == nc)
    def _():
        key = score(kin_ref[...])
        j = lax.broadcasted_iota(I32, (rq, ps), 0)
        n = lax.broadcasted_iota(I32, (rq, ps), 1)
        key = jnp.where((n <= j) & (n < tq), key, INT_MIN)
        new_ref[...] = key
        row_ref[rows_b, nc * ch:nc * ch + ps] = key

    @pl.when((c == nc) & (b == db - 1))
    def _():
        n_rows = row_ref.shape[0]

        def count_ge(scand):
            sk = row_ref[...]
            return _fold_lanes(jnp.where(sk >= jnp.tile(scand, (1, sk.shape[1] // LANES)), 1.0, 0.0))

        thr = _kth_threshold(count_ge, (n_rows, LANES), 1, float(topk), 32)
        new_keys = row_ref[:, nc * ch:nc * ch + ps]
        gt_new = jnp.sum(jnp.where(new_keys > jnp.tile(thr, (1, ps // LANES)), 1.0, 0.0), axis=1, keepdims=True)
        budget = jnp.broadcast_to(topk - gt_new, (n_rows, LANES)).astype(I32)
        thr_ref[...] = jnp.zeros(thr_ref.shape, I32)
        for bb in range(n_rows // rq):
            for jj in range(tq):
                thr_ref[bb * tq + jj, 0:1, :] = thr[bb * rq + jj:bb * rq + jj + 1, :]
                thr_ref[bb * tq + jj, 1:2, :] = budget[bb * rq + jj:bb * rq + jj + 1, :]


def _sample_select(page_table, qi_hm, w_hm, kin_t, cache_kidx_t, *, pages, tq, rq, iheads, topk):
    db, n_pages = page_table.shape
    _, idim, ps = cache_kidx_t.shape
    nc = n_pages // pages
    ch = pages * ps
    assert ps % LANES == 0 and (ch // LANES) % SUBLANES == 0
    page_spec = lambda p: pl.BlockSpec(
        (None, idim, ps), lambda b, c, pt: (pt[b, jnp.minimum(c, nc - 1) * pages + p], 0, 0))
    per_b = lambda shape: pl.BlockSpec((None,) + shape, lambda b, c, pt: (b, 0, 0))
    return pl.pallas_call(
        functools.partial(_sample_select_kernel, pages=pages, ps=ps, nc=nc, iheads=iheads, idim=idim, tq=tq, rq=rq,
                          topk=topk),
        grid_spec=pltpu.PrefetchScalarGridSpec(
            num_scalar_prefetch=1,
            grid=(db, nc + 1),
            in_specs=[per_b((iheads * rq, idim)), per_b((iheads * rq, LANES)), per_b((idim, ps))]
            + [page_spec(p) for p in range(pages)],
            out_specs=[pl.BlockSpec((tq, ch // LANES, LANES), lambda b, c, pt: (b, jnp.minimum(c, nc - 1), 0)),
                       per_b((rq, ps)),
                       pl.BlockSpec((db * tq, SUBLANES, LANES), lambda b, c, pt: (0, 0, 0))],
            scratch_shapes=[pltpu.VMEM((db * rq, nc * ch + ps), I32)],
        ),
        out_shape=[jax.ShapeDtypeStruct((db * tq, n_pages * ps // LANES, LANES), I32),
                   jax.ShapeDtypeStruct((db, rq, ps), I32),
                   jax.ShapeDtypeStruct((db * tq, SUBLANES, LANES), I32)],
        compiler_params=_cparams("arbitrary", "arbitrary"),
        name="sample_select",
    )(page_table, qi_hm, w_hm, kin_t, *([cache_kidx_t] * pages))


def _sc_gather_kernel(past_hbm, thr_hbm, pt_hbm, ck_hbm, cv_hbm, ksel_hbm, vsel_hbm, pos_hbm, cnt_hbm,
                      row_v, thr_v, pt_v, idx_v, phys_v, rows_v, cnt_v, sem,
                      *, nq, tq, topk, ps, n_cores, rows_per_copy):
    wid = lax.axis_index("s") * n_cores + lax.axis_index("c")

    @pl.when(wid < nq)
    def _():
        pltpu.sync_copy(past_hbm.at[wid], row_v)
        pltpu.sync_copy(thr_hbm.at[wid], thr_v)
        pltpu.sync_copy(pt_hbm.at[wid // tq], pt_v)
        thr = thr_v[0, pl.ds(0, SC_LANES)]
        budget = thr_v[1, pl.ds(0, SC_LANES)]
        lane = lax.iota(I32, SC_LANES)
        zero = jnp.zeros((SC_LANES,), I32)
        for t in range(idx_v.shape[0] // SC_LANES):
            idx_v[pl.ds(t * SC_LANES, SC_LANES)] = zero

        def compact(pred):
            def body(r, cnt):
                for t in range(LANES // SC_LANES):
                    x = row_v[r, pl.ds(t * SC_LANES, SC_LANES)]
                    m = pred(x, cnt)
                    rank = plsc.cumsum(jnp.where(m, 1, 0).astype(I32))
                    plsc.store_scatter(idx_v, [cnt + rank - 1], lane + (r * LANES + t * SC_LANES), mask=m)
                    cnt = cnt + plsc.all_reduce_population_count(m)
                return cnt
            return body

        cnt = lax.fori_loop(0, row_v.shape[0], compact(lambda x, cnt: x > thr), zero)
        cnt = lax.fori_loop(0, row_v.shape[0], compact(lambda x, cnt: (x == thr) & (cnt < budget)), cnt)
        cnt_v[...] = jnp.minimum(cnt, budget)
        pltpu.sync_copy(cnt_v, cnt_hbm.at[wid])
        pltpu.sync_copy(idx_v.at[pl.ds(0, topk)], pos_hbm.at[wid])

        shift = ps.bit_length() - 1
        for t in range(topk // SC_LANES):
            pos = idx_v[pl.ds(t * SC_LANES, SC_LANES)]
            page = plsc.load_gather(pt_v, [lax.shift_right_logical(pos, shift)])
            phys_v[pl.ds(t * SC_LANES, SC_LANES)] = page * ps + (pos & (ps - 1))
        for g in range(topk // rows_per_copy):
            sel = phys_v.at[pl.ds(g * rows_per_copy, rows_per_copy)]
            dst = pl.ds(wid * topk + g * rows_per_copy, rows_per_copy)
            for src_hbm, dst_hbm in ((ck_hbm, ksel_hbm), (cv_hbm, vsel_hbm)):
                pltpu.async_copy(src_hbm.at[sel], rows_v, sem).wait()
                pltpu.sync_copy(rows_v, dst_hbm.at[dst])


def _sc_gather(past_keys, thr, page_table, cache_k, cache_v, *, tq, topk, ps):
    nq, key_rows, _ = past_keys.shape
    _, heads, hd = cache_k.shape
    assert ps & (ps - 1) == 0 and topk % SC_LANES == 0
    rows_per_copy = 64
    assert topk % rows_per_copy == 0
    mesh = plsc.VectorSubcoreMesh(core_axis_name="c", subcore_axis_name="s", num_cores=V7X_SC_CORES,
                                  num_subcores=V7X_SC_SUBCORES)
    assert nq <= V7X_SC_CORES * V7X_SC_SUBCORES
    sel_shape = jax.ShapeDtypeStruct((nq * topk, heads, hd), F32)
    return pl.kernel(
        functools.partial(_sc_gather_kernel, nq=nq, tq=tq, topk=topk, ps=ps, n_cores=V7X_SC_CORES,
                          rows_per_copy=rows_per_copy),
        out_type=[sel_shape, sel_shape, jax.ShapeDtypeStruct((nq, topk), I32),
                  jax.ShapeDtypeStruct((nq, SC_LANES), I32)],
        mesh=mesh,
        scratch_types=[pltpu.VMEM((key_rows, LANES), I32), pltpu.VMEM((SUBLANES, LANES), I32),
                       pltpu.VMEM((page_table.shape[1],), I32), pltpu.VMEM((topk + SC_LANES,), I32),
                       pltpu.VMEM((topk,), I32), pltpu.VMEM((rows_per_copy, heads, hd), F32),
                       pltpu.VMEM((SC_LANES,), I32), pltpu.SemaphoreType.DMA],
        compiler_params=pltpu.CompilerParams(needs_layout_passes=False),
        name="sc_select_gather",
    )(past_keys, thr, page_table, cache_k, cache_v)


def _sample_attn_sel_kernel(cnt_ref, relbt_ref, q_ref, ksel_ref, vsel_ref, pos_ref, snew_ref, thr_ref, kn_ref, vn_ref,
                            o_ref, *, heads, tq, past, n_buckets):
    w = pl.program_id(0)
    j = w % tq
    q = q_ref[...]

    def head_bias(dist):
        bucket = _bucket(dist, n_buckets)
        acc = jnp.zeros(bucket.shape, F32)
        for bkt in range(n_buckets):
            acc = jnp.where(bucket == bkt, relbt_ref[:, bkt:bkt + 1], acc)
        return acc * LOG2E

    def logits(keys):
        lg = _dot_nt(q, keys)
        head = lax.broadcasted_iota(I32, lg.shape, 0)
        col = lax.broadcasted_iota(I32, lg.shape, 1)
        return lg, (col & (heads - 1)) == head, col

    lg, own, col = logits(ksel_ref[...].astype(BF16))
    keep = own & (col < cnt_ref[w, 0] * heads)
    lg = jnp.where(keep, lg + head_bias(jnp.broadcast_to(past + j - pos_ref[...], lg.shape)), NEG_BIG)
    lgn, own, col = logits(kn_ref[...])
    keep = own & (snew_ref[...] >= thr_ref[0:1, :])
    lgn = jnp.where(keep, lgn + head_bias(j - lax.shift_right_logical(col, heads.bit_length() - 1)), NEG_BIG)

    m = jnp.maximum(jnp.max(lg, axis=1, keepdims=True), jnp.max(lgn, axis=1, keepdims=True))
    p = jnp.exp2(lg - m)
    pn = jnp.exp2(lgn - m)
    denom = jnp.sum(p, axis=1, keepdims=True) + jnp.sum(pn, axis=1, keepdims=True)
    acc = _dot(p.astype(BF16), vsel_ref[...].astype(BF16)) + _dot(pn.astype(BF16), vn_ref[...])
    o_ref[...] = acc / denom


def _sample_attn_sel(cnt, rel_bias_t, q, ksel, vsel, pos, snew, thr, kn, vn, *, heads, hd, topk, tq, past):
    nq = q.shape[0]
    ps = kn.shape[1]
    assert ps == LANES and hd == LANES and heads & (heads - 1) == 0
    per_q = lambda shape: pl.BlockSpec((None,) + shape, lambda w, cnt: (w, 0, 0))
    per_b = lambda shape: pl.BlockSpec((None,) + shape, lambda w, cnt: (w // tq, 0, 0))
    sel_spec = pl.BlockSpec((topk * heads, hd), lambda w, cnt: (w, 0))
    return pl.pallas_call(
        functools.partial(_sample_attn_sel_kernel, heads=heads, tq=tq, past=past, n_buckets=rel_bias_t.shape[1]),
        grid_spec=pltpu.PrefetchScalarGridSpec(
            num_scalar_prefetch=1,
            grid=(nq,),
            in_specs=[pl.BlockSpec(rel_bias_t.shape, lambda w, cnt: (0, 0)), per_q((heads, hd)), sel_spec, sel_spec,
                      per_q((1, topk * heads)), per_q((1, ps)), per_q((SUBLANES, LANES)), per_b((ps, hd)),
                      per_b((ps, hd))],
            out_specs=per_q((heads, hd)),
        ),
        out_shape=jax.ShapeDtypeStruct((nq, heads, hd), F32),
        compiler_params=_cparams("arbitrary"),
        name="sample_attn_sel",
    )(cnt, rel_bias_t, q, ksel, vsel, pos, snew, thr, kn, vn)


def _mix_kernel(x_ref, g_ref, a_ref, c_ref, w_hbm, wpa_ref, wpb_ref, wo_ref, o_ref, wg_ref, *, d, row0):
    _stage_rows(w_hbm, wg_ref, row0)
    x = x_ref[...]
    hb = _rms_bf16(x, g_ref[...])
    a = _dot(a_ref[...], wpa_ref[...])
    m = jax.nn.sigmoid(_dot_nt(hb, wg_ref[0:d, :])) * a
    c = _dot(c_ref[...], wpb_ref[...])
    m = m + jax.nn.sigmoid(_dot_nt(hb, wg_ref[d:2 * d, :])) * c
    o_ref[...] = x + _dot(m.astype(BF16), wo_ref[...])


def _mix(x2d, g, attn, c_in, w_all, row0, w_pa, w_pb, w_o, *, tm):
    m, d = x2d.shape
    assert row0 % 16 == 0
    row = lambda width: pl.BlockSpec((tm, width), lambda i: (i, 0))
    return pl.pallas_call(
        functools.partial(_mix_kernel, d=d, row0=row0),
        grid=(m // tm,),
        in_specs=[row(d), _resident((1, d)), row(attn.shape[1]), row(c_in.shape[1]),
                  pl.BlockSpec(memory_space=pl.ANY), _resident(w_pa.shape), _resident(w_pb.shape),
                  _resident(w_o.shape)],
        out_specs=row(d),
        out_shape=jax.ShapeDtypeStruct((m, d), F32),
        scratch_shapes=[pltpu.VMEM((2 * d, d), BF16)],
        compiler_params=_cparams("arbitrary"),
        name="mix_out",
    )(x2d, g, attn, c_in, w_all, w_pa, w_pb, w_o)


def _mlp_kernel(x_ref, g_ref, gf_ref, w1_ref, w2_ref, y_ref, h_ref, *, final):
    f = pl.program_id(1)

    @pl.when(f == 0)
    def _():
        h_ref[...] = _rms_bf16(x_ref[...], g_ref[...])
        y_ref[...] = jnp.zeros(y_ref.shape, F32)

    t = jnp.square(jnp.maximum(_dot(h_ref[...], w1_ref[...]), 0.0))
    y_ref[...] += _dot(t.astype(BF16), w2_ref[...])

    @pl.when(f == pl.num_programs(1) - 1)
    def _():
        x2 = x_ref[...] + y_ref[...]
        if final:
            x2 = x2 * lax.rsqrt(jnp.mean(x2 * x2, axis=-1, keepdims=True) + EPS) * gf_ref[...]
        y_ref[...] = x2


def _mlp(x2d, g, gf, w1, w2, *, tm, tf, final):
    m, d = x2d.shape
    ff = w1.shape[1]
    return pl.pallas_call(
        functools.partial(_mlp_kernel, final=final),
        grid=(m // tm, ff // tf),
        in_specs=[pl.BlockSpec((tm, d), lambda i, f: (i, 0)), _resident((1, d)), _resident((1, d)),
                  pl.BlockSpec((d, tf), lambda i, f: (0, f)), pl.BlockSpec((tf, d), lambda i, f: (f, 0))],
        out_specs=pl.BlockSpec((tm, d), lambda i, f: (i, 0)),
        out_shape=jax.ShapeDtypeStruct((m, d), F32),
        scratch_shapes=[pltpu.VMEM((tm, d), BF16)],
        compiler_params=_cparams("arbitrary", "arbitrary"),
        name="mlp",
    )(x2d, g, gf, w1, w2)


def _tile(m, cap):
    return min(m, cap)


def kernel(x_prompt, x_sample, cache_k, cache_v, cache_kidx, state_conv, page_table, rel_bias, norm_mix_g, w_in,
           conv_w, w_pa, w_pb, w_o, norm_mlp_g, w_mlp_in, w_mlp_out, norm_final_g):
    batch, seq, d = x_prompt.shape
    db, tq, _ = x_sample.shape
    depth, n_pool, ps, heads, hd = cache_k.shape
    idim = cache_kidx.shape[-1]
    cw = conv_w.shape[-1]
    aw = heads * hd
    n_in = w_in.shape[-1]
    iheads = (n_in - 3 * aw - idim - 3 * cw - 2 * d) // (idim + 1)
    iw = iheads * idim
    assert 3 * aw + iw + idim + iheads + 3 * cw + 2 * d == n_in and idim + iheads <= LANES
    n_pages = page_table.shape[1]
    past = n_pages * ps
    rq = SUBLANES
    assert tq <= rq

    mp, ms = batch * seq, db * tq
    xp = x_prompt.reshape(mp, d)
    xs = x_sample.reshape(ms, d)
    tb = _tile(seq, 256)
    pages = math.gcd(n_pages, 8)
    o_small = 3 * aw + iw
    o_conv = o_small + idim + iheads
    o_gate = o_conv + 3 * cw
    gf = norm_final_g.reshape(1, d)
    bias_tiles = _bias_tiles(rel_bias, tb=tb)

    outs = {k: [] for k in ("kp", "vp", "kip", "sp", "ks", "vs", "kis", "ss")}
    for l in range(depth):
        wl = jnp.swapaxes(w_in[l], 0, 1).astype(BF16)
        g_mix = norm_mix_g[l].reshape(1, d)
        g_mlp = norm_mlp_g[l].reshape(1, d)

        tm = _tile(seq, 512)
        q, k, v, kb, vb, qi, sm, smb = _proj_attn(xp, g_mix, wl, aw=aw, iw=iw, n_small=idim + iheads, hd=hd, tm=tm)
        c_in, u_tail, wpa, wpb, wo = _proj_conv(xp, g_mix, wl, o_conv, conv_w[l], None, (w_pa[l], w_pb[l], w_o[l]),
                                                seq=seq, tm=_tile(seq, 512))
        attn, w1, w2 = _attn_prompt(bias_tiles, qi, sm, smb, q, kb, vb, (w_mlp_in[l], w_mlp_out[l]), batch=batch,
                                    seq=seq, tb=tb, heads=heads, hd=hd, iheads=iheads, idim=idim,
                                    topk=min(TOPK_MAX, seq // 4))
        x1 = _mix(xp, g_mix, attn, c_in, wl, o_gate, wpa, wpb, wo, tm=_tile(mp, 256))
        xp_next = _mlp(x1, g_mlp, gf, w1, w2, tm=_tile(mp, 1024), tf=_tile(w1.shape[1], 512), final=l == depth - 1)
        outs["kp"].append(k.reshape(batch, seq, heads, hd))
        outs["vp"].append(v.reshape(batch, seq, heads, hd))
        outs["kip"].append(sm[:, :idim].reshape(batch, seq, idim))
        outs["sp"].append(u_tail.reshape(batch, -1, SUBLANES, cw)[:, -1, SUBLANES - 2:])

        q, k, v, kb, vb, qi, sm, smb = _proj_attn(xs, g_mix, wl, aw=aw, iw=iw, n_small=idim + iheads, hd=hd, tm=ms)
        st = state_conv[l]
        zero = jnp.zeros((db, tq - 1, cw), F32)
        prev1 = jnp.concatenate([st[:, 1:2], zero], axis=1).reshape(ms, cw)
        prev2 = jnp.concatenate([st, zero[:, 1:]], axis=1).reshape(ms, cw)
        c_in, u_all = _proj_conv(xs, g_mix, wl, o_conv, conv_w[l], (prev1, prev2), seq=tq, tm=ms)

        def pad_rows(a, n):
            return jnp.pad(a, ((0, 0), (0, n - a.shape[1])) + ((0, 0),) * (a.ndim - 2))

        qi_hm = pad_rows(qi.reshape(db, tq, iheads, idim).transpose(0, 2, 1, 3).reshape(db * iheads, tq, idim), rq)
        qi_hm = qi_hm.reshape(db, iheads * rq, idim)
        w_hm = pad_rows(sm[:, idim:idim + iheads].reshape(db, tq, iheads).transpose(0, 2, 1).reshape(db * iheads, tq), rq)
        w_hm = jnp.broadcast_to(w_hm.reshape(db, iheads * rq, 1), (db, iheads * rq, LANES))
        kin_t = jnp.swapaxes(pad_rows(smb[:, :idim].reshape(db, tq, idim), ps), 1, 2)
        topk_s = min(TOPK_MAX, (past + tq) // 4)
        past_keys, snew, thr = _sample_select(page_table, qi_hm, w_hm, kin_t, jnp.swapaxes(cache_kidx[l], 1, 2),
                                              pages=math.gcd(n_pages, 32), tq=tq, rq=rq, iheads=iheads, topk=topk_s)
        ksel, vsel, pos, cnt = _sc_gather(past_keys, thr, page_table, cache_k[l].reshape(n_pool * ps, heads, hd),
                                          cache_v[l].reshape(n_pool * ps, heads, hd), tq=tq, topk=topk_s, ps=ps)
        pos_rows = jnp.repeat(pos, heads, axis=1).reshape(ms, 1, topk_s * heads)
        snew_rows = jnp.repeat(snew[:, :tq, :ps // heads], heads, axis=2).reshape(ms, 1, ps)
        attn = _sample_attn_sel(cnt, rel_bias.T, q.reshape(ms, heads, hd), ksel.reshape(ms * topk_s * heads, hd),
                                vsel.reshape(ms * topk_s * heads, hd), pos_rows, snew_rows, thr,
                                pad_rows(kb.reshape(db, tq * heads, hd), ps), pad_rows(vb.reshape(db, tq * heads, hd), ps),
                                heads=heads, hd=hd, topk=topk_s, tq=tq, past=past)
        attn = attn.reshape(ms, aw).astype(BF16)
        x1 = _mix(xs, g_mix, attn, c_in, wl, o_gate, wpa, wpb, wo, tm=ms)
        xs_next = _mlp(x1, g_mlp, gf, w1, w2, tm=ms, tf=_tile(w1.shape[1], 1024), final=l == depth - 1)
        outs["ks"].append(k.reshape(db, tq, heads, hd))
        outs["vs"].append(v.reshape(db, tq, heads, hd))
        outs["kis"].append(sm[:, :idim].reshape(db, tq, idim))
        outs["ss"].append(u_all.reshape(db, tq, cw)[:, tq - 2:])
        xp, xs = xp_next, xs_next

    st = {k: jnp.stack(v) for k, v in outs.items()}
    return (xp.reshape(batch, seq, d), xs.reshape(db, tq, d), st["kp"], st["vp"], st["kip"], st["sp"],
            st["ks"], st["vs"], st["kis"], st["ss"])
```

```python
import functools
import math

import jax
import jax.numpy as jnp
import numpy as np
from jax import lax
from jax.experimental import pallas as pl
from jax.experimental.pallas import tpu as pltpu
from jax.experimental.pallas import tpu_sc as plsc

F32 = jnp.float32
BF16 = jnp.bfloat16
I32 = jnp.int32

TOPK_MAX = 256
MAX_DISTANCE = 128
EPS = 1e-6

LANES = 128
SUBLANES = 8
V7X_SCOPED_VMEM_BYTES = 60000 * 1024
SC_LANES = 16
V7X_SC_CORES = 2
V7X_SC_SUBCORES = 16

LOG2E = math.log2(math.e)
INT_MIN = np.int32(-2 ** 31)
NEG_BIG = -1e30


def _cparams(*sem):
    return pltpu.CompilerParams(dimension_semantics=sem, vmem_limit_bytes=V7X_SCOPED_VMEM_BYTES)


def _resident(shape):
    nd = len(shape)
    return pl.BlockSpec(shape, lambda *_: (0,) * nd, pipeline_mode=pl.Buffered(1))


def _rms_bf16(x, g):
    y = x * lax.rsqrt(jnp.mean(x * x, axis=-1, keepdims=True) + EPS)
    return (y * g).astype(BF16)


def _dot(a, b):
    return jnp.dot(a, b, preferred_element_type=F32)


def _dot_nt(a, b):
    return lax.dot_general(a, b, (((1,), (1,)), ((), ())), preferred_element_type=F32)


def _ordered_key(x):
    b = lax.bitcast_convert_type(x, I32)
    return b ^ ((b >> 31) & np.int32(0x7FFFFFFF))


def _bucket(n, n_buckets):
    n = jnp.maximum(n, 0)
    me = n_buckets // 2
    nf = jnp.maximum(n, me).astype(F32)
    large = me + (jnp.log(nf / me) / math.log(MAX_DISTANCE / me) * (n_buckets - me)).astype(I32)
    large = jnp.minimum(large, n_buckets - 1)
    return jnp.where(n < me, n, large)


def _bias_of_bucket(bucket, relb_ref, h, n_buckets):
    acc = jnp.zeros(bucket.shape, F32)
    for bkt in range(n_buckets):
        acc = jnp.where(bucket == bkt, relb_ref[bkt, h], acc)
    return acc


def _fold_lanes(x):
    acc = x[:, 0:LANES]
    for t in range(1, x.shape[1] // LANES):
        acc = acc + x[:, t * LANES:(t + 1) * LANES]
    return acc


def _fold_rows(x, op=jnp.add):
    parts = [x[t * SUBLANES:(t + 1) * SUBLANES, :] for t in range(x.shape[0] // SUBLANES)]
    while len(parts) > 1:
        parts = [op(parts[t], parts[t + 1]) for t in range(0, len(parts) - 1, 2)] + parts[len(parts) & ~1:]
    return parts[0]


def _kth_threshold(count_ge, shape, axis, k, trips):
    def bit_body(t, uthr):
        cand = uthr | jnp.left_shift(np.int32(1), 31 - t)
        cnt = jnp.sum(count_ge(cand ^ INT_MIN), axis=axis, keepdims=True)
        return jnp.where(cnt >= k, cand, uthr)

    uthr = lax.fori_loop(0, trips, bit_body, jnp.zeros(shape, I32))
    return jnp.maximum(uthr ^ INT_MIN, INT_MIN + 1)


def _flash_update(lg, v, m_ref, l_ref, acc_ref, h, hd):
    reps = lg.shape[1] // LANES
    m_prev = m_ref[h]
    m_new = jnp.maximum(m_prev, jnp.max(lg, axis=1, keepdims=True))
    p = jnp.exp2(lg - jnp.tile(m_new, (1, reps)))
    alpha = jnp.exp2(m_prev - m_new)
    pv = _dot(p.astype(BF16), jnp.concatenate([v, jnp.ones((v.shape[0], LANES), BF16)], axis=1))
    l_ref[h] = alpha * l_ref[h] + pv[:, hd:]
    m_ref[h] = m_new
    hs = slice(h * hd, (h + 1) * hd)
    acc_ref[:, hs] = acc_ref[:, hs] * jnp.tile(alpha, (1, hd // LANES)) + pv[:, :hd]


def _stage_rows(w_hbm, w_ref, row0):
    @pl.when(pl.program_id(0) == 0)
    def _():
        pltpu.sync_copy(w_hbm.at[pl.ds(row0, w_ref.shape[0])], w_ref)


def _proj_attn_kernel(x_ref, g_ref, w_hbm, q_ref, k_ref, v_ref, kb_ref, vb_ref, qi_ref,
                      sm_ref, smb_ref, w_ref, ws_ref, *, aw, iw, hd, tm, n_small, qscale):
    heads = aw // hd
    _stage_rows(w_hbm, w_ref, 0)

    @pl.when(pl.program_id(0) == 0)
    def _():
        ws_ref[...] = jnp.zeros(ws_ref.shape, BF16)
        pltpu.sync_copy(w_hbm.at[pl.ds(w_ref.shape[0], n_small)], ws_ref.at[pl.ds(0, n_small)])

    hb = _rms_bf16(x_ref[...], g_ref[...])
    q_ref[...] = (_dot_nt(hb, w_ref[0:aw, :]) * qscale).astype(BF16)
    for o_ref, ob_ref, c0 in ((k_ref, kb_ref, aw), (v_ref, vb_ref, 2 * aw)):
        kv = _dot_nt(hb, w_ref[c0:c0 + aw, :])
        ob_ref[...] = kv.astype(BF16)
        for h in range(heads):
            o_ref[pl.ds(h, tm, stride=heads), :] = kv[:, h * hd:(h + 1) * hd]
    qi_ref[...] = _dot_nt(hb, w_ref[3 * aw:3 * aw + iw, :]).astype(BF16)
    sm = _dot_nt(hb, ws_ref[...])
    sm_ref[...] = sm
    smb_ref[...] = sm.astype(BF16)


def _proj_attn(x2d, g, w_all, *, aw, iw, n_small, hd, tm):
    m, d = x2d.shape
    wt = 3 * aw + iw
    heads = aw // hd
    assert hd == LANES and n_small <= LANES and wt % 16 == 0 and n_small % 16 == 0
    row = lambda width: pl.BlockSpec((tm, width), lambda i: (i, 0))
    kv_spec = pl.BlockSpec((tm * heads, hd), lambda i: (i, 0))
    kv_shape = jax.ShapeDtypeStruct((m * heads, hd), F32)
    outs = [(aw, BF16), None, None, (aw, BF16), (aw, BF16), (iw, BF16), (LANES, F32), (LANES, BF16)]
    return pl.pallas_call(
        functools.partial(_proj_attn_kernel, aw=aw, iw=iw, hd=hd, tm=tm, n_small=n_small,
                          qscale=hd ** -0.5 * LOG2E),
        grid=(m // tm,),
        in_specs=[row(d), _resident((1, d)), pl.BlockSpec(memory_space=pl.ANY)],
        out_specs=[kv_spec if o is None else row(o[0]) for o in outs],
        out_shape=[kv_shape if o is None else jax.ShapeDtypeStruct((m, o[0]), o[1]) for o in outs],
        scratch_shapes=[pltpu.VMEM((wt, d), BF16), pltpu.VMEM((LANES, d), BF16)],
        compiler_params=_cparams("arbitrary"),
        name="proj_attn",
    )(x2d, g, w_all)


def _proj_conv_kernel(*refs, cw, tm, seq, tiles_per_seq, tail, row0, n_cast):
    if tiles_per_seq:
        x_ref, g_ref, w_hbm, cwt_ref = refs[:4]
        c_ref, ut_ref = refs[4 + n_cast:6 + n_cast]
        w_ref, carry_ref = refs[6 + 2 * n_cast:]
        _cast_slabs(refs[4:4 + n_cast], refs[6 + n_cast:6 + 2 * n_cast])
    else:
        x_ref, g_ref, w_hbm, cwt_ref, p1_ref, p2_ref, c_ref, ut_ref, w_ref = refs
    _stage_rows(w_hbm, w_ref, row0)
    hb = _rms_bf16(x_ref[...], g_ref[...])
    cx = _dot_nt(hb, w_ref[0:cw, :])
    cb = _dot_nt(hb, w_ref[cw:2 * cw, :])
    cc = _dot_nt(hb, w_ref[2 * cw:3 * cw, :])
    u = cc * cx
    r = lax.broadcasted_iota(I32, (tm, cw), 0)
    um1 = pltpu.roll(u, 1, 0)
    um2 = pltpu.roll(u, 2, 0)
    if tiles_per_seq:
        @pl.when(pl.program_id(0) % tiles_per_seq == 0)
        def _():
            carry_ref[...] = jnp.zeros_like(carry_ref)
        prev1 = jnp.broadcast_to(carry_ref[SUBLANES - 1:SUBLANES, :], (tm, cw))
        prev2 = jnp.broadcast_to(carry_ref[SUBLANES - 2:SUBLANES - 1, :], (tm, cw))
        um1 = jnp.where(r == 0, prev1, um1)
        um2 = jnp.where(r == 0, prev2, jnp.where(r == 1, prev1, um2))
        carry_ref[...] = u[tm - SUBLANES:tm, :]
    else:
        assert seq & (seq - 1) == 0
        t = r & (seq - 1)
        um1 = jnp.where(t >= 1, um1, p1_ref[...])
        um2 = jnp.where(t >= 2, um2, p2_ref[...])
    y = cwt_ref[0:1, :] * um2 + cwt_ref[1:2, :] * um1 + cwt_ref[2:3, :] * u
    c_ref[...] = (cb * y).astype(BF16)
    ut_ref[...] = u[tm - tail:tm, :]


def _proj_conv(x2d, g, w_all, row0, conv_w, prev, to_cast=(), *, seq, tm):
    m, d = x2d.shape
    cw = conv_w.shape[1]
    assert conv_w.shape[0] == 3 and row0 % 16 == 0
    row = lambda width: pl.BlockSpec((tm, width), lambda i: (i, 0))
    in_specs = [row(d), _resident((1, d)), pl.BlockSpec(memory_space=pl.ANY), _resident((3, cw))]
    args = [x2d, g, w_all, conv_w]
    scratch = [pltpu.VMEM((3 * cw, d), BF16)]
    if prev is None:
        assert seq % tm == 0
        tiles_per_seq, tail = seq // tm, SUBLANES
        scratch += [pltpu.VMEM((SUBLANES, cw), F32)]
    else:
        assert tm % seq == 0 and seq >= 2
        tiles_per_seq, tail = 0, tm
        in_specs += [row(cw), row(cw)]
        args += list(prev)
    assert not (to_cast and prev is not None)
    c_in, c_out, c_shapes = _cast_specs(to_cast, m // tm, lambda i: i)
    return pl.pallas_call(
        functools.partial(_proj_conv_kernel, cw=cw, tm=tm, seq=seq, tiles_per_seq=tiles_per_seq, tail=tail,
                          row0=row0, n_cast=len(to_cast)),
        grid=(m // tm,),
        in_specs=in_specs + c_in,
        out_specs=[row(cw), pl.BlockSpec((tail, cw), lambda i: (i, 0))] + c_out,
        out_shape=[jax.ShapeDtypeStruct((m, cw), BF16), jax.ShapeDtypeStruct((m // tm * tail, cw), F32)] + c_shapes,
        scratch_shapes=scratch,
        compiler_params=_cparams("arbitrary"),
        name="proj_conv",
    )(*args, *to_cast)


def _bias_tiles_kernel(relb_ref, o_ref, *, tb, n_buckets):
    kind = pl.program_id(0)
    h = pl.program_id(1)
    qry = lax.broadcasted_iota(I32, (tb, tb), 0)
    key = lax.broadcasted_iota(I32, (tb, tb), 1)
    bias = _bias_of_bucket(_bucket(kind * tb + qry - key, n_buckets), relb_ref, h, n_buckets)
    o_ref[...] = (bias - relb_ref[n_buckets - 1, h]) * LOG2E


def _bias_tiles(rel_bias, *, tb):
    n_buckets, heads = rel_bias.shape
    return pl.pallas_call(
        functools.partial(_bias_tiles_kernel, tb=tb, n_buckets=n_buckets),
        grid=(2, heads),
        in_specs=[pl.BlockSpec(memory_space=pltpu.SMEM)],
        out_specs=pl.BlockSpec((None, None, tb, tb), lambda a, h: (a, h, 0, 0)),
        out_shape=jax.ShapeDtypeStruct((2, heads, tb, tb), F32),
        compiler_params=_cparams("arbitrary", "arbitrary"),
        name="bias_tiles",
    )(rel_bias)


def _cast_specs(weights, n_steps, step_of):
    in_specs, out_specs, out_shapes = [], [], []
    for w in weights:
        rows, cols = w.shape
        assert rows % n_steps == 0 and (rows // n_steps) % 16 == 0
        spec = pl.BlockSpec((rows // n_steps, cols), lambda *idx: (step_of(*idx), 0))
        in_specs.append(spec)
        out_specs.append(spec)
        out_shapes.append(jax.ShapeDtypeStruct((rows, cols), BF16))
    return in_specs, out_specs, out_shapes


def _cast_slabs(src_refs, dst_refs):
    for src, dst in zip(src_refs, dst_refs):
        dst[...] = src[...].astype(BF16)


def _attn_prompt_kernel(*refs, n_cast, tb, heads, hd, iheads, idim, topk):
    qi_ref, sm_ref, smb_ref, q_ref, kb_ref, vb_ref, bias_ref = refs[:7]
    o_ref = refs[7 + n_cast]
    skey_ref, thr_ref, wt_ref, madd_ref, m_ref, l_ref, acc_ref = refs[8 + 2 * n_cast:]
    _cast_slabs(refs[7:7 + n_cast], refs[8 + n_cast:8 + 2 * n_cast])
    i = pl.program_id(1)
    wscale = idim ** -0.5 * iheads ** -0.5

    m_ref[...] = jnp.full(m_ref.shape, NEG_BIG, F32)
    l_ref[...] = jnp.zeros(l_ref.shape, F32)
    acc_ref[...] = jnp.zeros(acc_ref.shape, F32)
    wt_ref[...] = sm_ref[...].T * wscale

    def chunk(j):
        return pl.ds(pl.multiple_of(j * tb, tb), tb)

    def score_chunk(j, diag):
        kic = smb_ref[chunk(j), 0:idim]
        acc = jnp.zeros((tb, tb), F32)
        for h in range(iheads):
            s = _dot_nt(kic, qi_ref[:, h * idim:(h + 1) * idim])
            acc = acc + jnp.maximum(s, 0.0) * wt_ref[idim + h:idim + h + 1, :]
        key = _ordered_key(acc)
        if diag:
            kpos = lax.broadcasted_iota(I32, (tb, tb), 0)
            qpos = lax.broadcasted_iota(I32, (tb, tb), 1)
            key = jnp.where(kpos > qpos, INT_MIN, key)
        skey_ref[chunk(j), :] = key

    def score_body(j, carry):
        score_chunk(j, False)
        return carry

    lax.fori_loop(0, i, score_body, 0)
    score_chunk(i, True)

    def count_ge(scand):
        def body(j, cnt):
            return cnt + _fold_rows(jnp.where(skey_ref[chunk(j), :] >= scand[0:1, :], 1.0, 0.0))

        return lax.fori_loop(0, i + 1, body, jnp.zeros((SUBLANES, tb), F32))

    trips = jnp.where((i + 1) * tb <= topk, 0, 32)
    thr_ref[...] = _kth_threshold(count_ge, (SUBLANES, tb), 0, float(topk), trips)

    def attend_chunk(j, kind):
        madd_ref[...] = jnp.where(skey_ref[chunk(j), :] >= thr_ref[0:1, :], 0.0, NEG_BIG).T
        for h in range(heads):
            hs = slice(h * hd, (h + 1) * hd)
            lg = _dot_nt(q_ref[:, hs], kb_ref[chunk(j), hs]) + madd_ref[...]
            if kind is not None:
                lg = lg + bias_ref[kind, h]
            _flash_update(lg, vb_ref[chunk(j), hs], m_ref, l_ref, acc_ref, h, hd)

    def attend_body(j, carry):
        attend_chunk(j, None)
        return carry

    lax.fori_loop(0, jnp.maximum(i - 1, 0), attend_body, 0)

    @pl.when(i >= 1)
    def _():
        attend_chunk(i - 1, 1)

    attend_chunk(i, 0)

    for h in range(heads):
        hs = slice(h * hd, (h + 1) * hd)
        o_ref[:, hs] = (acc_ref[:, hs] / jnp.tile(l_ref[h], (1, hd // LANES))).astype(BF16)


def _attn_prompt(bias_tiles, qi, sm, smb, q, kb, vb, to_cast, *, batch, seq, tb, heads, hd, iheads, idim, topk):
    m, aw = q.shape
    iw = qi.shape[1]
    nq = seq // tb
    assert tb >= MAX_DISTANCE and tb % LANES == 0 and hd % LANES == 0
    qrow = lambda width: pl.BlockSpec((tb, width), lambda b, i: (b * nq + i, 0))
    seqblk = lambda width: pl.BlockSpec((seq, width), lambda b, i: (b, 0))
    c_in, c_out, c_shapes = _cast_specs(to_cast, batch * nq, lambda b, i: b * nq + i)
    return pl.pallas_call(
        functools.partial(_attn_prompt_kernel, n_cast=len(to_cast), tb=tb, heads=heads, hd=hd, iheads=iheads,
                          idim=idim, topk=topk),
        grid=(batch, nq),
        in_specs=[qrow(iw), qrow(LANES), seqblk(LANES), qrow(aw), seqblk(aw), seqblk(aw),
                  _resident(bias_tiles.shape)] + c_in,
        out_specs=[qrow(aw)] + c_out,
        out_shape=[jax.ShapeDtypeStruct((m, aw), BF16)] + c_shapes,
        scratch_shapes=[pltpu.VMEM((seq, tb), I32), pltpu.VMEM((SUBLANES, tb), I32),
                        pltpu.VMEM((LANES, tb), F32), pltpu.VMEM((tb, tb), F32),
                        pltpu.VMEM((heads, tb, LANES), F32), pltpu.VMEM((heads, tb, LANES), F32),
                        pltpu.VMEM((tb, aw), F32)],
        compiler_params=_cparams("arbitrary", "arbitrary"),
        name="attn_prompt",
    )(qi, sm, smb, q, kb, vb, bias_tiles, *to_cast)


def _sample_select_kernel(pt_ref, qi_ref, w_ref, kin_ref, *rest, pages, ps, nc, iheads, idim, tq, rq, topk):
    page_refs = rest[:pages]
    past_ref, new_ref, thr_ref, row_ref = rest[pages:]
    b = pl.program_id(0)
    c = pl.program_id(1)
    db = pl.num_programs(0)
    ch = pages * ps
    wscale = idim ** -0.5 * iheads ** -0.5
    rows_b = pl.ds(pl.multiple_of(b * rq, rq), rq)

    def score(keys_t):
        s = _dot(qi_ref[...], keys_t)
        t = jnp.maximum(s, 0.0) * (w_ref[:, 0:1] * wscale)
        acc = t[0:rq]
        for h in range(1, iheads):
            acc = acc + t[h * rq:(h + 1) * rq]
        return _ordered_key(acc)

    @pl.when(c < nc)
    def _():
        for p in range(pages):
            key = score(page_refs[p][...].astype(BF16))
            row_ref[rows_b, pl.ds(pl.multiple_of(c * ch + p * ps, LANES), ps)] = key
            for t in range(ps // LANES):
                for j in range(tq):
                    past_ref[j, p * (ps // LANES) + t:p * (ps // LANES) + t + 1, :] = \
                        key[j:j + 1, t * LANES:(t + 1) * LANES]

    @pl.when(c == nc)
    def _():
        key = score(kin_ref[...])
        j = lax.broadcasted_iota(I32, (rq, ps), 0)
        n = lax.broadcasted_iota(I32, (rq, ps), 1)
        key = jnp.where((n <= j) & (n < tq), key, INT_MIN)
        new_ref[...] = key
        row_ref[rows_b, nc * ch:nc * ch + ps] = key

    @pl.when((c == nc) & (b == db - 1))
    def _():
        n_rows = row_ref.shape[0]

        def count_ge(scand):
            sk = row_ref[...]
            return _fold_lanes(jnp.where(sk >= jnp.tile(scand, (1, sk.shape[1] // LANES)), 1.0, 0.0))

        thr = _kth_threshold(count_ge, (n_rows, LANES), 1, float(topk), 32)
        new_keys = row_ref[:, nc * ch:nc * ch + ps]
        gt_new = jnp.sum(jnp.where(new_keys > jnp.tile(thr, (1, ps // LANES)), 1.0, 0.0), axis=1, keepdims=True)
        budget = jnp.broadcast_to(topk - gt_new, (n_rows, LANES)).astype(I32)
        thr_ref[...] = jnp.zeros(thr_ref.shape, I32)
        for bb in range(n_rows // rq):
            for jj in range(tq):
                thr_ref[bb * tq + jj, 0:1, :] = thr[bb * rq + jj:bb * rq + jj + 1, :]
                thr_ref[bb * tq + jj, 1:2, :] = budget[bb * rq + jj:bb * rq + jj + 1, :]


def _sample_select(page_table, qi_hm, w_hm, kin_t, cache_kidx_t, *, pages, tq, rq, iheads, topk):
    db, n_pages = page_table.shape
    _, idim, ps = cache_kidx_t.shape
    nc = n_pages // pages
    ch = pages * ps
    assert ps % LANES == 0 and (ch // LANES) % SUBLANES == 0
    page_spec = lambda p: pl.BlockSpec(
        (None, idim, ps), lambda b, c, pt: (pt[b, jnp.minimum(c, nc - 1) * pages + p], 0, 0))
    per_b = lambda shape: pl.BlockSpec((None,) + shape, lambda b, c, pt: (b, 0, 0))
    return pl.pallas_call(
        functools.partial(_sample_select_kernel, pages=pages, ps=ps, nc=nc, iheads=iheads, idim=idim, tq=tq, rq=rq,
                          topk=topk),
        grid_spec=pltpu.PrefetchScalarGridSpec(
            num_scalar_prefetch=1,
            grid=(db, nc + 1),
            in_specs=[per_b((iheads * rq, idim)), per_b((iheads * rq, LANES)), per_b((idim, ps))]
            + [page_spec(p) for p in range(pages)],
            out_specs=[pl.BlockSpec((tq, ch // LANES, LANES), lambda b, c, pt: (b, jnp.minimum(c, nc - 1), 0)),
                       per_b((rq, ps)),
                       pl.BlockSpec((db * tq, SUBLANES, LANES), lambda b, c, pt: (0, 0, 0))],
            scratch_shapes=[pltpu.VMEM((db * rq, nc * ch + ps), I32)],
        ),
        out_shape=[jax.ShapeDtypeStruct((db * tq, n_pages * ps // LANES, LANES), I32),
                   jax.ShapeDtypeStruct((db, rq, ps), I32),
                   jax.ShapeDtypeStruct((db * tq, SUBLANES, LANES), I32)],
        compiler_params=_cparams("arbitrary", "arbitrary"),
        name="sample_select",
    )(page_table, qi_hm, w_hm, kin_t, *([cache_kidx_t] * pages))


def _sc_gather_kernel(past_hbm, thr_hbm, pt_hbm, ck_hbm, cv_hbm, ksel_hbm, vsel_hbm, pos_hbm, cnt_hbm,
                      row_v, thr_v, pt_v, idx_v, phys_v, rows_v, cnt_v, sem,
                      *, nq, tq, topk, ps, n_cores, rows_per_copy):
    wid = lax.axis_index("s") * n_cores + lax.axis_index("c")

    @pl.when(wid < nq)
    def _():
        pltpu.sync_copy(past_hbm.at[wid], row_v)
        pltpu.sync_copy(thr_hbm.at[wid], thr_v)
        pltpu.sync_copy(pt_hbm.at[wid // tq], pt_v)
        thr = thr_v[0, pl.ds(0, SC_LANES)]
        budget = thr_v[1, pl.ds(0, SC_LANES)]
        lane = lax.iota(I32, SC_LANES)
        zero = jnp.zeros((SC_LANES,), I32)
        for t in range(idx_v.shape[0] // SC_LANES):
            idx_v[pl.ds(t * SC_LANES, SC_LANES)] = zero

        def compact(pred):
            def body(r, cnt):
                for t in range(LANES // SC_LANES):
                    x = row_v[r, pl.ds(t * SC_LANES, SC_LANES)]
                    m = pred(x, cnt)
                    rank = plsc.cumsum(jnp.where(m, 1, 0).astype(I32))
                    plsc.store_scatter(idx_v, [cnt + rank - 1], lane + (r * LANES + t * SC_LANES), mask=m)
                    cnt = cnt + plsc.all_reduce_population_count(m)
                return cnt
            return body

        cnt = lax.fori_loop(0, row_v.shape[0], compact(lambda x, cnt: x > thr), zero)
        cnt = lax.fori_loop(0, row_v.shape[0], compact(lambda x, cnt: (x == thr) & (cnt < budget)), cnt)
        cnt_v[...] = jnp.minimum(cnt, budget)
        pltpu.sync_copy(cnt_v, cnt_hbm.at[wid])
        pltpu.sync_copy(idx_v.at[pl.ds(0, topk)], pos_hbm.at[wid])

        shift = ps.bit_length() - 1
        for t in range(topk // SC_LANES):
            pos = idx_v[pl.ds(t * SC_LANES, SC_LANES)]
            page = plsc.load_gather(pt_v, [lax.shift_right_logical(pos, shift)])
            phys_v[pl.ds(t * SC_LANES, SC_LANES)] = page * ps + (pos & (ps - 1))
        for g in range(topk // rows_per_copy):
            sel = phys_v.at[pl.ds(g * rows_per_copy, rows_per_copy)]
            dst = pl.ds(wid * topk + g * rows_per_copy, rows_per_copy)
            for src_hbm, dst_hbm in ((ck_hbm, ksel_hbm), (cv_hbm, vsel_hbm)):
                pltpu.async_copy(src_hbm.at[sel], rows_v, sem).wait()
                pltpu.sync_copy(rows_v, dst_hbm.at[dst])


def _sc_gather(past_keys, thr, page_table, cache_k, cache_v, *, tq, topk, ps):
    nq, key_rows, _ = past_keys.shape
    _, heads, hd = cache_k.shape
    assert ps & (ps - 1) == 0 and topk % SC_LANES == 0
    rows_per_copy = 64
    assert topk % rows_per_copy == 0
    mesh = plsc.VectorSubcoreMesh(core_axis_name="c", subcore_axis_name="s", num_cores=V7X_SC_CORES,
                                  num_subcores=V7X_SC_SUBCORES)
    assert nq <= V7X_SC_CORES * V7X_SC_SUBCORES
    sel_shape = jax.ShapeDtypeStruct((nq * topk, heads, hd), F32)
    return pl.kernel(
        functools.partial(_sc_gather_kernel, nq=nq, tq=tq, topk=topk, ps=ps, n_cores=V7X_SC_CORES,
                          rows_per_copy=rows_per_copy),
        out_type=[sel_shape, sel_shape, jax.ShapeDtypeStruct((nq, topk), I32),
                  jax.ShapeDtypeStruct((nq, SC_LANES), I32)],
        mesh=mesh,
        scratch_types=[pltpu.VMEM((key_rows, LANES), I32), pltpu.VMEM((SUBLANES, LANES), I32),
                       pltpu.VMEM((page_table.shape[1],), I32), pltpu.VMEM((topk + SC_LANES,), I32),
                       pltpu.VMEM((topk,), I32), pltpu.VMEM((rows_per_copy, heads, hd), F32),
                       pltpu.VMEM((SC_LANES,), I32), pltpu.SemaphoreType.DMA],
        compiler_params=pltpu.CompilerParams(needs_layout_passes=False),
        name="sc_select_gather",
    )(past_keys, thr, page_table, cache_k, cache_v)


def _sample_attn_sel_kernel(cnt_ref, relbt_ref, q_ref, ksel_ref, vsel_ref, pos_ref, snew_ref, thr_ref, kn_ref, vn_ref,
                            o_ref, *, heads, tq, past, n_buckets):
    w = pl.program_id(0)
    j = w % tq
    q = q_ref[...]

    def head_bias(dist):
        bucket = _bucket(dist, n_buckets)
        acc = jnp.zeros(bucket.shape, F32)
        for bkt in range(n_buckets):
            acc = jnp.where(bucket == bkt, relbt_ref[:, bkt:bkt + 1], acc)
        return acc * LOG2E

    def logits(keys):
        lg = _dot_nt(q, keys)
        head = lax.broadcasted_iota(I32, lg.shape, 0)
        col = lax.broadcasted_iota(I32, lg.shape, 1)
        return lg, (col & (heads - 1)) == head, col

    lg, own, col = logits(ksel_ref[...].astype(BF16))
    keep = own & (col < cnt_ref[w, 0] * heads)
    lg = jnp.where(keep, lg + head_bias(jnp.broadcast_to(past + j - pos_ref[...], lg.shape)), NEG_BIG)
    lgn, own, col = logits(kn_ref[...])
    keep = own & (snew_ref[...] >= thr_ref[0:1, :])
    lgn = jnp.where(keep, lgn + head_bias(j - lax.shift_right_logical(col, heads.bit_length() - 1)), NEG_BIG)

    m = jnp.maximum(jnp.max(lg, axis=1, keepdims=True), jnp.max(lgn, axis=1, keepdims=True))
    p = jnp.exp2(lg - m)
    pn = jnp.exp2(lgn - m)
    denom = jnp.sum(p, axis=1, keepdims=True) + jnp.sum(pn, axis=1, keepdims=True)
    acc = _dot(p.astype(BF16), vsel_ref[...].astype(BF16)) + _dot(pn.astype(BF16), vn_ref[...])
    o_ref[...] = acc / denom


def _sample_attn_sel(cnt, rel_bias_t, q, ksel, vsel, pos, snew, thr, kn, vn, *, heads, hd, topk, tq, past):
    nq = q.shape[0]
    ps = kn.shape[1]
    assert ps == LANES and hd == LANES and heads & (heads - 1) == 0
    per_q = lambda shape: pl.BlockSpec((None,) + shape, lambda w, cnt: (w, 0, 0))
    per_b = lambda shape: pl.BlockSpec((None,) + shape, lambda w, cnt: (w // tq, 0, 0))
    sel_spec = pl.BlockSpec((topk * heads, hd), lambda w, cnt: (w, 0))
    return pl.pallas_call(
        functools.partial(_sample_attn_sel_kernel, heads=heads, tq=tq, past=past, n_buckets=rel_bias_t.shape[1]),
        grid_spec=pltpu.PrefetchScalarGridSpec(
            num_scalar_prefetch=1,
            grid=(nq,),
            in_specs=[pl.BlockSpec(rel_bias_t.shape, lambda w, cnt: (0, 0)), per_q((heads, hd)), sel_spec, sel_spec,
                      per_q((1, topk * heads)), per_q((1, ps)), per_q((SUBLANES, LANES)), per_b((ps, hd)),
                      per_b((ps, hd))],
            out_specs=per_q((heads, hd)),
        ),
        out_shape=jax.ShapeDtypeStruct((nq, heads, hd), F32),
        compiler_params=_cparams("arbitrary"),
        name="sample_attn_sel",
    )(cnt, rel_bias_t, q, ksel, vsel, pos, snew, thr, kn, vn)


def _mix_kernel(x_ref, xs_ref, g_ref, a_ref, as_ref, c_ref, cs_ref, w_hbm, wpa_ref, wpb_ref, wo_ref, o_ref, os_ref,
                wg_ref, *, d, row0):
    _stage_rows(w_hbm, wg_ref, row0)
    n = pl.num_programs(0) - 1

    def rows(x_ref, a_ref, c_ref, o_ref):
        x = x_ref[...]
        hb = _rms_bf16(x, g_ref[...])
        a = _dot(a_ref[...], wpa_ref[...])
        m = jax.nn.sigmoid(_dot_nt(hb, wg_ref[0:d, :])) * a
        c = _dot(c_ref[...], wpb_ref[...])
        m = m + jax.nn.sigmoid(_dot_nt(hb, wg_ref[d:2 * d, :])) * c
        o_ref[...] = x + _dot(m.astype(BF16), wo_ref[...])

    @pl.when(pl.program_id(0) < n)
    def _():
        rows(x_ref, a_ref, c_ref, o_ref)

    @pl.when(pl.program_id(0) == n)
    def _():
        rows(xs_ref, as_ref, cs_ref, os_ref)


def _mix(xp, xs, g, attn_p, attn_s, c_p, c_s, w_all, row0, w_pa, w_pb, w_o, *, tm):
    m, d = xp.shape
    ms = xs.shape[0]
    n = m // tm
    assert row0 % 16 == 0
    prow = lambda width: pl.BlockSpec((tm, width), lambda i: (jnp.minimum(i, n - 1), 0))
    srow = lambda width: pl.BlockSpec((ms, width), lambda i: (0, 0))
    return pl.pallas_call(
        functools.partial(_mix_kernel, d=d, row0=row0),
        grid=(n + 1,),
        in_specs=[prow(d), srow(d), _resident((1, d)), prow(attn_p.shape[1]), srow(attn_s.shape[1]),
                  prow(c_p.shape[1]), srow(c_s.shape[1]), pl.BlockSpec(memory_space=pl.ANY), _resident(w_pa.shape),
                  _resident(w_pb.shape), _resident(w_o.shape)],
        out_specs=[prow(d), srow(d)],
        out_shape=[jax.ShapeDtypeStruct((m, d), F32), jax.ShapeDtypeStruct((ms, d), F32)],
        scratch_shapes=[pltpu.VMEM((2 * d, d), BF16)],
        compiler_params=_cparams("arbitrary"),
        name="mix_out",
    )(xp, xs, g, attn_p, attn_s, c_p, c_s, w_all, w_pa, w_pb, w_o)


def _mlp_kernel(x_ref, xs_ref, g_ref, gf_ref, w1_ref, w2_ref, y_ref, ys_ref, h_ref, hs_ref, *, final):
    i = pl.program_id(0)
    f = pl.program_id(1)
    last_f = f == pl.num_programs(1) - 1

    def rows(x_ref, y_ref, h_ref):
        @pl.when(f == 0)
        def _():
            h_ref[...] = _rms_bf16(x_ref[...], g_ref[...])
            y_ref[...] = jnp.zeros(y_ref.shape, F32)

        t = jnp.square(jnp.maximum(_dot(h_ref[...], w1_ref[...]), 0.0))
        y_ref[...] += _dot(t.astype(BF16), w2_ref[...])

        @pl.when(last_f)
        def _():
            x2 = x_ref[...] + y_ref[...]
            if final:
                x2 = x2 * lax.rsqrt(jnp.mean(x2 * x2, axis=-1, keepdims=True) + EPS) * gf_ref[...]
            y_ref[...] = x2

    rows(x_ref, y_ref, h_ref)

    @pl.when(i == pl.num_programs(0) - 1)
    def _():
        rows(xs_ref, ys_ref, hs_ref)


def _mlp(xp, xs, g, gf, w1, w2, *, tm, tf, final):
    m, d = xp.shape
    ms = xs.shape[0]
    ff = w1.shape[1]
    srow = pl.BlockSpec((ms, d), lambda i, f: (0, 0))
    return pl.pallas_call(
        functools.partial(_mlp_kernel, final=final),
        grid=(m // tm, ff // tf),
        in_specs=[pl.BlockSpec((tm, d), lambda i, f: (i, 0)), srow, _resident((1, d)), _resident((1, d)),
                  pl.BlockSpec((d, tf), lambda i, f: (0, f)), pl.BlockSpec((tf, d), lambda i, f: (f, 0))],
        out_specs=[pl.BlockSpec((tm, d), lambda i, f: (i, 0)), srow],
        out_shape=[jax.ShapeDtypeStruct((m, d), F32), jax.ShapeDtypeStruct((ms, d), F32)],
        scratch_shapes=[pltpu.VMEM((tm, d), BF16), pltpu.VMEM((ms, d), BF16)],
        compiler_params=_cparams("arbitrary", "arbitrary"),
        name="mlp",
    )(xp, xs, g, gf, w1, w2)


def _tile(m, cap):
    return min(m, cap)


def kernel(x_prompt, x_sample, cache_k, cache_v, cache_kidx, state_conv, page_table, rel_bias, norm_mix_g, w_in,
           conv_w, w_pa, w_pb, w_o, norm_mlp_g, w_mlp_in, w_mlp_out, norm_final_g):
    batch, seq, d = x_prompt.shape
    db, tq, _ = x_sample.shape
    depth, n_pool, ps, heads, hd = cache_k.shape
    idim = cache_kidx.shape[-1]
    cw = conv_w.shape[-1]
    aw = heads * hd
    n_in = w_in.shape[-1]
    iheads = (n_in - 3 * aw - idim - 3 * cw - 2 * d) // (idim + 1)
    iw = iheads * idim
    assert 3 * aw + iw + idim + iheads + 3 * cw + 2 * d == n_in and idim + iheads <= LANES
    n_pages = page_table.shape[1]
    past = n_pages * ps
    rq = SUBLANES
    assert tq <= rq

    mp, ms = batch * seq, db * tq
    xp = x_prompt.reshape(mp, d)
    xs = x_sample.reshape(ms, d)
    tb = _tile(seq, 256)
    pages = math.gcd(n_pages, 8)
    o_small = 3 * aw + iw
    o_conv = o_small + idim + iheads
    o_gate = o_conv + 3 * cw
    gf = norm_final_g.reshape(1, d)
    bias_tiles = _bias_tiles(rel_bias, tb=tb)

    outs = {k: [] for k in ("kp", "vp", "kip", "sp", "ks", "vs", "kis", "ss")}
    for l in range(depth):
        wl = jnp.swapaxes(w_in[l], 0, 1).astype(BF16)
        g_mix = norm_mix_g[l].reshape(1, d)
        g_mlp = norm_mlp_g[l].reshape(1, d)

        tm = _tile(seq, 512)
        q, k, v, kb, vb, qi, sm, smb = _proj_attn(xp, g_mix, wl, aw=aw, iw=iw, n_small=idim + iheads, hd=hd, tm=tm)
        c_in, u_tail, wpa, wpb, wo = _proj_conv(xp, g_mix, wl, o_conv, conv_w[l], None, (w_pa[l], w_pb[l], w_o[l]),
                                                seq=seq, tm=_tile(seq, 512))
        attn, w1, w2 = _attn_prompt(bias_tiles, qi, sm, smb, q, kb, vb, (w_mlp_in[l], w_mlp_out[l]), batch=batch,
                                    seq=seq, tb=tb, heads=heads, hd=hd, iheads=iheads, idim=idim,
                                    topk=min(TOPK_MAX, seq // 4))
        attn_p, c_in_p = attn, c_in
        outs["kp"].append(k.reshape(batch, seq, heads, hd))
        outs["vp"].append(v.reshape(batch, seq, heads, hd))
        outs["kip"].append(sm[:, :idim].reshape(batch, seq, idim))
        outs["sp"].append(u_tail.reshape(batch, -1, SUBLANES, cw)[:, -1, SUBLANES - 2:])

        q, k, v, kb, vb, qi, sm, smb = _proj_attn(xs, g_mix, wl, aw=aw, iw=iw, n_small=idim + iheads, hd=hd, tm=ms)
        st = state_conv[l]
        zero = jnp.zeros((db, tq - 1, cw), F32)
        prev1 = jnp.concatenate([st[:, 1:2], zero], axis=1).reshape(ms, cw)
        prev2 = jnp.concatenate([st, zero[:, 1:]], axis=1).reshape(ms, cw)
        c_in, u_all = _proj_conv(xs, g_mix, wl, o_conv, conv_w[l], (prev1, prev2), seq=tq, tm=ms)

        def pad_rows(a, n):
            return jnp.pad(a, ((0, 0), (0, n - a.shape[1])) + ((0, 0),) * (a.ndim - 2))

        qi_hm = pad_rows(qi.reshape(db, tq, iheads, idim).transpose(0, 2, 1, 3).reshape(db * iheads, tq, idim), rq)
        qi_hm = qi_hm.reshape(db, iheads * rq, idim)
        w_hm = pad_rows(sm[:, idim:idim + iheads].reshape(db, tq, iheads).transpose(0, 2, 1).reshape(db * iheads, tq), rq)
        w_hm = jnp.broadcast_to(w_hm.reshape(db, iheads * rq, 1), (db, iheads * rq, LANES))
        kin_t = jnp.swapaxes(pad_rows(smb[:, :idim].reshape(db, tq, idim), ps), 1, 2)
        topk_s = min(TOPK_MAX, (past + tq) // 4)
        past_keys, snew, thr = _sample_select(page_table, qi_hm, w_hm, kin_t, jnp.swapaxes(cache_kidx[l], 1, 2),
                                              pages=math.gcd(n_pages, 32), tq=tq, rq=rq, iheads=iheads, topk=topk_s)
        ksel, vsel, pos, cnt = _sc_gather(past_keys, thr, page_table, cache_k[l].reshape(n_pool * ps, heads, hd),
                                          cache_v[l].reshape(n_pool * ps, heads, hd), tq=tq, topk=topk_s, ps=ps)
        pos_rows = jnp.repeat(pos, heads, axis=1).reshape(ms, 1, topk_s * heads)
        snew_rows = jnp.repeat(snew[:, :tq, :ps // heads], heads, axis=2).reshape(ms, 1, ps)
        attn = _sample_attn_sel(cnt, rel_bias.T, q.reshape(ms, heads, hd), ksel.reshape(ms * topk_s * heads, hd),
                                vsel.reshape(ms * topk_s * heads, hd), pos_rows, snew_rows, thr,
                                pad_rows(kb.reshape(db, tq * heads, hd), ps), pad_rows(vb.reshape(db, tq * heads, hd), ps),
                                heads=heads, hd=hd, topk=topk_s, tq=tq, past=past)
        attn = attn.reshape(ms, aw).astype(BF16)

        x1p, x1s = _mix(xp, xs, g_mix, attn_p, attn, c_in_p, c_in, wl, o_gate, wpa, wpb, wo, tm=_tile(mp, 256))
        xp_next, xs_next = _mlp(x1p, x1s, g_mlp, gf, w1, w2, tm=_tile(mp, 1024), tf=_tile(w1.shape[1], 512),
                                final=l == depth - 1)
        outs["ks"].append(k.reshape(db, tq, heads, hd))
        outs["vs"].append(v.reshape(db, tq, heads, hd))
        outs["kis"].append(sm[:, :idim].reshape(db, tq, idim))
        outs["ss"].append(u_all.reshape(db, tq, cw)[:, tq - 2:])
        xp, xs = xp_next, xs_next

    st = {k: jnp.stack(v) for k, v in outs.items()}
    return (xp.reshape(batch, seq, d), xs.reshape(db, tq, d), st["kp"], st["vp"], st["kip"], st["sp"],
            st["ks"], st["vs"], st["kis"], st["ss"])
```

```python
import functools
import math

import jax
import jax.numpy as jnp
import numpy as np
from jax import lax
from jax.experimental import pallas as pl
from jax.experimental.pallas import tpu as pltpu
from jax.experimental.pallas import tpu_sc as plsc

F32 = jnp.float32
BF16 = jnp.bfloat16
I32 = jnp.int32

TOPK_MAX = 256
MAX_DISTANCE = 128
EPS = 1e-6

LANES = 128
SUBLANES = 8
V7X_SCOPED_VMEM_BYTES = 60000 * 1024
SC_LANES = 16
V7X_SC_CORES = 2
V7X_SC_SUBCORES = 16

LOG2E = math.log2(math.e)
INT_MIN = np.int32(-2 ** 31)
NEG_BIG = -1e30


def _cparams(*sem):
    return pltpu.CompilerParams(dimension_semantics=sem, vmem_limit_bytes=V7X_SCOPED_VMEM_BYTES)


def _resident(shape):
    nd = len(shape)
    return pl.BlockSpec(shape, lambda *_: (0,) * nd, pipeline_mode=pl.Buffered(1))


def _rms_bf16(x, g):
    y = x * lax.rsqrt(jnp.mean(x * x, axis=-1, keepdims=True) + EPS)
    return (y * g).astype(BF16)


def _dot(a, b):
    return jnp.dot(a, b, preferred_element_type=F32)


def _dot_nt(a, b):
    return lax.dot_general(a, b, (((1,), (1,)), ((), ())), preferred_element_type=F32)


def _ordered_key(x):
    b = lax.bitcast_convert_type(x, I32)
    return b ^ ((b >> 31) & np.int32(0x7FFFFFFF))


def _bucket(n, n_buckets):
    n = jnp.maximum(n, 0)
    me = n_buckets // 2
    nf = jnp.maximum(n, me).astype(F32)
    large = me + (jnp.log(nf / me) / math.log(MAX_DISTANCE / me) * (n_buckets - me)).astype(I32)
    large = jnp.minimum(large, n_buckets - 1)
    return jnp.where(n < me, n, large)


def _bias_of_bucket(bucket, relb_ref, h, n_buckets):
    acc = jnp.zeros(bucket.shape, F32)
    for bkt in range(n_buckets):
        acc = jnp.where(bucket == bkt, relb_ref[bkt, h], acc)
    return acc


def _fold_lanes(x):
    acc = x[:, 0:LANES]
    for t in range(1, x.shape[1] // LANES):
        acc = acc + x[:, t * LANES:(t + 1) * LANES]
    return acc


def _fold_rows(x, op=jnp.add):
    parts = [x[t * SUBLANES:(t + 1) * SUBLANES, :] for t in range(x.shape[0] // SUBLANES)]
    while len(parts) > 1:
        parts = [op(parts[t], parts[t + 1]) for t in range(0, len(parts) - 1, 2)] + parts[len(parts) & ~1:]
    return parts[0]


def _kth_threshold(count_ge, shape, axis, k, trips):
    def bit_body(t, uthr):
        cand = uthr | jnp.left_shift(np.int32(1), 31 - t)
        cnt = jnp.sum(count_ge(cand ^ INT_MIN), axis=axis, keepdims=True)
        return jnp.where(cnt >= k, cand, uthr)

    uthr = lax.fori_loop(0, trips, bit_body, jnp.zeros(shape, I32))
    return jnp.maximum(uthr ^ INT_MIN, INT_MIN + 1)


def _flash_update(lg, v, m_ref, l_ref, acc_ref, h, hd):
    reps = lg.shape[1] // LANES
    m_prev = m_ref[h]
    m_new = jnp.maximum(m_prev, jnp.max(lg, axis=1, keepdims=True))
    p = jnp.exp2(lg - jnp.tile(m_new, (1, reps)))
    alpha = jnp.exp2(m_prev - m_new)
    pv = _dot(p.astype(BF16), jnp.concatenate([v, jnp.ones((v.shape[0], LANES), BF16)], axis=1))
    l_ref[h] = alpha * l_ref[h] + pv[:, hd:]
    m_ref[h] = m_new
    hs = slice(h * hd, (h + 1) * hd)
    acc_ref[:, hs] = acc_ref[:, hs] * jnp.tile(alpha, (1, hd // LANES)) + pv[:, :hd]


def _stage_rows(w_hbm, w_ref, row0):
    @pl.when(pl.program_id(0) == 0)
    def _():
        pltpu.sync_copy(w_hbm.at[pl.ds(row0, w_ref.shape[0])], w_ref)


def _proj_attn_kernel(x_ref, g_ref, w_hbm, q_ref, k_ref, v_ref, kb_ref, vb_ref, qi_ref,
                      sm_ref, smb_ref, w_ref, ws_ref, *, aw, iw, hd, tm, n_small, qscale):
    heads = aw // hd
    _stage_rows(w_hbm, w_ref, 0)

    @pl.when(pl.program_id(0) == 0)
    def _():
        ws_ref[...] = jnp.zeros(ws_ref.shape, BF16)
        pltpu.sync_copy(w_hbm.at[pl.ds(w_ref.shape[0], n_small)], ws_ref.at[pl.ds(0, n_small)])

    hb = _rms_bf16(x_ref[...], g_ref[...])
    q_ref[...] = (_dot_nt(hb, w_ref[0:aw, :]) * qscale).astype(BF16)
    for o_ref, ob_ref, c0 in ((k_ref, kb_ref, aw), (v_ref, vb_ref, 2 * aw)):
        kv = _dot_nt(hb, w_ref[c0:c0 + aw, :])
        ob_ref[...] = kv.astype(BF16)
        for h in range(heads):
            o_ref[pl.ds(h, tm, stride=heads), :] = kv[:, h * hd:(h + 1) * hd]
    qi_ref[...] = _dot_nt(hb, w_ref[3 * aw:3 * aw + iw, :]).astype(BF16)
    sm = _dot_nt(hb, ws_ref[...])
    sm_ref[...] = sm
    smb_ref[...] = sm.astype(BF16)


def _proj_attn(x2d, g, w_all, *, aw, iw, n_small, hd, tm):
    m, d = x2d.shape
    wt = 3 * aw + iw
    heads = aw // hd
    assert hd == LANES and n_small <= LANES and wt % 16 == 0 and n_small % 16 == 0
    row = lambda width: pl.BlockSpec((tm, width), lambda i: (i, 0))
    kv_spec = pl.BlockSpec((tm * heads, hd), lambda i: (i, 0))
    kv_shape = jax.ShapeDtypeStruct((m * heads, hd), F32)
    outs = [(aw, BF16), None, None, (aw, BF16), (aw, BF16), (iw, BF16), (LANES, F32), (LANES, BF16)]
    return pl.pallas_call(
        functools.partial(_proj_attn_kernel, aw=aw, iw=iw, hd=hd, tm=tm, n_small=n_small,
                          qscale=hd ** -0.5 * LOG2E),
        grid=(m // tm,),
        in_specs=[row(d), _resident((1, d)), pl.BlockSpec(memory_space=pl.ANY)],
        out_specs=[kv_spec if o is None else row(o[0]) for o in outs],
        out_shape=[kv_shape if o is None else jax.ShapeDtypeStruct((m, o[0]), o[1]) for o in outs],
        scratch_shapes=[pltpu.VMEM((wt, d), BF16), pltpu.VMEM((LANES, d), BF16)],
        compiler_params=_cparams("arbitrary"),
        name="proj_attn",
    )(x2d, g, w_all)


def _proj_conv_kernel(*refs, cw, tm, seq, tiles_per_seq, tail, row0, n_cast):
    if tiles_per_seq:
        x_ref, g_ref, w_hbm, cwt_ref = refs[:4]
        c_ref, ut_ref = refs[4 + n_cast:6 + n_cast]
        w_ref, carry_ref = refs[6 + 2 * n_cast:]
        _cast_slabs(refs[4:4 + n_cast], refs[6 + n_cast:6 + 2 * n_cast])
    else:
        x_ref, g_ref, w_hbm, cwt_ref, p1_ref, p2_ref, c_ref, ut_ref, w_ref = refs
    _stage_rows(w_hbm, w_ref, row0)
    hb = _rms_bf16(x_ref[...], g_ref[...])
    cx = _dot_nt(hb, w_ref[0:cw, :])
    cb = _dot_nt(hb, w_ref[cw:2 * cw, :])
    cc = _dot_nt(hb, w_ref[2 * cw:3 * cw, :])
    u = cc * cx
    r = lax.broadcasted_iota(I32, (tm, cw), 0)
    um1 = pltpu.roll(u, 1, 0)
    um2 = pltpu.roll(u, 2, 0)
    if tiles_per_seq:
        @pl.when(pl.program_id(0) % tiles_per_seq == 0)
        def _():
            carry_ref[...] = jnp.zeros_like(carry_ref)
        prev1 = jnp.broadcast_to(carry_ref[SUBLANES - 1:SUBLANES, :], (tm, cw))
        prev2 = jnp.broadcast_to(carry_ref[SUBLANES - 2:SUBLANES - 1, :], (tm, cw))
        um1 = jnp.where(r == 0, prev1, um1)
        um2 = jnp.where(r == 0, prev2, jnp.where(r == 1, prev1, um2))
        carry_ref[...] = u[tm - SUBLANES:tm, :]
    else:
        assert seq & (seq - 1) == 0
        t = r & (seq - 1)
        um1 = jnp.where(t >= 1, um1, p1_ref[...])
        um2 = jnp.where(t >= 2, um2, p2_ref[...])
    y = cwt_ref[0:1, :] * um2 + cwt_ref[1:2, :] * um1 + cwt_ref[2:3, :] * u
    c_ref[...] = (cb * y).astype(BF16)
    ut_ref[...] = u[tm - tail:tm, :]


def _proj_conv(x2d, g, w_all, row0, conv_w, prev, to_cast=(), *, seq, tm):
    m, d = x2d.shape
    cw = conv_w.shape[1]
    assert conv_w.shape[0] == 3 and row0 % 16 == 0
    row = lambda width: pl.BlockSpec((tm, width), lambda i: (i, 0))
    in_specs = [row(d), _resident((1, d)), pl.BlockSpec(memory_space=pl.ANY), _resident((3, cw))]
    args = [x2d, g, w_all, conv_w]
    scratch = [pltpu.VMEM((3 * cw, d), BF16)]
    if prev is None:
        assert seq % tm == 0
        tiles_per_seq, tail = seq // tm, SUBLANES
        scratch += [pltpu.VMEM((SUBLANES, cw), F32)]
    else:
        assert tm % seq == 0 and seq >= 2
        tiles_per_seq, tail = 0, tm
        in_specs += [row(cw), row(cw)]
        args += list(prev)
    assert not (to_cast and prev is not None)
    c_in, c_out, c_shapes = _cast_specs(to_cast, m // tm, lambda i: i)
    return pl.pallas_call(
        functools.partial(_proj_conv_kernel, cw=cw, tm=tm, seq=seq, tiles_per_seq=tiles_per_seq, tail=tail,
                          row0=row0, n_cast=len(to_cast)),
        grid=(m // tm,),
        in_specs=in_specs + c_in,
        out_specs=[row(cw), pl.BlockSpec((tail, cw), lambda i: (i, 0))] + c_out,
        out_shape=[jax.ShapeDtypeStruct((m, cw), BF16), jax.ShapeDtypeStruct((m // tm * tail, cw), F32)] + c_shapes,
        scratch_shapes=scratch,
        compiler_params=_cparams("arbitrary"),
        name="proj_conv",
    )(*args, *to_cast)


def _bias_tiles_kernel(relb_ref, o_ref, *, tb, n_buckets):
    kind = pl.program_id(0)
    h = pl.program_id(1)
    qry = lax.broadcasted_iota(I32, (tb, tb), 0)
    key = lax.broadcasted_iota(I32, (tb, tb), 1)
    bias = _bias_of_bucket(_bucket(kind * tb + qry - key, n_buckets), relb_ref, h, n_buckets)
    o_ref[...] = (bias - relb_ref[n_buckets - 1, h]) * LOG2E


def _bias_tiles(rel_bias, *, tb):
    n_buckets, heads = rel_bias.shape
    return pl.pallas_call(
        functools.partial(_bias_tiles_kernel, tb=tb, n_buckets=n_buckets),
        grid=(2, heads),
        in_specs=[pl.BlockSpec(memory_space=pltpu.SMEM)],
        out_specs=pl.BlockSpec((None, None, tb, tb), lambda a, h: (a, h, 0, 0)),
        out_shape=jax.ShapeDtypeStruct((2, heads, tb, tb), F32),
        compiler_params=_cparams("arbitrary", "arbitrary"),
        name="bias_tiles",
    )(rel_bias)


def _cast_specs(weights, n_steps, step_of):
    in_specs, out_specs, out_shapes = [], [], []
    for w in weights:
        rows, cols = w.shape
        assert rows % n_steps == 0 and (rows // n_steps) % 16 == 0
        spec = pl.BlockSpec((rows // n_steps, cols), lambda *idx: (step_of(*idx), 0))
        in_specs.append(spec)
        out_specs.append(spec)
        out_shapes.append(jax.ShapeDtypeStruct((rows, cols), BF16))
    return in_specs, out_specs, out_shapes


def _cast_slabs(src_refs, dst_refs):
    for src, dst in zip(src_refs, dst_refs):
        dst[...] = src[...].astype(BF16)


def _attn_prompt_kernel(*refs, n_cast, tb, heads, hd, iheads, idim, topk):
    qi_ref, sm_ref, smb_ref, q_ref, kb_ref, vb_ref, bias_ref = refs[:7]
    o_ref = refs[7 + n_cast]
    skey_ref, thr_ref, wt_ref, madd_ref, m_ref, l_ref, acc_ref = refs[8 + 2 * n_cast:]
    _cast_slabs(refs[7:7 + n_cast], refs[8 + n_cast:8 + 2 * n_cast])
    i = pl.program_id(1)
    wscale = idim ** -0.5 * iheads ** -0.5

    m_ref[...] = jnp.full(m_ref.shape, NEG_BIG, F32)
    l_ref[...] = jnp.zeros(l_ref.shape, F32)
    acc_ref[...] = jnp.zeros(acc_ref.shape, F32)
    wt_ref[...] = sm_ref[...].T * wscale

    def chunk(j):
        return pl.ds(pl.multiple_of(j * tb, tb), tb)

    def score_chunk(j, diag):
        kic = smb_ref[chunk(j), 0:idim]
        acc = jnp.zeros((tb, tb), F32)
        for h in range(iheads):
            s = _dot_nt(kic, qi_ref[:, h * idim:(h + 1) * idim])
            acc = acc + jnp.maximum(s, 0.0) * wt_ref[idim + h:idim + h + 1, :]
        key = _ordered_key(acc)
        if diag:
            kpos = lax.broadcasted_iota(I32, (tb, tb), 0)
            qpos = lax.broadcasted_iota(I32, (tb, tb), 1)
            key = jnp.where(kpos > qpos, INT_MIN, key)
        skey_ref[chunk(j), :] = key

    def score_body(j, carry):
        score_chunk(j, False)
        return carry

    lax.fori_loop(0, i, score_body, 0)
    score_chunk(i, True)

    def count_ge(scand):
        def body(j, cnt):
            return cnt + _fold_rows(jnp.where(skey_ref[chunk(j), :] >= scand[0:1, :], 1.0, 0.0))

        return lax.fori_loop(0, i + 1, body, jnp.zeros((SUBLANES, tb), F32))

    trips = jnp.where((i + 1) * tb <= topk, 0, 32)
    thr_ref[...] = _kth_threshold(count_ge, (SUBLANES, tb), 0, float(topk), trips)

    def attend_chunk(j, kind):
        madd_ref[...] = jnp.where(skey_ref[chunk(j), :] >= thr_ref[0:1, :], 0.0, NEG_BIG).T
        for h in range(heads):
            hs = slice(h * hd, (h + 1) * hd)
            lg = _dot_nt(q_ref[:, hs], kb_ref[chunk(j), hs]) + madd_ref[...]
            if kind is not None:
                lg = lg + bias_ref[kind, h]
            _flash_update(lg, vb_ref[chunk(j), hs], m_ref, l_ref, acc_ref, h, hd)

    def attend_body(j, carry):
        attend_chunk(j, None)
        return carry

    lax.fori_loop(0, jnp.maximum(i - 1, 0), attend_body, 0)

    @pl.when(i >= 1)
    def _():
        attend_chunk(i - 1, 1)

    attend_chunk(i, 0)

    for h in range(heads):
        hs = slice(h * hd, (h + 1) * hd)
        o_ref[:, hs] = (acc_ref[:, hs] / jnp.tile(l_ref[h], (1, hd // LANES))).astype(BF16)


def _attn_prompt(bias_tiles, qi, sm, smb, q, kb, vb, to_cast, *, batch, seq, tb, heads, hd, iheads, idim, topk):
    m, aw = q.shape
    iw = qi.shape[1]
    nq = seq // tb
    assert tb >= MAX_DISTANCE and tb % LANES == 0 and hd % LANES == 0
    qrow = lambda width: pl.BlockSpec((tb, width), lambda b, i: (b * nq + i, 0))
    seqblk = lambda width: pl.BlockSpec((seq, width), lambda b, i: (b, 0))
    c_in, c_out, c_shapes = _cast_specs(to_cast, batch * nq, lambda b, i: b * nq + i)
    return pl.pallas_call(
        functools.partial(_attn_prompt_kernel, n_cast=len(to_cast), tb=tb, heads=heads, hd=hd, iheads=iheads,
                          idim=idim, topk=topk),
        grid=(batch, nq),
        in_specs=[qrow(iw), qrow(LANES), seqblk(LANES), qrow(aw), seqblk(aw), seqblk(aw),
                  _resident(bias_tiles.shape)] + c_in,
        out_specs=[qrow(aw)] + c_out,
        out_shape=[jax.ShapeDtypeStruct((m, aw), BF16)] + c_shapes,
        scratch_shapes=[pltpu.VMEM((seq, tb), I32), pltpu.VMEM((SUBLANES, tb), I32),
                        pltpu.VMEM((LANES, tb), F32), pltpu.VMEM((tb, tb), F32),
                        pltpu.VMEM((heads, tb, LANES), F32), pltpu.VMEM((heads, tb, LANES), F32),
                        pltpu.VMEM((tb, aw), F32)],
        compiler_params=_cparams("arbitrary", "arbitrary"),
        name="attn_prompt",
    )(qi, sm, smb, q, kb, vb, bias_tiles, *to_cast)


def _sample_select_kernel(pt_ref, qi_ref, w_ref, kin_ref, *rest, pages, ps, nc, iheads, idim, tq, rq, topk):
    page_refs = rest[:pages]
    past_ref, new_ref, thr_ref, row_ref = rest[pages:]
    b = pl.program_id(0)
    c = pl.program_id(1)
    db = pl.num_programs(0)
    ch = pages * ps
    wscale = idim ** -0.5 * iheads ** -0.5
    rows_b = pl.ds(pl.multiple_of(b * rq, rq), rq)

    def score(keys_t):
        s = _dot(qi_ref[...], keys_t)
        t = jnp.maximum(s, 0.0) * (w_ref[:, 0:1] * wscale)
        acc = t[0:rq]
        for h in range(1, iheads):
            acc = acc + t[h * rq:(h + 1) * rq]
        return _ordered_key(acc)

    @pl.when(c < nc)
    def _():
        for p in range(pages):
            key = score(page_refs[p][...].astype(BF16))
            row_ref[rows_b, pl.ds(pl.multiple_of(c * ch + p * ps, LANES), ps)] = key
            for t in range(ps // LANES):
                for j in range(tq):
                    past_ref[j, p * (ps // LANES) + t:p * (ps // LANES) + t + 1, :] = \
                        key[j:j + 1, t * LANES:(t + 1) * LANES]

    @pl.when(c == nc)
    def _():
        key = score(kin_ref[...])
        j = lax.broadcasted_iota(I32, (rq, ps), 0)
        n = lax.broadcasted_iota(I32, (rq, ps), 1)
        key = jnp.where((n <= j) & (n < tq), key, INT_MIN)
        new_ref[...] = key
        row_ref[rows_b, nc * ch:nc * ch + ps] = key

    @pl.when((c == nc) & (b == db - 1))
    def _():
        n_rows = row_ref.shape[0]

        def count_ge(scand):
            sk = row_ref[...]
            return _fold_lanes(jnp.where(sk >= jnp.tile(scand, (1, sk.shape[1] // LANES)), 1.0, 0.0))

        thr = _kth_threshold(count_ge, (n_rows, LANES), 1, float(topk), 32)
        new_keys = row_ref[:, nc * ch:nc * ch + ps]
        gt_new = jnp.sum(jnp.where(new_keys > jnp.tile(thr, (1, ps // LANES)), 1.0, 0.0), axis=1, keepdims=True)
        budget = jnp.broadcast_to(topk - gt_new, (n_rows, LANES)).astype(I32)
        thr_ref[...] = jnp.zeros(thr_ref.shape, I32)
        for bb in range(n_rows // rq):
            for jj in range(tq):
                thr_ref[bb * tq + jj, 0:1, :] = thr[bb * rq + jj:bb * rq + jj + 1, :]
                thr_ref[bb * tq + jj, 1:2, :] = budget[bb * rq + jj:bb * rq + jj + 1, :]


def _sample_select(page_table, qi_hm, w_hm, kin_t, cache_kidx_t, *, pages, tq, rq, iheads, topk):
    db, n_pages = page_table.shape
    _, idim, ps = cache_kidx_t.shape
    nc = n_pages // pages
    ch = pages * ps
    assert ps % LANES == 0 and (ch // LANES) % SUBLANES == 0
    page_spec = lambda p: pl.BlockSpec(
        (None, idim, ps), lambda b, c, pt: (pt[b, jnp.minimum(c, nc - 1) * pages + p], 0, 0))
    per_b = lambda shape: pl.BlockSpec((None,) + shape, lambda b, c, pt: (b, 0, 0))
    return pl.pallas_call(
        functools.partial(_sample_select_kernel, pages=pages, ps=ps, nc=nc, iheads=iheads, idim=idim, tq=tq, rq=rq,
                          topk=topk),
        grid_spec=pltpu.PrefetchScalarGridSpec(
            num_scalar_prefetch=1,
            grid=(db, nc + 1),
            in_specs=[per_b((iheads * rq, idim)), per_b((iheads * rq, LANES)), per_b((idim, ps))]
            + [page_spec(p) for p in range(pages)],
            out_specs=[pl.BlockSpec((tq, ch // LANES, LANES), lambda b, c, pt: (b, jnp.minimum(c, nc - 1), 0)),
                       per_b((rq, ps)),
                       pl.BlockSpec((db * tq, SUBLANES, LANES), lambda b, c, pt: (0, 0, 0))],
            scratch_shapes=[pltpu.VMEM((db * rq, nc * ch + ps), I32)],
        ),
        out_shape=[jax.ShapeDtypeStruct((db * tq, n_pages * ps // LANES, LANES), I32),
                   jax.ShapeDtypeStruct((db, rq, ps), I32),
                   jax.ShapeDtypeStruct((db * tq, SUBLANES, LANES), I32)],
        compiler_params=_cparams("arbitrary", "arbitrary"),
        name="sample_select",
    )(page_table, qi_hm, w_hm, kin_t, *([cache_kidx_t] * pages))


def _sc_gather_kernel(past_hbm, thr_hbm, pt_hbm, ck_hbm, cv_hbm, ksel_hbm, vsel_hbm, pos_hbm, cnt_hbm,
                      row_v, thr_v, pt_v, idx_v, phys_v, rows_v, cnt_v, sem,
                      *, nq, tq, topk, ps, n_cores, rows_per_copy):
    wid = lax.axis_index("s") * n_cores + lax.axis_index("c")

    @pl.when(wid < nq)
    def _():
        pltpu.sync_copy(past_hbm.at[wid], row_v)
        pltpu.sync_copy(thr_hbm.at[wid], thr_v)
        pltpu.sync_copy(pt_hbm.at[wid // tq], pt_v)
        thr = thr_v[0, pl.ds(0, SC_LANES)]
        budget = thr_v[1, pl.ds(0, SC_LANES)]
        lane = lax.iota(I32, SC_LANES)
        zero = jnp.zeros((SC_LANES,), I32)
        for t in range(idx_v.shape[0] // SC_LANES):
            idx_v[pl.ds(t * SC_LANES, SC_LANES)] = zero

        def compact(pred):
            def body(r, cnt):
                for t in range(LANES // SC_LANES):
                    x = row_v[r, pl.ds(t * SC_LANES, SC_LANES)]
                    m = pred(x, cnt)
                    rank = plsc.cumsum(jnp.where(m, 1, 0).astype(I32))
                    plsc.store_scatter(idx_v, [cnt + rank - 1], lane + (r * LANES + t * SC_LANES), mask=m)
                    cnt = cnt + plsc.all_reduce_population_count(m)
                return cnt
            return body

        cnt = lax.fori_loop(0, row_v.shape[0], compact(lambda x, cnt: x > thr), zero)
        cnt = lax.fori_loop(0, row_v.shape[0], compact(lambda x, cnt: (x == thr) & (cnt < budget)), cnt)
        cnt_v[...] = jnp.minimum(cnt, budget)
        pltpu.sync_copy(cnt_v, cnt_hbm.at[wid])
        pltpu.sync_copy(idx_v.at[pl.ds(0, topk)], pos_hbm.at[wid])

        shift = ps.bit_length() - 1
        for t in range(topk // SC_LANES):
            pos = idx_v[pl.ds(t * SC_LANES, SC_LANES)]
            page = plsc.load_gather(pt_v, [lax.shift_right_logical(pos, shift)])
            phys_v[pl.ds(t * SC_LANES, SC_LANES)] = page * ps + (pos & (ps - 1))
        for g in range(topk // rows_per_copy):
            sel = phys_v.at[pl.ds(g * rows_per_copy, rows_per_copy)]
            dst = pl.ds(wid * topk + g * rows_per_copy, rows_per_copy)
            for src_hbm, dst_hbm in ((ck_hbm, ksel_hbm), (cv_hbm, vsel_hbm)):
                pltpu.async_copy(src_hbm.at[sel], rows_v, sem).wait()
                pltpu.sync_copy(rows_v, dst_hbm.at[dst])


def _sc_gather(past_keys, thr, page_table, cache_k, cache_v, *, tq, topk, ps):
    nq, key_rows, _ = past_keys.shape
    _, heads, hd = cache_k.shape
    assert ps & (ps - 1) == 0 and topk % SC_LANES == 0
    rows_per_copy = 64
    assert topk % rows_per_copy == 0
    mesh = plsc.VectorSubcoreMesh(core_axis_name="c", subcore_axis_name="s", num_cores=V7X_SC_CORES,
                                  num_subcores=V7X_SC_SUBCORES)
    assert nq <= V7X_SC_CORES * V7X_SC_SUBCORES
    sel_shape = jax.ShapeDtypeStruct((nq * topk, heads, hd), F32)
    return pl.kernel(
        functools.partial(_sc_gather_kernel, nq=nq, tq=tq, topk=topk, ps=ps, n_cores=V7X_SC_CORES,
                          rows_per_copy=rows_per_copy),
        out_type=[sel_shape, sel_shape, jax.ShapeDtypeStruct((nq, topk), I32),
                  jax.ShapeDtypeStruct((nq, SC_LANES), I32)],
        mesh=mesh,
        scratch_types=[pltpu.VMEM((key_rows, LANES), I32), pltpu.VMEM((SUBLANES, LANES), I32),
                       pltpu.VMEM((page_table.shape[1],), I32), pltpu.VMEM((topk + SC_LANES,), I32),
                       pltpu.VMEM((topk,), I32), pltpu.VMEM((rows_per_copy, heads, hd), F32),
                       pltpu.VMEM((SC_LANES,), I32), pltpu.SemaphoreType.DMA],
        compiler_params=pltpu.CompilerParams(needs_layout_passes=False),
        name="sc_select_gather",
    )(past_keys, thr, page_table, cache_k, cache_v)


def _sample_attn_sel_kernel(cnt_ref, relbt_ref, q_ref, ksel_ref, vsel_ref, pos_ref, snew_ref, thr_ref, kn_ref, vn_ref,
                            o_ref, *, heads, tq, past, n_buckets):
    w = pl.program_id(0)
    j = w % tq
    q = q_ref[...]

    def head_bias(dist):
        bucket = _bucket(dist, n_buckets)
        acc = jnp.zeros(bucket.shape, F32)
        for bkt in range(n_buckets):
            acc = jnp.where(bucket == bkt, relbt_ref[:, bkt:bkt + 1], acc)
        return acc * LOG2E

    def logits(keys):
        lg = _dot_nt(q, keys)
        head = lax.broadcasted_iota(I32, lg.shape, 0)
        col = lax.broadcasted_iota(I32, lg.shape, 1)
        return lg, (col & (heads - 1)) == head, col

    lg, own, col = logits(ksel_ref[...].astype(BF16))
    keep = own & (col < cnt_ref[w, 0] * heads)
    lg = jnp.where(keep, lg + head_bias(jnp.broadcast_to(past + j - pos_ref[...], lg.shape)), NEG_BIG)
    lgn, own, col = logits(kn_ref[...])
    keep = own & (snew_ref[...] >= thr_ref[0:1, :])
    lgn = jnp.where(keep, lgn + head_bias(j - lax.shift_right_logical(col, heads.bit_length() - 1)), NEG_BIG)

    m = jnp.maximum(jnp.max(lg, axis=1, keepdims=True), jnp.max(lgn, axis=1, keepdims=True))
    p = jnp.exp2(lg - m)
    pn = jnp.exp2(lgn - m)
    denom = jnp.sum(p, axis=1, keepdims=True) + jnp.sum(pn, axis=1, keepdims=True)
    acc = _dot(p.astype(BF16), vsel_ref[...].astype(BF16)) + _dot(pn.astype(BF16), vn_ref[...])
    o_ref[...] = acc / denom


def _sample_attn_sel(cnt, rel_bias_t, q, ksel, vsel, pos, snew, thr, kn, vn, *, heads, hd, topk, tq, past):
    nq = q.shape[0]
    ps = kn.shape[1]
    assert ps == LANES and hd == LANES and heads & (heads - 1) == 0
    per_q = lambda shape: pl.BlockSpec((None,) + shape, lambda w, cnt: (w, 0, 0))
    per_b = lambda shape: pl.BlockSpec((None,) + shape, lambda w, cnt: (w // tq, 0, 0))
    sel_spec = pl.BlockSpec((topk * heads, hd), lambda w, cnt: (w, 0))
    return pl.pallas_call(
        functools.partial(_sample_attn_sel_kernel, heads=heads, tq=tq, past=past, n_buckets=rel_bias_t.shape[1]),
        grid_spec=pltpu.PrefetchScalarGridSpec(
            num_scalar_prefetch=1,
            grid=(nq,),
            in_specs=[pl.BlockSpec(rel_bias_t.shape, lambda w, cnt: (0, 0)), per_q((heads, hd)), sel_spec, sel_spec,
                      per_q((1, topk * heads)), per_q((1, ps)), per_q((SUBLANES, LANES)), per_b((ps, hd)),
                      per_b((ps, hd))],
            out_specs=per_q((heads, hd)),
        ),
        out_shape=jax.ShapeDtypeStruct((nq, heads, hd), F32),
        compiler_params=_cparams("arbitrary"),
        name="sample_attn_sel",
    )(cnt, rel_bias_t, q, ksel, vsel, pos, snew, thr, kn, vn)


def _mix_kernel(x_ref, xs_ref, g_ref, a_ref, as_ref, c_ref, cs_ref, w_hbm, wpa_ref, wpb_ref, wo_ref, o_ref, os_ref,
                wg_ref, *, d, row0):
    _stage_rows(w_hbm, wg_ref, row0)
    n = pl.num_programs(0) - 1

    def rows(x_ref, a_ref, c_ref, o_ref):
        x = x_ref[...]
        hb = _rms_bf16(x, g_ref[...])
        a = _dot(a_ref[...], wpa_ref[...])
        m = jax.nn.sigmoid(_dot_nt(hb, wg_ref[0:d, :])) * a
        c = _dot(c_ref[...], wpb_ref[...])
        m = m + jax.nn.sigmoid(_dot_nt(hb, wg_ref[d:2 * d, :])) * c
        o_ref[...] = x + _dot(m.astype(BF16), wo_ref[...])

    @pl.when(pl.program_id(0) < n)
    def _():
        rows(x_ref, a_ref, c_ref, o_ref)

    @pl.when(pl.program_id(0) == n)
    def _():
        rows(xs_ref, as_ref, cs_ref, os_ref)


def _mix(xp, xs, g, attn_p, attn_s, c_p, c_s, w_all, row0, w_pa, w_pb, w_o, *, tm):
    m, d = xp.shape
    ms = xs.shape[0]
    n = m // tm
    assert row0 % 16 == 0
    prow = lambda width: pl.BlockSpec((tm, width), lambda i: (jnp.minimum(i, n - 1), 0))
    srow = lambda width: pl.BlockSpec((ms, width), lambda i: (0, 0))
    return pl.pallas_call(
        functools.partial(_mix_kernel, d=d, row0=row0),
        grid=(n + 1,),
        in_specs=[prow(d), srow(d), _resident((1, d)), prow(attn_p.shape[1]), srow(attn_s.shape[1]),
                  prow(c_p.shape[1]), srow(c_s.shape[1]), pl.BlockSpec(memory_space=pl.ANY), _resident(w_pa.shape),
                  _resident(w_pb.shape), _resident(w_o.shape)],
        out_specs=[prow(d), srow(d)],
        out_shape=[jax.ShapeDtypeStruct((m, d), F32), jax.ShapeDtypeStruct((ms, d), F32)],
        scratch_shapes=[pltpu.VMEM((2 * d, d), BF16)],
        compiler_params=_cparams("arbitrary"),
        name="mix_out",
    )(xp, xs, g, attn_p, attn_s, c_p, c_s, w_all, w_pa, w_pb, w_o)


def _mlp_kernel(x_ref, xs_ref, g_ref, gf_ref, w1_ref, w2_ref, y_ref, ys_ref, h_ref, *, tm, final):
    f = pl.program_id(1)
    last_tile = pl.program_id(0) == pl.num_programs(0) - 1

    def start(x_ref, y_ref, h_rows):
        h_ref[h_rows, :] = _rms_bf16(x_ref[...], g_ref[...])
        y_ref[...] = jnp.zeros(y_ref.shape, F32)

    def finish(x_ref, y_ref):
        x2 = x_ref[...] + y_ref[...]
        if final:
            x2 = x2 * lax.rsqrt(jnp.mean(x2 * x2, axis=-1, keepdims=True) + EPS) * gf_ref[...]
        y_ref[...] = x2

    def ff(h):
        t = jnp.square(jnp.maximum(_dot(h, w1_ref[...]), 0.0))
        return _dot(t.astype(BF16), w2_ref[...])

    @pl.when(f == 0)
    def _():
        start(x_ref, y_ref, slice(0, tm))

    @pl.when((f == 0) & last_tile)
    def _():
        start(xs_ref, ys_ref, slice(tm, h_ref.shape[0]))

    @pl.when(jnp.logical_not(last_tile))
    def _():
        y_ref[...] += ff(h_ref[0:tm, :])

    @pl.when(last_tile)
    def _():
        r = ff(h_ref[...])
        y_ref[...] += r[0:tm]
        ys_ref[...] += r[tm:]

    @pl.when(f == pl.num_programs(1) - 1)
    def _():
        finish(x_ref, y_ref)

    @pl.when((f == pl.num_programs(1) - 1) & last_tile)
    def _():
        finish(xs_ref, ys_ref)


def _mlp(xp, xs, g, gf, w1, w2, *, tm, tf, final):
    m, d = xp.shape
    ms = xs.shape[0]
    ff = w1.shape[1]
    srow = pl.BlockSpec((ms, d), lambda i, f: (0, 0))
    return pl.pallas_call(
        functools.partial(_mlp_kernel, tm=tm, final=final),
        grid=(m // tm, ff // tf),
        in_specs=[pl.BlockSpec((tm, d), lambda i, f: (i, 0)), srow, _resident((1, d)), _resident((1, d)),
                  pl.BlockSpec((d, tf), lambda i, f: (0, f)), pl.BlockSpec((tf, d), lambda i, f: (f, 0))],
        out_specs=[pl.BlockSpec((tm, d), lambda i, f: (i, 0)), srow],
        out_shape=[jax.ShapeDtypeStruct((m, d), F32), jax.ShapeDtypeStruct((ms, d), F32)],
        scratch_shapes=[pltpu.VMEM((tm + ms, d), BF16)],
        compiler_params=_cparams("arbitrary", "arbitrary"),
        name="mlp",
    )(xp, xs, g, gf, w1, w2)


def _tile(m, cap):
    return min(m, cap)


def kernel(x_prompt, x_sample, cache_k, cache_v, cache_kidx, state_conv, page_table, rel_bias, norm_mix_g, w_in,
           conv_w, w_pa, w_pb, w_o, norm_mlp_g, w_mlp_in, w_mlp_out, norm_final_g):
    batch, seq, d = x_prompt.shape
    db, tq, _ = x_sample.shape
    depth, n_pool, ps, heads, hd = cache_k.shape
    idim = cache_kidx.shape[-1]
    cw = conv_w.shape[-1]
    aw = heads * hd
    n_in = w_in.shape[-1]
    iheads = (n_in - 3 * aw - idim - 3 * cw - 2 * d) // (idim + 1)
    iw = iheads * idim
    assert 3 * aw + iw + idim + iheads + 3 * cw + 2 * d == n_in and idim + iheads <= LANES
    n_pages = page_table.shape[1]
    past = n_pages * ps
    rq = SUBLANES
    assert tq <= rq

    mp, ms = batch * seq, db * tq
    xp = x_prompt.reshape(mp, d)
    xs = x_sample.reshape(ms, d)
    tb = _tile(seq, 256)
    pages = math.gcd(n_pages, 8)
    o_small = 3 * aw + iw
    o_conv = o_small + idim + iheads
    o_gate = o_conv + 3 * cw
    gf = norm_final_g.reshape(1, d)
    bias_tiles = _bias_tiles(rel_bias, tb=tb)

    outs = {k: [] for k in ("kp", "vp", "kip", "sp", "ks", "vs", "kis", "ss")}
    for l in range(depth):
        wl = jnp.swapaxes(w_in[l], 0, 1).astype(BF16)
        g_mix = norm_mix_g[l].reshape(1, d)
        g_mlp = norm_mlp_g[l].reshape(1, d)

        tm = _tile(seq, 512)
        q, k, v, kb, vb, qi, sm, smb = _proj_attn(xp, g_mix, wl, aw=aw, iw=iw, n_small=idim + iheads, hd=hd, tm=tm)
        c_in, u_tail, wpa, wpb, wo = _proj_conv(xp, g_mix, wl, o_conv, conv_w[l], None, (w_pa[l], w_pb[l], w_o[l]),
                                                seq=seq, tm=_tile(seq, 512))
        attn, w1, w2 = _attn_prompt(bias_tiles, qi, sm, smb, q, kb, vb, (w_mlp_in[l], w_mlp_out[l]), batch=batch,
                                    seq=seq, tb=tb, heads=heads, hd=hd, iheads=iheads, idim=idim,
                                    topk=min(TOPK_MAX, seq // 4))
        attn_p, c_in_p = attn, c_in
        outs["kp"].append(k.reshape(batch, seq, heads, hd))
        outs["vp"].append(v.reshape(batch, seq, heads, hd))
        outs["kip"].append(sm[:, :idim].reshape(batch, seq, idim))
        outs["sp"].append(u_tail.reshape(batch, -1, SUBLANES, cw)[:, -1, SUBLANES - 2:])

        q, k, v, kb, vb, qi, sm, smb = _proj_attn(xs, g_mix, wl, aw=aw, iw=iw, n_small=idim + iheads, hd=hd, tm=ms)
        st = state_conv[l]
        zero = jnp.zeros((db, tq - 1, cw), F32)
        prev1 = jnp.concatenate([st[:, 1:2], zero], axis=1).reshape(ms, cw)
        prev2 = jnp.concatenate([st, zero[:, 1:]], axis=1).reshape(ms, cw)
        c_in, u_all = _proj_conv(xs, g_mix, wl, o_conv, conv_w[l], (prev1, prev2), seq=tq, tm=ms)

        def pad_rows(a, n):
            return jnp.pad(a, ((0, 0), (0, n - a.shape[1])) + ((0, 0),) * (a.ndim - 2))

        qi_hm = pad_rows(qi.reshape(db, tq, iheads, idim).transpose(0, 2, 1, 3).reshape(db * iheads, tq, idim), rq)
        qi_hm = qi_hm.reshape(db, iheads * rq, idim)
        w_hm = pad_rows(sm[:, idim:idim + iheads].reshape(db, tq, iheads).transpose(0, 2, 1).reshape(db * iheads, tq), rq)
        w_hm = jnp.broadcast_to(w_hm.reshape(db, iheads * rq, 1), (db, iheads * rq, LANES))
        kin_t = jnp.swapaxes(pad_rows(smb[:, :idim].reshape(db, tq, idim), ps), 1, 2)
        topk_s = min(TOPK_MAX, (past + tq) // 4)
        past_keys, snew, thr = _sample_select(page_table, qi_hm, w_hm, kin_t, jnp.swapaxes(cache_kidx[l], 1, 2),
                                              pages=math.gcd(n_pages, 32), tq=tq, rq=rq, iheads=iheads, topk=topk_s)
        ksel, vsel, pos, cnt = _sc_gather(past_keys, thr, page_table, cache_k[l].reshape(n_pool * ps, heads, hd),
                                          cache_v[l].reshape(n_pool * ps, heads, hd), tq=tq, topk=topk_s, ps=ps)
        pos_rows = jnp.repeat(pos, heads, axis=1).reshape(ms, 1, topk_s * heads)
        snew_rows = jnp.repeat(snew[:, :tq, :ps // heads], heads, axis=2).reshape(ms, 1, ps)
        attn = _sample_attn_sel(cnt, rel_bias.T, q.reshape(ms, heads, hd), ksel.reshape(ms * topk_s * heads, hd),
                                vsel.reshape(ms * topk_s * heads, hd), pos_rows, snew_rows, thr,
                                pad_rows(kb.reshape(db, tq * heads, hd), ps), pad_rows(vb.reshape(db, tq * heads, hd), ps),
                                heads=heads, hd=hd, topk=topk_s, tq=tq, past=past)
        attn = attn.reshape(ms, aw).astype(BF16)

        x1p, x1s = _mix(xp, xs, g_mix, attn_p, attn, c_in_p, c_in, wl, o_gate, wpa, wpb, wo, tm=_tile(mp, 256))
        xp_next, xs_next = _mlp(x1p, x1s, g_mlp, gf, w1, w2, tm=_tile(mp, 1024), tf=_tile(w1.shape[1], 512),
                                final=l == depth - 1)
        outs["ks"].append(k.reshape(db, tq, heads, hd))
        outs["vs"].append(v.reshape(db, tq, heads, hd))
        outs["kis"].append(sm[:, :idim].reshape(db, tq, idim))
        outs["ss"].append(u_all.reshape(db, tq, cw)[:, tq - 2:])
        xp, xs = xp_next, xs_next

    st = {k: jnp.stack(v) for k, v in outs.items()}
    return (xp.reshape(batch, seq, d), xs.reshape(db, tq, d), st["kp"], st["vp"], st["kip"], st["sp"],
            st["ks"], st["vs"], st["kis"], st["ss"])
```

```python
import functools
import math

import jax
import jax.numpy as jnp
import numpy as np
from jax import lax
from jax.experimental import pallas as pl
from jax.experimental.pallas import tpu as pltpu
from jax.experimental.pallas import tpu_sc as plsc

F32 = jnp.float32
BF16 = jnp.bfloat16
I32 = jnp.int32

TOPK_MAX = 256
MAX_DISTANCE = 128
EPS = 1e-6

LANES = 128
SUBLANES = 8
V7X_SCOPED_VMEM_BYTES = 60000 * 1024
SC_LANES = 16
V7X_SC_CORES = 2
V7X_SC_SUBCORES = 16

LOG2E = math.log2(math.e)
INT_MIN = np.int32(-2 ** 31)
NEG_BIG = -1e30


def _cparams(*sem):
    return pltpu.CompilerParams(dimension_semantics=sem, vmem_limit_bytes=V7X_SCOPED_VMEM_BYTES)


def _resident(shape):
    nd = len(shape)
    return pl.BlockSpec(shape, lambda *_: (0,) * nd, pipeline_mode=pl.Buffered(1))


def _rms_bf16(x, g):
    y = x * lax.rsqrt(jnp.mean(x * x, axis=-1, keepdims=True) + EPS)
    return (y * g).astype(BF16)


def _dot(a, b):
    return jnp.dot(a, b, preferred_element_type=F32)


def _dot_nt(a, b):
    return lax.dot_general(a, b, (((1,), (1,)), ((), ())), preferred_element_type=F32)


def _ordered_key(x):
    b = lax.bitcast_convert_type(x, I32)
    return b ^ ((b >> 31) & np.int32(0x7FFFFFFF))


def _bucket(n, n_buckets):
    n = jnp.maximum(n, 0)
    me = n_buckets // 2
    nf = jnp.maximum(n, me).astype(F32)
    large = me + (jnp.log(nf / me) / math.log(MAX_DISTANCE / me) * (n_buckets - me)).astype(I32)
    large = jnp.minimum(large, n_buckets - 1)
    return jnp.where(n < me, n, large)


def _bias_of_bucket(bucket, relb_ref, h, n_buckets):
    acc = jnp.zeros(bucket.shape, F32)
    for bkt in range(n_buckets):
        acc = jnp.where(bucket == bkt, relb_ref[bkt, h], acc)
    return acc


def _fold_lanes(x):
    acc = x[:, 0:LANES]
    for t in range(1, x.shape[1] // LANES):
        acc = acc + x[:, t * LANES:(t + 1) * LANES]
    return acc


def _fold_rows(x, op=jnp.add):
    parts = [x[t * SUBLANES:(t + 1) * SUBLANES, :] for t in range(x.shape[0] // SUBLANES)]
    while len(parts) > 1:
        parts = [op(parts[t], parts[t + 1]) for t in range(0, len(parts) - 1, 2)] + parts[len(parts) & ~1:]
    return parts[0]


def _kth_threshold(count_ge, shape, axis, k, trips):
    def bit_body(t, carry):
        uthr, at_thr = carry
        cand = uthr | jnp.left_shift(np.int32(1), 31 - t)
        cnt = jnp.sum(count_ge(cand ^ INT_MIN), axis=axis, keepdims=True)
        return jnp.where(cnt >= k, cand, uthr), jnp.where(cnt >= k, jnp.broadcast_to(cnt, shape), at_thr)

    uthr, at_thr = lax.fori_loop(0, trips, bit_body, (jnp.zeros(shape, I32), jnp.full(shape, k, F32)))
    return jnp.maximum(uthr ^ INT_MIN, INT_MIN + 1), at_thr - k


def _flash_update(lg, v, m_ref, l_ref, acc_ref, h, hd):
    reps = lg.shape[1] // LANES
    m_prev = m_ref[h]
    m_new = jnp.maximum(m_prev, jnp.max(lg, axis=1, keepdims=True))
    p = jnp.exp2(lg - jnp.tile(m_new, (1, reps)))
    alpha = jnp.exp2(m_prev - m_new)
    pv = _dot(p.astype(BF16), jnp.concatenate([v, jnp.ones((v.shape[0], LANES), BF16)], axis=1))
    l_ref[h] = alpha * l_ref[h] + pv[:, hd:]
    m_ref[h] = m_new
    hs = slice(h * hd, (h + 1) * hd)
    acc_ref[:, hs] = acc_ref[:, hs] * jnp.tile(alpha, (1, hd // LANES)) + pv[:, :hd]


def _stage_rows(w_hbm, w_ref, row0):
    @pl.when(pl.program_id(0) == 0)
    def _():
        pltpu.sync_copy(w_hbm.at[pl.ds(row0, w_ref.shape[0])], w_ref)


def _proj_attn_kernel(x_ref, g_ref, w_hbm, q_ref, k_ref, v_ref, kb_ref, vb_ref, qi_ref,
                      sm_ref, smb_ref, w_ref, ws_ref, *, aw, iw, hd, tm, n_small, qscale):
    heads = aw // hd
    _stage_rows(w_hbm, w_ref, 0)

    @pl.when(pl.program_id(0) == 0)
    def _():
        ws_ref[...] = jnp.zeros(ws_ref.shape, BF16)
        pltpu.sync_copy(w_hbm.at[pl.ds(w_ref.shape[0], n_small)], ws_ref.at[pl.ds(0, n_small)])

    hb = _rms_bf16(x_ref[...], g_ref[...])
    q_ref[...] = (_dot_nt(hb, w_ref[0:aw, :]) * qscale).astype(BF16)
    for o_ref, ob_ref, c0 in ((k_ref, kb_ref, aw), (v_ref, vb_ref, 2 * aw)):
        kv = _dot_nt(hb, w_ref[c0:c0 + aw, :])
        ob_ref[...] = kv.astype(BF16)
        for h in range(heads):
            o_ref[pl.ds(h, tm, stride=heads), :] = kv[:, h * hd:(h + 1) * hd]
    qi_ref[...] = _dot_nt(hb, w_ref[3 * aw:3 * aw + iw, :]).astype(BF16)
    sm = _dot_nt(hb, ws_ref[...])
    sm_ref[...] = sm
    smb_ref[...] = sm.astype(BF16)


def _proj_attn(x2d, g, w_all, *, aw, iw, n_small, hd, tm):
    m, d = x2d.shape
    wt = 3 * aw + iw
    heads = aw // hd
    assert hd == LANES and n_small <= LANES and wt % 16 == 0 and n_small % 16 == 0
    row = lambda width: pl.BlockSpec((tm, width), lambda i: (i, 0))
    kv_spec = pl.BlockSpec((tm * heads, hd), lambda i: (i, 0))
    kv_shape = jax.ShapeDtypeStruct((m * heads, hd), F32)
    outs = [(aw, BF16), None, None, (aw, BF16), (aw, BF16), (iw, BF16), (LANES, F32), (LANES, BF16)]
    return pl.pallas_call(
        functools.partial(_proj_attn_kernel, aw=aw, iw=iw, hd=hd, tm=tm, n_small=n_small,
                          qscale=hd ** -0.5 * LOG2E),
        grid=(m // tm,),
        in_specs=[row(d), _resident((1, d)), pl.BlockSpec(memory_space=pl.ANY)],
        out_specs=[kv_spec if o is None else row(o[0]) for o in outs],
        out_shape=[kv_shape if o is None else jax.ShapeDtypeStruct((m, o[0]), o[1]) for o in outs],
        scratch_shapes=[pltpu.VMEM((wt, d), BF16), pltpu.VMEM((LANES, d), BF16)],
        compiler_params=_cparams("arbitrary"),
        name="proj_attn",
    )(x2d, g, w_all)


def _proj_conv_kernel(*refs, cw, tm, seq, tiles_per_seq, tail, row0, n_cast):
    if tiles_per_seq:
        x_ref, g_ref, w_hbm, cwt_ref = refs[:4]
        c_ref, ut_ref = refs[4 + n_cast:6 + n_cast]
        w_ref, carry_ref = refs[6 + 2 * n_cast:]
        _cast_slabs(refs[4:4 + n_cast], refs[6 + n_cast:6 + 2 * n_cast])
    else:
        x_ref, g_ref, w_hbm, cwt_ref, p1_ref, p2_ref, c_ref, ut_ref, w_ref = refs
    _stage_rows(w_hbm, w_ref, row0)
    hb = _rms_bf16(x_ref[...], g_ref[...])
    cx = _dot_nt(hb, w_ref[0:cw, :])
    cb = _dot_nt(hb, w_ref[cw:2 * cw, :])
    cc = _dot_nt(hb, w_ref[2 * cw:3 * cw, :])
    u = cc * cx
    r = lax.broadcasted_iota(I32, (tm, cw), 0)
    um1 = pltpu.roll(u, 1, 0)
    um2 = pltpu.roll(u, 2, 0)
    if tiles_per_seq:
        @pl.when(pl.program_id(0) % tiles_per_seq == 0)
        def _():
            carry_ref[...] = jnp.zeros_like(carry_ref)
        prev1 = jnp.broadcast_to(carry_ref[SUBLANES - 1:SUBLANES, :], (tm, cw))
        prev2 = jnp.broadcast_to(carry_ref[SUBLANES - 2:SUBLANES - 1, :], (tm, cw))
        um1 = jnp.where(r == 0, prev1, um1)
        um2 = jnp.where(r == 0, prev2, jnp.where(r == 1, prev1, um2))
        carry_ref[...] = u[tm - SUBLANES:tm, :]
    else:
        assert seq & (seq - 1) == 0
        t = r & (seq - 1)
        um1 = jnp.where(t >= 1, um1, p1_ref[...])
        um2 = jnp.where(t >= 2, um2, p2_ref[...])
    y = cwt_ref[0:1, :] * um2 + cwt_ref[1:2, :] * um1 + cwt_ref[2:3, :] * u
    c_ref[...] = (cb * y).astype(BF16)
    ut_ref[...] = u[tm - tail:tm, :]


def _proj_conv(x2d, g, w_all, row0, conv_w, prev, to_cast=(), *, seq, tm):
    m, d = x2d.shape
    cw = conv_w.shape[1]
    assert conv_w.shape[0] == 3 and row0 % 16 == 0
    row = lambda width: pl.BlockSpec((tm, width), lambda i: (i, 0))
    in_specs = [row(d), _resident((1, d)), pl.BlockSpec(memory_space=pl.ANY), _resident((3, cw))]
    args = [x2d, g, w_all, conv_w]
    scratch = [pltpu.VMEM((3 * cw, d), BF16)]
    if prev is None:
        assert seq % tm == 0
        tiles_per_seq, tail = seq // tm, SUBLANES
        scratch += [pltpu.VMEM((SUBLANES, cw), F32)]
    else:
        assert tm % seq == 0 and seq >= 2
        tiles_per_seq, tail = 0, tm
        in_specs += [row(cw), row(cw)]
        args += list(prev)
    assert not (to_cast and prev is not None)
    c_in, c_out, c_shapes = _cast_specs(to_cast, m // tm, lambda i: i)
    return pl.pallas_call(
        functools.partial(_proj_conv_kernel, cw=cw, tm=tm, seq=seq, tiles_per_seq=tiles_per_seq, tail=tail,
                          row0=row0, n_cast=len(to_cast)),
        grid=(m // tm,),
        in_specs=in_specs + c_in,
        out_specs=[row(cw), pl.BlockSpec((tail, cw), lambda i: (i, 0))] + c_out,
        out_shape=[jax.ShapeDtypeStruct((m, cw), BF16), jax.ShapeDtypeStruct((m // tm * tail, cw), F32)] + c_shapes,
        scratch_shapes=scratch,
        compiler_params=_cparams("arbitrary"),
        name="proj_conv",
    )(*args, *to_cast)


def _bias_tiles_kernel(relb_ref, o_ref, *, tb, n_buckets):
    kind = pl.program_id(0)
    h = pl.program_id(1)
    qry = lax.broadcasted_iota(I32, (tb, tb), 0)
    key = lax.broadcasted_iota(I32, (tb, tb), 1)
    bias = _bias_of_bucket(_bucket(kind * tb + qry - key, n_buckets), relb_ref, h, n_buckets)
    o_ref[...] = (bias - relb_ref[n_buckets - 1, h]) * LOG2E


def _bias_tiles(rel_bias, *, tb):
    n_buckets, heads = rel_bias.shape
    return pl.pallas_call(
        functools.partial(_bias_tiles_kernel, tb=tb, n_buckets=n_buckets),
        grid=(2, heads),
        in_specs=[pl.BlockSpec(memory_space=pltpu.SMEM)],
        out_specs=pl.BlockSpec((None, None, tb, tb), lambda a, h: (a, h, 0, 0)),
        out_shape=jax.ShapeDtypeStruct((2, heads, tb, tb), F32),
        compiler_params=_cparams("arbitrary", "arbitrary"),
        name="bias_tiles",
    )(rel_bias)


def _cast_specs(weights, n_steps, step_of):
    in_specs, out_specs, out_shapes = [], [], []
    for w in weights:
        rows, cols = w.shape
        assert rows % n_steps == 0 and (rows // n_steps) % 16 == 0
        spec = pl.BlockSpec((rows // n_steps, cols), lambda *idx: (step_of(*idx), 0))
        in_specs.append(spec)
        out_specs.append(spec)
        out_shapes.append(jax.ShapeDtypeStruct((rows, cols), BF16))
    return in_specs, out_specs, out_shapes


def _cast_slabs(src_refs, dst_refs):
    for src, dst in zip(src_refs, dst_refs):
        dst[...] = src[...].astype(BF16)


def _attn_prompt_kernel(*refs, n_cast, tb, heads, hd, iheads, idim, topk):
    qi_ref, sm_ref, smb_ref, q_ref, kb_ref, vb_ref, bias_ref = refs[:7]
    o_ref = refs[7 + n_cast]
    skey_ref, thr_ref, wt_ref, madd_ref, m_ref, l_ref, acc_ref = refs[8 + 2 * n_cast:]
    _cast_slabs(refs[7:7 + n_cast], refs[8 + n_cast:8 + 2 * n_cast])
    i = pl.program_id(1)
    wscale = idim ** -0.5 * iheads ** -0.5

    m_ref[...] = jnp.full(m_ref.shape, NEG_BIG, F32)
    l_ref[...] = jnp.zeros(l_ref.shape, F32)
    acc_ref[...] = jnp.zeros(acc_ref.shape, F32)
    wt_ref[...] = sm_ref[...].T * wscale

    def chunk(j):
        return pl.ds(pl.multiple_of(j * tb, tb), tb)

    def score_chunk(j, diag):
        kic = smb_ref[chunk(j), 0:idim]
        acc = jnp.zeros((tb, tb), F32)
        for h in range(iheads):
            s = _dot_nt(kic, qi_ref[:, h * idim:(h + 1) * idim])
            acc = acc + jnp.maximum(s, 0.0) * wt_ref[idim + h:idim + h + 1, :]
        key = _ordered_key(acc)
        if diag:
            kpos = lax.broadcasted_iota(I32, (tb, tb), 0)
            qpos = lax.broadcasted_iota(I32, (tb, tb), 1)
            key = jnp.where(kpos > qpos, INT_MIN, key)
        skey_ref[chunk(j), :] = key

    def score_body(j, carry):
        score_chunk(j, False)
        return carry

    lax.fori_loop(0, i, score_body, 0)
    score_chunk(i, True)

    def count_ge(scand):
        def body(j, cnt):
            return cnt + _fold_rows(jnp.where(skey_ref[chunk(j), :] >= scand[0:1, :], 1.0, 0.0))

        return lax.fori_loop(0, i + 1, body, jnp.zeros((SUBLANES, tb), F32))

    trips = jnp.where((i + 1) * tb <= topk, 0, 32)
    thr, surplus = _kth_threshold(count_ge, (SUBLANES, tb), 0, float(topk), trips)
    thr_ref[...] = thr

    @pl.when(jnp.max(surplus) > 0.0)
    def _():
        thr_row = thr_ref[0:1, :]

        def gt_body(j, cnt):
            return cnt + _fold_rows(jnp.where(skey_ref[chunk(j), :] > thr_row, 1.0, 0.0))

        above = lax.fori_loop(0, i + 1, gt_body, jnp.zeros((SUBLANES, tb), F32))
        budget = topk - jnp.sum(above, axis=0, keepdims=True)
        kpos = lax.broadcasted_iota(I32, (tb, tb), 0)
        earlier = jnp.where(kpos > lax.broadcasted_iota(I32, (tb, tb), 1), 1.0, 0.0).astype(BF16)

        def fix_body(j, seen):
            sk = skey_ref[chunk(j), :]
            tie = jnp.where(sk == thr_row, 1.0, 0.0)
            rank = seen[0:1, :] + _dot(earlier, tie.astype(BF16))
            skey_ref[chunk(j), :] = jnp.where((tie > 0.0) & (rank >= budget), thr_row - 1, sk)
            return seen + jnp.sum(_fold_rows(tie), axis=0, keepdims=True)

        lax.fori_loop(0, i + 1, fix_body, jnp.zeros((SUBLANES, tb), F32))

    def attend_chunk(j, kind):
        madd_ref[...] = jnp.where(skey_ref[chunk(j), :] >= thr_ref[0:1, :], 0.0, NEG_BIG).T
        for h in range(heads):
            hs = slice(h * hd, (h + 1) * hd)
            lg = _dot_nt(q_ref[:, hs], kb_ref[chunk(j), hs]) + madd_ref[...]
            if kind is not None:
                lg = lg + bias_ref[kind, h]
            _flash_update(lg, vb_ref[chunk(j), hs], m_ref, l_ref, acc_ref, h, hd)

    def attend_body(j, carry):
        attend_chunk(j, None)
        return carry

    lax.fori_loop(0, jnp.maximum(i - 1, 0), attend_body, 0)

    @pl.when(i >= 1)
    def _():
        attend_chunk(i - 1, 1)

    attend_chunk(i, 0)

    for h in range(heads):
        hs = slice(h * hd, (h + 1) * hd)
        o_ref[:, hs] = (acc_ref[:, hs] / jnp.tile(l_ref[h], (1, hd // LANES))).astype(BF16)


def _attn_prompt(bias_tiles, qi, sm, smb, q, kb, vb, to_cast, *, batch, seq, tb, heads, hd, iheads, idim, topk):
    m, aw = q.shape
    iw = qi.shape[1]
    nq = seq // tb
    assert tb >= MAX_DISTANCE and tb % LANES == 0 and hd % LANES == 0
    qrow = lambda width: pl.BlockSpec((tb, width), lambda b, i: (b * nq + i, 0))
    seqblk = lambda width: pl.BlockSpec((seq, width), lambda b, i: (b, 0))
    c_in, c_out, c_shapes = _cast_specs(to_cast, batch * nq, lambda b, i: b * nq + i)
    return pl.pallas_call(
        functools.partial(_attn_prompt_kernel, n_cast=len(to_cast), tb=tb, heads=heads, hd=hd, iheads=iheads,
                          idim=idim, topk=topk),
        grid=(batch, nq),
        in_specs=[qrow(iw), qrow(LANES), seqblk(LANES), qrow(aw), seqblk(aw), seqblk(aw),
                  _resident(bias_tiles.shape)] + c_in,
        out_specs=[qrow(aw)] + c_out,
        out_shape=[jax.ShapeDtypeStruct((m, aw), BF16)] + c_shapes,
        scratch_shapes=[pltpu.VMEM((seq, tb), I32), pltpu.VMEM((SUBLANES, tb), I32),
                        pltpu.VMEM((LANES, tb), F32), pltpu.VMEM((tb, tb), F32),
                        pltpu.VMEM((heads, tb, LANES), F32), pltpu.VMEM((heads, tb, LANES), F32),
                        pltpu.VMEM((tb, aw), F32)],
        compiler_params=_cparams("arbitrary", "arbitrary"),
        name="attn_prompt",
    )(qi, sm, smb, q, kb, vb, bias_tiles, *to_cast)


def _sample_select_kernel(pt_ref, qi_ref, w_ref, kin_ref, *rest, pages, ps, nc, iheads, idim, tq, rq, topk):
    page_refs = rest[:pages]
    past_ref, new_ref, thr_ref, row_ref = rest[pages:]
    b = pl.program_id(0)
    c = pl.program_id(1)
    db = pl.num_programs(0)
    ch = pages * ps
    wscale = idim ** -0.5 * iheads ** -0.5
    rows_b = pl.ds(pl.multiple_of(b * rq, rq), rq)

    def score(keys_t):
        s = _dot(qi_ref[...], keys_t)
        t = jnp.maximum(s, 0.0) * (w_ref[:, 0:1] * wscale)
        acc = t[0:rq]
        for h in range(1, iheads):
            acc = acc + t[h * rq:(h + 1) * rq]
        return _ordered_key(acc)

    @pl.when(c < nc)
    def _():
        for p in range(pages):
            key = score(page_refs[p][...].astype(BF16))
            row_ref[rows_b, pl.ds(pl.multiple_of(c * ch + p * ps, LANES), ps)] = key
            for t in range(ps // LANES):
                for j in range(tq):
                    past_ref[j, p * (ps // LANES) + t:p * (ps // LANES) + t + 1, :] = \
                        key[j:j + 1, t * LANES:(t + 1) * LANES]

    @pl.when(c == nc)
    def _():
        key = score(kin_ref[...])
        j = lax.broadcasted_iota(I32, (rq, ps), 0)
        n = lax.broadcasted_iota(I32, (rq, ps), 1)
        key = jnp.where((n <= j) & (n < tq), key, INT_MIN)
        row_ref[rows_b, nc * ch:nc * ch + ps] = key

    @pl.when((c == nc) & (b == db - 1))
    def _():
        n_rows = row_ref.shape[0]

        def count_ge(scand):
            sk = row_ref[...]
            return _fold_lanes(jnp.where(sk >= jnp.tile(scand, (1, sk.shape[1] // LANES)), 1.0, 0.0))

        thr, _ = _kth_threshold(count_ge, (n_rows, LANES), 1, float(topk), 32)
        count = lambda m: jnp.sum(_fold_lanes(jnp.where(m, 1.0, 0.0)), axis=1, keepdims=True)
        past_keys = row_ref[:, 0:nc * ch]
        new_keys = row_ref[:, nc * ch:nc * ch + ps]
        thr_new = jnp.tile(thr, (1, ps // LANES))
        gt_new = count(new_keys > thr_new)
        ties_allowed = topk - gt_new - count(past_keys > jnp.tile(thr, (1, nc * ch // LANES)))
        ties_past = count(past_keys == jnp.tile(thr, (1, nc * ch // LANES)))
        ties_new_allowed = ties_allowed - jnp.minimum(ties_past, ties_allowed)
        tie = jnp.where(new_keys == thr_new, 1.0, 0.0)
        lane = lax.broadcasted_iota(I32, tie.shape, 1)
        rank = jnp.zeros(tie.shape, F32)
        for s in range(1, tq):
            rank = rank + jnp.where(lane >= s, pltpu.roll(tie, s, 1), 0.0)
        new_keys = jnp.where((tie > 0.0) & (rank >= ties_new_allowed), thr_new - 1, new_keys)
        budget = jnp.broadcast_to(topk - gt_new, (n_rows, LANES)).astype(I32)
        thr_ref[...] = jnp.zeros(thr_ref.shape, I32)
        for bb in range(n_rows // rq):
            new_ref[bb] = new_keys[bb * rq:(bb + 1) * rq]
            for jj in range(tq):
                thr_ref[bb * tq + jj, 0:1, :] = thr[bb * rq + jj:bb * rq + jj + 1, :]
                thr_ref[bb * tq + jj, 1:2, :] = budget[bb * rq + jj:bb * rq + jj + 1, :]


def _sample_select(page_table, qi_hm, w_hm, kin_t, cache_kidx_t, *, pages, tq, rq, iheads, topk):
    db, n_pages = page_table.shape
    _, idim, ps = cache_kidx_t.shape
    nc = n_pages // pages
    ch = pages * ps
    assert ps % LANES == 0 and (ch // LANES) % SUBLANES == 0
    page_spec = lambda p: pl.BlockSpec(
        (None, idim, ps), lambda b, c, pt: (pt[b, jnp.minimum(c, nc - 1) * pages + p], 0, 0))
    per_b = lambda shape: pl.BlockSpec((None,) + shape, lambda b, c, pt: (b, 0, 0))
    return pl.pallas_call(
        functools.partial(_sample_select_kernel, pages=pages, ps=ps, nc=nc, iheads=iheads, idim=idim, tq=tq, rq=rq,
                          topk=topk),
        grid_spec=pltpu.PrefetchScalarGridSpec(
            num_scalar_prefetch=1,
            grid=(db, nc + 1),
            in_specs=[per_b((iheads * rq, idim)), per_b((iheads * rq, LANES)), per_b((idim, ps))]
            + [page_spec(p) for p in range(pages)],
            out_specs=[pl.BlockSpec((tq, ch // LANES, LANES), lambda b, c, pt: (b, jnp.minimum(c, nc - 1), 0)),
                       pl.BlockSpec((db, rq, ps), lambda b, c, pt: (0, 0, 0)),
                       pl.BlockSpec((db * tq, SUBLANES, LANES), lambda b, c, pt: (0, 0, 0))],
            scratch_shapes=[pltpu.VMEM((db * rq, nc * ch + ps), I32)],
        ),
        out_shape=[jax.ShapeDtypeStruct((db * tq, n_pages * ps // LANES, LANES), I32),
                   jax.ShapeDtypeStruct((db, rq, ps), I32),
                   jax.ShapeDtypeStruct((db * tq, SUBLANES, LANES), I32)],
        compiler_params=_cparams("arbitrary", "arbitrary"),
        name="sample_select",
    )(page_table, qi_hm, w_hm, kin_t, *([cache_kidx_t] * pages))


def _sc_gather_kernel(past_hbm, thr_hbm, pt_hbm, ck_hbm, cv_hbm, ksel_hbm, vsel_hbm, pos_hbm, cnt_hbm,
                      row_v, thr_v, pt_v, idx_v, phys_v, rows_v, cnt_v, sem,
                      *, nq, tq, topk, ps, n_cores, rows_per_copy):
    wid = lax.axis_index("s") * n_cores + lax.axis_index("c")

    @pl.when(wid < nq)
    def _():
        pltpu.sync_copy(past_hbm.at[wid], row_v)
        pltpu.sync_copy(thr_hbm.at[wid], thr_v)
        pltpu.sync_copy(pt_hbm.at[wid // tq], pt_v)
        thr = thr_v[0, pl.ds(0, SC_LANES)]
        budget = thr_v[1, pl.ds(0, SC_LANES)]
        lane = lax.iota(I32, SC_LANES)
        zero = jnp.zeros((SC_LANES,), I32)
        for t in range(idx_v.shape[0] // SC_LANES):
            idx_v[pl.ds(t * SC_LANES, SC_LANES)] = zero

        def compact(pred):
            def body(r, cnt):
                for t in range(LANES // SC_LANES):
                    x = row_v[r, pl.ds(t * SC_LANES, SC_LANES)]
                    m = pred(x, cnt)
                    rank = plsc.cumsum(jnp.where(m, 1, 0).astype(I32))
                    plsc.store_scatter(idx_v, [cnt + rank - 1], lane + (r * LANES + t * SC_LANES), mask=m)
                    cnt = cnt + plsc.all_reduce_population_count(m)
                return cnt
            return body

        cnt = lax.fori_loop(0, row_v.shape[0], compact(lambda x, cnt: x > thr), zero)
        cnt = lax.fori_loop(0, row_v.shape[0], compact(lambda x, cnt: (x == thr) & (cnt < budget)), cnt)
        cnt_v[...] = jnp.minimum(cnt, budget)
        pltpu.sync_copy(cnt_v, cnt_hbm.at[wid])
        pltpu.sync_copy(idx_v.at[pl.ds(0, topk)], pos_hbm.at[wid])

        shift = ps.bit_length() - 1
        for t in range(topk // SC_LANES):
            pos = idx_v[pl.ds(t * SC_LANES, SC_LANES)]
            page = plsc.load_gather(pt_v, [lax.shift_right_logical(pos, shift)])
            phys_v[pl.ds(t * SC_LANES, SC_LANES)] = page * ps + (pos & (ps - 1))
        for g in range(topk // rows_per_copy):
            sel = phys_v.at[pl.ds(g * rows_per_copy, rows_per_copy)]
            dst = pl.ds(wid * topk + g * rows_per_copy, rows_per_copy)
            for src_hbm, dst_hbm in ((ck_hbm, ksel_hbm), (cv_hbm, vsel_hbm)):
                pltpu.async_copy(src_hbm.at[sel], rows_v, sem).wait()
                pltpu.sync_copy(rows_v, dst_hbm.at[dst])


def _sc_gather(past_keys, thr, page_table, cache_k, cache_v, *, tq, topk, ps):
    nq, key_rows, _ = past_keys.shape
    _, heads, hd = cache_k.shape
    assert ps & (ps - 1) == 0 and topk % SC_LANES == 0
    rows_per_copy = 64
    assert topk % rows_per_copy == 0
    mesh = plsc.VectorSubcoreMesh(core_axis_name="c", subcore_axis_name="s", num_cores=V7X_SC_CORES,
                                  num_subcores=V7X_SC_SUBCORES)
    assert nq <= V7X_SC_CORES * V7X_SC_SUBCORES
    sel_shape = jax.ShapeDtypeStruct((nq * topk, heads, hd), F32)
    return pl.kernel(
        functools.partial(_sc_gather_kernel, nq=nq, tq=tq, topk=topk, ps=ps, n_cores=V7X_SC_CORES,
                          rows_per_copy=rows_per_copy),
        out_type=[sel_shape, sel_shape, jax.ShapeDtypeStruct((nq, topk), I32),
                  jax.ShapeDtypeStruct((nq, SC_LANES), I32)],
        mesh=mesh,
        scratch_types=[pltpu.VMEM((key_rows, LANES), I32), pltpu.VMEM((SUBLANES, LANES), I32),
                       pltpu.VMEM((page_table.shape[1],), I32), pltpu.VMEM((topk + SC_LANES,), I32),
                       pltpu.VMEM((topk,), I32), pltpu.VMEM((rows_per_copy, heads, hd), F32),
                       pltpu.VMEM((SC_LANES,), I32), pltpu.SemaphoreType.DMA],
        compiler_params=pltpu.CompilerParams(needs_layout_passes=False),
        name="sc_select_gather",
    )(past_keys, thr, page_table, cache_k, cache_v)


def _sample_attn_sel_kernel(cnt_ref, relbt_ref, q_ref, ksel_ref, vsel_ref, pos_ref, snew_ref, thr_ref, kn_ref, vn_ref,
                            o_ref, *, heads, tq, past, n_buckets):
    w = pl.program_id(0)
    j = w % tq
    q = q_ref[...]

    def head_bias(dist):
        bucket = _bucket(dist, n_buckets)
        acc = jnp.zeros(bucket.shape, F32)
        for bkt in range(n_buckets):
            acc = jnp.where(bucket == bkt, relbt_ref[:, bkt:bkt + 1], acc)
        return acc * LOG2E

    def logits(keys):
        lg = _dot_nt(q, keys)
        head = lax.broadcasted_iota(I32, lg.shape, 0)
        col = lax.broadcasted_iota(I32, lg.shape, 1)
        return lg, (col & (heads - 1)) == head, col

    lg, own, col = logits(ksel_ref[...].astype(BF16))
    keep = own & (col < cnt_ref[w, 0] * heads)
    lg = jnp.where(keep, lg + head_bias(jnp.broadcast_to(past + j - pos_ref[...], lg.shape)), NEG_BIG)
    lgn, own, col = logits(kn_ref[...])
    keep = own & (snew_ref[...] >= thr_ref[0:1, :])
    lgn = jnp.where(keep, lgn + head_bias(j - lax.shift_right_logical(col, heads.bit_length() - 1)), NEG_BIG)

    m = jnp.maximum(jnp.max(lg, axis=1, keepdims=True), jnp.max(lgn, axis=1, keepdims=True))
    p = jnp.exp2(lg - m)
    pn = jnp.exp2(lgn - m)
    denom = jnp.sum(p, axis=1, keepdims=True) + jnp.sum(pn, axis=1, keepdims=True)
    acc = _dot(p.astype(BF16), vsel_ref[...].astype(BF16)) + _dot(pn.astype(BF16), vn_ref[...])
    o_ref[...] = acc / denom


def _sample_attn_sel(cnt, rel_bias_t, q, ksel, vsel, pos, snew, thr, kn, vn, *, heads, hd, topk, tq, past):
    nq = q.shape[0]
    ps = kn.shape[1]
    assert ps == LANES and hd == LANES and heads & (heads - 1) == 0
    per_q = lambda shape: pl.BlockSpec((None,) + shape, lambda w, cnt: (w, 0, 0))
    per_b = lambda shape: pl.BlockSpec((None,) + shape, lambda w, cnt: (w // tq, 0, 0))
    sel_spec = pl.BlockSpec((topk * heads, hd), lambda w, cnt: (w, 0))
    return pl.pallas_call(
        functools.partial(_sample_attn_sel_kernel, heads=heads, tq=tq, past=past, n_buckets=rel_bias_t.shape[1]),
        grid_spec=pltpu.PrefetchScalarGridSpec(
            num_scalar_prefetch=1,
            grid=(nq,),
            in_specs=[pl.BlockSpec(rel_bias_t.shape, lambda w, cnt: (0, 0)), per_q((heads, hd)), sel_spec, sel_spec,
                      per_q((1, topk * heads)), per_q((1, ps)), per_q((SUBLANES, LANES)), per_b((ps, hd)),
                      per_b((ps, hd))],
            out_specs=per_q((heads, hd)),
        ),
        out_shape=jax.ShapeDtypeStruct((nq, heads, hd), F32),
        compiler_params=_cparams("arbitrary"),
        name="sample_attn_sel",
    )(cnt, rel_bias_t, q, ksel, vsel, pos, snew, thr, kn, vn)


def _mix_kernel(x_ref, xs_ref, g_ref, a_ref, as_ref, c_ref, cs_ref, w_hbm, wpa_ref, wpb_ref, wo_ref, o_ref, os_ref,
                wg_ref, *, d, row0):
    _stage_rows(w_hbm, wg_ref, row0)
    n = pl.num_programs(0) - 1

    def rows(x_ref, a_ref, c_ref, o_ref):
        x = x_ref[...]
        hb = _rms_bf16(x, g_ref[...])
        a = _dot(a_ref[...], wpa_ref[...])
        m = jax.nn.sigmoid(_dot_nt(hb, wg_ref[0:d, :])) * a
        c = _dot(c_ref[...], wpb_ref[...])
        m = m + jax.nn.sigmoid(_dot_nt(hb, wg_ref[d:2 * d, :])) * c
        o_ref[...] = x + _dot(m.astype(BF16), wo_ref[...])

    @pl.when(pl.program_id(0) < n)
    def _():
        rows(x_ref, a_ref, c_ref, o_ref)

    @pl.when(pl.program_id(0) == n)
    def _():
        rows(xs_ref, as_ref, cs_ref, os_ref)


def _mix(xp, xs, g, attn_p, attn_s, c_p, c_s, w_all, row0, w_pa, w_pb, w_o, *, tm):
    m, d = xp.shape
    ms = xs.shape[0]
    n = m // tm
    assert row0 % 16 == 0
    prow = lambda width: pl.BlockSpec((tm, width), lambda i: (jnp.minimum(i, n - 1), 0))
    srow = lambda width: pl.BlockSpec((ms, width), lambda i: (0, 0))
    return pl.pallas_call(
        functools.partial(_mix_kernel, d=d, row0=row0),
        grid=(n + 1,),
        in_specs=[prow(d), srow(d), _resident((1, d)), prow(attn_p.shape[1]), srow(attn_s.shape[1]),
                  prow(c_p.shape[1]), srow(c_s.shape[1]), pl.BlockSpec(memory_space=pl.ANY), _resident(w_pa.shape),
                  _resident(w_pb.shape), _resident(w_o.shape)],
        out_specs=[prow(d), srow(d)],
        out_shape=[jax.ShapeDtypeStruct((m, d), F32), jax.ShapeDtypeStruct((ms, d), F32)],
        scratch_shapes=[pltpu.VMEM((2 * d, d), BF16)],
        compiler_params=_cparams("arbitrary"),
        name="mix_out",
    )(xp, xs, g, attn_p, attn_s, c_p, c_s, w_all, w_pa, w_pb, w_o)


def _mlp_kernel(x_ref, xs_ref, g_ref, gf_ref, w1_ref, w2_ref, y_ref, ys_ref, h_ref, *, tm, final):
    f = pl.program_id(1)
    last_tile = pl.program_id(0) == pl.num_programs(0) - 1

    def start(x_ref, y_ref, h_rows):
        h_ref[h_rows, :] = _rms_bf16(x_ref[...], g_ref[...])
        y_ref[...] = jnp.zeros(y_ref.shape, F32)

    def finish(x_ref, y_ref):
        x2 = x_ref[...] + y_ref[...]
        if final:
            x2 = x2 * lax.rsqrt(jnp.mean(x2 * x2, axis=-1, keepdims=True) + EPS) * gf_ref[...]
        y_ref[...] = x2

    def ff(h):
        t = jnp.square(jnp.maximum(_dot(h, w1_ref[...]), 0.0))
        return _dot(t.astype(BF16), w2_ref[...])

    @pl.when(f == 0)
    def _():
        start(x_ref, y_ref, slice(0, tm))

    @pl.when((f == 0) & last_tile)
    def _():
        start(xs_ref, ys_ref, slice(tm, h_ref.shape[0]))

    @pl.when(jnp.logical_not(last_tile))
    def _():
        y_ref[...] += ff(h_ref[0:tm, :])

    @pl.when(last_tile)
    def _():
        r = ff(h_ref[...])
        y_ref[...] += r[0:tm]
        ys_ref[...] += r[tm:]

    @pl.when(f == pl.num_programs(1) - 1)
    def _():
        finish(x_ref, y_ref)

    @pl.when((f == pl.num_programs(1) - 1) & last_tile)
    def _():
        finish(xs_ref, ys_ref)


def _mlp(xp, xs, g, gf, w1, w2, *, tm, tf, final):
    m, d = xp.shape
    ms = xs.shape[0]
    ff = w1.shape[1]
    srow = pl.BlockSpec((ms, d), lambda i, f: (0, 0))
    return pl.pallas_call(
        functools.partial(_mlp_kernel, tm=tm, final=final),
        grid=(m // tm, ff // tf),
        in_specs=[pl.BlockSpec((tm, d), lambda i, f: (i, 0)), srow, _resident((1, d)), _resident((1, d)),
                  pl.BlockSpec((d, tf), lambda i, f: (0, f)), pl.BlockSpec((tf, d), lambda i, f: (f, 0))],
        out_specs=[pl.BlockSpec((tm, d), lambda i, f: (i, 0)), srow],
        out_shape=[jax.ShapeDtypeStruct((m, d), F32), jax.ShapeDtypeStruct((ms, d), F32)],
        scratch_shapes=[pltpu.VMEM((tm + ms, d), BF16)],
        compiler_params=_cparams("arbitrary", "arbitrary"),
        name="mlp",
    )(xp, xs, g, gf, w1, w2)


def _tile(m, cap):
    return min(m, cap)


def kernel(x_prompt, x_sample, cache_k, cache_v, cache_kidx, state_conv, page_table, rel_bias, norm_mix_g, w_in,
           conv_w, w_pa, w_pb, w_o, norm_mlp_g, w_mlp_in, w_mlp_out, norm_final_g):
    batch, seq, d = x_prompt.shape
    db, tq, _ = x_sample.shape
    depth, n_pool, ps, heads, hd = cache_k.shape
    idim = cache_kidx.shape[-1]
    cw = conv_w.shape[-1]
    aw = heads * hd
    n_in = w_in.shape[-1]
    iheads = (n_in - 3 * aw - idim - 3 * cw - 2 * d) // (idim + 1)
    iw = iheads * idim
    assert 3 * aw + iw + idim + iheads + 3 * cw + 2 * d == n_in and idim + iheads <= LANES
    n_pages = page_table.shape[1]
    past = n_pages * ps
    rq = SUBLANES
    assert tq <= rq

    mp, ms = batch * seq, db * tq
    xp = x_prompt.reshape(mp, d)
    xs = x_sample.reshape(ms, d)
    tb = _tile(seq, 256)
    pages = math.gcd(n_pages, 8)
    o_small = 3 * aw + iw
    o_conv = o_small + idim + iheads
    o_gate = o_conv + 3 * cw
    gf = norm_final_g.reshape(1, d)
    bias_tiles = _bias_tiles(rel_bias, tb=tb)

    outs = {k: [] for k in ("kp", "vp", "kip", "sp", "ks", "vs", "kis", "ss")}
    for l in range(depth):
        wl = jnp.swapaxes(w_in[l], 0, 1).astype(BF16)
        g_mix = norm_mix_g[l].reshape(1, d)
        g_mlp = norm_mlp_g[l].reshape(1, d)

        tm = _tile(seq, 512)
        q, k, v, kb, vb, qi, sm, smb = _proj_attn(xp, g_mix, wl, aw=aw, iw=iw, n_small=idim + iheads, hd=hd, tm=tm)
        c_in, u_tail, wpa, wpb, wo = _proj_conv(xp, g_mix, wl, o_conv, conv_w[l], None, (w_pa[l], w_pb[l], w_o[l]),
                                                seq=seq, tm=_tile(seq, 512))
        attn, w1, w2 = _attn_prompt(bias_tiles, qi, sm, smb, q, kb, vb, (w_mlp_in[l], w_mlp_out[l]), batch=batch,
                                    seq=seq, tb=tb, heads=heads, hd=hd, iheads=iheads, idim=idim,
                                    topk=min(TOPK_MAX, seq // 4))
        attn_p, c_in_p = attn, c_in
        outs["kp"].append(k.reshape(batch, seq, heads, hd))
        outs["vp"].append(v.reshape(batch, seq, heads, hd))
        outs["kip"].append(sm[:, :idim].reshape(batch, seq, idim))
        outs["sp"].append(u_tail.reshape(batch, -1, SUBLANES, cw)[:, -1, SUBLANES - 2:])

        q, k, v, kb, vb, qi, sm, smb = _proj_attn(xs, g_mix, wl, aw=aw, iw=iw, n_small=idim + iheads, hd=hd, tm=ms)
        st = state_conv[l]
        zero = jnp.zeros((db, tq - 1, cw), F32)
        prev1 = jnp.concatenate([st[:, 1:2], zero], axis=1).reshape(ms, cw)
        prev2 = jnp.concatenate([st, zero[:, 1:]], axis=1).reshape(ms, cw)
        c_in, u_all = _proj_conv(xs, g_mix, wl, o_conv, conv_w[l], (prev1, prev2), seq=tq, tm=ms)

        def pad_rows(a, n):
            return jnp.pad(a, ((0, 0), (0, n - a.shape[1])) + ((0, 0),) * (a.ndim - 2))

        qi_hm = pad_rows(qi.reshape(db, tq, iheads, idim).transpose(0, 2, 1, 3).reshape(db * iheads, tq, idim), rq)
        qi_hm = qi_hm.reshape(db, iheads * rq, idim)
        w_hm = pad_rows(sm[:, idim:idim + iheads].reshape(db, tq, iheads).transpose(0, 2, 1).reshape(db * iheads, tq), rq)
        w_hm = jnp.broadcast_to(w_hm.reshape(db, iheads * rq, 1), (db, iheads * rq, LANES))
        kin_t = jnp.swapaxes(pad_rows(smb[:, :idim].reshape(db, tq, idim), ps), 1, 2)
        topk_s = min(TOPK_MAX, (past + tq) // 4)
        past_keys, snew, thr = _sample_select(page_table, qi_hm, w_hm, kin_t, jnp.swapaxes(cache_kidx[l], 1, 2),
                                              pages=math.gcd(n_pages, 32), tq=tq, rq=rq, iheads=iheads, topk=topk_s)
        ksel, vsel, pos, cnt = _sc_gather(past_keys, thr, page_table, cache_k[l].reshape(n_pool * ps, heads, hd),
                                          cache_v[l].reshape(n_pool * ps, heads, hd), tq=tq, topk=topk_s, ps=ps)
        pos_rows = jnp.repeat(pos, heads, axis=1).reshape(ms, 1, topk_s * heads)
        snew_rows = jnp.repeat(snew[:, :tq, :ps // heads], heads, axis=2).reshape(ms, 1, ps)
        attn = _sample_attn_sel(cnt, rel_bias.T, q.reshape(ms, heads, hd), ksel.reshape(ms * topk_s * heads, hd),
                                vsel.reshape(ms * topk_s * heads, hd), pos_rows, snew_rows, thr,
                                pad_rows(kb.reshape(db, tq * heads, hd), ps), pad_rows(vb.reshape(db, tq * heads, hd), ps),
                                heads=heads, hd=hd, topk=topk_s, tq=tq, past=past)
        attn = attn.reshape(ms, aw).astype(BF16)

        x1p, x1s = _mix(xp, xs, g_mix, attn_p, attn, c_in_p, c_in, wl, o_gate, wpa, wpb, wo, tm=_tile(mp, 256))
        xp_next, xs_next = _mlp(x1p, x1s, g_mlp, gf, w1, w2, tm=_tile(mp, 1024), tf=_tile(w1.shape[1], 512),
                                final=l == depth - 1)
        outs["ks"].append(k.reshape(db, tq, heads, hd))
        outs["vs"].append(v.reshape(db, tq, heads, hd))
        outs["kis"].append(sm[:, :idim].reshape(db, tq, idim))
        outs["ss"].append(u_all.reshape(db, tq, cw)[:, tq - 2:])
        xp, xs = xp_next, xs_next

    st = {k: jnp.stack(v) for k, v in outs.items()}
    return (xp.reshape(batch, seq, d), xs.reshape(db, tq, d), st["kp"], st["vp"], st["kip"], st["sp"],
            st["ks"], st["vs"], st["kis"], st["ss"])
```

```python
import functools
import math

import jax
import jax.numpy as jnp
import numpy as np
from jax import lax
from jax.experimental import pallas as pl
from jax.experimental.pallas import tpu as pltpu
from jax.experimental.pallas import tpu_sc as plsc

F32 = jnp.float32
BF16 = jnp.bfloat16
I32 = jnp.int32

TOPK_MAX = 256
MAX_DISTANCE = 128
EPS = 1e-6

LANES = 128
SUBLANES = 8
V7X_SCOPED_VMEM_BYTES = 60000 * 1024
SC_LANES = 16
V7X_SC_CORES = 2
V7X_SC_SUBCORES = 16

LOG2E = math.log2(math.e)
INT_MIN = np.int32(-2 ** 31)
NEG_BIG = -1e30
SOFTMAX_MIN_MASS = 2.0 ** -60


def _cparams(*sem):
    return pltpu.CompilerParams(dimension_semantics=sem, vmem_limit_bytes=V7X_SCOPED_VMEM_BYTES)


def _resident(shape):
    nd = len(shape)
    return pl.BlockSpec(shape, lambda *_: (0,) * nd, pipeline_mode=pl.Buffered(1))


def _rms_bf16(x, g):
    y = x * lax.rsqrt(jnp.mean(x * x, axis=-1, keepdims=True) + EPS)
    return (y * g).astype(BF16)


def _dot(a, b):
    return jnp.dot(a, b, preferred_element_type=F32)


def _dot_nt(a, b):
    return lax.dot_general(a, b, (((1,), (1,)), ((), ())), preferred_element_type=F32)


def _ordered_key(x):
    b = lax.bitcast_convert_type(x, I32)
    return b ^ ((b >> 31) & np.int32(0x7FFFFFFF))


def _bucket(n, n_buckets):
    n = jnp.maximum(n, 0)
    me = n_buckets // 2
    nf = jnp.maximum(n, me).astype(F32)
    large = me + (jnp.log(nf / me) / math.log(MAX_DISTANCE / me) * (n_buckets - me)).astype(I32)
    large = jnp.minimum(large, n_buckets - 1)
    return jnp.where(n < me, n, large)


def _bias_of_bucket(bucket, relb_ref, h, n_buckets):
    acc = jnp.zeros(bucket.shape, F32)
    for bkt in range(n_buckets):
        acc = jnp.where(bucket == bkt, relb_ref[bkt, h], acc)
    return acc


def _fold_lanes(x):
    acc = x[:, 0:LANES]
    for t in range(1, x.shape[1] // LANES):
        acc = acc + x[:, t * LANES:(t + 1) * LANES]
    return acc


def _fold_rows(x, op=jnp.add):
    parts = [x[t * SUBLANES:(t + 1) * SUBLANES, :] for t in range(x.shape[0] // SUBLANES)]
    while len(parts) > 1:
        parts = [op(parts[t], parts[t + 1]) for t in range(0, len(parts) - 1, 2)] + parts[len(parts) & ~1:]
    return parts[0]


def _kth_threshold(count_ge, shape, axis, k, trips):
    def bit_body(t, carry):
        uthr, at_thr = carry
        cand = uthr | jnp.left_shift(np.int32(1), 31 - t)
        cnt = jnp.sum(count_ge(cand ^ INT_MIN), axis=axis, keepdims=True)
        return jnp.where(cnt >= k, cand, uthr), jnp.where(cnt >= k, jnp.broadcast_to(cnt, shape), at_thr)

    uthr, at_thr = lax.fori_loop(0, trips, bit_body, (jnp.zeros(shape, I32), jnp.full(shape, k, F32)))
    return jnp.maximum(uthr ^ INT_MIN, INT_MIN + 1), at_thr - k


def _flash_update(lg, v, m_ref, l_ref, acc_ref, h, hd):
    reps = lg.shape[1] // LANES
    m_prev = m_ref[h]
    m_new = jnp.maximum(m_prev, jnp.max(lg, axis=1, keepdims=True))
    p = jnp.exp2(lg - jnp.tile(m_new, (1, reps)))
    alpha = jnp.exp2(m_prev - m_new)
    pv = _dot(p.astype(BF16), jnp.concatenate([v, jnp.ones((v.shape[0], LANES), BF16)], axis=1))
    l_ref[h] = alpha * l_ref[h] + pv[:, hd:]
    m_ref[h] = m_new
    hs = slice(h * hd, (h + 1) * hd)
    acc_ref[:, hs] = acc_ref[:, hs] * jnp.tile(alpha, (1, hd // LANES)) + pv[:, :hd]


def _stage_rows(w_hbm, w_ref, row0):
    @pl.when(pl.program_id(0) == 0)
    def _():
        pltpu.sync_copy(w_hbm.at[pl.ds(row0, w_ref.shape[0])], w_ref)


def _proj_attn_kernel(x_ref, g_ref, w_hbm, q_ref, k_ref, v_ref, kb_ref, vb_ref, qi_ref,
                      sm_ref, smb_ref, kn_ref, w_ref, ws_ref, *, aw, iw, hd, tm, n_small, qscale):
    heads = aw // hd
    _stage_rows(w_hbm, w_ref, 0)

    @pl.when(pl.program_id(0) == 0)
    def _():
        ws_ref[...] = jnp.zeros(ws_ref.shape, BF16)
        pltpu.sync_copy(w_hbm.at[pl.ds(w_ref.shape[0], n_small)], ws_ref.at[pl.ds(0, n_small)])

    hb = _rms_bf16(x_ref[...], g_ref[...])
    q_ref[...] = (_dot_nt(hb, w_ref[0:aw, :]) * qscale).astype(BF16)
    for o_ref, ob_ref, c0 in ((k_ref, kb_ref, aw), (v_ref, vb_ref, 2 * aw)):
        kv = _dot_nt(hb, w_ref[c0:c0 + aw, :])
        ob_ref[...] = kv.astype(BF16)
        if o_ref is k_ref:
            kf = kv.astype(BF16).astype(F32)
            lane = lax.broadcasted_iota(I32, (tm, LANES), 1)
            kn = jnp.zeros((tm, LANES), F32)
            for h in range(heads):
                kh = kf[:, h * hd:(h + 1) * hd]
                kn = jnp.where(lane == h, jnp.sum(kh * kh, axis=1, keepdims=True), kn)
            kn_ref[...] = kn
        for h in range(heads):
            o_ref[pl.ds(h, tm, stride=heads), :] = kv[:, h * hd:(h + 1) * hd]
    qi_ref[...] = _dot_nt(hb, w_ref[3 * aw:3 * aw + iw, :]).astype(BF16)
    sm = _dot_nt(hb, ws_ref[...])
    sm_ref[...] = sm
    smb_ref[...] = sm.astype(BF16)


def _proj_attn(x2d, g, w_all, *, aw, iw, n_small, hd, tm):
    m, d = x2d.shape
    wt = 3 * aw + iw
    heads = aw // hd
    assert hd == LANES and n_small <= LANES and wt % 16 == 0 and n_small % 16 == 0
    row = lambda width: pl.BlockSpec((tm, width), lambda i: (i, 0))
    kv_spec = pl.BlockSpec((tm * heads, hd), lambda i: (i, 0))
    kv_shape = jax.ShapeDtypeStruct((m * heads, hd), F32)
    outs = [(aw, BF16), None, None, (aw, BF16), (aw, BF16), (iw, BF16), (LANES, F32), (LANES, BF16), (LANES, F32)]
    return pl.pallas_call(
        functools.partial(_proj_attn_kernel, aw=aw, iw=iw, hd=hd, tm=tm, n_small=n_small,
                          qscale=hd ** -0.5 * LOG2E),
        grid=(m // tm,),
        in_specs=[row(d), _resident((1, d)), pl.BlockSpec(memory_space=pl.ANY)],
        out_specs=[kv_spec if o is None else row(o[0]) for o in outs],
        out_shape=[kv_shape if o is None else jax.ShapeDtypeStruct((m, o[0]), o[1]) for o in outs],
        scratch_shapes=[pltpu.VMEM((wt, d), BF16), pltpu.VMEM((LANES, d), BF16)],
        compiler_params=_cparams("arbitrary"),
        name="proj_attn",
    )(x2d, g, w_all)


def _proj_conv_kernel(*refs, cw, tm, seq, tiles_per_seq, tail, row0, n_cast):
    if tiles_per_seq:
        x_ref, g_ref, w_hbm, cwt_ref = refs[:4]
        c_ref, ut_ref = refs[4 + n_cast:6 + n_cast]
        w_ref, carry_ref = refs[6 + 2 * n_cast:]
        _cast_slabs(refs[4:4 + n_cast], refs[6 + n_cast:6 + 2 * n_cast])
    else:
        x_ref, g_ref, w_hbm, cwt_ref, p1_ref, p2_ref, c_ref, ut_ref, w_ref = refs
    _stage_rows(w_hbm, w_ref, row0)
    hb = _rms_bf16(x_ref[...], g_ref[...])
    cx = _dot_nt(hb, w_ref[0:cw, :])
    cb = _dot_nt(hb, w_ref[cw:2 * cw, :])
    cc = _dot_nt(hb, w_ref[2 * cw:3 * cw, :])
    u = cc * cx
    r = lax.broadcasted_iota(I32, (tm, cw), 0)
    um1 = pltpu.roll(u, 1, 0)
    um2 = pltpu.roll(u, 2, 0)
    if tiles_per_seq:
        @pl.when(pl.program_id(0) % tiles_per_seq == 0)
        def _():
            carry_ref[...] = jnp.zeros_like(carry_ref)
        prev1 = jnp.broadcast_to(carry_ref[SUBLANES - 1:SUBLANES, :], (tm, cw))
        prev2 = jnp.broadcast_to(carry_ref[SUBLANES - 2:SUBLANES - 1, :], (tm, cw))
        um1 = jnp.where(r == 0, prev1, um1)
        um2 = jnp.where(r == 0, prev2, jnp.where(r == 1, prev1, um2))
        carry_ref[...] = u[tm - SUBLANES:tm, :]
    else:
        assert seq & (seq - 1) == 0
        t = r & (seq - 1)
        um1 = jnp.where(t >= 1, um1, p1_ref[...])
        um2 = jnp.where(t >= 2, um2, p2_ref[...])
    y = cwt_ref[0:1, :] * um2 + cwt_ref[1:2, :] * um1 + cwt_ref[2:3, :] * u
    c_ref[...] = (cb * y).astype(BF16)
    ut_ref[...] = u[tm - tail:tm, :]


def _proj_conv(x2d, g, w_all, row0, conv_w, prev, to_cast=(), *, seq, tm):
    m, d = x2d.shape
    cw = conv_w.shape[1]
    assert conv_w.shape[0] == 3 and row0 % 16 == 0
    row = lambda width: pl.BlockSpec((tm, width), lambda i: (i, 0))
    in_specs = [row(d), _resident((1, d)), pl.BlockSpec(memory_space=pl.ANY), _resident((3, cw))]
    args = [x2d, g, w_all, conv_w]
    scratch = [pltpu.VMEM((3 * cw, d), BF16)]
    if prev is None:
        assert seq % tm == 0
        tiles_per_seq, tail = seq // tm, SUBLANES
        scratch += [pltpu.VMEM((SUBLANES, cw), F32)]
    else:
        assert tm % seq == 0 and seq >= 2
        tiles_per_seq, tail = 0, tm
        in_specs += [row(cw), row(cw)]
        args += list(prev)
    assert not (to_cast and prev is not None)
    c_in, c_out, c_shapes = _cast_specs(to_cast, m // tm, lambda i: i)
    return pl.pallas_call(
        functools.partial(_proj_conv_kernel, cw=cw, tm=tm, seq=seq, tiles_per_seq=tiles_per_seq, tail=tail,
                          row0=row0, n_cast=len(to_cast)),
        grid=(m // tm,),
        in_specs=in_specs + c_in,
        out_specs=[row(cw), pl.BlockSpec((tail, cw), lambda i: (i, 0))] + c_out,
        out_shape=[jax.ShapeDtypeStruct((m, cw), BF16), jax.ShapeDtypeStruct((m // tm * tail, cw), F32)] + c_shapes,
        scratch_shapes=scratch,
        compiler_params=_cparams("arbitrary"),
        name="proj_conv",
    )(*args, *to_cast)


def _bias_tiles_kernel(relb_ref, o_ref, *, tb, n_buckets):
    kind = pl.program_id(0)
    h = pl.program_id(1)
    qry = lax.broadcasted_iota(I32, (tb, tb), 0)
    key = lax.broadcasted_iota(I32, (tb, tb), 1)
    bias = _bias_of_bucket(_bucket(kind * tb + qry - key, n_buckets), relb_ref, h, n_buckets)
    o_ref[...] = (bias - relb_ref[n_buckets - 1, h]) * LOG2E


def _bias_tiles(rel_bias, *, tb):
    n_buckets, heads = rel_bias.shape
    return pl.pallas_call(
        functools.partial(_bias_tiles_kernel, tb=tb, n_buckets=n_buckets),
        grid=(2, heads),
        in_specs=[pl.BlockSpec(memory_space=pltpu.SMEM)],
        out_specs=pl.BlockSpec((None, None, tb, tb), lambda a, h: (a, h, 0, 0)),
        out_shape=jax.ShapeDtypeStruct((2, heads, tb, tb), F32),
        compiler_params=_cparams("arbitrary", "arbitrary"),
        name="bias_tiles",
    )(rel_bias)


def _cast_specs(weights, n_steps, step_of):
    in_specs, out_specs, out_shapes = [], [], []
    for w in weights:
        rows, cols = w.shape
        assert rows % n_steps == 0 and (rows // n_steps) % 16 == 0
        spec = pl.BlockSpec((rows // n_steps, cols), lambda *idx: (step_of(*idx), 0))
        in_specs.append(spec)
        out_specs.append(spec)
        out_shapes.append(jax.ShapeDtypeStruct((rows, cols), BF16))
    return in_specs, out_specs, out_shapes


def _cast_slabs(src_refs, dst_refs):
    for src, dst in zip(src_refs, dst_refs):
        dst[...] = src[...].astype(BF16)


def _attn_prompt_kernel(*refs, n_cast, tb, heads, hd, iheads, idim, topk):
    relb_ref, qi_ref, sm_ref, smb_ref, q_ref, kb_ref, vb_ref, kn_ref, bias_ref = refs[:9]
    o_ref = refs[9 + n_cast]
    skey_ref, thr_ref, wt_ref, madd_ref, kmax_ref, shift_ref, m_ref, l_ref, acc_ref = refs[10 + 2 * n_cast:]
    _cast_slabs(refs[9:9 + n_cast], refs[10 + n_cast:10 + 2 * n_cast])
    i = pl.program_id(1)
    n_buckets = relb_ref.shape[0]
    wscale = idim ** -0.5 * iheads ** -0.5

    l_ref[...] = jnp.zeros(l_ref.shape, F32)
    acc_ref[...] = jnp.zeros(acc_ref.shape, F32)
    wt_ref[...] = sm_ref[...].T * wscale

    def chunk(j):
        return pl.ds(pl.multiple_of(j * tb, tb), tb)

    def score_chunk(j, diag):
        kic = smb_ref[chunk(j), 0:idim]
        acc = jnp.zeros((tb, tb), F32)
        for h in range(iheads):
            s = _dot_nt(kic, qi_ref[:, h * idim:(h + 1) * idim])
            acc = acc + jnp.maximum(s, 0.0) * wt_ref[idim + h:idim + h + 1, :]
        key = _ordered_key(acc)
        if diag:
            kpos = lax.broadcasted_iota(I32, (tb, tb), 0)
            qpos = lax.broadcasted_iota(I32, (tb, tb), 1)
            key = jnp.where(kpos > qpos, INT_MIN, key)
        skey_ref[chunk(j), :] = key

    def score_body(j, carry):
        score_chunk(j, False)
        return carry

    lax.fori_loop(0, i, score_body, 0)
    score_chunk(i, True)

    def count_ge(scand):
        def body(j, cnt):
            return cnt + _fold_rows(jnp.where(skey_ref[chunk(j), :] >= scand[0:1, :], 1.0, 0.0))

        return lax.fori_loop(0, i + 1, body, jnp.zeros((SUBLANES, tb), F32))

    trips = jnp.where((i + 1) * tb <= topk, 0, 32)
    thr, surplus = _kth_threshold(count_ge, (SUBLANES, tb), 0, float(topk), trips)
    thr_ref[...] = thr

    @pl.when(jnp.max(surplus) > 0.0)
    def _():
        thr_row = thr_ref[0:1, :]

        def gt_body(j, cnt):
            return cnt + _fold_rows(jnp.where(skey_ref[chunk(j), :] > thr_row, 1.0, 0.0))

        above = lax.fori_loop(0, i + 1, gt_body, jnp.zeros((SUBLANES, tb), F32))
        budget = topk - jnp.sum(above, axis=0, keepdims=True)
        kpos = lax.broadcasted_iota(I32, (tb, tb), 0)
        earlier = jnp.where(kpos > lax.broadcasted_iota(I32, (tb, tb), 1), 1.0, 0.0).astype(BF16)

        def fix_body(j, seen):
            sk = skey_ref[chunk(j), :]
            tie = jnp.where(sk == thr_row, 1.0, 0.0)
            rank = seen[0:1, :] + _dot(earlier, tie.astype(BF16))
            skey_ref[chunk(j), :] = jnp.where((tie > 0.0) & (rank >= budget), thr_row - 1, sk)
            return seen + jnp.sum(_fold_rows(tie), axis=0, keepdims=True)

        lax.fori_loop(0, i + 1, fix_body, jnp.zeros((SUBLANES, tb), F32))

    @pl.when(i == 0)
    def _():
        kmax_ref[...] = jnp.zeros(kmax_ref.shape, F32)

    kmax_ref[...] = jnp.maximum(kmax_ref[...], jnp.max(kn_ref[chunk(i), :], axis=0, keepdims=True))
    qf = q_ref[...].astype(F32)
    for h in range(heads):
        hs = slice(h * hd, (h + 1) * hd)
        bmax = relb_ref[0, h]
        for bkt in range(1, n_buckets):
            bmax = jnp.maximum(bmax, relb_ref[bkt, h])
        bmax = (bmax - relb_ref[n_buckets - 1, h]) * LOG2E
        qn2 = jnp.sum(qf[:, hs] * qf[:, hs], axis=1, keepdims=True)
        shift_ref[h] = jnp.broadcast_to(jnp.sqrt(qn2 * kmax_ref[0:1, h:h + 1]) + bmax, (tb, LANES))

    def logits(j, h, kind):
        hs = slice(h * hd, (h + 1) * hd)
        lg = _dot_nt(q_ref[:, hs], kb_ref[chunk(j), hs]) + madd_ref[...]
        return lg if kind is None else lg + bias_ref[kind, h]

    def bounded_chunk(j, kind):
        for h in range(heads):
            hs = slice(h * hd, (h + 1) * hd)
            p = jnp.exp2(logits(j, h, kind) - jnp.tile(shift_ref[h], (1, tb // LANES)))
            v1 = jnp.concatenate([vb_ref[chunk(j), hs], jnp.ones((tb, LANES), BF16)], axis=1)
            pv = _dot(p.astype(BF16), v1)
            l_ref[h] += pv[:, hd:]
            acc_ref[:, hs] += pv[:, :hd]

    def running_max_chunk(j, kind):
        for h in range(heads):
            _flash_update(logits(j, h, kind), vb_ref[chunk(j), h * hd:(h + 1) * hd], m_ref, l_ref, acc_ref, h, hd)

    def attend(update):
        def one(j, kind):
            madd_ref[...] = jnp.where(skey_ref[chunk(j), :] >= thr_ref[0:1, :], 0.0, NEG_BIG).T
            update(j, kind)

        def far_body(j, carry):
            one(j, None)
            return carry

        lax.fori_loop(0, jnp.maximum(i - 1, 0), far_body, 0)

        @pl.when(i >= 1)
        def _():
            one(i - 1, 1)

        one(i, 0)

    attend(bounded_chunk)

    @pl.when(jnp.min(l_ref[...]) < SOFTMAX_MIN_MASS)
    def _():
        m_ref[...] = jnp.full(m_ref.shape, NEG_BIG, F32)
        l_ref[...] = jnp.zeros(l_ref.shape, F32)
        acc_ref[...] = jnp.zeros(acc_ref.shape, F32)
        attend(running_max_chunk)

    for h in range(heads):
        hs = slice(h * hd, (h + 1) * hd)
        o_ref[:, hs] = (acc_ref[:, hs] / jnp.tile(l_ref[h], (1, hd // LANES))).astype(BF16)


def _attn_prompt(rel_bias, bias_tiles, qi, sm, smb, q, kb, vb, kn, to_cast, *, batch, seq, tb, heads, hd, iheads,
                 idim, topk):
    m, aw = q.shape
    iw = qi.shape[1]
    nq = seq // tb
    assert tb >= MAX_DISTANCE and tb % LANES == 0 and hd % LANES == 0
    qrow = lambda width: pl.BlockSpec((tb, width), lambda b, i: (b * nq + i, 0))
    seqblk = lambda width: pl.BlockSpec((seq, width), lambda b, i: (b, 0))
    c_in, c_out, c_shapes = _cast_specs(to_cast, batch * nq, lambda b, i: b * nq + i)
    return pl.pallas_call(
        functools.partial(_attn_prompt_kernel, n_cast=len(to_cast), tb=tb, heads=heads, hd=hd, iheads=iheads,
                          idim=idim, topk=topk),
        grid=(batch, nq),
        in_specs=[pl.BlockSpec(memory_space=pltpu.SMEM), qrow(iw), qrow(LANES), seqblk(LANES), qrow(aw), seqblk(aw),
                  seqblk(aw), seqblk(LANES), _resident(bias_tiles.shape)] + c_in,
        out_specs=[qrow(aw)] + c_out,
        out_shape=[jax.ShapeDtypeStruct((m, aw), BF16)] + c_shapes,
        scratch_shapes=[pltpu.VMEM((seq, tb), I32), pltpu.VMEM((SUBLANES, tb), I32),
                        pltpu.VMEM((LANES, tb), F32), pltpu.VMEM((tb, tb), F32),
                        pltpu.VMEM((SUBLANES, LANES), F32), pltpu.VMEM((heads, tb, LANES), F32),
                        pltpu.VMEM((heads, tb, LANES), F32), pltpu.VMEM((heads, tb, LANES), F32),
                        pltpu.VMEM((tb, aw), F32)],
        compiler_params=_cparams("arbitrary", "arbitrary"),
        name="attn_prompt",
    )(rel_bias, qi, sm, smb, q, kb, vb, kn, bias_tiles, *to_cast)


def _sample_select_kernel(pt_ref, qi_ref, w_ref, kin_ref, *rest, pages, ps, nc, iheads, idim, tq, rq, topk):
    page_refs = rest[:pages]
    past_ref, new_ref, thr_ref, row_ref = rest[pages:]
    b = pl.program_id(0)
    c = pl.program_id(1)
    db = pl.num_programs(0)
    ch = pages * ps
    wscale = idim ** -0.5 * iheads ** -0.5
    rows_b = pl.ds(pl.multiple_of(b * rq, rq), rq)

    def score(keys_t):
        s = _dot(qi_ref[...], keys_t)
        t = jnp.maximum(s, 0.0) * (w_ref[:, 0:1] * wscale)
        acc = t[0:rq]
        for h in range(1, iheads):
            acc = acc + t[h * rq:(h + 1) * rq]
        return _ordered_key(acc)

    @pl.when(c < nc)
    def _():
        for p in range(pages):
            key = score(page_refs[p][...].astype(BF16))
            row_ref[rows_b, pl.ds(pl.multiple_of(c * ch + p * ps, LANES), ps)] = key
            for t in range(ps // LANES):
                for j in range(tq):
                    past_ref[j, p * (ps // LANES) + t:p * (ps // LANES) + t + 1, :] = \
                        key[j:j + 1, t * LANES:(t + 1) * LANES]

    @pl.when(c == nc)
    def _():
        key = score(kin_ref[...])
        j = lax.broadcasted_iota(I32, (rq, ps), 0)
        n = lax.broadcasted_iota(I32, (rq, ps), 1)
        key = jnp.where((n <= j) & (n < tq), key, INT_MIN)
        row_ref[rows_b, nc * ch:nc * ch + ps] = key

    @pl.when((c == nc) & (b == db - 1))
    def _():
        n_rows = row_ref.shape[0]

        def count_ge(scand):
            sk = row_ref[...]
            return _fold_lanes(jnp.where(sk >= jnp.tile(scand, (1, sk.shape[1] // LANES)), 1.0, 0.0))

        thr, _ = _kth_threshold(count_ge, (n_rows, LANES), 1, float(topk), 32)
        count = lambda m: jnp.sum(_fold_lanes(jnp.where(m, 1.0, 0.0)), axis=1, keepdims=True)
        past_keys = row_ref[:, 0:nc * ch]
        new_keys = row_ref[:, nc * ch:nc * ch + ps]
        thr_new = jnp.tile(thr, (1, ps // LANES))
        gt_new = count(new_keys > thr_new)
        ties_allowed = topk - gt_new - count(past_keys > jnp.tile(thr, (1, nc * ch // LANES)))
        ties_past = count(past_keys == jnp.tile(thr, (1, nc * ch // LANES)))
        ties_new_allowed = ties_allowed - jnp.minimum(ties_past, ties_allowed)
        tie = jnp.where(new_keys == thr_new, 1.0, 0.0)
        lane = lax.broadcasted_iota(I32, tie.shape, 1)
        rank = jnp.zeros(tie.shape, F32)
        for s in range(1, tq):
            rank = rank + jnp.where(lane >= s, pltpu.roll(tie, s, 1), 0.0)
        new_keys = jnp.where((tie > 0.0) & (rank >= ties_new_allowed), thr_new - 1, new_keys)
        budget = jnp.broadcast_to(topk - gt_new, (n_rows, LANES)).astype(I32)
        thr_ref[...] = jnp.zeros(thr_ref.shape, I32)
        for bb in range(n_rows // rq):
            new_ref[bb] = new_keys[bb * rq:(bb + 1) * rq]
            for jj in range(tq):
                thr_ref[bb * tq + jj, 0:1, :] = thr[bb * rq + jj:bb * rq + jj + 1, :]
                thr_ref[bb * tq + jj, 1:2, :] = budget[bb * rq + jj:bb * rq + jj + 1, :]


def _sample_select(page_table, qi_hm, w_hm, kin_t, cache_kidx_t, *, pages, tq, rq, iheads, topk):
    db, n_pages = page_table.shape
    _, idim, ps = cache_kidx_t.shape
    nc = n_pages // pages
    ch = pages * ps
    assert ps % LANES == 0 and (ch // LANES) % SUBLANES == 0
    page_spec = lambda p: pl.BlockSpec(
        (None, idim, ps), lambda b, c, pt: (pt[b, jnp.minimum(c, nc - 1) * pages + p], 0, 0))
    per_b = lambda shape: pl.BlockSpec((None,) + shape, lambda b, c, pt: (b, 0, 0))
    return pl.pallas_call(
        functools.partial(_sample_select_kernel, pages=pages, ps=ps, nc=nc, iheads=iheads, idim=idim, tq=tq, rq=rq,
                          topk=topk),
        grid_spec=pltpu.PrefetchScalarGridSpec(
            num_scalar_prefetch=1,
            grid=(db, nc + 1),
            in_specs=[per_b((iheads * rq, idim)), per_b((iheads * rq, LANES)), per_b((idim, ps))]
            + [page_spec(p) for p in range(pages)],
            out_specs=[pl.BlockSpec((tq, ch // LANES, LANES), lambda b, c, pt: (b, jnp.minimum(c, nc - 1), 0)),
                       pl.BlockSpec((db, rq, ps), lambda b, c, pt: (0, 0, 0)),
                       pl.BlockSpec((db * tq, SUBLANES, LANES), lambda b, c, pt: (0, 0, 0))],
            scratch_shapes=[pltpu.VMEM((db * rq, nc * ch + ps), I32)],
        ),
        out_shape=[jax.ShapeDtypeStruct((db * tq, n_pages * ps // LANES, LANES), I32),
                   jax.ShapeDtypeStruct((db, rq, ps), I32),
                   jax.ShapeDtypeStruct((db * tq, SUBLANES, LANES), I32)],
        compiler_params=_cparams("arbitrary", "arbitrary"),
        name="sample_select",
    )(page_table, qi_hm, w_hm, kin_t, *([cache_kidx_t] * pages))


def _sc_gather_kernel(past_hbm, thr_hbm, pt_hbm, ck_hbm, cv_hbm, ksel_hbm, vsel_hbm, pos_hbm, cnt_hbm,
                      row_v, thr_v, pt_v, idx_v, phys_v, rows_v, cnt_v, sem,
                      *, nq, tq, topk, ps, n_cores, rows_per_copy):
    wid = lax.axis_index("s") * n_cores + lax.axis_index("c")

    @pl.when(wid < nq)
    def _():
        pltpu.sync_copy(past_hbm.at[wid], row_v)
        pltpu.sync_copy(thr_hbm.at[wid], thr_v)
        pltpu.sync_copy(pt_hbm.at[wid // tq], pt_v)
        thr = thr_v[0, pl.ds(0, SC_LANES)]
        budget = thr_v[1, pl.ds(0, SC_LANES)]
        lane = lax.iota(I32, SC_LANES)
        zero = jnp.zeros((SC_LANES,), I32)
        for t in range(idx_v.shape[0] // SC_LANES):
            idx_v[pl.ds(t * SC_LANES, SC_LANES)] = zero

        def compact(pred):
            def body(r, cnt):
                for t in range(LANES // SC_LANES):
                    x = row_v[r, pl.ds(t * SC_LANES, SC_LANES)]
                    m = pred(x, cnt)
                    rank = plsc.cumsum(jnp.where(m, 1, 0).astype(I32))
                    plsc.store_scatter(idx_v, [cnt + rank - 1], lane + (r * LANES + t * SC_LANES), mask=m)
                    cnt = cnt + plsc.all_reduce_population_count(m)
                return cnt
            return body

        cnt = lax.fori_loop(0, row_v.shape[0], compact(lambda x, cnt: x > thr), zero)
        cnt = lax.fori_loop(0, row_v.shape[0], compact(lambda x, cnt: (x == thr) & (cnt < budget)), cnt)
        cnt_v[...] = jnp.minimum(cnt, budget)
        pltpu.sync_copy(cnt_v, cnt_hbm.at[wid])
        pltpu.sync_copy(idx_v.at[pl.ds(0, topk)], pos_hbm.at[wid])

        shift = ps.bit_length() - 1
        for t in range(topk // SC_LANES):
            pos = idx_v[pl.ds(t * SC_LANES, SC_LANES)]
            page = plsc.load_gather(pt_v, [lax.shift_right_logical(pos, shift)])
            phys_v[pl.ds(t * SC_LANES, SC_LANES)] = page * ps + (pos & (ps - 1))
        for g in range(topk // rows_per_copy):
            sel = phys_v.at[pl.ds(g * rows_per_copy, rows_per_copy)]
            dst = pl.ds(wid * topk + g * rows_per_copy, rows_per_copy)
            for src_hbm, dst_hbm in ((ck_hbm, ksel_hbm), (cv_hbm, vsel_hbm)):
                pltpu.async_copy(src_hbm.at[sel], rows_v, sem).wait()
                pltpu.sync_copy(rows_v, dst_hbm.at[dst])


def _sc_gather(past_keys, thr, page_table, cache_k, cache_v, *, tq, topk, ps):
    nq, key_rows, _ = past_keys.shape
    _, heads, hd = cache_k.shape
    assert ps & (ps - 1) == 0 and topk % SC_LANES == 0
    rows_per_copy = 64
    assert topk % rows_per_copy == 0
    mesh = plsc.VectorSubcoreMesh(core_axis_name="c", subcore_axis_name="s", num_cores=V7X_SC_CORES,
                                  num_subcores=V7X_SC_SUBCORES)
    assert nq <= V7X_SC_CORES * V7X_SC_SUBCORES
    sel_shape = jax.ShapeDtypeStruct((nq * topk, heads, hd), F32)
    return pl.kernel(
        functools.partial(_sc_gather_kernel, nq=nq, tq=tq, topk=topk, ps=ps, n_cores=V7X_SC_CORES,
                          rows_per_copy=rows_per_copy),
        out_type=[sel_shape, sel_shape, jax.ShapeDtypeStruct((nq, topk), I32),
                  jax.ShapeDtypeStruct((nq, SC_LANES), I32)],
        mesh=mesh,
        scratch_types=[pltpu.VMEM((key_rows, LANES), I32), pltpu.VMEM((SUBLANES, LANES), I32),
                       pltpu.VMEM((page_table.shape[1],), I32), pltpu.VMEM((topk + SC_LANES,), I32),
                       pltpu.VMEM((topk,), I32), pltpu.VMEM((rows_per_copy, heads, hd), F32),
                       pltpu.VMEM((SC_LANES,), I32), pltpu.SemaphoreType.DMA],
        compiler_params=pltpu.CompilerParams(needs_layout_passes=False),
        name="sc_select_gather",
    )(past_keys, thr, page_table, cache_k, cache_v)


def _sample_attn_sel_kernel(cnt_ref, relbt_ref, q_ref, ksel_ref, vsel_ref, pos_ref, snew_ref, thr_ref, kn_ref, vn_ref,
                            o_ref, *, heads, tq, past, n_buckets):
    w = pl.program_id(0)
    j = w % tq
    q = q_ref[...]

    def head_bias(dist):
        bucket = _bucket(dist, n_buckets)
        acc = jnp.zeros(bucket.shape, F32)
        for bkt in range(n_buckets):
            acc = jnp.where(bucket == bkt, relbt_ref[:, bkt:bkt + 1], acc)
        return acc * LOG2E

    def logits(keys):
        lg = _dot_nt(q, keys)
        head = lax.broadcasted_iota(I32, lg.shape, 0)
        col = lax.broadcasted_iota(I32, lg.shape, 1)
        return lg, (col & (heads - 1)) == head, col

    lg, own, col = logits(ksel_ref[...].astype(BF16))
    keep = own & (col < cnt_ref[w, 0] * heads)
    lg = jnp.where(keep, lg + head_bias(jnp.broadcast_to(past + j - pos_ref[...], lg.shape)), NEG_BIG)
    lgn, own, col = logits(kn_ref[...])
    keep = own & (snew_ref[...] >= thr_ref[0:1, :])
    lgn = jnp.where(keep, lgn + head_bias(j - lax.shift_right_logical(col, heads.bit_length() - 1)), NEG_BIG)

    m = jnp.maximum(jnp.max(lg, axis=1, keepdims=True), jnp.max(lgn, axis=1, keepdims=True))
    p = jnp.exp2(lg - m)
    pn = jnp.exp2(lgn - m)
    denom = jnp.sum(p, axis=1, keepdims=True) + jnp.sum(pn, axis=1, keepdims=True)
    acc = _dot(p.astype(BF16), vsel_ref[...].astype(BF16)) + _dot(pn.astype(BF16), vn_ref[...])
    o_ref[...] = acc / denom


def _sample_attn_sel(cnt, rel_bias_t, q, ksel, vsel, pos, snew, thr, kn, vn, *, heads, hd, topk, tq, past):
    nq = q.shape[0]
    ps = kn.shape[1]
    assert ps == LANES and hd == LANES and heads & (heads - 1) == 0
    per_q = lambda shape: pl.BlockSpec((None,) + shape, lambda w, cnt: (w, 0, 0))
    per_b = lambda shape: pl.BlockSpec((None,) + shape, lambda w, cnt: (w // tq, 0, 0))
    sel_spec = pl.BlockSpec((topk * heads, hd), lambda w, cnt: (w, 0))
    return pl.pallas_call(
        functools.partial(_sample_attn_sel_kernel, heads=heads, tq=tq, past=past, n_buckets=rel_bias_t.shape[1]),
        grid_spec=pltpu.PrefetchScalarGridSpec(
            num_scalar_prefetch=1,
            grid=(nq,),
            in_specs=[pl.BlockSpec(rel_bias_t.shape, lambda w, cnt: (0, 0)), per_q((heads, hd)), sel_spec, sel_spec,
                      per_q((1, topk * heads)), per_q((1, ps)), per_q((SUBLANES, LANES)), per_b((ps, hd)),
                      per_b((ps, hd))],
            out_specs=per_q((heads, hd)),
        ),
        out_shape=jax.ShapeDtypeStruct((nq, heads, hd), F32),
        compiler_params=_cparams("arbitrary"),
        name="sample_attn_sel",
    )(cnt, rel_bias_t, q, ksel, vsel, pos, snew, thr, kn, vn)


def _mix_kernel(x_ref, xs_ref, g_ref, a_ref, as_ref, c_ref, cs_ref, w_hbm, wpa_ref, wpb_ref, wo_ref, o_ref, os_ref,
                wg_ref, *, d, row0):
    _stage_rows(w_hbm, wg_ref, row0)
    n = pl.num_programs(0) - 1

    def rows(x_ref, a_ref, c_ref, o_ref):
        x = x_ref[...]
        hb = _rms_bf16(x, g_ref[...])
        a = _dot(a_ref[...], wpa_ref[...])
        m = jax.nn.sigmoid(_dot_nt(hb, wg_ref[0:d, :])) * a
        c = _dot(c_ref[...], wpb_ref[...])
        m = m + jax.nn.sigmoid(_dot_nt(hb, wg_ref[d:2 * d, :])) * c
        o_ref[...] = x + _dot(m.astype(BF16), wo_ref[...])

    @pl.when(pl.program_id(0) < n)
    def _():
        rows(x_ref, a_ref, c_ref, o_ref)

    @pl.when(pl.program_id(0) == n)
    def _():
        rows(xs_ref, as_ref, cs_ref, os_ref)


def _mix(xp, xs, g, attn_p, attn_s, c_p, c_s, w_all, row0, w_pa, w_pb, w_o, *, tm):
    m, d = xp.shape
    ms = xs.shape[0]
    n = m // tm
    assert row0 % 16 == 0
    prow = lambda width: pl.BlockSpec((tm, width), lambda i: (jnp.minimum(i, n - 1), 0))
    srow = lambda width: pl.BlockSpec((ms, width), lambda i: (0, 0))
    return pl.pallas_call(
        functools.partial(_mix_kernel, d=d, row0=row0),
        grid=(n + 1,),
        in_specs=[prow(d), srow(d), _resident((1, d)), prow(attn_p.shape[1]), srow(attn_s.shape[1]),
                  prow(c_p.shape[1]), srow(c_s.shape[1]), pl.BlockSpec(memory_space=pl.ANY), _resident(w_pa.shape),
                  _resident(w_pb.shape), _resident(w_o.shape)],
        out_specs=[prow(d), srow(d)],
        out_shape=[jax.ShapeDtypeStruct((m, d), F32), jax.ShapeDtypeStruct((ms, d), F32)],
        scratch_shapes=[pltpu.VMEM((2 * d, d), BF16)],
        compiler_params=_cparams("arbitrary"),
        name="mix_out",
    )(xp, xs, g, attn_p, attn_s, c_p, c_s, w_all, w_pa, w_pb, w_o)


def _mlp_kernel(x_ref, xs_ref, g_ref, gf_ref, w1_ref, w2_ref, y_ref, ys_ref, h_ref, *, tm, final):
    f = pl.program_id(1)
    last_tile = pl.program_id(0) == pl.num_programs(0) - 1

    def start(x_ref, y_ref, h_rows):
        h_ref[h_rows, :] = _rms_bf16(x_ref[...], g_ref[...])
        y_ref[...] = jnp.zeros(y_ref.shape, F32)

    def finish(x_ref, y_ref):
        x2 = x_ref[...] + y_ref[...]
        if final:
            x2 = x2 * lax.rsqrt(jnp.mean(x2 * x2, axis=-1, keepdims=True) + EPS) * gf_ref[...]
        y_ref[...] = x2

    def ff(h):
        t = jnp.square(jnp.maximum(_dot(h, w1_ref[...]), 0.0))
        return _dot(t.astype(BF16), w2_ref[...])

    @pl.when(f == 0)
    def _():
        start(x_ref, y_ref, slice(0, tm))

    @pl.when((f == 0) & last_tile)
    def _():
        start(xs_ref, ys_ref, slice(tm, h_ref.shape[0]))

    @pl.when(jnp.logical_not(last_tile))
    def _():
        y_ref[...] += ff(h_ref[0:tm, :])

    @pl.when(last_tile)
    def _():
        r = ff(h_ref[...])
        y_ref[...] += r[0:tm]
        ys_ref[...] += r[tm:]

    @pl.when(f == pl.num_programs(1) - 1)
    def _():
        finish(x_ref, y_ref)

    @pl.when((f == pl.num_programs(1) - 1) & last_tile)
    def _():
        finish(xs_ref, ys_ref)


def _mlp(xp, xs, g, gf, w1, w2, *, tm, tf, final):
    m, d = xp.shape
    ms = xs.shape[0]
    ff = w1.shape[1]
    srow = pl.BlockSpec((ms, d), lambda i, f: (0, 0))
    return pl.pallas_call(
        functools.partial(_mlp_kernel, tm=tm, final=final),
        grid=(m // tm, ff // tf),
        in_specs=[pl.BlockSpec((tm, d), lambda i, f: (i, 0)), srow, _resident((1, d)), _resident((1, d)),
                  pl.BlockSpec((d, tf), lambda i, f: (0, f)), pl.BlockSpec((tf, d), lambda i, f: (f, 0))],
        out_specs=[pl.BlockSpec((tm, d), lambda i, f: (i, 0)), srow],
        out_shape=[jax.ShapeDtypeStruct((m, d), F32), jax.ShapeDtypeStruct((ms, d), F32)],
        scratch_shapes=[pltpu.VMEM((tm + ms, d), BF16)],
        compiler_params=_cparams("arbitrary", "arbitrary"),
        name="mlp",
    )(xp, xs, g, gf, w1, w2)


def _tile(m, cap):
    return min(m, cap)


def kernel(x_prompt, x_sample, cache_k, cache_v, cache_kidx, state_conv, page_table, rel_bias, norm_mix_g, w_in,
           conv_w, w_pa, w_pb, w_o, norm_mlp_g, w_mlp_in, w_mlp_out, norm_final_g):
    batch, seq, d = x_prompt.shape
    db, tq, _ = x_sample.shape
    depth, n_pool, ps, heads, hd = cache_k.shape
    idim = cache_kidx.shape[-1]
    cw = conv_w.shape[-1]
    aw = heads * hd
    n_in = w_in.shape[-1]
    iheads = (n_in - 3 * aw - idim - 3 * cw - 2 * d) // (idim + 1)
    iw = iheads * idim
    assert 3 * aw + iw + idim + iheads + 3 * cw + 2 * d == n_in and idim + iheads <= LANES
    n_pages = page_table.shape[1]
    past = n_pages * ps
    rq = SUBLANES
    assert tq <= rq

    mp, ms = batch * seq, db * tq
    xp = x_prompt.reshape(mp, d)
    xs = x_sample.reshape(ms, d)
    tb = _tile(seq, 256)
    pages = math.gcd(n_pages, 8)
    o_small = 3 * aw + iw
    o_conv = o_small + idim + iheads
    o_gate = o_conv + 3 * cw
    gf = norm_final_g.reshape(1, d)
    bias_tiles = _bias_tiles(rel_bias, tb=tb)

    outs = {k: [] for k in ("kp", "vp", "kip", "sp", "ks", "vs", "kis", "ss")}
    for l in range(depth):
        wl = jnp.swapaxes(w_in[l], 0, 1).astype(BF16)
        g_mix = norm_mix_g[l].reshape(1, d)
        g_mlp = norm_mlp_g[l].reshape(1, d)

        tm = _tile(seq, 512)
        q, k, v, kb, vb, qi, sm, smb, kn = _proj_attn(xp, g_mix, wl, aw=aw, iw=iw, n_small=idim + iheads, hd=hd, tm=tm)
        c_in, u_tail, wpa, wpb, wo = _proj_conv(xp, g_mix, wl, o_conv, conv_w[l], None, (w_pa[l], w_pb[l], w_o[l]),
                                                seq=seq, tm=_tile(seq, 512))
        attn, w1, w2 = _attn_prompt(rel_bias, bias_tiles, qi, sm, smb, q, kb, vb, kn, (w_mlp_in[l], w_mlp_out[l]),
                                    batch=batch, seq=seq, tb=tb, heads=heads, hd=hd, iheads=iheads, idim=idim,
                                    topk=min(TOPK_MAX, seq // 4))
        attn_p, c_in_p = attn, c_in
        outs["kp"].append(k.reshape(batch, seq, heads, hd))
        outs["vp"].append(v.reshape(batch, seq, heads, hd))
        outs["kip"].append(sm[:, :idim].reshape(batch, seq, idim))
        outs["sp"].append(u_tail.reshape(batch, -1, SUBLANES, cw)[:, -1, SUBLANES - 2:])

        q, k, v, kb, vb, qi, sm, smb, _ = _proj_attn(xs, g_mix, wl, aw=aw, iw=iw, n_small=idim + iheads, hd=hd, tm=ms)
        st = state_conv[l]
        zero = jnp.zeros((db, tq - 1, cw), F32)
        prev1 = jnp.concatenate([st[:, 1:2], zero], axis=1).reshape(ms, cw)
        prev2 = jnp.concatenate([st, zero[:, 1:]], axis=1).reshape(ms, cw)
        c_in, u_all = _proj_conv(xs, g_mix, wl, o_conv, conv_w[l], (prev1, prev2), seq=tq, tm=ms)

        def pad_rows(a, n):
            return jnp.pad(a, ((0, 0), (0, n - a.shape[1])) + ((0, 0),) * (a.ndim - 2))

        qi_hm = pad_rows(qi.reshape(db, tq, iheads, idim).transpose(0, 2, 1, 3).reshape(db * iheads, tq, idim), rq)
        qi_hm = qi_hm.reshape(db, iheads * rq, idim)
        w_hm = pad_rows(sm[:, idim:idim + iheads].reshape(db, tq, iheads).transpose(0, 2, 1).reshape(db * iheads, tq), rq)
        w_hm = jnp.broadcast_to(w_hm.reshape(db, iheads * rq, 1), (db, iheads * rq, LANES))
        kin_t = jnp.swapaxes(pad_rows(smb[:, :idim].reshape(db, tq, idim), ps), 1, 2)
        topk_s = min(TOPK_MAX, (past + tq) // 4)
        past_keys, snew, thr = _sample_select(page_table, qi_hm, w_hm, kin_t, jnp.swapaxes(cache_kidx[l], 1, 2),
                                              pages=math.gcd(n_pages, 32), tq=tq, rq=rq, iheads=iheads, topk=topk_s)
        ksel, vsel, pos, cnt = _sc_gather(past_keys, thr, page_table, cache_k[l].reshape(n_pool * ps, heads, hd),
                                          cache_v[l].reshape(n_pool * ps, heads, hd), tq=tq, topk=topk_s, ps=ps)
        pos_rows = jnp.repeat(pos, heads, axis=1).reshape(ms, 1, topk_s * heads)
        snew_rows = jnp.repeat(snew[:, :tq, :ps // heads], heads, axis=2).reshape(ms, 1, ps)
        attn = _sample_attn_sel(cnt, rel_bias.T, q.reshape(ms, heads, hd), ksel.reshape(ms * topk_s * heads, hd),
                                vsel.reshape(ms * topk_s * heads, hd), pos_rows, snew_rows, thr,
                                pad_rows(kb.reshape(db, tq * heads, hd), ps), pad_rows(vb.reshape(db, tq * heads, hd), ps),
                                heads=heads, hd=hd, topk=topk_s, tq=tq, past=past)
        attn = attn.reshape(ms, aw).astype(BF16)

        x1p, x1s = _mix(xp, xs, g_mix, attn_p, attn, c_in_p, c_in, wl, o_gate, wpa, wpb, wo, tm=_tile(mp, 256))
        xp_next, xs_next = _mlp(x1p, x1s, g_mlp, gf, w1, w2, tm=_tile(mp, 1024), tf=_tile(w1.shape[1], 512),
                                final=l == depth - 1)
        outs["ks"].append(k.reshape(db, tq, heads, hd))
        outs["vs"].append(v.reshape(db, tq, heads, hd))
        outs["kis"].append(sm[:, :idim].reshape(db, tq, idim))
        outs["ss"].append(u_all.reshape(db, tq, cw)[:, tq - 2:])
        xp, xs = xp_next, xs_next

    st = {k: jnp.stack(v) for k, v in outs.items()}
    return (xp.reshape(batch, seq, d), xs.reshape(db, tq, d), st["kp"], st["vp"], st["kip"], st["sp"],
            st["ks"], st["vs"], st["kis"], st["ss"])
```

```python
import functools
import math

import jax
import jax.numpy as jnp
import numpy as np
from jax import lax
from jax.experimental import pallas as pl
from jax.experimental.pallas import tpu as pltpu
from jax.experimental.pallas import tpu_sc as plsc

F32 = jnp.float32
BF16 = jnp.bfloat16
I32 = jnp.int32

TOPK_MAX = 256
MAX_DISTANCE = 128
EPS = 1e-6

LANES = 128
SUBLANES = 8
V7X_SCOPED_VMEM_BYTES = 60000 * 1024
SC_LANES = 16
V7X_SC_CORES = 2
V7X_SC_SUBCORES = 16

LOG2E = math.log2(math.e)
INT_MIN = np.int32(-2 ** 31)
NEG_BIG = -1e30
SOFTMAX_MIN_MASS = 2.0 ** -60


def _cparams(*sem):
    return pltpu.CompilerParams(dimension_semantics=sem, vmem_limit_bytes=V7X_SCOPED_VMEM_BYTES)


def _resident(shape):
    nd = len(shape)
    return pl.BlockSpec(shape, lambda *_: (0,) * nd, pipeline_mode=pl.Buffered(1))


def _rms_bf16(x, g):
    y = x * lax.rsqrt(jnp.mean(x * x, axis=-1, keepdims=True) + EPS)
    return (y * g).astype(BF16)


def _dot(a, b):
    return jnp.dot(a, b, preferred_element_type=F32)


def _dot_nt(a, b):
    return lax.dot_general(a, b, (((1,), (1,)), ((), ())), preferred_element_type=F32)


def _ordered_key(x):
    b = lax.bitcast_convert_type(x, I32)
    return b ^ ((b >> 31) & np.int32(0x7FFFFFFF))


def _bucket(n, n_buckets):
    n = jnp.maximum(n, 0)
    me = n_buckets // 2
    nf = jnp.maximum(n, me).astype(F32)
    large = me + (jnp.log(nf / me) / math.log(MAX_DISTANCE / me) * (n_buckets - me)).astype(I32)
    large = jnp.minimum(large, n_buckets - 1)
    return jnp.where(n < me, n, large)


def _bias_of_bucket(bucket, relb_ref, h, n_buckets):
    acc = jnp.zeros(bucket.shape, F32)
    for bkt in range(n_buckets):
        acc = jnp.where(bucket == bkt, relb_ref[bkt, h], acc)
    return acc


def _fold_lanes(x):
    acc = x[:, 0:LANES]
    for t in range(1, x.shape[1] // LANES):
        acc = acc + x[:, t * LANES:(t + 1) * LANES]
    return acc


def _fold_rows(x, op=jnp.add):
    parts = [x[t * SUBLANES:(t + 1) * SUBLANES, :] for t in range(x.shape[0] // SUBLANES)]
    while len(parts) > 1:
        parts = [op(parts[t], parts[t + 1]) for t in range(0, len(parts) - 1, 2)] + parts[len(parts) & ~1:]
    return parts[0]


def _kth_threshold(count_ge, shape, axis, k, trips):
    def bit_body(t, carry):
        uthr, at_thr = carry
        cand = uthr | jnp.left_shift(np.int32(1), 31 - t)
        cnt = jnp.sum(count_ge(cand ^ INT_MIN), axis=axis, keepdims=True)
        return jnp.where(cnt >= k, cand, uthr), jnp.where(cnt >= k, jnp.broadcast_to(cnt, shape), at_thr)

    uthr, at_thr = lax.fori_loop(0, trips, bit_body, (jnp.zeros(shape, I32), jnp.full(shape, k, F32)))
    return jnp.maximum(uthr ^ INT_MIN, INT_MIN + 1), at_thr - k


def _flash_update(lg, v, m_ref, l_ref, acc_ref, h, hd):
    reps = lg.shape[1] // LANES
    m_prev = m_ref[h]
    m_new = jnp.maximum(m_prev, jnp.max(lg, axis=1, keepdims=True))
    p = jnp.exp2(lg - jnp.tile(m_new, (1, reps)))
    alpha = jnp.exp2(m_prev - m_new)
    pv = _dot(p.astype(BF16), jnp.concatenate([v, jnp.ones((v.shape[0], LANES), BF16)], axis=1))
    l_ref[h] = alpha * l_ref[h] + pv[:, hd:]
    m_ref[h] = m_new
    hs = slice(h * hd, (h + 1) * hd)
    acc_ref[:, hs] = acc_ref[:, hs] * jnp.tile(alpha, (1, hd // LANES)) + pv[:, :hd]


def _stage_rows(w_hbm, w_ref, row0):
    @pl.when(pl.program_id(0) == 0)
    def _():
        pltpu.sync_copy(w_hbm.at[pl.ds(row0, w_ref.shape[0])], w_ref)


def _proj_attn_kernel(x_ref, g_ref, w_hbm, q_ref, k_ref, v_ref, kb_ref, vb_ref, qi_ref,
                      sm_ref, smb_ref, kn_ref, w_ref, ws_ref, *, aw, iw, hd, tm, n_small, qscale):
    heads = aw // hd
    _stage_rows(w_hbm, w_ref, 0)

    @pl.when(pl.program_id(0) == 0)
    def _():
        ws_ref[...] = jnp.zeros(ws_ref.shape, BF16)
        pltpu.sync_copy(w_hbm.at[pl.ds(w_ref.shape[0], n_small)], ws_ref.at[pl.ds(0, n_small)])

    hb = _rms_bf16(x_ref[...], g_ref[...])
    q_ref[...] = (_dot_nt(hb, w_ref[0:aw, :]) * qscale).astype(BF16)
    for o_ref, ob_ref, c0 in ((k_ref, kb_ref, aw), (v_ref, vb_ref, 2 * aw)):
        kv = _dot_nt(hb, w_ref[c0:c0 + aw, :])
        ob_ref[...] = kv.astype(BF16)
        if o_ref is k_ref:
            kf = kv.astype(BF16).astype(F32)
            lane = lax.broadcasted_iota(I32, (tm, LANES), 1)
            kn = jnp.zeros((tm, LANES), F32)
            for h in range(heads):
                kh = kf[:, h * hd:(h + 1) * hd]
                kn = jnp.where(lane == h, jnp.sum(kh * kh, axis=1, keepdims=True), kn)
            kn_ref[...] = kn
        for h in range(heads):
            o_ref[pl.ds(h, tm, stride=heads), :] = kv[:, h * hd:(h + 1) * hd]
    qi_ref[...] = _dot_nt(hb, w_ref[3 * aw:3 * aw + iw, :]).astype(BF16)
    sm = _dot_nt(hb, ws_ref[...])
    sm_ref[...] = sm
    smb_ref[...] = sm.astype(BF16)


def _proj_attn(x2d, g, w_all, *, aw, iw, n_small, hd, tm):
    m, d = x2d.shape
    wt = 3 * aw + iw
    heads = aw // hd
    assert hd == LANES and n_small <= LANES and wt % 16 == 0 and n_small % 16 == 0
    row = lambda width: pl.BlockSpec((tm, width), lambda i: (i, 0))
    kv_spec = pl.BlockSpec((tm * heads, hd), lambda i: (i, 0))
    kv_shape = jax.ShapeDtypeStruct((m * heads, hd), F32)
    outs = [(aw, BF16), None, None, (aw, BF16), (aw, BF16), (iw, BF16), (LANES, F32), (LANES, BF16), (LANES, F32)]
    return pl.pallas_call(
        functools.partial(_proj_attn_kernel, aw=aw, iw=iw, hd=hd, tm=tm, n_small=n_small,
                          qscale=hd ** -0.5 * LOG2E),
        grid=(m // tm,),
        in_specs=[row(d), _resident((1, d)), pl.BlockSpec(memory_space=pl.ANY)],
        out_specs=[kv_spec if o is None else row(o[0]) for o in outs],
        out_shape=[kv_shape if o is None else jax.ShapeDtypeStruct((m, o[0]), o[1]) for o in outs],
        scratch_shapes=[pltpu.VMEM((wt, d), BF16), pltpu.VMEM((LANES, d), BF16)],
        compiler_params=_cparams("arbitrary"),
        name="proj_attn",
    )(x2d, g, w_all)


def _proj_conv_kernel(*refs, cw, tm, seq, tiles_per_seq, tail, row0, n_cast):
    if tiles_per_seq:
        x_ref, g_ref, w_hbm, cwt_ref = refs[:4]
        c_ref, ut_ref = refs[4 + n_cast:6 + n_cast]
        w_ref, carry_ref = refs[6 + 2 * n_cast:]
        _cast_slabs(refs[4:4 + n_cast], refs[6 + n_cast:6 + 2 * n_cast])
    else:
        x_ref, g_ref, w_hbm, cwt_ref, p1_ref, p2_ref, c_ref, ut_ref, w_ref = refs
    _stage_rows(w_hbm, w_ref, row0)
    hb = _rms_bf16(x_ref[...], g_ref[...])
    cx = _dot_nt(hb, w_ref[0:cw, :])
    cb = _dot_nt(hb, w_ref[cw:2 * cw, :])
    cc = _dot_nt(hb, w_ref[2 * cw:3 * cw, :])
    u = cc * cx
    r = lax.broadcasted_iota(I32, (tm, cw), 0)
    um1 = pltpu.roll(u, 1, 0)
    um2 = pltpu.roll(u, 2, 0)
    if tiles_per_seq:
        @pl.when(pl.program_id(0) % tiles_per_seq == 0)
        def _():
            carry_ref[...] = jnp.zeros_like(carry_ref)
        prev1 = jnp.broadcast_to(carry_ref[SUBLANES - 1:SUBLANES, :], (tm, cw))
        prev2 = jnp.broadcast_to(carry_ref[SUBLANES - 2:SUBLANES - 1, :], (tm, cw))
        um1 = jnp.where(r == 0, prev1, um1)
        um2 = jnp.where(r == 0, prev2, jnp.where(r == 1, prev1, um2))
        carry_ref[...] = u[tm - SUBLANES:tm, :]
    else:
        assert seq & (seq - 1) == 0
        t = r & (seq - 1)
        um1 = jnp.where(t >= 1, um1, p1_ref[...])
        um2 = jnp.where(t >= 2, um2, p2_ref[...])
    y = cwt_ref[0:1, :] * um2 + cwt_ref[1:2, :] * um1 + cwt_ref[2:3, :] * u
    c_ref[...] = (cb * y).astype(BF16)
    ut_ref[...] = u[tm - tail:tm, :]


def _proj_conv(x2d, g, w_all, row0, conv_w, prev, to_cast=(), *, seq, tm):
    m, d = x2d.shape
    cw = conv_w.shape[1]
    assert conv_w.shape[0] == 3 and row0 % 16 == 0
    row = lambda width: pl.BlockSpec((tm, width), lambda i: (i, 0))
    in_specs = [row(d), _resident((1, d)), pl.BlockSpec(memory_space=pl.ANY), _resident((3, cw))]
    args = [x2d, g, w_all, conv_w]
    scratch = [pltpu.VMEM((3 * cw, d), BF16)]
    if prev is None:
        assert seq % tm == 0
        tiles_per_seq, tail = seq // tm, SUBLANES
        scratch += [pltpu.VMEM((SUBLANES, cw), F32)]
    else:
        assert tm % seq == 0 and seq >= 2
        tiles_per_seq, tail = 0, tm
        in_specs += [row(cw), row(cw)]
        args += list(prev)
    assert not (to_cast and prev is not None)
    c_in, c_out, c_shapes = _cast_specs(to_cast, m // tm, lambda i: i)
    return pl.pallas_call(
        functools.partial(_proj_conv_kernel, cw=cw, tm=tm, seq=seq, tiles_per_seq=tiles_per_seq, tail=tail,
                          row0=row0, n_cast=len(to_cast)),
        grid=(m // tm,),
        in_specs=in_specs + c_in,
        out_specs=[row(cw), pl.BlockSpec((tail, cw), lambda i: (i, 0))] + c_out,
        out_shape=[jax.ShapeDtypeStruct((m, cw), BF16), jax.ShapeDtypeStruct((m // tm * tail, cw), F32)] + c_shapes,
        scratch_shapes=scratch,
        compiler_params=_cparams("arbitrary"),
        name="proj_conv",
    )(*args, *to_cast)


def _bias_tiles_kernel(relb_ref, o_ref, bmax_ref, *, tb, n_buckets):
    kind = pl.program_id(0)
    h = pl.program_id(1)
    qry = lax.broadcasted_iota(I32, (tb, tb), 0)
    key = lax.broadcasted_iota(I32, (tb, tb), 1)
    bias = _bias_of_bucket(_bucket(kind * tb + qry - key, n_buckets), relb_ref, h, n_buckets)
    o_ref[...] = (bias - relb_ref[n_buckets - 1, h]) * LOG2E
    bmax = relb_ref[0, h]
    for bkt in range(1, n_buckets):
        bmax = jnp.maximum(bmax, relb_ref[bkt, h])
    bmax_ref[...] = jnp.full(bmax_ref.shape, (bmax - relb_ref[n_buckets - 1, h]) * LOG2E, F32)


def _bias_tiles(rel_bias, *, tb):
    n_buckets, heads = rel_bias.shape
    return pl.pallas_call(
        functools.partial(_bias_tiles_kernel, tb=tb, n_buckets=n_buckets),
        grid=(2, heads),
        in_specs=[pl.BlockSpec(memory_space=pltpu.SMEM)],
        out_specs=[pl.BlockSpec((None, None, tb, tb), lambda a, h: (a, h, 0, 0)),
                   pl.BlockSpec((None, None, SUBLANES, LANES), lambda a, h: (a, h, 0, 0))],
        out_shape=[jax.ShapeDtypeStruct((2, heads, tb, tb), F32),
                   jax.ShapeDtypeStruct((2, heads, SUBLANES, LANES), F32)],
        compiler_params=_cparams("arbitrary", "arbitrary"),
        name="bias_tiles",
    )(rel_bias)


def _cast_specs(weights, n_steps, step_of):
    in_specs, out_specs, out_shapes = [], [], []
    for w in weights:
        rows, cols = w.shape
        assert rows % n_steps == 0 and (rows // n_steps) % 16 == 0
        spec = pl.BlockSpec((rows // n_steps, cols), lambda *idx: (step_of(*idx), 0))
        in_specs.append(spec)
        out_specs.append(spec)
        out_shapes.append(jax.ShapeDtypeStruct((rows, cols), BF16))
    return in_specs, out_specs, out_shapes


def _cast_slabs(src_refs, dst_refs):
    for src, dst in zip(src_refs, dst_refs):
        dst[...] = src[...].astype(BF16)


def _attn_prompt_kernel(*refs, n_cast, tb, heads, hd, iheads, idim, topk):
    bmax_ref, qi_ref, sm_ref, smb_ref, q_ref, kb_ref, vb_ref, kn_ref, bias_ref = refs[:9]
    o_ref = refs[9 + n_cast]
    skey_ref, thr_ref, wt_ref, madd_ref, kmax_ref, shift_ref, m_ref, l_ref, acc_ref = refs[10 + 2 * n_cast:]
    _cast_slabs(refs[9:9 + n_cast], refs[10 + n_cast:10 + 2 * n_cast])
    i = pl.program_id(1)
    wscale = idim ** -0.5 * iheads ** -0.5

    l_ref[...] = jnp.zeros(l_ref.shape, F32)
    acc_ref[...] = jnp.zeros(acc_ref.shape, F32)
    wt_ref[...] = sm_ref[...].T * wscale

    def chunk(j):
        return pl.ds(pl.multiple_of(j * tb, tb), tb)

    def score_chunk(j, diag):
        kic = smb_ref[chunk(j), 0:idim]
        acc = jnp.zeros((tb, tb), F32)
        for h in range(iheads):
            s = _dot_nt(kic, qi_ref[:, h * idim:(h + 1) * idim])
            acc = acc + jnp.maximum(s, 0.0) * wt_ref[idim + h:idim + h + 1, :]
        key = _ordered_key(acc)
        if diag:
            kpos = lax.broadcasted_iota(I32, (tb, tb), 0)
            qpos = lax.broadcasted_iota(I32, (tb, tb), 1)
            key = jnp.where(kpos > qpos, INT_MIN, key)
        skey_ref[chunk(j), :] = key

    def score_body(j, carry):
        score_chunk(j, False)
        return carry

    lax.fori_loop(0, i, score_body, 0)
    score_chunk(i, True)

    def count_ge(scand):
        def body(j, cnt):
            return cnt + _fold_rows(jnp.where(skey_ref[chunk(j), :] >= scand[0:1, :], 1.0, 0.0))

        return lax.fori_loop(0, i + 1, body, jnp.zeros((SUBLANES, tb), F32))

    trips = jnp.where((i + 1) * tb <= topk, 0, 32)
    thr, surplus = _kth_threshold(count_ge, (SUBLANES, tb), 0, float(topk), trips)
    thr_ref[...] = thr

    @pl.when(jnp.max(surplus) > 0.0)
    def _():
        thr_row = thr_ref[0:1, :]

        def gt_body(j, cnt):
            return cnt + _fold_rows(jnp.where(skey_ref[chunk(j), :] > thr_row, 1.0, 0.0))

        above = lax.fori_loop(0, i + 1, gt_body, jnp.zeros((SUBLANES, tb), F32))
        budget = topk - jnp.sum(above, axis=0, keepdims=True)
        kpos = lax.broadcasted_iota(I32, (tb, tb), 0)
        earlier = jnp.where(kpos > lax.broadcasted_iota(I32, (tb, tb), 1), 1.0, 0.0).astype(BF16)

        def fix_body(j, seen):
            sk = skey_ref[chunk(j), :]
            tie = jnp.where(sk == thr_row, 1.0, 0.0)
            rank = seen[0:1, :] + _dot(earlier, tie.astype(BF16))
            skey_ref[chunk(j), :] = jnp.where((tie > 0.0) & (rank >= budget), thr_row - 1, sk)
            return seen + jnp.sum(_fold_rows(tie), axis=0, keepdims=True)

        lax.fori_loop(0, i + 1, fix_body, jnp.zeros((SUBLANES, tb), F32))

    @pl.when(i == 0)
    def _():
        kmax_ref[...] = jnp.zeros(kmax_ref.shape, F32)

    kmax_ref[...] = jnp.maximum(kmax_ref[...], jnp.max(kn_ref[chunk(i), :], axis=0, keepdims=True))
    qsq = q_ref[...] * q_ref[...]
    for h in range(heads):
        qn2 = _dot(qsq[:, h * hd:(h + 1) * hd], jnp.ones((hd, LANES), BF16)) * (1.0 + 2.0 ** -6)
        shift_ref[h] = jnp.sqrt(qn2 * kmax_ref[0:1, h:h + 1]) + bmax_ref[0, h][0:1, :]

    def logits(j, h, kind):
        hs = slice(h * hd, (h + 1) * hd)
        lg = _dot_nt(q_ref[:, hs], kb_ref[chunk(j), hs]) + madd_ref[...]
        return lg if kind is None else lg + bias_ref[kind, h]

    def bounded_chunk(j, kind):
        for h in range(heads):
            hs = slice(h * hd, (h + 1) * hd)
            p = jnp.exp2(logits(j, h, kind) - jnp.tile(shift_ref[h], (1, tb // LANES)))
            v1 = jnp.concatenate([vb_ref[chunk(j), hs], jnp.ones((tb, LANES), BF16)], axis=1)
            pv = _dot(p.astype(BF16), v1)
            l_ref[h] += pv[:, hd:]
            acc_ref[:, hs] += pv[:, :hd]

    def running_max_chunk(j, kind):
        for h in range(heads):
            _flash_update(logits(j, h, kind), vb_ref[chunk(j), h * hd:(h + 1) * hd], m_ref, l_ref, acc_ref, h, hd)

    def attend(update):
        def one(j, kind):
            madd_ref[...] = jnp.where(skey_ref[chunk(j), :] >= thr_ref[0:1, :], 0.0, NEG_BIG).T
            update(j, kind)

        def far_body(j, carry):
            one(j, None)
            return carry

        lax.fori_loop(0, jnp.maximum(i - 1, 0), far_body, 0)

        @pl.when(i >= 1)
        def _():
            one(i - 1, 1)

        one(i, 0)

    attend(bounded_chunk)

    @pl.when(jnp.min(l_ref[...]) < SOFTMAX_MIN_MASS)
    def _():
        m_ref[...] = jnp.full(m_ref.shape, NEG_BIG, F32)
        l_ref[...] = jnp.zeros(l_ref.shape, F32)
        acc_ref[...] = jnp.zeros(acc_ref.shape, F32)
        attend(running_max_chunk)

    for h in range(heads):
        hs = slice(h * hd, (h + 1) * hd)
        o_ref[:, hs] = (acc_ref[:, hs] / jnp.tile(l_ref[h], (1, hd // LANES))).astype(BF16)


def _attn_prompt(bias_max, bias_tiles, qi, sm, smb, q, kb, vb, kn, to_cast, *, batch, seq, tb, heads, hd, iheads,
                 idim, topk):
    m, aw = q.shape
    iw = qi.shape[1]
    nq = seq // tb
    assert tb >= MAX_DISTANCE and tb % LANES == 0 and hd % LANES == 0
    qrow = lambda width: pl.BlockSpec((tb, width), lambda b, i: (b * nq + i, 0))
    seqblk = lambda width: pl.BlockSpec((seq, width), lambda b, i: (b, 0))
    c_in, c_out, c_shapes = _cast_specs(to_cast, batch * nq, lambda b, i: b * nq + i)
    return pl.pallas_call(
        functools.partial(_attn_prompt_kernel, n_cast=len(to_cast), tb=tb, heads=heads, hd=hd, iheads=iheads,
                          idim=idim, topk=topk),
        grid=(batch, nq),
        in_specs=[_resident(bias_max.shape), qrow(iw), qrow(LANES), seqblk(LANES), qrow(aw), seqblk(aw),
                  seqblk(aw), seqblk(LANES), _resident(bias_tiles.shape)] + c_in,
        out_specs=[qrow(aw)] + c_out,
        out_shape=[jax.ShapeDtypeStruct((m, aw), BF16)] + c_shapes,
        scratch_shapes=[pltpu.VMEM((seq, tb), I32), pltpu.VMEM((SUBLANES, tb), I32),
                        pltpu.VMEM((LANES, tb), F32), pltpu.VMEM((tb, tb), F32),
                        pltpu.VMEM((SUBLANES, LANES), F32), pltpu.VMEM((heads, tb, LANES), F32),
                        pltpu.VMEM((heads, tb, LANES), F32), pltpu.VMEM((heads, tb, LANES), F32),
                        pltpu.VMEM((tb, aw), F32)],
        compiler_params=_cparams("arbitrary", "arbitrary"),
        name="attn_prompt",
    )(bias_max, qi, sm, smb, q, kb, vb, kn, bias_tiles, *to_cast)


def _sample_select_kernel(pt_ref, qi_ref, w_ref, kin_ref, *rest, pages, ps, nc, iheads, idim, tq, rq, topk):
    page_refs = rest[:pages]
    past_ref, new_ref, thr_ref, row_ref = rest[pages:]
    b = pl.program_id(0)
    c = pl.program_id(1)
    db = pl.num_programs(0)
    ch = pages * ps
    wscale = idim ** -0.5 * iheads ** -0.5
    rows_b = pl.ds(pl.multiple_of(b * rq, rq), rq)

    def score(keys_t):
        s = _dot(qi_ref[...], keys_t)
        t = jnp.maximum(s, 0.0) * (w_ref[:, 0:1] * wscale)
        acc = t[0:rq]
        for h in range(1, iheads):
            acc = acc + t[h * rq:(h + 1) * rq]
        return _ordered_key(acc)

    @pl.when(c < nc)
    def _():
        for p in range(pages):
            key = score(page_refs[p][...].astype(BF16))
            row_ref[rows_b, pl.ds(pl.multiple_of(c * ch + p * ps, LANES), ps)] = key
            for t in range(ps // LANES):
                for j in range(tq):
                    past_ref[j, p * (ps // LANES) + t:p * (ps // LANES) + t + 1, :] = \
                        key[j:j + 1, t * LANES:(t + 1) * LANES]

    @pl.when(c == nc)
    def _():
        key = score(kin_ref[...])
        j = lax.broadcasted_iota(I32, (rq, ps), 0)
        n = lax.broadcasted_iota(I32, (rq, ps), 1)
        key = jnp.where((n <= j) & (n < tq), key, INT_MIN)
        row_ref[rows_b, nc * ch:nc * ch + ps] = key

    @pl.when((c == nc) & (b == db - 1))
    def _():
        n_rows = row_ref.shape[0]

        def count_ge(scand):
            sk = row_ref[...]
            return _fold_lanes(jnp.where(sk >= jnp.tile(scand, (1, sk.shape[1] // LANES)), 1.0, 0.0))

        thr, _ = _kth_threshold(count_ge, (n_rows, LANES), 1, float(topk), 32)
        count = lambda m: jnp.sum(_fold_lanes(jnp.where(m, 1.0, 0.0)), axis=1, keepdims=True)
        past_keys = row_ref[:, 0:nc * ch]
        new_keys = row_ref[:, nc * ch:nc * ch + ps]
        thr_new = jnp.tile(thr, (1, ps // LANES))
        gt_new = count(new_keys > thr_new)
        ties_allowed = topk - gt_new - count(past_keys > jnp.tile(thr, (1, nc * ch // LANES)))
        ties_past = count(past_keys == jnp.tile(thr, (1, nc * ch // LANES)))
        ties_new_allowed = ties_allowed - jnp.minimum(ties_past, ties_allowed)
        tie = jnp.where(new_keys == thr_new, 1.0, 0.0)
        lane = lax.broadcasted_iota(I32, tie.shape, 1)
        rank = jnp.zeros(tie.shape, F32)
        for s in range(1, tq):
            rank = rank + jnp.where(lane >= s, pltpu.roll(tie, s, 1), 0.0)
        new_keys = jnp.where((tie > 0.0) & (rank >= ties_new_allowed), thr_new - 1, new_keys)
        budget = jnp.broadcast_to(topk - gt_new, (n_rows, LANES)).astype(I32)
        thr_ref[...] = jnp.zeros(thr_ref.shape, I32)
        for bb in range(n_rows // rq):
            new_ref[bb] = new_keys[bb * rq:(bb + 1) * rq]
            for jj in range(tq):
                thr_ref[bb * tq + jj, 0:1, :] = thr[bb * rq + jj:bb * rq + jj + 1, :]
                thr_ref[bb * tq + jj, 1:2, :] = budget[bb * rq + jj:bb * rq + jj + 1, :]


def _sample_select(page_table, qi_hm, w_hm, kin_t, cache_kidx_t, *, pages, tq, rq, iheads, topk):
    db, n_pages = page_table.shape
    _, idim, ps = cache_kidx_t.shape
    nc = n_pages // pages
    ch = pages * ps
    assert ps % LANES == 0 and (ch // LANES) % SUBLANES == 0
    page_spec = lambda p: pl.BlockSpec(
        (None, idim, ps), lambda b, c, pt: (pt[b, jnp.minimum(c, nc - 1) * pages + p], 0, 0))
    per_b = lambda shape: pl.BlockSpec((None,) + shape, lambda b, c, pt: (b, 0, 0))
    return pl.pallas_call(
        functools.partial(_sample_select_kernel, pages=pages, ps=ps, nc=nc, iheads=iheads, idim=idim, tq=tq, rq=rq,
                          topk=topk),
        grid_spec=pltpu.PrefetchScalarGridSpec(
            num_scalar_prefetch=1,
            grid=(db, nc + 1),
            in_specs=[per_b((iheads * rq, idim)), per_b((iheads * rq, LANES)), per_b((idim, ps))]
            + [page_spec(p) for p in range(pages)],
            out_specs=[pl.BlockSpec((tq, ch // LANES, LANES), lambda b, c, pt: (b, jnp.minimum(c, nc - 1), 0)),
                       pl.BlockSpec((db, rq, ps), lambda b, c, pt: (0, 0, 0)),
                       pl.BlockSpec((db * tq, SUBLANES, LANES), lambda b, c, pt: (0, 0, 0))],
            scratch_shapes=[pltpu.VMEM((db * rq, nc * ch + ps), I32)],
        ),
        out_shape=[jax.ShapeDtypeStruct((db * tq, n_pages * ps // LANES, LANES), I32),
                   jax.ShapeDtypeStruct((db, rq, ps), I32),
                   jax.ShapeDtypeStruct((db * tq, SUBLANES, LANES), I32)],
        compiler_params=_cparams("arbitrary", "arbitrary"),
        name="sample_select",
    )(page_table, qi_hm, w_hm, kin_t, *([cache_kidx_t] * pages))


def _sc_gather_kernel(past_hbm, thr_hbm, pt_hbm, ck_hbm, cv_hbm, ksel_hbm, vsel_hbm, pos_hbm, cnt_hbm,
                      row_v, thr_v, pt_v, idx_v, phys_v, rows_v, cnt_v, sem,
                      *, nq, tq, topk, ps, n_cores, rows_per_copy):
    wid = lax.axis_index("s") * n_cores + lax.axis_index("c")

    @pl.when(wid < nq)
    def _():
        pltpu.sync_copy(past_hbm.at[wid], row_v)
        pltpu.sync_copy(thr_hbm.at[wid], thr_v)
        pltpu.sync_copy(pt_hbm.at[wid // tq], pt_v)
        thr = thr_v[0, pl.ds(0, SC_LANES)]
        budget = thr_v[1, pl.ds(0, SC_LANES)]
        lane = lax.iota(I32, SC_LANES)
        zero = jnp.zeros((SC_LANES,), I32)
        for t in range(idx_v.shape[0] // SC_LANES):
            idx_v[pl.ds(t * SC_LANES, SC_LANES)] = zero

        def compact(pred):
            def body(r, cnt):
                for t in range(LANES // SC_LANES):
                    x = row_v[r, pl.ds(t * SC_LANES, SC_LANES)]
                    m = pred(x, cnt)
                    rank = plsc.cumsum(jnp.where(m, 1, 0).astype(I32))
                    plsc.store_scatter(idx_v, [cnt + rank - 1], lane + (r * LANES + t * SC_LANES), mask=m)
                    cnt = cnt + plsc.all_reduce_population_count(m)
                return cnt
            return body

        cnt = lax.fori_loop(0, row_v.shape[0], compact(lambda x, cnt: x > thr), zero)
        cnt = lax.fori_loop(0, row_v.shape[0], compact(lambda x, cnt: (x == thr) & (cnt < budget)), cnt)
        cnt_v[...] = jnp.minimum(cnt, budget)
        pltpu.sync_copy(cnt_v, cnt_hbm.at[wid])
        pltpu.sync_copy(idx_v.at[pl.ds(0, topk)], pos_hbm.at[wid])

        shift = ps.bit_length() - 1
        for t in range(topk // SC_LANES):
            pos = idx_v[pl.ds(t * SC_LANES, SC_LANES)]
            page = plsc.load_gather(pt_v, [lax.shift_right_logical(pos, shift)])
            phys_v[pl.ds(t * SC_LANES, SC_LANES)] = page * ps + (pos & (ps - 1))
        for g in range(topk // rows_per_copy):
            sel = phys_v.at[pl.ds(g * rows_per_copy, rows_per_copy)]
            dst = pl.ds(wid * topk + g * rows_per_copy, rows_per_copy)
            for src_hbm, dst_hbm in ((ck_hbm, ksel_hbm), (cv_hbm, vsel_hbm)):
                pltpu.async_copy(src_hbm.at[sel], rows_v, sem).wait()
                pltpu.sync_copy(rows_v, dst_hbm.at[dst])


def _sc_gather(past_keys, thr, page_table, cache_k, cache_v, *, tq, topk, ps):
    nq, key_rows, _ = past_keys.shape
    _, heads, hd = cache_k.shape
    assert ps & (ps - 1) == 0 and topk % SC_LANES == 0
    rows_per_copy = 64
    assert topk % rows_per_copy == 0
    mesh = plsc.VectorSubcoreMesh(core_axis_name="c", subcore_axis_name="s", num_cores=V7X_SC_CORES,
                                  num_subcores=V7X_SC_SUBCORES)
    assert nq <= V7X_SC_CORES * V7X_SC_SUBCORES
    sel_shape = jax.ShapeDtypeStruct((nq * topk, heads, hd), F32)
    return pl.kernel(
        functools.partial(_sc_gather_kernel, nq=nq, tq=tq, topk=topk, ps=ps, n_cores=V7X_SC_CORES,
                          rows_per_copy=rows_per_copy),
        out_type=[sel_shape, sel_shape, jax.ShapeDtypeStruct((nq, topk), I32),
                  jax.ShapeDtypeStruct((nq, SC_LANES), I32)],
        mesh=mesh,
        scratch_types=[pltpu.VMEM((key_rows, LANES), I32), pltpu.VMEM((SUBLANES, LANES), I32),
                       pltpu.VMEM((page_table.shape[1],), I32), pltpu.VMEM((topk + SC_LANES,), I32),
                       pltpu.VMEM((topk,), I32), pltpu.VMEM((rows_per_copy, heads, hd), F32),
                       pltpu.VMEM((SC_LANES,), I32), pltpu.SemaphoreType.DMA],
        compiler_params=pltpu.CompilerParams(needs_layout_passes=False),
        name="sc_select_gather",
    )(past_keys, thr, page_table, cache_k, cache_v)


def _sample_attn_sel_kernel(cnt_ref, relbt_ref, q_ref, ksel_ref, vsel_ref, pos_ref, snew_ref, thr_ref, kn_ref, vn_ref,
                            o_ref, *, heads, tq, past, n_buckets):
    w = pl.program_id(0)
    j = w % tq
    q = q_ref[...]

    def head_bias(dist):
        bucket = _bucket(dist, n_buckets)
        acc = jnp.zeros(bucket.shape, F32)
        for bkt in range(n_buckets):
            acc = jnp.where(bucket == bkt, relbt_ref[:, bkt:bkt + 1], acc)
        return acc * LOG2E

    def logits(keys):
        lg = _dot_nt(q, keys)
        head = lax.broadcasted_iota(I32, lg.shape, 0)
        col = lax.broadcasted_iota(I32, lg.shape, 1)
        return lg, (col & (heads - 1)) == head, col

    lg, own, col = logits(ksel_ref[...].astype(BF16))
    keep = own & (col < cnt_ref[w, 0] * heads)
    lg = jnp.where(keep, lg + head_bias(jnp.broadcast_to(past + j - pos_ref[...], lg.shape)), NEG_BIG)
    lgn, own, col = logits(kn_ref[...])
    keep = own & (snew_ref[...] >= thr_ref[0:1, :])
    lgn = jnp.where(keep, lgn + head_bias(j - lax.shift_right_logical(col, heads.bit_length() - 1)), NEG_BIG)

    m = jnp.maximum(jnp.max(lg, axis=1, keepdims=True), jnp.max(lgn, axis=1, keepdims=True))
    p = jnp.exp2(lg - m)
    pn = jnp.exp2(lgn - m)
    denom = jnp.sum(p, axis=1, keepdims=True) + jnp.sum(pn, axis=1, keepdims=True)
    acc = _dot(p.astype(BF16), vsel_ref[...].astype(BF16)) + _dot(pn.astype(BF16), vn_ref[...])
    o_ref[...] = acc / denom


def _sample_attn_sel(cnt, rel_bias_t, q, ksel, vsel, pos, snew, thr, kn, vn, *, heads, hd, topk, tq, past):
    nq = q.shape[0]
    ps = kn.shape[1]
    assert ps == LANES and hd == LANES and heads & (heads - 1) == 0
    per_q = lambda shape: pl.BlockSpec((None,) + shape, lambda w, cnt: (w, 0, 0))
    per_b = lambda shape: pl.BlockSpec((None,) + shape, lambda w, cnt: (w // tq, 0, 0))
    sel_spec = pl.BlockSpec((topk * heads, hd), lambda w, cnt: (w, 0))
    return pl.pallas_call(
        functools.partial(_sample_attn_sel_kernel, heads=heads, tq=tq, past=past, n_buckets=rel_bias_t.shape[1]),
        grid_spec=pltpu.PrefetchScalarGridSpec(
            num_scalar_prefetch=1,
            grid=(nq,),
            in_specs=[pl.BlockSpec(rel_bias_t.shape, lambda w, cnt: (0, 0)), per_q((heads, hd)), sel_spec, sel_spec,
                      per_q((1, topk * heads)), per_q((1, ps)), per_q((SUBLANES, LANES)), per_b((ps, hd)),
                      per_b((ps, hd))],
            out_specs=per_q((heads, hd)),
        ),
        out_shape=jax.ShapeDtypeStruct((nq, heads, hd), F32),
        compiler_params=_cparams("arbitrary"),
        name="sample_attn_sel",
    )(cnt, rel_bias_t, q, ksel, vsel, pos, snew, thr, kn, vn)


def _mix_kernel(x_ref, xs_ref, g_ref, a_ref, as_ref, c_ref, cs_ref, w_hbm, wpa_ref, wpb_ref, wo_ref, o_ref, os_ref,
                wg_ref, *, d, row0):
    _stage_rows(w_hbm, wg_ref, row0)
    n = pl.num_programs(0) - 1

    def rows(x_ref, a_ref, c_ref, o_ref):
        x = x_ref[...]
        hb = _rms_bf16(x, g_ref[...])
        a = _dot(a_ref[...], wpa_ref[...])
        m = jax.nn.sigmoid(_dot_nt(hb, wg_ref[0:d, :])) * a
        c = _dot(c_ref[...], wpb_ref[...])
        m = m + jax.nn.sigmoid(_dot_nt(hb, wg_ref[d:2 * d, :])) * c
        o_ref[...] = x + _dot(m.astype(BF16), wo_ref[...])

    @pl.when(pl.program_id(0) < n)
    def _():
        rows(x_ref, a_ref, c_ref, o_ref)

    @pl.when(pl.program_id(0) == n)
    def _():
        rows(xs_ref, as_ref, cs_ref, os_ref)


def _mix(xp, xs, g, attn_p, attn_s, c_p, c_s, w_all, row0, w_pa, w_pb, w_o, *, tm):
    m, d = xp.shape
    ms = xs.shape[0]
    n = m // tm
    assert row0 % 16 == 0
    prow = lambda width: pl.BlockSpec((tm, width), lambda i: (jnp.minimum(i, n - 1), 0))
    srow = lambda width: pl.BlockSpec((ms, width), lambda i: (0, 0))
    return pl.pallas_call(
        functools.partial(_mix_kernel, d=d, row0=row0),
        grid=(n + 1,),
        in_specs=[prow(d), srow(d), _resident((1, d)), prow(attn_p.shape[1]), srow(attn_s.shape[1]),
                  prow(c_p.shape[1]), srow(c_s.shape[1]), pl.BlockSpec(memory_space=pl.ANY), _resident(w_pa.shape),
                  _resident(w_pb.shape), _resident(w_o.shape)],
        out_specs=[prow(d), srow(d)],
        out_shape=[jax.ShapeDtypeStruct((m, d), F32), jax.ShapeDtypeStruct((ms, d), F32)],
        scratch_shapes=[pltpu.VMEM((2 * d, d), BF16)],
        compiler_params=_cparams("arbitrary"),
        name="mix_out",
    )(xp, xs, g, attn_p, attn_s, c_p, c_s, w_all, w_pa, w_pb, w_o)


def _mlp_kernel(x_ref, xs_ref, g_ref, gf_ref, w1_ref, w2_ref, y_ref, ys_ref, h_ref, *, tm, final):
    f = pl.program_id(1)
    last_tile = pl.program_id(0) == pl.num_programs(0) - 1

    def start(x_ref, y_ref, h_rows):
        h_ref[h_rows, :] = _rms_bf16(x_ref[...], g_ref[...])
        y_ref[...] = jnp.zeros(y_ref.shape, F32)

    def finish(x_ref, y_ref):
        x2 = x_ref[...] + y_ref[...]
        if final:
            x2 = x2 * lax.rsqrt(jnp.mean(x2 * x2, axis=-1, keepdims=True) + EPS) * gf_ref[...]
        y_ref[...] = x2

    def ff(h):
        t = jnp.square(jnp.maximum(_dot(h, w1_ref[...]), 0.0))
        return _dot(t.astype(BF16), w2_ref[...])

    @pl.when(f == 0)
    def _():
        start(x_ref, y_ref, slice(0, tm))

    @pl.when((f == 0) & last_tile)
    def _():
        start(xs_ref, ys_ref, slice(tm, h_ref.shape[0]))

    @pl.when(jnp.logical_not(last_tile))
    def _():
        y_ref[...] += ff(h_ref[0:tm, :])

    @pl.when(last_tile)
    def _():
        r = ff(h_ref[...])
        y_ref[...] += r[0:tm]
        ys_ref[...] += r[tm:]

    @pl.when(f == pl.num_programs(1) - 1)
    def _():
        finish(x_ref, y_ref)

    @pl.when((f == pl.num_programs(1) - 1) & last_tile)
    def _():
        finish(xs_ref, ys_ref)


def _mlp(xp, xs, g, gf, w1, w2, *, tm, tf, final):
    m, d = xp.shape
    ms = xs.shape[0]
    ff = w1.shape[1]
    srow = pl.BlockSpec((ms, d), lambda i, f: (0, 0))
    return pl.pallas_call(
        functools.partial(_mlp_kernel, tm=tm, final=final),
        grid=(m // tm, ff // tf),
        in_specs=[pl.BlockSpec((tm, d), lambda i, f: (i, 0)), srow, _resident((1, d)), _resident((1, d)),
                  pl.BlockSpec((d, tf), lambda i, f: (0, f)), pl.BlockSpec((tf, d), lambda i, f: (f, 0))],
        out_specs=[pl.BlockSpec((tm, d), lambda i, f: (i, 0)), srow],
        out_shape=[jax.ShapeDtypeStruct((m, d), F32), jax.ShapeDtypeStruct((ms, d), F32)],
        scratch_shapes=[pltpu.VMEM((tm + ms, d), BF16)],
        compiler_params=_cparams("arbitrary", "arbitrary"),
        name="mlp",
    )(xp, xs, g, gf, w1, w2)


def _tile(m, cap):
    return min(m, cap)


def kernel(x_prompt, x_sample, cache_k, cache_v, cache_kidx, state_conv, page_table, rel_bias, norm_mix_g, w_in,
           conv_w, w_pa, w_pb, w_o, norm_mlp_g, w_mlp_in, w_mlp_out, norm_final_g):
    batch, seq, d = x_prompt.shape
    db, tq, _ = x_sample.shape
    depth, n_pool, ps, heads, hd = cache_k.shape
    idim = cache_kidx.shape[-1]
    cw = conv_w.shape[-1]
    aw = heads * hd
    n_in = w_in.shape[-1]
    iheads = (n_in - 3 * aw - idim - 3 * cw - 2 * d) // (idim + 1)
    iw = iheads * idim
    assert 3 * aw + iw + idim + iheads + 3 * cw + 2 * d == n_in and idim + iheads <= LANES
    n_pages = page_table.shape[1]
    past = n_pages * ps
    rq = SUBLANES
    assert tq <= rq

    mp, ms = batch * seq, db * tq
    xp = x_prompt.reshape(mp, d)
    xs = x_sample.reshape(ms, d)
    tb = _tile(seq, 256)
    pages = math.gcd(n_pages, 8)
    o_small = 3 * aw + iw
    o_conv = o_small + idim + iheads
    o_gate = o_conv + 3 * cw
    gf = norm_final_g.reshape(1, d)
    bias_tiles, bias_max = _bias_tiles(rel_bias, tb=tb)

    outs = {k: [] for k in ("kp", "vp", "kip", "sp", "ks", "vs", "kis", "ss")}
    for l in range(depth):
        wl = jnp.swapaxes(w_in[l], 0, 1).astype(BF16)
        g_mix = norm_mix_g[l].reshape(1, d)
        g_mlp = norm_mlp_g[l].reshape(1, d)

        tm = _tile(seq, 512)
        q, k, v, kb, vb, qi, sm, smb, kn = _proj_attn(xp, g_mix, wl, aw=aw, iw=iw, n_small=idim + iheads, hd=hd, tm=tm)
        c_in, u_tail, wpa, wpb, wo = _proj_conv(xp, g_mix, wl, o_conv, conv_w[l], None, (w_pa[l], w_pb[l], w_o[l]),
                                                seq=seq, tm=_tile(seq, 512))
        attn, w1, w2 = _attn_prompt(bias_max, bias_tiles, qi, sm, smb, q, kb, vb, kn, (w_mlp_in[l], w_mlp_out[l]),
                                    batch=batch, seq=seq, tb=tb, heads=heads, hd=hd, iheads=iheads, idim=idim,
                                    topk=min(TOPK_MAX, seq // 4))
        attn_p, c_in_p = attn, c_in
        outs["kp"].append(k.reshape(batch, seq, heads, hd))
        outs["vp"].append(v.reshape(batch, seq, heads, hd))
        outs["kip"].append(sm[:, :idim].reshape(batch, seq, idim))
        outs["sp"].append(u_tail.reshape(batch, -1, SUBLANES, cw)[:, -1, SUBLANES - 2:])

        q, k, v, kb, vb, qi, sm, smb, _ = _proj_attn(xs, g_mix, wl, aw=aw, iw=iw, n_small=idim + iheads, hd=hd, tm=ms)
        st = state_conv[l]
        zero = jnp.zeros((db, tq - 1, cw), F32)
        prev1 = jnp.concatenate([st[:, 1:2], zero], axis=1).reshape(ms, cw)
        prev2 = jnp.concatenate([st, zero[:, 1:]], axis=1).reshape(ms, cw)
        c_in, u_all = _proj_conv(xs, g_mix, wl, o_conv, conv_w[l], (prev1, prev2), seq=tq, tm=ms)

        def pad_rows(a, n):
            return jnp.pad(a, ((0, 0), (0, n - a.shape[1])) + ((0, 0),) * (a.ndim - 2))

        qi_hm = pad_rows(qi.reshape(db, tq, iheads, idim).transpose(0, 2, 1, 3).reshape(db * iheads, tq, idim), rq)
        qi_hm = qi_hm.reshape(db, iheads * rq, idim)
        w_hm = pad_rows(sm[:, idim:idim + iheads].reshape(db, tq, iheads).transpose(0, 2, 1).reshape(db * iheads, tq), rq)
        w_hm = jnp.broadcast_to(w_hm.reshape(db, iheads * rq, 1), (db, iheads * rq, LANES))
        kin_t = jnp.swapaxes(pad_rows(smb[:, :idim].reshape(db, tq, idim), ps), 1, 2)
        topk_s = min(TOPK_MAX, (past + tq) // 4)
        past_keys, snew, thr = _sample_select(page_table, qi_hm, w_hm, kin_t, jnp.swapaxes(cache_kidx[l], 1, 2),
                                              pages=math.gcd(n_pages, 32), tq=tq, rq=rq, iheads=iheads, topk=topk_s)
        ksel, vsel, pos, cnt = _sc_gather(past_keys, thr, page_table, cache_k[l].reshape(n_pool * ps, heads, hd),
                                          cache_v[l].reshape(n_pool * ps, heads, hd), tq=tq, topk=topk_s, ps=ps)
        pos_rows = jnp.repeat(pos, heads, axis=1).reshape(ms, 1, topk_s * heads)
        snew_rows = jnp.repeat(snew[:, :tq, :ps // heads], heads, axis=2).reshape(ms, 1, ps)
        attn = _sample_attn_sel(cnt, rel_bias.T, q.reshape(ms, heads, hd), ksel.reshape(ms * topk_s * heads, hd),
                                vsel.reshape(ms * topk_s * heads, hd), pos_rows, snew_rows, thr,
                                pad_rows(kb.reshape(db, tq * heads, hd), ps), pad_rows(vb.reshape(db, tq * heads, hd), ps),
                                heads=heads, hd=hd, topk=topk_s, tq=tq, past=past)
        attn = attn.reshape(ms, aw).astype(BF16)

        x1p, x1s = _mix(xp, xs, g_mix, attn_p, attn, c_in_p, c_in, wl, o_gate, wpa, wpb, wo, tm=_tile(mp, 256))
        xp_next, xs_next = _mlp(x1p, x1s, g_mlp, gf, w1, w2, tm=_tile(mp, 1024), tf=_tile(w1.shape[1], 512),
                                final=l == depth - 1)
        outs["ks"].append(k.reshape(db, tq, heads, hd))
        outs["vs"].append(v.reshape(db, tq, heads, hd))
        outs["kis"].append(sm[:, :idim].reshape(db, tq, idim))
        outs["ss"].append(u_all.reshape(db, tq, cw)[:, tq - 2:])
        xp, xs = xp_next, xs_next

    st = {k: jnp.stack(v) for k, v in outs.items()}
    return (xp.reshape(batch, seq, d), xs.reshape(db, tq, d), st["kp"], st["vp"], st["kip"], st["sp"],
            st["ks"], st["vs"], st["kis"], st["ss"])
```

```python
import functools
import math

import jax
import jax.numpy as jnp
import numpy as np
from jax import lax
from jax.experimental import pallas as pl
from jax.experimental.pallas import tpu as pltpu
from jax.experimental.pallas import tpu_sc as plsc

F32 = jnp.float32
BF16 = jnp.bfloat16
I32 = jnp.int32
I16 = jnp.int16
HALF = 1 << 15

TOPK_MAX = 256
MAX_DISTANCE = 128
EPS = 1e-6

LANES = 128
SUBLANES = 8
V7X_SCOPED_VMEM_BYTES = 60000 * 1024
SC_LANES = 16
V7X_SC_CORES = 2
V7X_SC_SUBCORES = 16

LOG2E = math.log2(math.e)
INT_MIN = np.int32(-2 ** 31)
NEG_BIG = -1e30
SOFTMAX_MIN_MASS = 2.0 ** -60


def _cparams(*sem):
    return pltpu.CompilerParams(dimension_semantics=sem, vmem_limit_bytes=V7X_SCOPED_VMEM_BYTES)


def _resident(shape):
    nd = len(shape)
    return pl.BlockSpec(shape, lambda *_: (0,) * nd, pipeline_mode=pl.Buffered(1))


def _rms_bf16(x, g):
    y = x * lax.rsqrt(jnp.mean(x * x, axis=-1, keepdims=True) + EPS)
    return (y * g).astype(BF16)


def _dot(a, b):
    return jnp.dot(a, b, preferred_element_type=F32)


def _dot_nt(a, b):
    return lax.dot_general(a, b, (((1,), (1,)), ((), ())), preferred_element_type=F32)


def _ordered_key(x):
    b = lax.bitcast_convert_type(x, I32)
    return b ^ ((b >> 31) & np.int32(0x7FFFFFFF))


def _bucket(n, n_buckets):
    n = jnp.maximum(n, 0)
    me = n_buckets // 2
    nf = jnp.maximum(n, me).astype(F32)
    large = me + (jnp.log(nf / me) / math.log(MAX_DISTANCE / me) * (n_buckets - me)).astype(I32)
    large = jnp.minimum(large, n_buckets - 1)
    return jnp.where(n < me, n, large)


def _bias_of_bucket(bucket, relb_ref, h, n_buckets):
    acc = jnp.zeros(bucket.shape, F32)
    for bkt in range(n_buckets):
        acc = jnp.where(bucket == bkt, relb_ref[bkt, h], acc)
    return acc


def _fold_lanes(x):
    acc = x[:, 0:LANES]
    for t in range(1, x.shape[1] // LANES):
        acc = acc + x[:, t * LANES:(t + 1) * LANES]
    return acc


def _fold_rows(x, op=jnp.add, rows=SUBLANES):
    parts = [x[t * rows:(t + 1) * rows, :] for t in range(x.shape[0] // rows)]
    while len(parts) > 1:
        parts = [op(parts[t], parts[t + 1]) for t in range(0, len(parts) - 1, 2)] + parts[len(parts) & ~1:]
    return parts[0]


def _bisect(count_ge, shape, axis, k, trips, bits, at_none):
    def bit_body(t, carry):
        u, at = carry
        cand = u | jnp.left_shift(np.int32(1), bits - 1 - t)
        cnt = jnp.sum(count_ge(cand), axis=axis, keepdims=True)
        return jnp.where(cnt >= k, cand, u), jnp.where(cnt >= k, jnp.broadcast_to(cnt, shape), at)

    return lax.fori_loop(0, trips, bit_body, (jnp.zeros(shape, I32), at_none))


def _kth_threshold(count_ge, shape, axis, k, trips):
    uthr, at_thr = _bisect(lambda u: count_ge(u ^ INT_MIN), shape, axis, k, trips, 32, jnp.full(shape, k, F32))
    return jnp.maximum(uthr ^ INT_MIN, INT_MIN + 1), at_thr - k


def _flash_update(lg, v, m_ref, l_ref, acc_ref, h, hd):
    reps = lg.shape[1] // LANES
    m_prev = m_ref[h]
    m_new = jnp.maximum(m_prev, jnp.max(lg, axis=1, keepdims=True))
    p = jnp.exp2(lg - jnp.tile(m_new, (1, reps)))
    alpha = jnp.exp2(m_prev - m_new)
    pv = _dot(p.astype(BF16), jnp.concatenate([v, jnp.ones((v.shape[0], LANES), BF16)], axis=1))
    l_ref[h] = alpha * l_ref[h] + pv[:, hd:]
    m_ref[h] = m_new
    hs = slice(h * hd, (h + 1) * hd)
    acc_ref[:, hs] = acc_ref[:, hs] * jnp.tile(alpha, (1, hd // LANES)) + pv[:, :hd]


def _stage_rows(w_hbm, w_ref, row0):
    @pl.when(pl.program_id(0) == 0)
    def _():
        pltpu.sync_copy(w_hbm.at[pl.ds(row0, w_ref.shape[0])], w_ref)


def _proj_attn_kernel(x_ref, g_ref, w_hbm, q_ref, k_ref, v_ref, kb_ref, vb_ref, qi_ref,
                      sm_ref, smb_ref, kn_ref, w_ref, ws_ref, *, aw, iw, hd, tm, n_small, qscale):
    heads = aw // hd
    _stage_rows(w_hbm, w_ref, 0)

    @pl.when(pl.program_id(0) == 0)
    def _():
        ws_ref[...] = jnp.zeros(ws_ref.shape, BF16)
        pltpu.sync_copy(w_hbm.at[pl.ds(w_ref.shape[0], n_small)], ws_ref.at[pl.ds(0, n_small)])

    hb = _rms_bf16(x_ref[...], g_ref[...])
    q_ref[...] = (_dot_nt(hb, w_ref[0:aw, :]) * qscale).astype(BF16)
    for o_ref, ob_ref, c0 in ((k_ref, kb_ref, aw), (v_ref, vb_ref, 2 * aw)):
        kv = _dot_nt(hb, w_ref[c0:c0 + aw, :])
        ob_ref[...] = kv.astype(BF16)
        if o_ref is k_ref:
            kf = kv.astype(BF16).astype(F32)
            lane = lax.broadcasted_iota(I32, (tm, LANES), 1)
            kn = jnp.zeros((tm, LANES), F32)
            for h in range(heads):
                kh = kf[:, h * hd:(h + 1) * hd]
                kn = jnp.where(lane == h, jnp.sum(kh * kh, axis=1, keepdims=True), kn)
            kn_ref[...] = kn
        for h in range(heads):
            o_ref[pl.ds(h, tm, stride=heads), :] = kv[:, h * hd:(h + 1) * hd]
    qi_ref[...] = _dot_nt(hb, w_ref[3 * aw:3 * aw + iw, :]).astype(BF16)
    sm = _dot_nt(hb, ws_ref[...])
    sm_ref[...] = sm
    smb_ref[...] = sm.astype(BF16)


def _proj_attn(x2d, g, w_all, *, aw, iw, n_small, hd, tm):
    m, d = x2d.shape
    wt = 3 * aw + iw
    heads = aw // hd
    assert hd == LANES and n_small <= LANES and wt % 16 == 0 and n_small % 16 == 0
    row = lambda width: pl.BlockSpec((tm, width), lambda i: (i, 0))
    kv_spec = pl.BlockSpec((tm * heads, hd), lambda i: (i, 0))
    kv_shape = jax.ShapeDtypeStruct((m * heads, hd), F32)
    outs = [(aw, BF16), None, None, (aw, BF16), (aw, BF16), (iw, BF16), (LANES, F32), (LANES, BF16), (LANES, F32)]
    return pl.pallas_call(
        functools.partial(_proj_attn_kernel, aw=aw, iw=iw, hd=hd, tm=tm, n_small=n_small,
                          qscale=hd ** -0.5 * LOG2E),
        grid=(m // tm,),
        in_specs=[row(d), _resident((1, d)), pl.BlockSpec(memory_space=pl.ANY)],
        out_specs=[kv_spec if o is None else row(o[0]) for o in outs],
        out_shape=[kv_shape if o is None else jax.ShapeDtypeStruct((m, o[0]), o[1]) for o in outs],
        scratch_shapes=[pltpu.VMEM((wt, d), BF16), pltpu.VMEM((LANES, d), BF16)],
        compiler_params=_cparams("arbitrary"),
        name="proj_attn",
    )(x2d, g, w_all)


def _proj_conv_kernel(*refs, cw, tm, seq, tiles_per_seq, tail, row0, n_cast):
    if tiles_per_seq:
        x_ref, g_ref, w_hbm, cwt_ref = refs[:4]
        c_ref, ut_ref = refs[4 + n_cast:6 + n_cast]
        w_ref, carry_ref = refs[6 + 2 * n_cast:]
        _cast_slabs(refs[4:4 + n_cast], refs[6 + n_cast:6 + 2 * n_cast])
    else:
        x_ref, g_ref, w_hbm, cwt_ref, p1_ref, p2_ref, c_ref, ut_ref, w_ref = refs
    _stage_rows(w_hbm, w_ref, row0)
    hb = _rms_bf16(x_ref[...], g_ref[...])
    cx = _dot_nt(hb, w_ref[0:cw, :])
    cb = _dot_nt(hb, w_ref[cw:2 * cw, :])
    cc = _dot_nt(hb, w_ref[2 * cw:3 * cw, :])
    u = cc * cx
    r = lax.broadcasted_iota(I32, (tm, cw), 0)
    um1 = pltpu.roll(u, 1, 0)
    um2 = pltpu.roll(u, 2, 0)
    if tiles_per_seq:
        @pl.when(pl.program_id(0) % tiles_per_seq == 0)
        def _():
            carry_ref[...] = jnp.zeros_like(carry_ref)
        prev1 = jnp.broadcast_to(carry_ref[SUBLANES - 1:SUBLANES, :], (tm, cw))
        prev2 = jnp.broadcast_to(carry_ref[SUBLANES - 2:SUBLANES - 1, :], (tm, cw))
        um1 = jnp.where(r == 0, prev1, um1)
        um2 = jnp.where(r == 0, prev2, jnp.where(r == 1, prev1, um2))
        carry_ref[...] = u[tm - SUBLANES:tm, :]
    else:
        assert seq & (seq - 1) == 0
        t = r & (seq - 1)
        um1 = jnp.where(t >= 1, um1, p1_ref[...])
        um2 = jnp.where(t >= 2, um2, p2_ref[...])
    y = cwt_ref[0:1, :] * um2 + cwt_ref[1:2, :] * um1 + cwt_ref[2:3, :] * u
    c_ref[...] = (cb * y).astype(BF16)
    ut_ref[...] = u[tm - tail:tm, :]


def _proj_conv(x2d, g, w_all, row0, conv_w, prev, to_cast=(), *, seq, tm):
    m, d = x2d.shape
    cw = conv_w.shape[1]
    assert conv_w.shape[0] == 3 and row0 % 16 == 0
    row = lambda width: pl.BlockSpec((tm, width), lambda i: (i, 0))
    in_specs = [row(d), _resident((1, d)), pl.BlockSpec(memory_space=pl.ANY), _resident((3, cw))]
    args = [x2d, g, w_all, conv_w]
    scratch = [pltpu.VMEM((3 * cw, d), BF16)]
    if prev is None:
        assert seq % tm == 0
        tiles_per_seq, tail = seq // tm, SUBLANES
        scratch += [pltpu.VMEM((SUBLANES, cw), F32)]
    else:
        assert tm % seq == 0 and seq >= 2
        tiles_per_seq, tail = 0, tm
        in_specs += [row(cw), row(cw)]
        args += list(prev)
    assert not (to_cast and prev is not None)
    c_in, c_out, c_shapes = _cast_specs(to_cast, m // tm, lambda i: i)
    return pl.pallas_call(
        functools.partial(_proj_conv_kernel, cw=cw, tm=tm, seq=seq, tiles_per_seq=tiles_per_seq, tail=tail,
                          row0=row0, n_cast=len(to_cast)),
        grid=(m // tm,),
        in_specs=in_specs + c_in,
        out_specs=[row(cw), pl.BlockSpec((tail, cw), lambda i: (i, 0))] + c_out,
        out_shape=[jax.ShapeDtypeStruct((m, cw), BF16), jax.ShapeDtypeStruct((m // tm * tail, cw), F32)] + c_shapes,
        scratch_shapes=scratch,
        compiler_params=_cparams("arbitrary"),
        name="proj_conv",
    )(*args, *to_cast)


def _bias_tiles_kernel(relb_ref, o_ref, bmax_ref, *, tb, n_buckets):
    kind = pl.program_id(0)
    h = pl.program_id(1)
    qry = lax.broadcasted_iota(I32, (tb, tb), 0)
    key = lax.broadcasted_iota(I32, (tb, tb), 1)
    bias = _bias_of_bucket(_bucket(kind * tb + qry - key, n_buckets), relb_ref, h, n_buckets)
    o_ref[...] = (bias - relb_ref[n_buckets - 1, h]) * LOG2E
    bmax = relb_ref[0, h]
    for bkt in range(1, n_buckets):
        bmax = jnp.maximum(bmax, relb_ref[bkt, h])
    bmax_ref[...] = jnp.full(bmax_ref.shape, (bmax - relb_ref[n_buckets - 1, h]) * LOG2E, F32)


def _bias_tiles(rel_bias, *, tb):
    n_buckets, heads = rel_bias.shape
    return pl.pallas_call(
        functools.partial(_bias_tiles_kernel, tb=tb, n_buckets=n_buckets),
        grid=(2, heads),
        in_specs=[pl.BlockSpec(memory_space=pltpu.SMEM)],
        out_specs=[pl.BlockSpec((None, None, tb, tb), lambda a, h: (a, h, 0, 0)),
                   pl.BlockSpec((None, None, SUBLANES, LANES), lambda a, h: (a, h, 0, 0))],
        out_shape=[jax.ShapeDtypeStruct((2, heads, tb, tb), F32),
                   jax.ShapeDtypeStruct((2, heads, SUBLANES, LANES), F32)],
        compiler_params=_cparams("arbitrary", "arbitrary"),
        name="bias_tiles",
    )(rel_bias)


def _cast_specs(weights, n_steps, step_of):
    in_specs, out_specs, out_shapes = [], [], []
    for w in weights:
        rows, cols = w.shape
        assert rows % n_steps == 0 and (rows // n_steps) % 16 == 0
        spec = pl.BlockSpec((rows // n_steps, cols), lambda *idx: (step_of(*idx), 0))
        in_specs.append(spec)
        out_specs.append(spec)
        out_shapes.append(jax.ShapeDtypeStruct((rows, cols), BF16))
    return in_specs, out_specs, out_shapes


def _cast_slabs(src_refs, dst_refs):
    for src, dst in zip(src_refs, dst_refs):
        dst[...] = src[...].astype(BF16)


def _attn_prompt_kernel(*refs, n_cast, tb, heads, hd, iheads, idim, topk):
    bmax_ref, qi_ref, sm_ref, smb_ref, q_ref, kb_ref, vb_ref, kn_ref, bias_ref = refs[:9]
    o_ref = refs[9 + n_cast]
    (skey_ref, hi_ref, lo_ref, thr_ref, wt_ref, madd_ref, kmax_ref, shift_ref, m_ref, l_ref,
     acc_ref) = refs[10 + 2 * n_cast:]
    _cast_slabs(refs[9:9 + n_cast], refs[10 + n_cast:10 + 2 * n_cast])
    i = pl.program_id(1)
    wscale = idim ** -0.5 * iheads ** -0.5

    l_ref[...] = jnp.zeros(l_ref.shape, F32)
    acc_ref[...] = jnp.zeros(acc_ref.shape, F32)
    wt_ref[...] = sm_ref[...].T * wscale

    def chunk(j):
        return pl.ds(pl.multiple_of(j * tb, tb), tb)

    def score_chunk(j, diag):
        kic = smb_ref[chunk(j), 0:idim]
        acc = jnp.zeros((tb, tb), F32)
        for h in range(iheads):
            s = _dot_nt(kic, qi_ref[:, h * idim:(h + 1) * idim])
            acc = acc + jnp.maximum(s, 0.0) * wt_ref[idim + h:idim + h + 1, :]
        key = _ordered_key(acc)
        if diag:
            kpos = lax.broadcasted_iota(I32, (tb, tb), 0)
            qpos = lax.broadcasted_iota(I32, (tb, tb), 1)
            key = jnp.where(kpos > qpos, INT_MIN, key)
        skey_ref[chunk(j), :] = key
        hi_ref[chunk(j), :] = (key >> 16).astype(I16)
        lo_ref[chunk(j), :] = ((key & 0xFFFF) - HALF).astype(I16)

    def score_body(j, carry):
        score_chunk(j, False)
        return carry

    lax.fori_loop(0, i, score_body, 0)
    score_chunk(i, True)

    def counter(half_ref):
        def count_ge(u):
            cand = (u - HALF).astype(I16)[0:1, :]

            def body(j, cnt):
                ge = jnp.where(half_ref[chunk(j), :] >= cand, jnp.int16(1), jnp.int16(0))
                return cnt + _fold_rows(ge, rows=2 * SUBLANES)

            return lax.fori_loop(0, i + 1, body, jnp.zeros((2 * SUBLANES, tb), I16)).astype(F32)

        return count_ge

    shape = (SUBLANES, tb)
    trips = jnp.where((i + 1) * tb <= topk, 0, 16)
    u_hi, n_ge_hi = _bisect(counter(hi_ref), shape, 0, float(topk), trips, 16, jnp.full(shape, topk, F32))
    t_hi = (u_hi - HALF).astype(I16)[0:1, :]
    n_gt_hi = jnp.where(u_hi == 2 * HALF - 1, 0.0,
                        jnp.sum(counter(hi_ref)(jnp.minimum(u_hi + 1, 2 * HALF - 1)), axis=0, keepdims=True))

    def keep_low_of_kth_high(j, carry):
        lo_ref[chunk(j), :] = jnp.where(hi_ref[chunk(j), :] == t_hi, lo_ref[chunk(j), :], jnp.int16(-HALF))
        return carry

    lax.fori_loop(0, i + 1, keep_low_of_kth_high, 0)
    u_lo, n_ge_lo = _bisect(counter(lo_ref), shape, 0, topk - n_gt_hi, trips, 16, n_ge_hi - n_gt_hi)
    thr = jnp.maximum(jnp.left_shift(u_hi - HALF, 16) | u_lo, INT_MIN + 1)
    surplus = n_gt_hi + n_ge_lo - topk
    thr_ref[...] = thr

    @pl.when(jnp.max(surplus) > 0.0)
    def _():
        thr_row = thr_ref[0:1, :]

        def gt_body(j, cnt):
            return cnt + _fold_rows(jnp.where(skey_ref[chunk(j), :] > thr_row, 1.0, 0.0))

        above = lax.fori_loop(0, i + 1, gt_body, jnp.zeros((SUBLANES, tb), F32))
        budget = topk - jnp.sum(above, axis=0, keepdims=True)
        kpos = lax.broadcasted_iota(I32, (tb, tb), 0)
        earlier = jnp.where(kpos > lax.broadcasted_iota(I32, (tb, tb), 1), 1.0, 0.0).astype(BF16)

        def fix_body(j, seen):
            sk = skey_ref[chunk(j), :]
            tie = jnp.where(sk == thr_row, 1.0, 0.0)
            rank = seen[0:1, :] + _dot(earlier, tie.astype(BF16))
            skey_ref[chunk(j), :] = jnp.where((tie > 0.0) & (rank >= budget), thr_row - 1, sk)
            return seen + jnp.sum(_fold_rows(tie), axis=0, keepdims=True)

        lax.fori_loop(0, i + 1, fix_body, jnp.zeros((SUBLANES, tb), F32))

    @pl.when(i == 0)
    def _():
        kmax_ref[...] = jnp.zeros(kmax_ref.shape, F32)

    kmax_ref[...] = jnp.maximum(kmax_ref[...], jnp.max(kn_ref[chunk(i), :], axis=0, keepdims=True))
    qsq = q_ref[...] * q_ref[...]
    for h in range(heads):
        qn2 = _dot(qsq[:, h * hd:(h + 1) * hd], jnp.ones((hd, LANES), BF16)) * (1.0 + 2.0 ** -6)
        shift_ref[h] = jnp.sqrt(qn2 * kmax_ref[0:1, h:h + 1]) + bmax_ref[0, h][0:1, :]

    def logits(j, h, kind):
        hs = slice(h * hd, (h + 1) * hd)
        lg = _dot_nt(q_ref[:, hs], kb_ref[chunk(j), hs]) + madd_ref[...]
        return lg if kind is None else lg + bias_ref[kind, h]

    def bounded_chunk(j, kind):
        for h in range(heads):
            hs = slice(h * hd, (h + 1) * hd)
            p = jnp.exp2(logits(j, h, kind) - jnp.tile(shift_ref[h], (1, tb // LANES)))
            v1 = jnp.concatenate([vb_ref[chunk(j), hs], jnp.ones((tb, LANES), BF16)], axis=1)
            pv = _dot(p.astype(BF16), v1)
            l_ref[h] += pv[:, hd:]
            acc_ref[:, hs] += pv[:, :hd]

    def running_max_chunk(j, kind):
        for h in range(heads):
            _flash_update(logits(j, h, kind), vb_ref[chunk(j), h * hd:(h + 1) * hd], m_ref, l_ref, acc_ref, h, hd)

    def attend(update):
        def one(j, kind):
            madd_ref[...] = jnp.where(skey_ref[chunk(j), :] >= thr_ref[0:1, :], 0.0, NEG_BIG).T
            update(j, kind)

        def far_body(j, carry):
            one(j, None)
            return carry

        lax.fori_loop(0, jnp.maximum(i - 1, 0), far_body, 0)

        @pl.when(i >= 1)
        def _():
            one(i - 1, 1)

        one(i, 0)

    attend(bounded_chunk)

    @pl.when(jnp.min(l_ref[...]) < SOFTMAX_MIN_MASS)
    def _():
        m_ref[...] = jnp.full(m_ref.shape, NEG_BIG, F32)
        l_ref[...] = jnp.zeros(l_ref.shape, F32)
        acc_ref[...] = jnp.zeros(acc_ref.shape, F32)
        attend(running_max_chunk)

    for h in range(heads):
        hs = slice(h * hd, (h + 1) * hd)
        o_ref[:, hs] = (acc_ref[:, hs] / jnp.tile(l_ref[h], (1, hd // LANES))).astype(BF16)


def _attn_prompt(bias_max, bias_tiles, qi, sm, smb, q, kb, vb, kn, to_cast, *, batch, seq, tb, heads, hd, iheads,
                 idim, topk):
    m, aw = q.shape
    iw = qi.shape[1]
    nq = seq // tb
    assert tb >= MAX_DISTANCE and tb % LANES == 0 and hd % LANES == 0
    qrow = lambda width: pl.BlockSpec((tb, width), lambda b, i: (b * nq + i, 0))
    seqblk = lambda width: pl.BlockSpec((seq, width), lambda b, i: (b, 0))
    c_in, c_out, c_shapes = _cast_specs(to_cast, batch * nq, lambda b, i: b * nq + i)
    return pl.pallas_call(
        functools.partial(_attn_prompt_kernel, n_cast=len(to_cast), tb=tb, heads=heads, hd=hd, iheads=iheads,
                          idim=idim, topk=topk),
        grid=(batch, nq),
        in_specs=[_resident(bias_max.shape), qrow(iw), qrow(LANES), seqblk(LANES), qrow(aw), seqblk(aw),
                  seqblk(aw), seqblk(LANES), _resident(bias_tiles.shape)] + c_in,
        out_specs=[qrow(aw)] + c_out,
        out_shape=[jax.ShapeDtypeStruct((m, aw), BF16)] + c_shapes,
        scratch_shapes=[pltpu.VMEM((seq, tb), I32), pltpu.VMEM((seq, tb), I16), pltpu.VMEM((seq, tb), I16),
                        pltpu.VMEM((SUBLANES, tb), I32),
                        pltpu.VMEM((LANES, tb), F32), pltpu.VMEM((tb, tb), F32),
                        pltpu.VMEM((SUBLANES, LANES), F32), pltpu.VMEM((heads, tb, LANES), F32),
                        pltpu.VMEM((heads, tb, LANES), F32), pltpu.VMEM((heads, tb, LANES), F32),
                        pltpu.VMEM((tb, aw), F32)],
        compiler_params=_cparams("arbitrary", "arbitrary"),
        name="attn_prompt",
    )(bias_max, qi, sm, smb, q, kb, vb, kn, bias_tiles, *to_cast)


def _sample_select_kernel(pt_ref, qi_ref, w_ref, kin_ref, *rest, pages, ps, nc, iheads, idim, tq, rq, topk):
    page_refs = rest[:pages]
    past_ref, new_ref, thr_ref, row_ref, real_ref = rest[pages:]
    b = pl.program_id(0)
    c = pl.program_id(1)
    db = pl.num_programs(0)
    ch = pages * ps
    wscale = idim ** -0.5 * iheads ** -0.5
    rows_b = pl.ds(pl.multiple_of(b * rq, rq), rq)

    def score(keys_t):
        s = _dot(qi_ref[...], keys_t)
        t = jnp.maximum(s, 0.0) * (w_ref[:, 0:1] * wscale)
        acc = t[0:rq]
        for h in range(1, iheads):
            acc = acc + t[h * rq:(h + 1) * rq]
        return _ordered_key(acc)

    @pl.when(c < nc)
    def _():
        for p in range(pages):
            key = score(page_refs[p][...].astype(BF16))
            row_ref[rows_b, pl.ds(pl.multiple_of(c * ch + p * ps, LANES), ps)] = key
            for t in range(ps // LANES):
                for j in range(tq):
                    past_ref[j, p * (ps // LANES) + t:p * (ps // LANES) + t + 1, :] = \
                        key[j:j + 1, t * LANES:(t + 1) * LANES]

    @pl.when(c == nc)
    def _():
        key = score(kin_ref[...])
        j = lax.broadcasted_iota(I32, (rq, ps), 0)
        n = lax.broadcasted_iota(I32, (rq, ps), 1)
        key = jnp.where((n <= j) & (n < tq), key, INT_MIN)
        row_ref[rows_b, nc * ch:nc * ch + ps] = key

    @pl.when((c == nc) & (b == db - 1))
    def _():
        n_rows = real_ref.shape[0]
        for bb in range(n_rows // tq):
            real_ref[bb * tq:(bb + 1) * tq, :] = row_ref[bb * rq:bb * rq + tq, :]

        def count_ge(scand):
            sk = real_ref[...]
            return _fold_lanes(jnp.where(sk >= jnp.tile(scand, (1, sk.shape[1] // LANES)), 1.0, 0.0))

        thr, _ = _kth_threshold(count_ge, (n_rows, LANES), 1, float(topk), 32)
        count = lambda m: jnp.sum(_fold_lanes(jnp.where(m, 1.0, 0.0)), axis=1, keepdims=True)
        past_keys = real_ref[:, 0:nc * ch]
        new_keys = real_ref[:, nc * ch:nc * ch + ps]
        thr_new = jnp.tile(thr, (1, ps // LANES))
        gt_new = count(new_keys > thr_new)
        ties_allowed = topk - gt_new - count(past_keys > jnp.tile(thr, (1, nc * ch // LANES)))
        ties_past = count(past_keys == jnp.tile(thr, (1, nc * ch // LANES)))
        ties_new_allowed = ties_allowed - jnp.minimum(ties_past, ties_allowed)
        tie = jnp.where(new_keys == thr_new, 1.0, 0.0)
        lane = lax.broadcasted_iota(I32, tie.shape, 1)
        rank = jnp.zeros(tie.shape, F32)
        for s in range(1, tq):
            rank = rank + jnp.where(lane >= s, pltpu.roll(tie, s, 1), 0.0)
        new_keys = jnp.where((tie > 0.0) & (rank >= ties_new_allowed), thr_new - 1, new_keys)
        budget = jnp.broadcast_to(topk - gt_new, (n_rows, LANES)).astype(I32)
        thr_ref[...] = jnp.zeros(thr_ref.shape, I32)
        new_ref[...] = jnp.full(new_ref.shape, INT_MIN, I32)
        for bb in range(n_rows // tq):
            new_ref[bb, 0:tq, :] = new_keys[bb * tq:(bb + 1) * tq]
            for jj in range(tq):
                thr_ref[bb * tq + jj, 0:1, :] = thr[bb * tq + jj:bb * tq + jj + 1, :]
                thr_ref[bb * tq + jj, 1:2, :] = budget[bb * tq + jj:bb * tq + jj + 1, :]


def _sample_select(page_table, qi_hm, w_hm, kin_t, cache_kidx_t, *, pages, tq, rq, iheads, topk):
    db, n_pages = page_table.shape
    _, idim, ps = cache_kidx_t.shape
    nc = n_pages // pages
    ch = pages * ps
    assert ps % LANES == 0 and (ch // LANES) % SUBLANES == 0 and (db * tq) % SUBLANES == 0
    page_spec = lambda p: pl.BlockSpec(
        (None, idim, ps), lambda b, c, pt: (pt[b, jnp.minimum(c, nc - 1) * pages + p], 0, 0))
    per_b = lambda shape: pl.BlockSpec((None,) + shape, lambda b, c, pt: (b, 0, 0))
    return pl.pallas_call(
        functools.partial(_sample_select_kernel, pages=pages, ps=ps, nc=nc, iheads=iheads, idim=idim, tq=tq, rq=rq,
                          topk=topk),
        grid_spec=pltpu.PrefetchScalarGridSpec(
            num_scalar_prefetch=1,
            grid=(db, nc + 1),
            in_specs=[per_b((iheads * rq, idim)), per_b((iheads * rq, LANES)), per_b((idim, ps))]
            + [page_spec(p) for p in range(pages)],
            out_specs=[pl.BlockSpec((tq, ch // LANES, LANES), lambda b, c, pt: (b, jnp.minimum(c, nc - 1), 0)),
                       pl.BlockSpec((db, rq, ps), lambda b, c, pt: (0, 0, 0)),
                       pl.BlockSpec((db * tq, SUBLANES, LANES), lambda b, c, pt: (0, 0, 0))],
            scratch_shapes=[pltpu.VMEM((db * rq, nc * ch + ps), I32), pltpu.VMEM((db * tq, nc * ch + ps), I32)],
        ),
        out_shape=[jax.ShapeDtypeStruct((db * tq, n_pages * ps // LANES, LANES), I32),
                   jax.ShapeDtypeStruct((db, rq, ps), I32),
                   jax.ShapeDtypeStruct((db * tq, SUBLANES, LANES), I32)],
        compiler_params=_cparams("arbitrary", "arbitrary"),
        name="sample_select",
    )(page_table, qi_hm, w_hm, kin_t, *([cache_kidx_t] * pages))


def _sc_gather_kernel(past_hbm, thr_hbm, pt_hbm, ck_hbm, cv_hbm, ksel_hbm, vsel_hbm, pos_hbm, cnt_hbm,
                      row_v, thr_v, pt_v, idx_v, phys_v, rows_v, cnt_v, sem,
                      *, nq, tq, topk, ps, n_cores, rows_per_copy):
    wid = lax.axis_index("s") * n_cores + lax.axis_index("c")

    @pl.when(wid < nq)
    def _():
        pltpu.sync_copy(past_hbm.at[wid], row_v)
        pltpu.sync_copy(thr_hbm.at[wid], thr_v)
        pltpu.sync_copy(pt_hbm.at[wid // tq], pt_v)
        thr = thr_v[0, pl.ds(0, SC_LANES)]
        budget = thr_v[1, pl.ds(0, SC_LANES)]
        lane = lax.iota(I32, SC_LANES)
        zero = jnp.zeros((SC_LANES,), I32)
        for t in range(idx_v.shape[0] // SC_LANES):
            idx_v[pl.ds(t * SC_LANES, SC_LANES)] = zero

        def compact(pred):
            def body(r, cnt):
                for t in range(LANES // SC_LANES):
                    x = row_v[r, pl.ds(t * SC_LANES, SC_LANES)]
                    m = pred(x, cnt)
                    rank = plsc.cumsum(jnp.where(m, 1, 0).astype(I32))
                    plsc.store_scatter(idx_v, [cnt + rank - 1], lane + (r * LANES + t * SC_LANES), mask=m)
                    cnt = cnt + plsc.all_reduce_population_count(m)
                return cnt
            return body

        cnt = lax.fori_loop(0, row_v.shape[0], compact(lambda x, cnt: x > thr), zero)
        cnt = lax.fori_loop(0, row_v.shape[0], compact(lambda x, cnt: (x == thr) & (cnt < budget)), cnt)
        cnt_v[...] = jnp.minimum(cnt, budget)
        pltpu.sync_copy(cnt_v, cnt_hbm.at[wid])
        pltpu.sync_copy(idx_v.at[pl.ds(0, topk)], pos_hbm.at[wid])

        shift = ps.bit_length() - 1
        for t in range(topk // SC_LANES):
            pos = idx_v[pl.ds(t * SC_LANES, SC_LANES)]
            page = plsc.load_gather(pt_v, [lax.shift_right_logical(pos, shift)])
            phys_v[pl.ds(t * SC_LANES, SC_LANES)] = page * ps + (pos & (ps - 1))
        for g in range(topk // rows_per_copy):
            sel = phys_v.at[pl.ds(g * rows_per_copy, rows_per_copy)]
            dst = pl.ds(wid * topk + g * rows_per_copy, rows_per_copy)
            for src_hbm, dst_hbm in ((ck_hbm, ksel_hbm), (cv_hbm, vsel_hbm)):
                pltpu.async_copy(src_hbm.at[sel], rows_v, sem).wait()
                pltpu.sync_copy(rows_v, dst_hbm.at[dst])


def _sc_gather(past_keys, thr, page_table, cache_k, cache_v, *, tq, topk, ps):
    nq, key_rows, _ = past_keys.shape
    _, heads, hd = cache_k.shape
    assert ps & (ps - 1) == 0 and topk % SC_LANES == 0
    rows_per_copy = 64
    assert topk % rows_per_copy == 0
    mesh = plsc.VectorSubcoreMesh(core_axis_name="c", subcore_axis_name="s", num_cores=V7X_SC_CORES,
                                  num_subcores=V7X_SC_SUBCORES)
    assert nq <= V7X_SC_CORES * V7X_SC_SUBCORES
    sel_shape = jax.ShapeDtypeStruct((nq * topk, heads, hd), F32)
    return pl.kernel(
        functools.partial(_sc_gather_kernel, nq=nq, tq=tq, topk=topk, ps=ps, n_cores=V7X_SC_CORES,
                          rows_per_copy=rows_per_copy),
        out_type=[sel_shape, sel_shape, jax.ShapeDtypeStruct((nq, topk), I32),
                  jax.ShapeDtypeStruct((nq, SC_LANES), I32)],
        mesh=mesh,
        scratch_types=[pltpu.VMEM((key_rows, LANES), I32), pltpu.VMEM((SUBLANES, LANES), I32),
                       pltpu.VMEM((page_table.shape[1],), I32), pltpu.VMEM((topk + SC_LANES,), I32),
                       pltpu.VMEM((topk,), I32), pltpu.VMEM((rows_per_copy, heads, hd), F32),
                       pltpu.VMEM((SC_LANES,), I32), pltpu.SemaphoreType.DMA],
        compiler_params=pltpu.CompilerParams(needs_layout_passes=False),
        name="sc_select_gather",
    )(past_keys, thr, page_table, cache_k, cache_v)


def _sample_attn_sel_kernel(cnt_ref, relbt_ref, q_ref, ksel_ref, vsel_ref, pos_ref, snew_ref, thr_ref, kn_ref, vn_ref,
                            o_ref, *, heads, tq, past, n_buckets):
    w = pl.program_id(0)
    j = w % tq
    q = q_ref[...]

    def head_bias(dist):
        bucket = _bucket(dist, n_buckets)
        acc = jnp.zeros(bucket.shape, F32)
        for bkt in range(n_buckets):
            acc = jnp.where(bucket == bkt, relbt_ref[:, bkt:bkt + 1], acc)
        return acc * LOG2E

    def logits(keys):
        lg = _dot_nt(q, keys)
        head = lax.broadcasted_iota(I32, lg.shape, 0)
        col = lax.broadcasted_iota(I32, lg.shape, 1)
        return lg, (col & (heads - 1)) == head, col

    lg, own, col = logits(ksel_ref[...].astype(BF16))
    keep = own & (col < cnt_ref[w, 0] * heads)
    lg = jnp.where(keep, lg + head_bias(jnp.broadcast_to(past + j - pos_ref[...], lg.shape)), NEG_BIG)
    lgn, own, col = logits(kn_ref[...])
    keep = own & (snew_ref[...] >= thr_ref[0:1, :])
    lgn = jnp.where(keep, lgn + head_bias(j - lax.shift_right_logical(col, heads.bit_length() - 1)), NEG_BIG)

    m = jnp.maximum(jnp.max(lg, axis=1, keepdims=True), jnp.max(lgn, axis=1, keepdims=True))
    p = jnp.exp2(lg - m)
    pn = jnp.exp2(lgn - m)
    denom = jnp.sum(p, axis=1, keepdims=True) + jnp.sum(pn, axis=1, keepdims=True)
    acc = _dot(p.astype(BF16), vsel_ref[...].astype(BF16)) + _dot(pn.astype(BF16), vn_ref[...])
    o_ref[...] = acc / denom


def _sample_attn_sel(cnt, rel_bias_t, q, ksel, vsel, pos, snew, thr, kn, vn, *, heads, hd, topk, tq, past):
    nq = q.shape[0]
    ps = kn.shape[1]
    assert ps == LANES and hd == LANES and heads & (heads - 1) == 0
    per_q = lambda shape: pl.BlockSpec((None,) + shape, lambda w, cnt: (w, 0, 0))
    per_b = lambda shape: pl.BlockSpec((None,) + shape, lambda w, cnt: (w // tq, 0, 0))
    sel_spec = pl.BlockSpec((topk * heads, hd), lambda w, cnt: (w, 0))
    return pl.pallas_call(
        functools.partial(_sample_attn_sel_kernel, heads=heads, tq=tq, past=past, n_buckets=rel_bias_t.shape[1]),
        grid_spec=pltpu.PrefetchScalarGridSpec(
            num_scalar_prefetch=1,
            grid=(nq,),
            in_specs=[pl.BlockSpec(rel_bias_t.shape, lambda w, cnt: (0, 0)), per_q((heads, hd)), sel_spec, sel_spec,
                      per_q((1, topk * heads)), per_q((1, ps)), per_q((SUBLANES, LANES)), per_b((ps, hd)),
                      per_b((ps, hd))],
            out_specs=per_q((heads, hd)),
        ),
        out_shape=jax.ShapeDtypeStruct((nq, heads, hd), F32),
        compiler_params=_cparams("arbitrary"),
        name="sample_attn_sel",
    )(cnt, rel_bias_t, q, ksel, vsel, pos, snew, thr, kn, vn)


def _mix_kernel(x_ref, xs_ref, g_ref, a_ref, as_ref, c_ref, cs_ref, w_hbm, wpa_ref, wpb_ref, wo_ref, o_ref, os_ref,
                wg_ref, *, d, row0):
    _stage_rows(w_hbm, wg_ref, row0)
    n = pl.num_programs(0) - 1

    def rows(x_ref, a_ref, c_ref, o_ref):
        x = x_ref[...]
        hb = _rms_bf16(x, g_ref[...])
        a = _dot(a_ref[...], wpa_ref[...])
        m = jax.nn.sigmoid(_dot_nt(hb, wg_ref[0:d, :])) * a
        c = _dot(c_ref[...], wpb_ref[...])
        m = m + jax.nn.sigmoid(_dot_nt(hb, wg_ref[d:2 * d, :])) * c
        o_ref[...] = x + _dot(m.astype(BF16), wo_ref[...])

    @pl.when(pl.program_id(0) < n)
    def _():
        rows(x_ref, a_ref, c_ref, o_ref)

    @pl.when(pl.program_id(0) == n)
    def _():
        rows(xs_ref, as_ref, cs_ref, os_ref)


def _mix(xp, xs, g, attn_p, attn_s, c_p, c_s, w_all, row0, w_pa, w_pb, w_o, *, tm):
    m, d = xp.shape
    ms = xs.shape[0]
    n = m // tm
    assert row0 % 16 == 0
    prow = lambda width: pl.BlockSpec((tm, width), lambda i: (jnp.minimum(i, n - 1), 0))
    srow = lambda width: pl.BlockSpec((ms, width), lambda i: (0, 0))
    return pl.pallas_call(
        functools.partial(_mix_kernel, d=d, row0=row0),
        grid=(n + 1,),
        in_specs=[prow(d), srow(d), _resident((1, d)), prow(attn_p.shape[1]), srow(attn_s.shape[1]),
                  prow(c_p.shape[1]), srow(c_s.shape[1]), pl.BlockSpec(memory_space=pl.ANY), _resident(w_pa.shape),
                  _resident(w_pb.shape), _resident(w_o.shape)],
        out_specs=[prow(d), srow(d)],
        out_shape=[jax.ShapeDtypeStruct((m, d), F32), jax.ShapeDtypeStruct((ms, d), F32)],
        scratch_shapes=[pltpu.VMEM((2 * d, d), BF16)],
        compiler_params=_cparams("arbitrary"),
        name="mix_out",
    )(xp, xs, g, attn_p, attn_s, c_p, c_s, w_all, w_pa, w_pb, w_o)


def _mlp_kernel(x_ref, xs_ref, g_ref, gf_ref, w1_ref, w2_ref, y_ref, ys_ref, h_ref, *, tm, final):
    f = pl.program_id(1)
    last_tile = pl.program_id(0) == pl.num_programs(0) - 1

    def start(x_ref, y_ref, h_rows):
        h_ref[h_rows, :] = _rms_bf16(x_ref[...], g_ref[...])
        y_ref[...] = jnp.zeros(y_ref.shape, F32)

    def finish(x_ref, y_ref):
        x2 = x_ref[...] + y_ref[...]
        if final:
            x2 = x2 * lax.rsqrt(jnp.mean(x2 * x2, axis=-1, keepdims=True) + EPS) * gf_ref[...]
        y_ref[...] = x2

    def ff(h):
        t = jnp.square(jnp.maximum(_dot(h, w1_ref[...]), 0.0))
        return _dot(t.astype(BF16), w2_ref[...])

    @pl.when(f == 0)
    def _():
        start(x_ref, y_ref, slice(0, tm))

    @pl.when((f == 0) & last_tile)
    def _():
        start(xs_ref, ys_ref, slice(tm, h_ref.shape[0]))

    @pl.when(jnp.logical_not(last_tile))
    def _():
        y_ref[...] += ff(h_ref[0:tm, :])

    @pl.when(last_tile)
    def _():
        r = ff(h_ref[...])
        y_ref[...] += r[0:tm]
        ys_ref[...] += r[tm:]

    @pl.when(f == pl.num_programs(1) - 1)
    def _():
        finish(x_ref, y_ref)

    @pl.when((f == pl.num_programs(1) - 1) & last_tile)
    def _():
        finish(xs_ref, ys_ref)


def _mlp(xp, xs, g, gf, w1, w2, *, tm, tf, final):
    m, d = xp.shape
    ms = xs.shape[0]
    ff = w1.shape[1]
    srow = pl.BlockSpec((ms, d), lambda i, f: (0, 0))
    return pl.pallas_call(
        functools.partial(_mlp_kernel, tm=tm, final=final),
        grid=(m // tm, ff // tf),
        in_specs=[pl.BlockSpec((tm, d), lambda i, f: (i, 0)), srow, _resident((1, d)), _resident((1, d)),
                  pl.BlockSpec((d, tf), lambda i, f: (0, f)), pl.BlockSpec((tf, d), lambda i, f: (f, 0))],
        out_specs=[pl.BlockSpec((tm, d), lambda i, f: (i, 0)), srow],
        out_shape=[jax.ShapeDtypeStruct((m, d), F32), jax.ShapeDtypeStruct((ms, d), F32)],
        scratch_shapes=[pltpu.VMEM((tm + ms, d), BF16)],
        compiler_params=_cparams("arbitrary", "arbitrary"),
        name="mlp",
    )(xp, xs, g, gf, w1, w2)


def _tile(m, cap):
    return min(m, cap)


def kernel(x_prompt, x_sample, cache_k, cache_v, cache_kidx, state_conv, page_table, rel_bias, norm_mix_g, w_in,
           conv_w, w_pa, w_pb, w_o, norm_mlp_g, w_mlp_in, w_mlp_out, norm_final_g):
    batch, seq, d = x_prompt.shape
    db, tq, _ = x_sample.shape
    depth, n_pool, ps, heads, hd = cache_k.shape
    idim = cache_kidx.shape[-1]
    cw = conv_w.shape[-1]
    aw = heads * hd
    n_in = w_in.shape[-1]
    iheads = (n_in - 3 * aw - idim - 3 * cw - 2 * d) // (idim + 1)
    iw = iheads * idim
    assert 3 * aw + iw + idim + iheads + 3 * cw + 2 * d == n_in and idim + iheads <= LANES
    n_pages = page_table.shape[1]
    past = n_pages * ps
    rq = SUBLANES
    assert tq <= rq

    mp, ms = batch * seq, db * tq
    xp = x_prompt.reshape(mp, d)
    xs = x_sample.reshape(ms, d)
    tb = _tile(seq, 256)
    o_small = 3 * aw + iw
    o_conv = o_small + idim + iheads
    o_gate = o_conv + 3 * cw
    gf = norm_final_g.reshape(1, d)
    bias_tiles, bias_max = _bias_tiles(rel_bias, tb=tb)

    outs = {k: [] for k in ("kp", "vp", "kip", "sp", "ks", "vs", "kis", "ss")}
    for l in range(depth):
        wl = jnp.swapaxes(w_in[l], 0, 1).astype(BF16)
        g_mix = norm_mix_g[l].reshape(1, d)
        g_mlp = norm_mlp_g[l].reshape(1, d)

        tm = _tile(seq, 512)
        q, k, v, kb, vb, qi, sm, smb, kn = _proj_attn(xp, g_mix, wl, aw=aw, iw=iw, n_small=idim + iheads, hd=hd, tm=tm)
        c_in, u_tail, wpa, wpb, wo = _proj_conv(xp, g_mix, wl, o_conv, conv_w[l], None, (w_pa[l], w_pb[l], w_o[l]),
                                                seq=seq, tm=_tile(seq, 512))
        attn, w1, w2 = _attn_prompt(bias_max, bias_tiles, qi, sm, smb, q, kb, vb, kn, (w_mlp_in[l], w_mlp_out[l]),
                                    batch=batch, seq=seq, tb=tb, heads=heads, hd=hd, iheads=iheads, idim=idim,
                                    topk=min(TOPK_MAX, seq // 4))
        attn_p, c_in_p = attn, c_in
        outs["kp"].append(k.reshape(batch, seq, heads, hd))
        outs["vp"].append(v.reshape(batch, seq, heads, hd))
        outs["kip"].append(sm[:, :idim].reshape(batch, seq, idim))
        outs["sp"].append(u_tail.reshape(batch, -1, SUBLANES, cw)[:, -1, SUBLANES - 2:])

        q, k, v, kb, vb, qi, sm, smb, _ = _proj_attn(xs, g_mix, wl, aw=aw, iw=iw, n_small=idim + iheads, hd=hd, tm=ms)
        st = state_conv[l]
        zero = jnp.zeros((db, tq - 1, cw), F32)
        prev1 = jnp.concatenate([st[:, 1:2], zero], axis=1).reshape(ms, cw)
        prev2 = jnp.concatenate([st, zero[:, 1:]], axis=1).reshape(ms, cw)
        c_in, u_all = _proj_conv(xs, g_mix, wl, o_conv, conv_w[l], (prev1, prev2), seq=tq, tm=ms)

        def pad_rows(a, n):
            return jnp.pad(a, ((0, 0), (0, n - a.shape[1])) + ((0, 0),) * (a.ndim - 2))

        qi_hm = pad_rows(qi.reshape(db, tq, iheads, idim).transpose(0, 2, 1, 3).reshape(db * iheads, tq, idim), rq)
        qi_hm = qi_hm.reshape(db, iheads * rq, idim)
        w_hm = pad_rows(sm[:, idim:idim + iheads].reshape(db, tq, iheads).transpose(0, 2, 1).reshape(db * iheads, tq), rq)
        w_hm = jnp.broadcast_to(w_hm.reshape(db, iheads * rq, 1), (db, iheads * rq, LANES))
        kin_t = jnp.swapaxes(pad_rows(smb[:, :idim].reshape(db, tq, idim), ps), 1, 2)
        topk_s = min(TOPK_MAX, (past + tq) // 4)
        past_keys, snew, thr = _sample_select(page_table, qi_hm, w_hm, kin_t, jnp.swapaxes(cache_kidx[l], 1, 2),
                                              pages=math.gcd(n_pages, 32), tq=tq, rq=rq, iheads=iheads, topk=topk_s)
        ksel, vsel, pos, cnt = _sc_gather(past_keys, thr, page_table, cache_k[l].reshape(n_pool * ps, heads, hd),
                                          cache_v[l].reshape(n_pool * ps, heads, hd), tq=tq, topk=topk_s, ps=ps)
        pos_rows = jnp.repeat(pos, heads, axis=1).reshape(ms, 1, topk_s * heads)
        snew_rows = jnp.repeat(snew[:, :tq, :ps // heads], heads, axis=2).reshape(ms, 1, ps)
        attn = _sample_attn_sel(cnt, rel_bias.T, q.reshape(ms, heads, hd), ksel.reshape(ms * topk_s * heads, hd),
                                vsel.reshape(ms * topk_s * heads, hd), pos_rows, snew_rows, thr,
                                pad_rows(kb.reshape(db, tq * heads, hd), ps), pad_rows(vb.reshape(db, tq * heads, hd), ps),
                                heads=heads, hd=hd, topk=topk_s, tq=tq, past=past)
        attn = attn.reshape(ms, aw).astype(BF16)

        x1p, x1s = _mix(xp, xs, g_mix, attn_p, attn, c_in_p, c_in, wl, o_gate, wpa, wpb, wo, tm=_tile(mp, 256))
        xp_next, xs_next = _mlp(x1p, x1s, g_mlp, gf, w1, w2, tm=_tile(mp, 1024), tf=_tile(w1.shape[1], 512),
                                final=l == depth - 1)
        outs["ks"].append(k.reshape(db, tq, heads, hd))
        outs["vs"].append(v.reshape(db, tq, heads, hd))
        outs["kis"].append(sm[:, :idim].reshape(db, tq, idim))
        outs["ss"].append(u_all.reshape(db, tq, cw)[:, tq - 2:])
        xp, xs = xp_next, xs_next

    st = {k: jnp.stack(v) for k, v in outs.items()}
    return (xp.reshape(batch, seq, d), xs.reshape(db, tq, d), st["kp"], st["vp"], st["kip"], st["sp"],
            st["ks"], st["vs"], st["kis"], st["ss"])
```

```python
import functools
import math

import jax
import jax.numpy as jnp
import numpy as np
from jax import lax
from jax.experimental import pallas as pl
from jax.experimental.pallas import tpu as pltpu
from jax.experimental.pallas import tpu_sc as plsc

F32 = jnp.float32
BF16 = jnp.bfloat16
I32 = jnp.int32
I16 = jnp.int16
HALF = 1 << 15

TOPK_MAX = 256
MAX_DISTANCE = 128
EPS = 1e-6

LANES = 128
SUBLANES = 8
V7X_SCOPED_VMEM_BYTES = 60000 * 1024
SC_LANES = 16
V7X_SC_CORES = 2
V7X_SC_SUBCORES = 16

LOG2E = math.log2(math.e)
INT_MIN = np.int32(-2 ** 31)
NEG_BIG = -1e30
SOFTMAX_MIN_MASS = 2.0 ** -60


def _cparams(*sem):
    return pltpu.CompilerParams(dimension_semantics=sem, vmem_limit_bytes=V7X_SCOPED_VMEM_BYTES)


def _resident(shape):
    nd = len(shape)
    return pl.BlockSpec(shape, lambda *_: (0,) * nd, pipeline_mode=pl.Buffered(1))


def _rms_bf16(x, g):
    y = x * lax.rsqrt(jnp.mean(x * x, axis=-1, keepdims=True) + EPS)
    return (y * g).astype(BF16)


def _dot(a, b):
    return jnp.dot(a, b, preferred_element_type=F32)


def _dot_nt(a, b):
    return lax.dot_general(a, b, (((1,), (1,)), ((), ())), preferred_element_type=F32)


def _ordered_key(x):
    b = lax.bitcast_convert_type(x, I32)
    return b ^ ((b >> 31) & np.int32(0x7FFFFFFF))


def _bucket(n, n_buckets):
    n = jnp.maximum(n, 0)
    me = n_buckets // 2
    nf = jnp.maximum(n, me).astype(F32)
    large = me + (jnp.log(nf / me) / math.log(MAX_DISTANCE / me) * (n_buckets - me)).astype(I32)
    large = jnp.minimum(large, n_buckets - 1)
    return jnp.where(n < me, n, large)


def _bias_of_bucket(bucket, relb_ref, h, n_buckets):
    acc = jnp.zeros(bucket.shape, F32)
    for bkt in range(n_buckets):
        acc = jnp.where(bucket == bkt, relb_ref[bkt, h], acc)
    return acc


def _fold_lanes(x):
    acc = x[:, 0:LANES]
    for t in range(1, x.shape[1] // LANES):
        acc = acc + x[:, t * LANES:(t + 1) * LANES]
    return acc


def _fold_rows(x, op=jnp.add, rows=SUBLANES):
    parts = [x[t * rows:(t + 1) * rows, :] for t in range(x.shape[0] // rows)]
    while len(parts) > 1:
        parts = [op(parts[t], parts[t + 1]) for t in range(0, len(parts) - 1, 2)] + parts[len(parts) & ~1:]
    return parts[0]


def _bisect(count_ge, shape, axis, k, trips, bits, at_none):
    def bit_body(t, carry):
        u, at = carry
        cand = u | jnp.left_shift(np.int32(1), bits - 1 - t)
        cnt = jnp.sum(count_ge(cand), axis=axis, keepdims=True)
        return jnp.where(cnt >= k, cand, u), jnp.where(cnt >= k, jnp.broadcast_to(cnt, shape), at)

    return lax.fori_loop(0, trips, bit_body, (jnp.zeros(shape, I32), at_none))


def _kth_threshold(count_ge, shape, axis, k, trips):
    uthr, at_thr = _bisect(lambda u: count_ge(u ^ INT_MIN), shape, axis, k, trips, 32, jnp.full(shape, k, F32))
    return jnp.maximum(uthr ^ INT_MIN, INT_MIN + 1), at_thr - k


def _flash_update(lg, v, m_ref, l_ref, acc_ref, h, hd):
    reps = lg.shape[1] // LANES
    m_prev = m_ref[h]
    m_new = jnp.maximum(m_prev, jnp.max(lg, axis=1, keepdims=True))
    p = jnp.exp2(lg - jnp.tile(m_new, (1, reps)))
    alpha = jnp.exp2(m_prev - m_new)
    pv = _dot(p.astype(BF16), jnp.concatenate([v, jnp.ones((v.shape[0], LANES), BF16)], axis=1))
    l_ref[h] = alpha * l_ref[h] + pv[:, hd:]
    m_ref[h] = m_new
    hs = slice(h * hd, (h + 1) * hd)
    acc_ref[:, hs] = acc_ref[:, hs] * jnp.tile(alpha, (1, hd // LANES)) + pv[:, :hd]


def _stage_rows(w_hbm, w_ref, row0):
    @pl.when(pl.program_id(0) == 0)
    def _():
        pltpu.sync_copy(w_hbm.at[pl.ds(row0, w_ref.shape[0])], w_ref)


def _proj_attn_kernel(x_ref, g_ref, w_hbm, q_ref, k_ref, v_ref, kb_ref, vb_ref, qi_ref,
                      sm_ref, smb_ref, kn_ref, w_ref, ws_ref, *, aw, iw, hd, tm, n_small, qscale):
    heads = aw // hd
    _stage_rows(w_hbm, w_ref, 0)

    @pl.when(pl.program_id(0) == 0)
    def _():
        ws_ref[...] = jnp.zeros(ws_ref.shape, BF16)
        pltpu.sync_copy(w_hbm.at[pl.ds(w_ref.shape[0], n_small)], ws_ref.at[pl.ds(0, n_small)])

    hb = _rms_bf16(x_ref[...], g_ref[...])
    q_ref[...] = (_dot_nt(hb, w_ref[0:aw, :]) * qscale).astype(BF16)
    for o_ref, ob_ref, c0 in ((k_ref, kb_ref, aw), (v_ref, vb_ref, 2 * aw)):
        kv = _dot_nt(hb, w_ref[c0:c0 + aw, :])
        ob_ref[...] = kv.astype(BF16)
        if o_ref is k_ref:
            kf = kv.astype(BF16).astype(F32)
            lane = lax.broadcasted_iota(I32, (tm, LANES), 1)
            kn = jnp.zeros((tm, LANES), F32)
            for h in range(heads):
                kh = kf[:, h * hd:(h + 1) * hd]
                kn = jnp.where(lane == h, jnp.sum(kh * kh, axis=1, keepdims=True), kn)
            kn_ref[...] = kn
        for h in range(heads):
            o_ref[pl.ds(h, tm, stride=heads), :] = kv[:, h * hd:(h + 1) * hd]
    qi_ref[...] = _dot_nt(hb, w_ref[3 * aw:3 * aw + iw, :]).astype(BF16)
    sm = _dot_nt(hb, ws_ref[...])
    sm_ref[...] = sm
    smb_ref[...] = sm.astype(BF16)


def _proj_attn(x2d, g, w_all, *, aw, iw, n_small, hd, tm):
    m, d = x2d.shape
    wt = 3 * aw + iw
    heads = aw // hd
    assert hd == LANES and n_small <= LANES and wt % 16 == 0 and n_small % 16 == 0
    row = lambda width: pl.BlockSpec((tm, width), lambda i: (i, 0))
    kv_spec = pl.BlockSpec((tm * heads, hd), lambda i: (i, 0))
    kv_shape = jax.ShapeDtypeStruct((m * heads, hd), F32)
    outs = [(aw, BF16), None, None, (aw, BF16), (aw, BF16), (iw, BF16), (LANES, F32), (LANES, BF16), (LANES, F32)]
    return pl.pallas_call(
        functools.partial(_proj_attn_kernel, aw=aw, iw=iw, hd=hd, tm=tm, n_small=n_small,
                          qscale=hd ** -0.5 * LOG2E),
        grid=(m // tm,),
        in_specs=[row(d), _resident((1, d)), pl.BlockSpec(memory_space=pl.ANY)],
        out_specs=[kv_spec if o is None else row(o[0]) for o in outs],
        out_shape=[kv_shape if o is None else jax.ShapeDtypeStruct((m, o[0]), o[1]) for o in outs],
        scratch_shapes=[pltpu.VMEM((wt, d), BF16), pltpu.VMEM((LANES, d), BF16)],
        compiler_params=_cparams("arbitrary"),
        name="proj_attn",
    )(x2d, g, w_all)


def _proj_conv_kernel(*refs, cw, tm, seq, tiles_per_seq, tail, row0, n_cast):
    if tiles_per_seq:
        x_ref, g_ref, w_hbm, cwt_ref = refs[:4]
        c_ref, ut_ref = refs[4 + n_cast:6 + n_cast]
        w_ref, carry_ref = refs[6 + 2 * n_cast:]
        _cast_slabs(refs[4:4 + n_cast], refs[6 + n_cast:6 + 2 * n_cast])
    else:
        x_ref, g_ref, w_hbm, cwt_ref, p1_ref, p2_ref, c_ref, ut_ref, w_ref = refs
    _stage_rows(w_hbm, w_ref, row0)
    hb = _rms_bf16(x_ref[...], g_ref[...])
    cx = _dot_nt(hb, w_ref[0:cw, :])
    cb = _dot_nt(hb, w_ref[cw:2 * cw, :])
    cc = _dot_nt(hb, w_ref[2 * cw:3 * cw, :])
    u = cc * cx
    r = lax.broadcasted_iota(I32, (tm, cw), 0)
    um1 = pltpu.roll(u, 1, 0)
    um2 = pltpu.roll(u, 2, 0)
    if tiles_per_seq:
        @pl.when(pl.program_id(0) % tiles_per_seq == 0)
        def _():
            carry_ref[...] = jnp.zeros_like(carry_ref)
        prev1 = jnp.broadcast_to(carry_ref[SUBLANES - 1:SUBLANES, :], (tm, cw))
        prev2 = jnp.broadcast_to(carry_ref[SUBLANES - 2:SUBLANES - 1, :], (tm, cw))
        um1 = jnp.where(r == 0, prev1, um1)
        um2 = jnp.where(r == 0, prev2, jnp.where(r == 1, prev1, um2))
        carry_ref[...] = u[tm - SUBLANES:tm, :]
    else:
        assert seq & (seq - 1) == 0
        t = r & (seq - 1)
        um1 = jnp.where(t >= 1, um1, p1_ref[...])
        um2 = jnp.where(t >= 2, um2, p2_ref[...])
    y = cwt_ref[0:1, :] * um2 + cwt_ref[1:2, :] * um1 + cwt_ref[2:3, :] * u
    c_ref[...] = (cb * y).astype(BF16)
    ut_ref[...] = u[tm - tail:tm, :]


def _proj_conv(x2d, g, w_all, row0, conv_w, prev, to_cast=(), *, seq, tm):
    m, d = x2d.shape
    cw = conv_w.shape[1]
    assert conv_w.shape[0] == 3 and row0 % 16 == 0
    row = lambda width: pl.BlockSpec((tm, width), lambda i: (i, 0))
    in_specs = [row(d), _resident((1, d)), pl.BlockSpec(memory_space=pl.ANY), _resident((3, cw))]
    args = [x2d, g, w_all, conv_w]
    scratch = [pltpu.VMEM((3 * cw, d), BF16)]
    if prev is None:
        assert seq % tm == 0
        tiles_per_seq, tail = seq // tm, SUBLANES
        scratch += [pltpu.VMEM((SUBLANES, cw), F32)]
    else:
        assert tm % seq == 0 and seq >= 2
        tiles_per_seq, tail = 0, tm
        in_specs += [row(cw), row(cw)]
        args += list(prev)
    assert not (to_cast and prev is not None)
    c_in, c_out, c_shapes = _cast_specs(to_cast, m // tm, lambda i: i)
    return pl.pallas_call(
        functools.partial(_proj_conv_kernel, cw=cw, tm=tm, seq=seq, tiles_per_seq=tiles_per_seq, tail=tail,
                          row0=row0, n_cast=len(to_cast)),
        grid=(m // tm,),
        in_specs=in_specs + c_in,
        out_specs=[row(cw), pl.BlockSpec((tail, cw), lambda i: (i, 0))] + c_out,
        out_shape=[jax.ShapeDtypeStruct((m, cw), BF16), jax.ShapeDtypeStruct((m // tm * tail, cw), F32)] + c_shapes,
        scratch_shapes=scratch,
        compiler_params=_cparams("arbitrary"),
        name="proj_conv",
    )(*args, *to_cast)


def _bias_tiles_kernel(relb_ref, o_ref, bmax_ref, *, tb, n_buckets):
    kind = pl.program_id(0)
    h = pl.program_id(1)
    qry = lax.broadcasted_iota(I32, (tb, tb), 0)
    key = lax.broadcasted_iota(I32, (tb, tb), 1)
    bias = _bias_of_bucket(_bucket(kind * tb + qry - key, n_buckets), relb_ref, h, n_buckets)
    o_ref[...] = (bias - relb_ref[n_buckets - 1, h]) * LOG2E
    bmax = relb_ref[0, h]
    for bkt in range(1, n_buckets):
        bmax = jnp.maximum(bmax, relb_ref[bkt, h])
    bmax_ref[...] = jnp.full(bmax_ref.shape, (bmax - relb_ref[n_buckets - 1, h]) * LOG2E, F32)


def _bias_tiles(rel_bias, *, tb):
    n_buckets, heads = rel_bias.shape
    return pl.pallas_call(
        functools.partial(_bias_tiles_kernel, tb=tb, n_buckets=n_buckets),
        grid=(2, heads),
        in_specs=[pl.BlockSpec(memory_space=pltpu.SMEM)],
        out_specs=[pl.BlockSpec((None, None, tb, tb), lambda a, h: (a, h, 0, 0)),
                   pl.BlockSpec((None, None, SUBLANES, LANES), lambda a, h: (a, h, 0, 0))],
        out_shape=[jax.ShapeDtypeStruct((2, heads, tb, tb), F32),
                   jax.ShapeDtypeStruct((2, heads, SUBLANES, LANES), F32)],
        compiler_params=_cparams("arbitrary", "arbitrary"),
        name="bias_tiles",
    )(rel_bias)


def _cast_specs(weights, n_steps, step_of):
    in_specs, out_specs, out_shapes = [], [], []
    for w in weights:
        rows, cols = w.shape
        assert rows % n_steps == 0 and (rows // n_steps) % 16 == 0
        spec = pl.BlockSpec((rows // n_steps, cols), lambda *idx: (step_of(*idx), 0))
        in_specs.append(spec)
        out_specs.append(spec)
        out_shapes.append(jax.ShapeDtypeStruct((rows, cols), BF16))
    return in_specs, out_specs, out_shapes


def _cast_slabs(src_refs, dst_refs):
    for src, dst in zip(src_refs, dst_refs):
        dst[...] = src[...].astype(BF16)


def _attn_prompt_kernel(*refs, n_cast, tb, heads, hd, iheads, idim, topk):
    bmax_ref, qi_ref, sm_ref, smb_ref, q_ref, kb_ref, vb_ref, kn_ref, bias_ref = refs[:9]
    o_ref = refs[9 + n_cast]
    (skey_ref, hi_ref, lo_ref, thr_ref, wt_ref, madd_ref, kmax_ref, shift_ref, m_ref, l_ref,
     acc_ref) = refs[10 + 2 * n_cast:]
    _cast_slabs(refs[9:9 + n_cast], refs[10 + n_cast:10 + 2 * n_cast])
    i = pl.program_id(1)
    wscale = idim ** -0.5 * iheads ** -0.5

    l_ref[...] = jnp.zeros(l_ref.shape, F32)
    acc_ref[...] = jnp.zeros(acc_ref.shape, F32)
    wt_ref[...] = sm_ref[...].T * wscale

    def chunk(j):
        return pl.ds(pl.multiple_of(j * tb, tb), tb)

    def score_chunk(j, diag):
        kic = smb_ref[chunk(j), 0:idim]
        acc = jnp.zeros((tb, tb), F32)
        for h in range(iheads):
            s = _dot_nt(kic, qi_ref[:, h * idim:(h + 1) * idim])
            acc = acc + jnp.maximum(s, 0.0) * wt_ref[idim + h:idim + h + 1, :]
        key = _ordered_key(acc)
        if diag:
            kpos = lax.broadcasted_iota(I32, (tb, tb), 0)
            qpos = lax.broadcasted_iota(I32, (tb, tb), 1)
            key = jnp.where(kpos > qpos, INT_MIN, key)
        skey_ref[chunk(j), :] = key
        hi_ref[chunk(j), :] = (key >> 16).astype(I16)
        lo_ref[chunk(j), :] = ((key & 0xFFFF) - HALF).astype(I16)

    def score_body(j, carry):
        score_chunk(j, False)
        return carry

    lax.fori_loop(0, i, score_body, 0)
    score_chunk(i, True)

    def counter(half_ref):
        def count_ge(u):
            cand = (u - HALF).astype(I16)[0:1, :]

            def body(j, cnt):
                ge = jnp.where(half_ref[chunk(j), :] >= cand, jnp.int16(1), jnp.int16(0))
                return cnt + _fold_rows(ge, rows=2 * SUBLANES)

            return lax.fori_loop(0, i + 1, body, jnp.zeros((2 * SUBLANES, tb), I16)).astype(F32)

        return count_ge

    shape = (SUBLANES, tb)
    trips = jnp.where((i + 1) * tb <= topk, 0, 16)
    u_hi, n_ge_hi = _bisect(counter(hi_ref), shape, 0, float(topk), trips, 16, jnp.full(shape, topk, F32))
    t_hi = (u_hi - HALF).astype(I16)[0:1, :]
    n_gt_hi = jnp.where(u_hi == 2 * HALF - 1, 0.0,
                        jnp.sum(counter(hi_ref)(jnp.minimum(u_hi + 1, 2 * HALF - 1)), axis=0, keepdims=True))

    def keep_low_of_kth_high(j, carry):
        lo_ref[chunk(j), :] = jnp.where(hi_ref[chunk(j), :] == t_hi, lo_ref[chunk(j), :], jnp.int16(-HALF))
        return carry

    lax.fori_loop(0, i + 1, keep_low_of_kth_high, 0)
    u_lo, n_ge_lo = _bisect(counter(lo_ref), shape, 0, topk - n_gt_hi, trips, 16, n_ge_hi - n_gt_hi)
    thr = jnp.maximum(jnp.left_shift(u_hi - HALF, 16) | u_lo, INT_MIN + 1)
    surplus = n_gt_hi + n_ge_lo - topk
    thr_ref[...] = thr

    @pl.when(jnp.max(surplus) > 0.0)
    def _():
        thr_row = thr_ref[0:1, :]

        def gt_body(j, cnt):
            return cnt + _fold_rows(jnp.where(skey_ref[chunk(j), :] > thr_row, 1.0, 0.0))

        above = lax.fori_loop(0, i + 1, gt_body, jnp.zeros((SUBLANES, tb), F32))
        budget = topk - jnp.sum(above, axis=0, keepdims=True)
        kpos = lax.broadcasted_iota(I32, (tb, tb), 0)
        earlier = jnp.where(kpos > lax.broadcasted_iota(I32, (tb, tb), 1), 1.0, 0.0).astype(BF16)

        def fix_body(j, seen):
            sk = skey_ref[chunk(j), :]
            tie = jnp.where(sk == thr_row, 1.0, 0.0)
            rank = seen[0:1, :] + _dot(earlier, tie.astype(BF16))
            skey_ref[chunk(j), :] = jnp.where((tie > 0.0) & (rank >= budget), thr_row - 1, sk)
            return seen + jnp.sum(_fold_rows(tie), axis=0, keepdims=True)

        lax.fori_loop(0, i + 1, fix_body, jnp.zeros((SUBLANES, tb), F32))

    @pl.when(i == 0)
    def _():
        kmax_ref[...] = jnp.zeros(kmax_ref.shape, F32)

    kmax_ref[...] = jnp.maximum(kmax_ref[...], jnp.max(kn_ref[chunk(i), :], axis=0, keepdims=True))
    qsq = q_ref[...] * q_ref[...]
    for h in range(heads):
        qn2 = _dot(qsq[:, h * hd:(h + 1) * hd], jnp.ones((hd, LANES), BF16)) * (1.0 + 2.0 ** -6)
        shift_ref[h] = jnp.sqrt(qn2 * kmax_ref[0:1, h:h + 1]) + bmax_ref[0, h][0:1, :]

    def logits(j, h, kind):
        hs = slice(h * hd, (h + 1) * hd)
        lg = _dot_nt(q_ref[:, hs], kb_ref[chunk(j), hs]) + madd_ref[...]
        return lg if kind is None else lg + bias_ref[kind, h]

    def bounded_chunk(j, kind):
        for h in range(heads):
            hs = slice(h * hd, (h + 1) * hd)
            p = jnp.exp2(logits(j, h, kind) - jnp.tile(shift_ref[h], (1, tb // LANES)))
            v1 = jnp.concatenate([vb_ref[chunk(j), hs], jnp.ones((tb, LANES), BF16)], axis=1)
            pv = _dot(p.astype(BF16), v1)
            l_ref[h] += pv[:, hd:]
            acc_ref[:, hs] += pv[:, :hd]

    def running_max_chunk(j, kind):
        for h in range(heads):
            _flash_update(logits(j, h, kind), vb_ref[chunk(j), h * hd:(h + 1) * hd], m_ref, l_ref, acc_ref, h, hd)

    def attend(update):
        def one(j, kind):
            madd_ref[...] = jnp.where(skey_ref[chunk(j), :] >= thr_ref[0:1, :], 0.0, NEG_BIG).T
            update(j, kind)

        def far_body(j, carry):
            one(j, None)
            return carry

        lax.fori_loop(0, jnp.maximum(i - 1, 0), far_body, 0)

        @pl.when(i >= 1)
        def _():
            one(i - 1, 1)

        one(i, 0)

    attend(bounded_chunk)

    @pl.when(jnp.min(l_ref[...]) < SOFTMAX_MIN_MASS)
    def _():
        m_ref[...] = jnp.full(m_ref.shape, NEG_BIG, F32)
        l_ref[...] = jnp.zeros(l_ref.shape, F32)
        acc_ref[...] = jnp.zeros(acc_ref.shape, F32)
        attend(running_max_chunk)

    for h in range(heads):
        hs = slice(h * hd, (h + 1) * hd)
        o_ref[:, hs] = (acc_ref[:, hs] / jnp.tile(l_ref[h], (1, hd // LANES))).astype(BF16)


def _attn_prompt(bias_max, bias_tiles, qi, sm, smb, q, kb, vb, kn, to_cast, *, batch, seq, tb, heads, hd, iheads,
                 idim, topk):
    m, aw = q.shape
    iw = qi.shape[1]
    nq = seq // tb
    assert tb >= MAX_DISTANCE and tb % LANES == 0 and hd % LANES == 0
    qrow = lambda width: pl.BlockSpec((tb, width), lambda b, i: (b * nq + i, 0))
    seqblk = lambda width: pl.BlockSpec((seq, width), lambda b, i: (b, 0))
    c_in, c_out, c_shapes = _cast_specs(to_cast, batch * nq, lambda b, i: b * nq + i)
    return pl.pallas_call(
        functools.partial(_attn_prompt_kernel, n_cast=len(to_cast), tb=tb, heads=heads, hd=hd, iheads=iheads,
                          idim=idim, topk=topk),
        grid=(batch, nq),
        in_specs=[_resident(bias_max.shape), qrow(iw), qrow(LANES), seqblk(LANES), qrow(aw), seqblk(aw),
                  seqblk(aw), seqblk(LANES), _resident(bias_tiles.shape)] + c_in,
        out_specs=[qrow(aw)] + c_out,
        out_shape=[jax.ShapeDtypeStruct((m, aw), BF16)] + c_shapes,
        scratch_shapes=[pltpu.VMEM((seq, tb), I32), pltpu.VMEM((seq, tb), I16), pltpu.VMEM((seq, tb), I16),
                        pltpu.VMEM((SUBLANES, tb), I32),
                        pltpu.VMEM((LANES, tb), F32), pltpu.VMEM((tb, tb), F32),
                        pltpu.VMEM((SUBLANES, LANES), F32), pltpu.VMEM((heads, tb, LANES), F32),
                        pltpu.VMEM((heads, tb, LANES), F32), pltpu.VMEM((heads, tb, LANES), F32),
                        pltpu.VMEM((tb, aw), F32)],
        compiler_params=_cparams("arbitrary", "arbitrary"),
        name="attn_prompt",
    )(bias_max, qi, sm, smb, q, kb, vb, kn, bias_tiles, *to_cast)


def _sample_select_kernel(pt_ref, qi_ref, w_ref, kin_ref, *rest, pages, ps, nc, iheads, idim, tq, rq, topk):
    page_refs = rest[:pages]
    past_ref, new_ref, thr_ref, row_ref, real_ref = rest[pages:]
    b = pl.program_id(0)
    c = pl.program_id(1)
    db = pl.num_programs(0)
    ch = pages * ps
    rh = qi_ref.shape[0] // iheads
    wscale = idim ** -0.5 * iheads ** -0.5
    rows_b = pl.ds(pl.multiple_of(b * rq, rq), rq)

    def score(keys_t):
        s = _dot(qi_ref[...], keys_t)
        t = jnp.maximum(s, 0.0) * (w_ref[:, 0:1] * wscale)
        acc = _fold_rows(t, rows=rq)
        step = rh
        while step < rq:
            acc = acc + pltpu.roll(acc, step, 0)
            step *= 2
        return _ordered_key(acc)

    @pl.when(c < nc)
    def _():
        for p in range(pages):
            key = score(page_refs[p][...].astype(BF16))
            row_ref[rows_b, pl.ds(pl.multiple_of(c * ch + p * ps, LANES), ps)] = key
            for t in range(ps // LANES):
                for j in range(tq):
                    past_ref[j, p * (ps // LANES) + t:p * (ps // LANES) + t + 1, :] = \
                        key[j:j + 1, t * LANES:(t + 1) * LANES]

    @pl.when(c == nc)
    def _():
        key = score(kin_ref[...])
        j = lax.broadcasted_iota(I32, (rq, ps), 0)
        n = lax.broadcasted_iota(I32, (rq, ps), 1)
        key = jnp.where((n <= j) & (n < tq), key, INT_MIN)
        row_ref[rows_b, nc * ch:nc * ch + ps] = key

    @pl.when((c == nc) & (b == db - 1))
    def _():
        n_rows = real_ref.shape[0]
        for bb in range(n_rows // tq):
            real_ref[bb * tq:(bb + 1) * tq, :] = row_ref[bb * rq:bb * rq + tq, :]

        def count_ge(scand):
            sk = real_ref[...]
            return _fold_lanes(jnp.where(sk >= jnp.tile(scand, (1, sk.shape[1] // LANES)), 1.0, 0.0))

        thr, _ = _kth_threshold(count_ge, (n_rows, LANES), 1, float(topk), 32)
        count = lambda m: jnp.sum(_fold_lanes(jnp.where(m, 1.0, 0.0)), axis=1, keepdims=True)
        past_keys = real_ref[:, 0:nc * ch]
        new_keys = real_ref[:, nc * ch:nc * ch + ps]
        thr_new = jnp.tile(thr, (1, ps // LANES))
        gt_new = count(new_keys > thr_new)
        ties_allowed = topk - gt_new - count(past_keys > jnp.tile(thr, (1, nc * ch // LANES)))
        ties_past = count(past_keys == jnp.tile(thr, (1, nc * ch // LANES)))
        ties_new_allowed = ties_allowed - jnp.minimum(ties_past, ties_allowed)
        tie = jnp.where(new_keys == thr_new, 1.0, 0.0)
        lane = lax.broadcasted_iota(I32, tie.shape, 1)
        rank = jnp.zeros(tie.shape, F32)
        for s in range(1, tq):
            rank = rank + jnp.where(lane >= s, pltpu.roll(tie, s, 1), 0.0)
        new_keys = jnp.where((tie > 0.0) & (rank >= ties_new_allowed), thr_new - 1, new_keys)
        budget = jnp.broadcast_to(topk - gt_new, (n_rows, LANES)).astype(I32)
        thr_ref[...] = jnp.zeros(thr_ref.shape, I32)
        new_ref[...] = jnp.full(new_ref.shape, INT_MIN, I32)
        for bb in range(n_rows // tq):
            new_ref[bb, 0:tq, :] = new_keys[bb * tq:(bb + 1) * tq]
            for jj in range(tq):
                thr_ref[bb * tq + jj, 0:1, :] = thr[bb * tq + jj:bb * tq + jj + 1, :]
                thr_ref[bb * tq + jj, 1:2, :] = budget[bb * tq + jj:bb * tq + jj + 1, :]


def _sample_select(page_table, qi_hm, w_hm, kin_t, cache_kidx_t, *, pages, tq, rq, iheads, topk):
    db, n_pages = page_table.shape
    assert qi_hm.shape[1] % iheads == 0 and rq % (qi_hm.shape[1] // iheads) == 0 and qi_hm.shape[1] % 16 == 0
    _, idim, ps = cache_kidx_t.shape
    nc = n_pages // pages
    ch = pages * ps
    assert ps % LANES == 0 and (ch // LANES) % SUBLANES == 0 and (db * tq) % SUBLANES == 0
    page_spec = lambda p: pl.BlockSpec(
        (None, idim, ps), lambda b, c, pt: (pt[b, jnp.minimum(c, nc - 1) * pages + p], 0, 0))
    per_b = lambda shape: pl.BlockSpec((None,) + shape, lambda b, c, pt: (b, 0, 0))
    return pl.pallas_call(
        functools.partial(_sample_select_kernel, pages=pages, ps=ps, nc=nc, iheads=iheads, idim=idim, tq=tq, rq=rq,
                          topk=topk),
        grid_spec=pltpu.PrefetchScalarGridSpec(
            num_scalar_prefetch=1,
            grid=(db, nc + 1),
            in_specs=[per_b(qi_hm.shape[1:]), per_b(w_hm.shape[1:]), per_b((idim, ps))]
            + [page_spec(p) for p in range(pages)],
            out_specs=[pl.BlockSpec((tq, ch // LANES, LANES), lambda b, c, pt: (b, jnp.minimum(c, nc - 1), 0)),
                       pl.BlockSpec((db, rq, ps), lambda b, c, pt: (0, 0, 0)),
                       pl.BlockSpec((db * tq, SUBLANES, LANES), lambda b, c, pt: (0, 0, 0))],
            scratch_shapes=[pltpu.VMEM((db * rq, nc * ch + ps), I32), pltpu.VMEM((db * tq, nc * ch + ps), I32)],
        ),
        out_shape=[jax.ShapeDtypeStruct((db * tq, n_pages * ps // LANES, LANES), I32),
                   jax.ShapeDtypeStruct((db, rq, ps), I32),
                   jax.ShapeDtypeStruct((db * tq, SUBLANES, LANES), I32)],
        compiler_params=_cparams("arbitrary", "arbitrary"),
        name="sample_select",
    )(page_table, qi_hm, w_hm, kin_t, *([cache_kidx_t] * pages))


def _sc_gather_kernel(past_hbm, thr_hbm, pt_hbm, ck_hbm, cv_hbm, ksel_hbm, vsel_hbm, pos_hbm, cnt_hbm,
                      row_v, thr_v, pt_v, idx_v, phys_v, rows_v, cnt_v, sem,
                      *, nq, tq, topk, ps, n_cores, rows_per_copy):
    wid = lax.axis_index("s") * n_cores + lax.axis_index("c")

    @pl.when(wid < nq)
    def _():
        pltpu.sync_copy(past_hbm.at[wid], row_v)
        pltpu.sync_copy(thr_hbm.at[wid], thr_v)
        pltpu.sync_copy(pt_hbm.at[wid // tq], pt_v)
        thr = thr_v[0, pl.ds(0, SC_LANES)]
        budget = thr_v[1, pl.ds(0, SC_LANES)]
        lane = lax.iota(I32, SC_LANES)
        zero = jnp.zeros((SC_LANES,), I32)
        for t in range(idx_v.shape[0] // SC_LANES):
            idx_v[pl.ds(t * SC_LANES, SC_LANES)] = zero

        def compact(pred):
            def body(r, cnt):
                for t in range(LANES // SC_LANES):
                    x = row_v[r, pl.ds(t * SC_LANES, SC_LANES)]
                    m = pred(x, cnt)
                    rank = plsc.cumsum(jnp.where(m, 1, 0).astype(I32))
                    plsc.store_scatter(idx_v, [cnt + rank - 1], lane + (r * LANES + t * SC_LANES), mask=m)
                    cnt = cnt + plsc.all_reduce_population_count(m)
                return cnt
            return body

        cnt = lax.fori_loop(0, row_v.shape[0], compact(lambda x, cnt: x > thr), zero)
        cnt = lax.fori_loop(0, row_v.shape[0], compact(lambda x, cnt: (x == thr) & (cnt < budget)), cnt)
        cnt_v[...] = jnp.minimum(cnt, budget)
        pltpu.sync_copy(cnt_v, cnt_hbm.at[wid])
        pltpu.sync_copy(idx_v.at[pl.ds(0, topk)], pos_hbm.at[wid])

        shift = ps.bit_length() - 1
        for t in range(topk // SC_LANES):
            pos = idx_v[pl.ds(t * SC_LANES, SC_LANES)]
            page = plsc.load_gather(pt_v, [lax.shift_right_logical(pos, shift)])
            phys_v[pl.ds(t * SC_LANES, SC_LANES)] = page * ps + (pos & (ps - 1))
        for g in range(topk // rows_per_copy):
            sel = phys_v.at[pl.ds(g * rows_per_copy, rows_per_copy)]
            dst = pl.ds(wid * topk + g * rows_per_copy, rows_per_copy)
            for src_hbm, dst_hbm in ((ck_hbm, ksel_hbm), (cv_hbm, vsel_hbm)):
                pltpu.async_copy(src_hbm.at[sel], rows_v, sem).wait()
                pltpu.sync_copy(rows_v, dst_hbm.at[dst])


def _sc_gather(past_keys, thr, page_table, cache_k, cache_v, *, tq, topk, ps):
    nq, key_rows, _ = past_keys.shape
    _, heads, hd = cache_k.shape
    assert ps & (ps - 1) == 0 and topk % SC_LANES == 0
    rows_per_copy = 64
    assert topk % rows_per_copy == 0
    mesh = plsc.VectorSubcoreMesh(core_axis_name="c", subcore_axis_name="s", num_cores=V7X_SC_CORES,
                                  num_subcores=V7X_SC_SUBCORES)
    assert nq <= V7X_SC_CORES * V7X_SC_SUBCORES
    sel_shape = jax.ShapeDtypeStruct((nq * topk, heads, hd), F32)
    return pl.kernel(
        functools.partial(_sc_gather_kernel, nq=nq, tq=tq, topk=topk, ps=ps, n_cores=V7X_SC_CORES,
                          rows_per_copy=rows_per_copy),
        out_type=[sel_shape, sel_shape, jax.ShapeDtypeStruct((nq, topk), I32),
                  jax.ShapeDtypeStruct((nq, SC_LANES), I32)],
        mesh=mesh,
        scratch_types=[pltpu.VMEM((key_rows, LANES), I32), pltpu.VMEM((SUBLANES, LANES), I32),
                       pltpu.VMEM((page_table.shape[1],), I32), pltpu.VMEM((topk + SC_LANES,), I32),
                       pltpu.VMEM((topk,), I32), pltpu.VMEM((rows_per_copy, heads, hd), F32),
                       pltpu.VMEM((SC_LANES,), I32), pltpu.SemaphoreType.DMA],
        compiler_params=pltpu.CompilerParams(needs_layout_passes=False),
        name="sc_select_gather",
    )(past_keys, thr, page_table, cache_k, cache_v)


def _sample_attn_sel_kernel(cnt_ref, relbt_ref, q_ref, ksel_ref, vsel_ref, pos_ref, snew_ref, thr_ref, kn_ref, vn_ref,
                            o_ref, *, heads, tq, past, n_buckets):
    w = pl.program_id(0)
    j = w % tq
    q = q_ref[...]

    def head_bias(dist):
        bucket = _bucket(dist, n_buckets)
        acc = jnp.zeros(bucket.shape, F32)
        for bkt in range(n_buckets):
            acc = jnp.where(bucket == bkt, relbt_ref[:, bkt:bkt + 1], acc)
        return acc * LOG2E

    def logits(keys):
        lg = _dot_nt(q, keys)
        head = lax.broadcasted_iota(I32, lg.shape, 0)
        col = lax.broadcasted_iota(I32, lg.shape, 1)
        return lg, (col & (heads - 1)) == head, col

    lg, own, col = logits(ksel_ref[...].astype(BF16))
    keep = own & (col < cnt_ref[w, 0] * heads)
    lg = jnp.where(keep, lg + head_bias(jnp.broadcast_to(past + j - pos_ref[...], lg.shape)), NEG_BIG)
    lgn, own, col = logits(kn_ref[...])
    keep = own & (snew_ref[...] >= thr_ref[0:1, :])
    lgn = jnp.where(keep, lgn + head_bias(j - lax.shift_right_logical(col, heads.bit_length() - 1)), NEG_BIG)

    m = jnp.maximum(jnp.max(lg, axis=1, keepdims=True), jnp.max(lgn, axis=1, keepdims=True))
    p = jnp.exp2(lg - m)
    pn = jnp.exp2(lgn - m)
    denom = jnp.sum(p, axis=1, keepdims=True) + jnp.sum(pn, axis=1, keepdims=True)
    acc = _dot(p.astype(BF16), vsel_ref[...].astype(BF16)) + _dot(pn.astype(BF16), vn_ref[...])
    o_ref[...] = acc / denom


def _sample_attn_sel(cnt, rel_bias_t, q, ksel, vsel, pos, snew, thr, kn, vn, *, heads, hd, topk, tq, past):
    nq = q.shape[0]
    ps = kn.shape[1]
    assert ps == LANES and hd == LANES and heads & (heads - 1) == 0
    per_q = lambda shape: pl.BlockSpec((None,) + shape, lambda w, cnt: (w, 0, 0))
    per_b = lambda shape: pl.BlockSpec((None,) + shape, lambda w, cnt: (w // tq, 0, 0))
    sel_spec = pl.BlockSpec((topk * heads, hd), lambda w, cnt: (w, 0))
    return pl.pallas_call(
        functools.partial(_sample_attn_sel_kernel, heads=heads, tq=tq, past=past, n_buckets=rel_bias_t.shape[1]),
        grid_spec=pltpu.PrefetchScalarGridSpec(
            num_scalar_prefetch=1,
            grid=(nq,),
            in_specs=[pl.BlockSpec(rel_bias_t.shape, lambda w, cnt: (0, 0)), per_q((heads, hd)), sel_spec, sel_spec,
                      per_q((1, topk * heads)), per_q((1, ps)), per_q((SUBLANES, LANES)), per_b((ps, hd)),
                      per_b((ps, hd))],
            out_specs=per_q((heads, hd)),
        ),
        out_shape=jax.ShapeDtypeStruct((nq, heads, hd), F32),
        compiler_params=_cparams("arbitrary"),
        name="sample_attn_sel",
    )(cnt, rel_bias_t, q, ksel, vsel, pos, snew, thr, kn, vn)


def _mix_kernel(x_ref, xs_ref, g_ref, a_ref, as_ref, c_ref, cs_ref, w_hbm, wpa_ref, wpb_ref, wo_ref, o_ref, os_ref,
                wg_ref, *, d, row0):
    _stage_rows(w_hbm, wg_ref, row0)
    n = pl.num_programs(0) - 1

    def rows(x_ref, a_ref, c_ref, o_ref):
        x = x_ref[...]
        hb = _rms_bf16(x, g_ref[...])
        a = _dot(a_ref[...], wpa_ref[...])
        m = jax.nn.sigmoid(_dot_nt(hb, wg_ref[0:d, :])) * a
        c = _dot(c_ref[...], wpb_ref[...])
        m = m + jax.nn.sigmoid(_dot_nt(hb, wg_ref[d:2 * d, :])) * c
        o_ref[...] = x + _dot(m.astype(BF16), wo_ref[...])

    @pl.when(pl.program_id(0) < n)
    def _():
        rows(x_ref, a_ref, c_ref, o_ref)

    @pl.when(pl.program_id(0) == n)
    def _():
        rows(xs_ref, as_ref, cs_ref, os_ref)


def _mix(xp, xs, g, attn_p, attn_s, c_p, c_s, w_all, row0, w_pa, w_pb, w_o, *, tm):
    m, d = xp.shape
    ms = xs.shape[0]
    n = m // tm
    assert row0 % 16 == 0
    prow = lambda width: pl.BlockSpec((tm, width), lambda i: (jnp.minimum(i, n - 1), 0))
    srow = lambda width: pl.BlockSpec((ms, width), lambda i: (0, 0))
    return pl.pallas_call(
        functools.partial(_mix_kernel, d=d, row0=row0),
        grid=(n + 1,),
        in_specs=[prow(d), srow(d), _resident((1, d)), prow(attn_p.shape[1]), srow(attn_s.shape[1]),
                  prow(c_p.shape[1]), srow(c_s.shape[1]), pl.BlockSpec(memory_space=pl.ANY), _resident(w_pa.shape),
                  _resident(w_pb.shape), _resident(w_o.shape)],
        out_specs=[prow(d), srow(d)],
        out_shape=[jax.ShapeDtypeStruct((m, d), F32), jax.ShapeDtypeStruct((ms, d), F32)],
        scratch_shapes=[pltpu.VMEM((2 * d, d), BF16)],
        compiler_params=_cparams("arbitrary"),
        name="mix_out",
    )(xp, xs, g, attn_p, attn_s, c_p, c_s, w_all, w_pa, w_pb, w_o)


def _mlp_kernel(x_ref, xs_ref, g_ref, gf_ref, w1_ref, w2_ref, y_ref, ys_ref, h_ref, *, tm, final):
    f = pl.program_id(1)
    last_tile = pl.program_id(0) == pl.num_programs(0) - 1

    def start(x_ref, y_ref, h_rows):
        h_ref[h_rows, :] = _rms_bf16(x_ref[...], g_ref[...])
        y_ref[...] = jnp.zeros(y_ref.shape, F32)

    def finish(x_ref, y_ref):
        x2 = x_ref[...] + y_ref[...]
        if final:
            x2 = x2 * lax.rsqrt(jnp.mean(x2 * x2, axis=-1, keepdims=True) + EPS) * gf_ref[...]
        y_ref[...] = x2

    def ff(h):
        t = jnp.square(jnp.maximum(_dot(h, w1_ref[...]), 0.0))
        return _dot(t.astype(BF16), w2_ref[...])

    @pl.when(f == 0)
    def _():
        start(x_ref, y_ref, slice(0, tm))

    @pl.when((f == 0) & last_tile)
    def _():
        start(xs_ref, ys_ref, slice(tm, h_ref.shape[0]))

    @pl.when(jnp.logical_not(last_tile))
    def _():
        y_ref[...] += ff(h_ref[0:tm, :])

    @pl.when(last_tile)
    def _():
        r = ff(h_ref[...])
        y_ref[...] += r[0:tm]
        ys_ref[...] += r[tm:]

    @pl.when(f == pl.num_programs(1) - 1)
    def _():
        finish(x_ref, y_ref)

    @pl.when((f == pl.num_programs(1) - 1) & last_tile)
    def _():
        finish(xs_ref, ys_ref)


def _mlp(xp, xs, g, gf, w1, w2, *, tm, tf, final):
    m, d = xp.shape
    ms = xs.shape[0]
    ff = w1.shape[1]
    srow = pl.BlockSpec((ms, d), lambda i, f: (0, 0))
    return pl.pallas_call(
        functools.partial(_mlp_kernel, tm=tm, final=final),
        grid=(m // tm, ff // tf),
        in_specs=[pl.BlockSpec((tm, d), lambda i, f: (i, 0)), srow, _resident((1, d)), _resident((1, d)),
                  pl.BlockSpec((d, tf), lambda i, f: (0, f)), pl.BlockSpec((tf, d), lambda i, f: (f, 0))],
        out_specs=[pl.BlockSpec((tm, d), lambda i, f: (i, 0)), srow],
        out_shape=[jax.ShapeDtypeStruct((m, d), F32), jax.ShapeDtypeStruct((ms, d), F32)],
        scratch_shapes=[pltpu.VMEM((tm + ms, d), BF16)],
        compiler_params=_cparams("arbitrary", "arbitrary"),
        name="mlp",
    )(xp, xs, g, gf, w1, w2)


def _tile(m, cap):
    return min(m, cap)


def kernel(x_prompt, x_sample, cache_k, cache_v, cache_kidx, state_conv, page_table, rel_bias, norm_mix_g, w_in,
           conv_w, w_pa, w_pb, w_o, norm_mlp_g, w_mlp_in, w_mlp_out, norm_final_g):
    batch, seq, d = x_prompt.shape
    db, tq, _ = x_sample.shape
    depth, n_pool, ps, heads, hd = cache_k.shape
    idim = cache_kidx.shape[-1]
    cw = conv_w.shape[-1]
    aw = heads * hd
    n_in = w_in.shape[-1]
    iheads = (n_in - 3 * aw - idim - 3 * cw - 2 * d) // (idim + 1)
    iw = iheads * idim
    assert 3 * aw + iw + idim + iheads + 3 * cw + 2 * d == n_in and idim + iheads <= LANES
    n_pages = page_table.shape[1]
    past = n_pages * ps
    rq = SUBLANES
    assert tq <= rq

    mp, ms = batch * seq, db * tq
    xp = x_prompt.reshape(mp, d)
    xs = x_sample.reshape(ms, d)
    tb = _tile(seq, 256)
    o_small = 3 * aw + iw
    o_conv = o_small + idim + iheads
    o_gate = o_conv + 3 * cw
    gf = norm_final_g.reshape(1, d)
    bias_tiles, bias_max = _bias_tiles(rel_bias, tb=tb)

    outs = {k: [] for k in ("kp", "vp", "kip", "sp", "ks", "vs", "kis", "ss")}
    for l in range(depth):
        wl = jnp.swapaxes(w_in[l], 0, 1).astype(BF16)
        g_mix = norm_mix_g[l].reshape(1, d)
        g_mlp = norm_mlp_g[l].reshape(1, d)

        tm = _tile(seq, 512)
        q, k, v, kb, vb, qi, sm, smb, kn = _proj_attn(xp, g_mix, wl, aw=aw, iw=iw, n_small=idim + iheads, hd=hd, tm=tm)
        c_in, u_tail, wpa, wpb, wo = _proj_conv(xp, g_mix, wl, o_conv, conv_w[l], None, (w_pa[l], w_pb[l], w_o[l]),
                                                seq=seq, tm=_tile(seq, 512))
        attn, w1, w2 = _attn_prompt(bias_max, bias_tiles, qi, sm, smb, q, kb, vb, kn, (w_mlp_in[l], w_mlp_out[l]),
                                    batch=batch, seq=seq, tb=tb, heads=heads, hd=hd, iheads=iheads, idim=idim,
                                    topk=min(TOPK_MAX, seq // 4))
        attn_p, c_in_p = attn, c_in
        outs["kp"].append(k.reshape(batch, seq, heads, hd))
        outs["vp"].append(v.reshape(batch, seq, heads, hd))
        outs["kip"].append(sm[:, :idim].reshape(batch, seq, idim))
        outs["sp"].append(u_tail.reshape(batch, -1, SUBLANES, cw)[:, -1, SUBLANES - 2:])

        q, k, v, kb, vb, qi, sm, smb, _ = _proj_attn(xs, g_mix, wl, aw=aw, iw=iw, n_small=idim + iheads, hd=hd, tm=ms)
        st = state_conv[l]
        zero = jnp.zeros((db, tq - 1, cw), F32)
        prev1 = jnp.concatenate([st[:, 1:2], zero], axis=1).reshape(ms, cw)
        prev2 = jnp.concatenate([st, zero[:, 1:]], axis=1).reshape(ms, cw)
        c_in, u_all = _proj_conv(xs, g_mix, wl, o_conv, conv_w[l], (prev1, prev2), seq=tq, tm=ms)

        def pad_rows(a, n):
            return jnp.pad(a, ((0, 0), (0, n - a.shape[1])) + ((0, 0),) * (a.ndim - 2))

        rh = tq if rq % tq == 0 else rq
        qi_hm = pad_rows(qi.reshape(db, tq, iheads, idim).transpose(0, 2, 1, 3).reshape(db * iheads, tq, idim), rh)
        qi_hm = qi_hm.reshape(db, iheads * rh, idim)
        w_hm = pad_rows(sm[:, idim:idim + iheads].reshape(db, tq, iheads).transpose(0, 2, 1).reshape(db * iheads, tq), rh)
        w_hm = jnp.broadcast_to(w_hm.reshape(db, iheads * rh, 1), (db, iheads * rh, LANES))
        kin_t = jnp.swapaxes(pad_rows(smb[:, :idim].reshape(db, tq, idim), ps), 1, 2)
        topk_s = min(TOPK_MAX, (past + tq) // 4)
        past_keys, snew, thr = _sample_select(page_table, qi_hm, w_hm, kin_t, jnp.swapaxes(cache_kidx[l], 1, 2),
                                              pages=math.gcd(n_pages, 32), tq=tq, rq=rq, iheads=iheads, topk=topk_s)
        ksel, vsel, pos, cnt = _sc_gather(past_keys, thr, page_table, cache_k[l].reshape(n_pool * ps, heads, hd),
                                          cache_v[l].reshape(n_pool * ps, heads, hd), tq=tq, topk=topk_s, ps=ps)
        pos_rows = jnp.repeat(pos, heads, axis=1).reshape(ms, 1, topk_s * heads)
        snew_rows = jnp.repeat(snew[:, :tq, :ps // heads], heads, axis=2).reshape(ms, 1, ps)
        attn = _sample_attn_sel(cnt, rel_bias.T, q.reshape(ms, heads, hd), ksel.reshape(ms * topk_s * heads, hd),
                                vsel.reshape(ms * topk_s * heads, hd), pos_rows, snew_rows, thr,
                                pad_rows(kb.reshape(db, tq * heads, hd), ps), pad_rows(vb.reshape(db, tq * heads, hd), ps),
                                heads=heads, hd=hd, topk=topk_s, tq=tq, past=past)
        attn = attn.reshape(ms, aw).astype(BF16)

        x1p, x1s = _mix(xp, xs, g_mix, attn_p, attn, c_in_p, c_in, wl, o_gate, wpa, wpb, wo, tm=_tile(mp, 256))
        xp_next, xs_next = _mlp(x1p, x1s, g_mlp, gf, w1, w2, tm=_tile(mp, 1024), tf=_tile(w1.shape[1], 512),
                                final=l == depth - 1)
        outs["ks"].append(k.reshape(db, tq, heads, hd))
        outs["vs"].append(v.reshape(db, tq, heads, hd))
        outs["kis"].append(sm[:, :idim].reshape(db, tq, idim))
        outs["ss"].append(u_all.reshape(db, tq, cw)[:, tq - 2:])
        xp, xs = xp_next, xs_next

    st = {k: jnp.stack(v) for k, v in outs.items()}
    return (xp.reshape(batch, seq, d), xs.reshape(db, tq, d), st["kp"], st["vp"], st["kip"], st["sp"],
            st["ks"], st["vs"], st["kis"], st["ss"])
```

```python
import functools
import math

import jax
import jax.numpy as jnp
import numpy as np
from jax import lax
from jax.experimental import pallas as pl
from jax.experimental.pallas import tpu as pltpu
from jax.experimental.pallas import tpu_sc as plsc

F32 = jnp.float32
BF16 = jnp.bfloat16
I32 = jnp.int32
I16 = jnp.int16
HALF = 1 << 15

TOPK_MAX = 256
MAX_DISTANCE = 128
EPS = 1e-6

LANES = 128
SUBLANES = 8
V7X_SCOPED_VMEM_BYTES = 60000 * 1024
SC_LANES = 16
V7X_SC_CORES = 2
V7X_SC_SUBCORES = 16

LOG2E = math.log2(math.e)
INT_MIN = np.int32(-2 ** 31)
NEG_BIG = -1e30
SOFTMAX_MIN_MASS = 2.0 ** -60


def _cparams(*sem):
    return pltpu.CompilerParams(dimension_semantics=sem, vmem_limit_bytes=V7X_SCOPED_VMEM_BYTES)


def _resident(shape):
    nd = len(shape)
    return pl.BlockSpec(shape, lambda *_: (0,) * nd, pipeline_mode=pl.Buffered(1))


def _rms_bf16(x, g):
    y = x * lax.rsqrt(jnp.mean(x * x, axis=-1, keepdims=True) + EPS)
    return (y * g).astype(BF16)


def _dot(a, b):
    return jnp.dot(a, b, preferred_element_type=F32)


def _dot_nt(a, b):
    return lax.dot_general(a, b, (((1,), (1,)), ((), ())), preferred_element_type=F32)


def _ordered_key(x):
    b = lax.bitcast_convert_type(x, I32)
    return b ^ ((b >> 31) & np.int32(0x7FFFFFFF))


def _bucket(n, n_buckets):
    n = jnp.maximum(n, 0)
    me = n_buckets // 2
    nf = jnp.maximum(n, me).astype(F32)
    large = me + (jnp.log(nf / me) / math.log(MAX_DISTANCE / me) * (n_buckets - me)).astype(I32)
    large = jnp.minimum(large, n_buckets - 1)
    return jnp.where(n < me, n, large)


def _bias_of_bucket(bucket, relb_ref, h, n_buckets):
    acc = jnp.zeros(bucket.shape, F32)
    for bkt in range(n_buckets):
        acc = jnp.where(bucket == bkt, relb_ref[bkt, h], acc)
    return acc


def _fold_lanes(x):
    acc = x[:, 0:LANES]
    for t in range(1, x.shape[1] // LANES):
        acc = acc + x[:, t * LANES:(t + 1) * LANES]
    return acc


def _fold_rows(x, op=jnp.add, rows=SUBLANES):
    parts = [x[t * rows:(t + 1) * rows, :] for t in range(x.shape[0] // rows)]
    while len(parts) > 1:
        parts = [op(parts[t], parts[t + 1]) for t in range(0, len(parts) - 1, 2)] + parts[len(parts) & ~1:]
    return parts[0]


def _bisect(count_ge, shape, axis, k, trips, bits, at_none):
    def bit_body(t, carry):
        u, at = carry
        cand = u | jnp.left_shift(np.int32(1), bits - 1 - t)
        cnt = jnp.sum(count_ge(cand), axis=axis, keepdims=True)
        return jnp.where(cnt >= k, cand, u), jnp.where(cnt >= k, jnp.broadcast_to(cnt, shape), at)

    return lax.fori_loop(0, trips, bit_body, (jnp.zeros(shape, I32), at_none))


def _kth_threshold(count_ge, shape, axis, k, trips):
    uthr, at_thr = _bisect(lambda u: count_ge(u ^ INT_MIN), shape, axis, k, trips, 32, jnp.full(shape, k, F32))
    return jnp.maximum(uthr ^ INT_MIN, INT_MIN + 1), at_thr - k


def _flash_update(lg, v, m_ref, l_ref, acc_ref, h, hd):
    reps = lg.shape[1] // LANES
    m_prev = m_ref[h]
    m_new = jnp.maximum(m_prev, jnp.max(lg, axis=1, keepdims=True))
    p = jnp.exp2(lg - jnp.tile(m_new, (1, reps)))
    alpha = jnp.exp2(m_prev - m_new)
    pv = _dot(p.astype(BF16), jnp.concatenate([v, jnp.ones((v.shape[0], LANES), BF16)], axis=1))
    l_ref[h] = alpha * l_ref[h] + pv[:, hd:]
    m_ref[h] = m_new
    hs = slice(h * hd, (h + 1) * hd)
    acc_ref[:, hs] = acc_ref[:, hs] * jnp.tile(alpha, (1, hd // LANES)) + pv[:, :hd]


def _stage_rows(w_hbm, w_ref, row0):
    @pl.when(pl.program_id(0) == 0)
    def _():
        pltpu.sync_copy(w_hbm.at[pl.ds(row0, w_ref.shape[0])], w_ref)


def _proj_attn_kernel(x_ref, g_ref, w_hbm, q_ref, k_ref, v_ref, kb_ref, vb_ref, qi_ref,
                      sm_ref, smb_ref, kn_ref, w_ref, ws_ref, *, aw, iw, hd, tm, n_small, qscale):
    heads = aw // hd
    _stage_rows(w_hbm, w_ref, 0)

    @pl.when(pl.program_id(0) == 0)
    def _():
        ws_ref[...] = jnp.zeros(ws_ref.shape, BF16)
        pltpu.sync_copy(w_hbm.at[pl.ds(w_ref.shape[0], n_small)], ws_ref.at[pl.ds(0, n_small)])

    hb = _rms_bf16(x_ref[...], g_ref[...])
    q_ref[...] = (_dot_nt(hb, w_ref[0:aw, :]) * qscale).astype(BF16)
    for o_ref, ob_ref, c0 in ((k_ref, kb_ref, aw), (v_ref, vb_ref, 2 * aw)):
        kv = _dot_nt(hb, w_ref[c0:c0 + aw, :])
        ob_ref[...] = kv.astype(BF16)
        if o_ref is k_ref:
            kf = kv.astype(BF16).astype(F32)
            lane = lax.broadcasted_iota(I32, (tm, LANES), 1)
            kn = jnp.zeros((tm, LANES), F32)
            for h in range(heads):
                kh = kf[:, h * hd:(h + 1) * hd]
                kn = jnp.where(lane == h, jnp.sum(kh * kh, axis=1, keepdims=True), kn)
            kn_ref[...] = kn
        for h in range(heads):
            o_ref[pl.ds(h, tm, stride=heads), :] = kv[:, h * hd:(h + 1) * hd]
    qi_ref[...] = _dot_nt(hb, w_ref[3 * aw:3 * aw + iw, :]).astype(BF16)
    sm = _dot_nt(hb, ws_ref[...])
    sm_ref[...] = sm
    smb_ref[...] = sm.astype(BF16)


def _proj_attn(x2d, g, w_all, *, aw, iw, n_small, hd, tm):
    m, d = x2d.shape
    wt = 3 * aw + iw
    heads = aw // hd
    assert hd == LANES and n_small <= LANES and wt % 16 == 0 and n_small % 16 == 0
    row = lambda width: pl.BlockSpec((tm, width), lambda i: (i, 0))
    kv_spec = pl.BlockSpec((tm * heads, hd), lambda i: (i, 0))
    kv_shape = jax.ShapeDtypeStruct((m * heads, hd), F32)
    outs = [(aw, BF16), None, None, (aw, BF16), (aw, BF16), (iw, BF16), (LANES, F32), (LANES, BF16), (LANES, F32)]
    return pl.pallas_call(
        functools.partial(_proj_attn_kernel, aw=aw, iw=iw, hd=hd, tm=tm, n_small=n_small,
                          qscale=hd ** -0.5 * LOG2E),
        grid=(m // tm,),
        in_specs=[row(d), _resident((1, d)), pl.BlockSpec(memory_space=pl.ANY)],
        out_specs=[kv_spec if o is None else row(o[0]) for o in outs],
        out_shape=[kv_shape if o is None else jax.ShapeDtypeStruct((m, o[0]), o[1]) for o in outs],
        scratch_shapes=[pltpu.VMEM((wt, d), BF16), pltpu.VMEM((LANES, d), BF16)],
        compiler_params=_cparams("arbitrary"),
        name="proj_attn",
    )(x2d, g, w_all)


def _proj_conv_kernel(*refs, cw, tm, seq, tiles_per_seq, tail, row0, n_cast):
    if tiles_per_seq:
        x_ref, g_ref, w_hbm, cwt_ref = refs[:4]
        c_ref, ut_ref = refs[4 + n_cast:6 + n_cast]
        w_ref, carry_ref = refs[6 + 2 * n_cast:]
        _cast_slabs(refs[4:4 + n_cast], refs[6 + n_cast:6 + 2 * n_cast])
    else:
        x_ref, g_ref, w_hbm, cwt_ref, p1_ref, p2_ref, c_ref, ut_ref, w_ref = refs
    _stage_rows(w_hbm, w_ref, row0)
    hb = _rms_bf16(x_ref[...], g_ref[...])
    cx = _dot_nt(hb, w_ref[0:cw, :])
    cb = _dot_nt(hb, w_ref[cw:2 * cw, :])
    cc = _dot_nt(hb, w_ref[2 * cw:3 * cw, :])
    u = cc * cx
    r = lax.broadcasted_iota(I32, (tm, cw), 0)
    um1 = pltpu.roll(u, 1, 0)
    um2 = pltpu.roll(u, 2, 0)
    if tiles_per_seq:
        @pl.when(pl.program_id(0) % tiles_per_seq == 0)
        def _():
            carry_ref[...] = jnp.zeros_like(carry_ref)
        prev1 = jnp.broadcast_to(carry_ref[SUBLANES - 1:SUBLANES, :], (tm, cw))
        prev2 = jnp.broadcast_to(carry_ref[SUBLANES - 2:SUBLANES - 1, :], (tm, cw))
        um1 = jnp.where(r == 0, prev1, um1)
        um2 = jnp.where(r == 0, prev2, jnp.where(r == 1, prev1, um2))
        carry_ref[...] = u[tm - SUBLANES:tm, :]
    else:
        assert seq & (seq - 1) == 0
        t = r & (seq - 1)
        um1 = jnp.where(t >= 1, um1, p1_ref[...])
        um2 = jnp.where(t >= 2, um2, p2_ref[...])
    y = cwt_ref[0:1, :] * um2 + cwt_ref[1:2, :] * um1 + cwt_ref[2:3, :] * u
    c_ref[...] = (cb * y).astype(BF16)
    ut_ref[...] = u[tm - tail:tm, :]


def _proj_conv(x2d, g, w_all, row0, conv_w, prev, to_cast=(), *, seq, tm):
    m, d = x2d.shape
    cw = conv_w.shape[1]
    assert conv_w.shape[0] == 3 and row0 % 16 == 0
    row = lambda width: pl.BlockSpec((tm, width), lambda i: (i, 0))
    in_specs = [row(d), _resident((1, d)), pl.BlockSpec(memory_space=pl.ANY), _resident((3, cw))]
    args = [x2d, g, w_all, conv_w]
    scratch = [pltpu.VMEM((3 * cw, d), BF16)]
    if prev is None:
        assert seq % tm == 0
        tiles_per_seq, tail = seq // tm, SUBLANES
        scratch += [pltpu.VMEM((SUBLANES, cw), F32)]
    else:
        assert tm % seq == 0 and seq >= 2
        tiles_per_seq, tail = 0, tm
        in_specs += [row(cw), row(cw)]
        args += list(prev)
    assert not (to_cast and prev is not None)
    c_in, c_out, c_shapes = _cast_specs(to_cast, m // tm, lambda i: i)
    return pl.pallas_call(
        functools.partial(_proj_conv_kernel, cw=cw, tm=tm, seq=seq, tiles_per_seq=tiles_per_seq, tail=tail,
                          row0=row0, n_cast=len(to_cast)),
        grid=(m // tm,),
        in_specs=in_specs + c_in,
        out_specs=[row(cw), pl.BlockSpec((tail, cw), lambda i: (i, 0))] + c_out,
        out_shape=[jax.ShapeDtypeStruct((m, cw), BF16), jax.ShapeDtypeStruct((m // tm * tail, cw), F32)] + c_shapes,
        scratch_shapes=scratch,
        compiler_params=_cparams("arbitrary"),
        name="proj_conv",
    )(*args, *to_cast)


def _bias_tiles_kernel(relb_ref, o_ref, bmax_ref, *, tb, n_buckets):
    kind = pl.program_id(0)
    h = pl.program_id(1)
    qry = lax.broadcasted_iota(I32, (tb, tb), 0)
    key = lax.broadcasted_iota(I32, (tb, tb), 1)
    bias = _bias_of_bucket(_bucket(kind * tb + qry - key, n_buckets), relb_ref, h, n_buckets)
    o_ref[...] = (bias - relb_ref[n_buckets - 1, h]) * LOG2E
    bmax = relb_ref[0, h]
    for bkt in range(1, n_buckets):
        bmax = jnp.maximum(bmax, relb_ref[bkt, h])
    bmax_ref[...] = jnp.full(bmax_ref.shape, (bmax - relb_ref[n_buckets - 1, h]) * LOG2E, F32)


def _bias_tiles(rel_bias, *, tb):
    n_buckets, heads = rel_bias.shape
    return pl.pallas_call(
        functools.partial(_bias_tiles_kernel, tb=tb, n_buckets=n_buckets),
        grid=(2, heads),
        in_specs=[pl.BlockSpec(memory_space=pltpu.SMEM)],
        out_specs=[pl.BlockSpec((None, None, tb, tb), lambda a, h: (a, h, 0, 0)),
                   pl.BlockSpec((None, None, SUBLANES, LANES), lambda a, h: (a, h, 0, 0))],
        out_shape=[jax.ShapeDtypeStruct((2, heads, tb, tb), F32),
                   jax.ShapeDtypeStruct((2, heads, SUBLANES, LANES), F32)],
        compiler_params=_cparams("arbitrary", "arbitrary"),
        name="bias_tiles",
    )(rel_bias)


def _cast_specs(weights, n_steps, step_of):
    in_specs, out_specs, out_shapes = [], [], []
    for w in weights:
        rows, cols = w.shape
        assert rows % n_steps == 0 and (rows // n_steps) % 16 == 0
        spec = pl.BlockSpec((rows // n_steps, cols), lambda *idx: (step_of(*idx), 0))
        in_specs.append(spec)
        out_specs.append(spec)
        out_shapes.append(jax.ShapeDtypeStruct((rows, cols), BF16))
    return in_specs, out_specs, out_shapes


def _cast_slabs(src_refs, dst_refs):
    for src, dst in zip(src_refs, dst_refs):
        dst[...] = src[...].astype(BF16)


def _attn_prompt_kernel(*refs, n_cast, tb, heads, hd, iheads, idim, topk):
    bmax_ref, qi_ref, sm_ref, smb_ref, q_ref, kb_ref, vb_ref, kn_ref, bias_ref = refs[:9]
    o_ref = refs[9 + n_cast]
    (skey_ref, hi_ref, lo_ref, thr_ref, wt_ref, madd_ref, kmax_ref, shift_ref, m_ref, l_ref,
     acc_ref) = refs[10 + 2 * n_cast:]
    _cast_slabs(refs[9:9 + n_cast], refs[10 + n_cast:10 + 2 * n_cast])
    i = pl.program_id(1)
    wscale = idim ** -0.5 * iheads ** -0.5

    l_ref[...] = jnp.zeros(l_ref.shape, F32)
    acc_ref[...] = jnp.zeros(acc_ref.shape, F32)
    wt_ref[...] = sm_ref[...].T * wscale

    def chunk(j):
        return pl.ds(pl.multiple_of(j * tb, tb), tb)

    def score_chunk(j, diag):
        kic = smb_ref[chunk(j), 0:idim]
        acc = jnp.zeros((tb, tb), F32)
        for h in range(iheads):
            s = _dot_nt(kic, qi_ref[:, h * idim:(h + 1) * idim])
            acc = acc + jnp.maximum(s, 0.0) * wt_ref[idim + h:idim + h + 1, :]
        key = _ordered_key(acc)
        if diag:
            kpos = lax.broadcasted_iota(I32, (tb, tb), 0)
            qpos = lax.broadcasted_iota(I32, (tb, tb), 1)
            key = jnp.where(kpos > qpos, INT_MIN, key)
        skey_ref[chunk(j), :] = key
        hi_ref[chunk(j), :] = (key >> 16).astype(I16)
        lo_ref[chunk(j), :] = ((key & 0xFFFF) - HALF).astype(I16)

    def score_body(j, carry):
        score_chunk(j, False)
        return carry

    lax.fori_loop(0, i, score_body, 0)
    score_chunk(i, True)

    def counter(half_ref):
        def count_ge(u):
            cand = (u - HALF).astype(I16)[0:1, :]

            def body(j, cnt):
                ge = jnp.where(half_ref[chunk(j), :] >= cand, jnp.int16(1), jnp.int16(0))
                return cnt + _fold_rows(ge, rows=2 * SUBLANES)

            return lax.fori_loop(0, i + 1, body, jnp.zeros((2 * SUBLANES, tb), I16)).astype(F32)

        return count_ge

    shape = (SUBLANES, tb)
    trips = jnp.where((i + 1) * tb <= topk, 0, 16)
    u_hi, n_ge_hi = _bisect(counter(hi_ref), shape, 0, float(topk), trips, 16, jnp.full(shape, topk, F32))
    t_hi = (u_hi - HALF).astype(I16)[0:1, :]
    n_gt_hi = jnp.where(u_hi == 2 * HALF - 1, 0.0,
                        jnp.sum(counter(hi_ref)(jnp.minimum(u_hi + 1, 2 * HALF - 1)), axis=0, keepdims=True))

    def keep_low_of_kth_high(j, carry):
        lo_ref[chunk(j), :] = jnp.where(hi_ref[chunk(j), :] == t_hi, lo_ref[chunk(j), :], jnp.int16(-HALF))
        return carry

    lax.fori_loop(0, i + 1, keep_low_of_kth_high, 0)
    u_lo, n_ge_lo = _bisect(counter(lo_ref), shape, 0, topk - n_gt_hi, trips, 16, n_ge_hi - n_gt_hi)
    thr = jnp.maximum(jnp.left_shift(u_hi - HALF, 16) | u_lo, INT_MIN + 1)
    surplus = n_gt_hi + n_ge_lo - topk
    thr_ref[...] = thr

    @pl.when(jnp.max(surplus) > 0.0)
    def _():
        thr_row = thr_ref[0:1, :]

        def gt_body(j, cnt):
            return cnt + _fold_rows(jnp.where(skey_ref[chunk(j), :] > thr_row, 1.0, 0.0))

        above = lax.fori_loop(0, i + 1, gt_body, jnp.zeros((SUBLANES, tb), F32))
        budget = topk - jnp.sum(above, axis=0, keepdims=True)
        kpos = lax.broadcasted_iota(I32, (tb, tb), 0)
        earlier = jnp.where(kpos > lax.broadcasted_iota(I32, (tb, tb), 1), 1.0, 0.0).astype(BF16)

        def fix_body(j, seen):
            sk = skey_ref[chunk(j), :]
            tie = jnp.where(sk == thr_row, 1.0, 0.0)
            rank = seen[0:1, :] + _dot(earlier, tie.astype(BF16))
            skey_ref[chunk(j), :] = jnp.where((tie > 0.0) & (rank >= budget), thr_row - 1, sk)
            return seen + jnp.sum(_fold_rows(tie), axis=0, keepdims=True)

        lax.fori_loop(0, i + 1, fix_body, jnp.zeros((SUBLANES, tb), F32))

    @pl.when(i == 0)
    def _():
        kmax_ref[...] = jnp.zeros(kmax_ref.shape, F32)

    kmax_ref[...] = jnp.maximum(kmax_ref[...], jnp.max(kn_ref[chunk(i), :], axis=0, keepdims=True))
    qsq = q_ref[...] * q_ref[...]
    for h in range(heads):
        qn2 = _dot(qsq[:, h * hd:(h + 1) * hd], jnp.ones((hd, LANES), BF16)) * (1.0 + 2.0 ** -6)
        shift_ref[h] = jnp.sqrt(qn2 * kmax_ref[0:1, h:h + 1]) + bmax_ref[0, h][0:1, :]

    def logits(j, h, kind):
        hs = slice(h * hd, (h + 1) * hd)
        lg = _dot_nt(q_ref[:, hs], kb_ref[chunk(j), hs]) + madd_ref[...]
        return lg if kind is None else lg + bias_ref[kind, h]

    def bounded_chunk(j, kind):
        for h in range(heads):
            hs = slice(h * hd, (h + 1) * hd)
            p = jnp.exp2(logits(j, h, kind) - jnp.tile(shift_ref[h], (1, tb // LANES)))
            v1 = jnp.concatenate([vb_ref[chunk(j), hs], jnp.ones((tb, LANES), BF16)], axis=1)
            pv = _dot(p.astype(BF16), v1)
            l_ref[h] += pv[:, hd:]
            acc_ref[:, hs] += pv[:, :hd]

    def running_max_chunk(j, kind):
        for h in range(heads):
            _flash_update(logits(j, h, kind), vb_ref[chunk(j), h * hd:(h + 1) * hd], m_ref, l_ref, acc_ref, h, hd)

    def attend(update):
        def one(j, kind):
            madd_ref[...] = jnp.where(skey_ref[chunk(j), :] >= thr_ref[0:1, :], 0.0, NEG_BIG).T
            update(j, kind)

        def far_body(j, carry):
            one(j, None)
            return carry

        lax.fori_loop(0, jnp.maximum(i - 1, 0), far_body, 0)

        @pl.when(i >= 1)
        def _():
            one(i - 1, 1)

        one(i, 0)

    attend(bounded_chunk)

    @pl.when(jnp.min(l_ref[...]) < SOFTMAX_MIN_MASS)
    def _():
        m_ref[...] = jnp.full(m_ref.shape, NEG_BIG, F32)
        l_ref[...] = jnp.zeros(l_ref.shape, F32)
        acc_ref[...] = jnp.zeros(acc_ref.shape, F32)
        attend(running_max_chunk)

    for h in range(heads):
        hs = slice(h * hd, (h + 1) * hd)
        o_ref[:, hs] = (acc_ref[:, hs] / jnp.tile(l_ref[h], (1, hd // LANES))).astype(BF16)


def _attn_prompt(bias_max, bias_tiles, qi, sm, smb, q, kb, vb, kn, to_cast, *, batch, seq, tb, heads, hd, iheads,
                 idim, topk):
    m, aw = q.shape
    iw = qi.shape[1]
    nq = seq // tb
    assert tb >= MAX_DISTANCE and tb % LANES == 0 and hd % LANES == 0
    qrow = lambda width: pl.BlockSpec((tb, width), lambda b, i: (b * nq + i, 0))
    seqblk = lambda width: pl.BlockSpec((seq, width), lambda b, i: (b, 0))
    c_in, c_out, c_shapes = _cast_specs(to_cast, batch * nq, lambda b, i: b * nq + i)
    return pl.pallas_call(
        functools.partial(_attn_prompt_kernel, n_cast=len(to_cast), tb=tb, heads=heads, hd=hd, iheads=iheads,
                          idim=idim, topk=topk),
        grid=(batch, nq),
        in_specs=[_resident(bias_max.shape), qrow(iw), qrow(LANES), seqblk(LANES), qrow(aw), seqblk(aw),
                  seqblk(aw), seqblk(LANES), _resident(bias_tiles.shape)] + c_in,
        out_specs=[qrow(aw)] + c_out,
        out_shape=[jax.ShapeDtypeStruct((m, aw), BF16)] + c_shapes,
        scratch_shapes=[pltpu.VMEM((seq, tb), I32), pltpu.VMEM((seq, tb), I16), pltpu.VMEM((seq, tb), I16),
                        pltpu.VMEM((SUBLANES, tb), I32),
                        pltpu.VMEM((LANES, tb), F32), pltpu.VMEM((tb, tb), F32),
                        pltpu.VMEM((SUBLANES, LANES), F32), pltpu.VMEM((heads, tb, LANES), F32),
                        pltpu.VMEM((heads, tb, LANES), F32), pltpu.VMEM((heads, tb, LANES), F32),
                        pltpu.VMEM((tb, aw), F32)],
        compiler_params=_cparams("arbitrary", "arbitrary"),
        name="attn_prompt",
    )(bias_max, qi, sm, smb, q, kb, vb, kn, bias_tiles, *to_cast)


def _sample_select_kernel(pt_ref, qi_ref, w_ref, kin_ref, *rest, pages, ps, nc, iheads, idim, tq, rq, topk):
    page_refs = rest[:pages]
    past_ref, new_ref, thr_ref, row_ref, real_ref = rest[pages:]
    b = pl.program_id(0)
    c = pl.program_id(1)
    db = pl.num_programs(0)
    ch = pages * ps
    rh = qi_ref.shape[0] // iheads
    wscale = idim ** -0.5 * iheads ** -0.5
    rows_b = pl.ds(pl.multiple_of(b * rq, rq), rq)

    def score(keys_t):
        s = _dot(qi_ref[...], keys_t)
        t = jnp.maximum(s, 0.0) * (w_ref[:, 0:1] * wscale)
        acc = _fold_rows(t, rows=rq)
        step = rh
        while step < rq:
            acc = acc + pltpu.roll(acc, step, 0)
            step *= 2
        return _ordered_key(acc)

    @pl.when(c < nc)
    def _():
        for p in range(pages):
            key = score(page_refs[p][...].astype(BF16))
            row_ref[rows_b, pl.ds(pl.multiple_of(c * ch + p * ps, LANES), ps)] = key
            for t in range(ps // LANES):
                for j in range(tq):
                    past_ref[j, p * (ps // LANES) + t:p * (ps // LANES) + t + 1, :] = \
                        key[j:j + 1, t * LANES:(t + 1) * LANES]

    @pl.when(c == nc)
    def _():
        key = score(kin_ref[...])
        j = lax.broadcasted_iota(I32, (rq, ps), 0)
        n = lax.broadcasted_iota(I32, (rq, ps), 1)
        key = jnp.where((n <= j) & (n < tq), key, INT_MIN)
        row_ref[rows_b, nc * ch:nc * ch + ps] = key

    @pl.when((c == nc) & (b == db - 1))
    def _():
        n_rows = real_ref.shape[0]
        for bb in range(n_rows // tq):
            real_ref[bb * tq:(bb + 1) * tq, :] = row_ref[bb * rq:bb * rq + tq, :]

        def count_ge(scand):
            sk = real_ref[...]
            return _fold_lanes(jnp.where(sk >= jnp.tile(scand, (1, sk.shape[1] // LANES)), 1.0, 0.0))

        thr, _ = _kth_threshold(count_ge, (n_rows, LANES), 1, float(topk), 32)
        count = lambda m: jnp.sum(_fold_lanes(jnp.where(m, 1.0, 0.0)), axis=1, keepdims=True)
        past_keys = real_ref[:, 0:nc * ch]
        new_keys = real_ref[:, nc * ch:nc * ch + ps]
        thr_new = jnp.tile(thr, (1, ps // LANES))
        gt_new = count(new_keys > thr_new)
        ties_allowed = topk - gt_new - count(past_keys > jnp.tile(thr, (1, nc * ch // LANES)))
        ties_past = count(past_keys == jnp.tile(thr, (1, nc * ch // LANES)))
        ties_new_allowed = ties_allowed - jnp.minimum(ties_past, ties_allowed)
        tie = jnp.where(new_keys == thr_new, 1.0, 0.0)
        lane = lax.broadcasted_iota(I32, tie.shape, 1)
        rank = jnp.zeros(tie.shape, F32)
        for s in range(1, tq):
            rank = rank + jnp.where(lane >= s, pltpu.roll(tie, s, 1), 0.0)
        new_keys = jnp.where((tie > 0.0) & (rank >= ties_new_allowed), thr_new - 1, new_keys)
        budget = jnp.broadcast_to(topk - gt_new, (n_rows, LANES)).astype(I32)
        thr_ref[...] = jnp.zeros(thr_ref.shape, I32)
        new_ref[...] = jnp.full(new_ref.shape, INT_MIN, I32)
        for bb in range(n_rows // tq):
            new_ref[bb, 0:tq, :] = new_keys[bb * tq:(bb + 1) * tq]
            for jj in range(tq):
                thr_ref[bb * tq + jj, 0:1, :] = thr[bb * tq + jj:bb * tq + jj + 1, :]
                thr_ref[bb * tq + jj, 1:2, :] = budget[bb * tq + jj:bb * tq + jj + 1, :]


def _sample_select(page_table, qi_hm, w_hm, kin_t, cache_kidx_t, *, pages, tq, rq, iheads, topk):
    db, n_pages = page_table.shape
    assert qi_hm.shape[1] % iheads == 0 and rq % (qi_hm.shape[1] // iheads) == 0 and qi_hm.shape[1] % 16 == 0
    _, idim, ps = cache_kidx_t.shape
    nc = n_pages // pages
    ch = pages * ps
    assert ps % LANES == 0 and (ch // LANES) % SUBLANES == 0 and (db * tq) % SUBLANES == 0
    page_spec = lambda p: pl.BlockSpec(
        (None, idim, ps), lambda b, c, pt: (pt[b, jnp.minimum(c, nc - 1) * pages + p], 0, 0))
    per_b = lambda shape: pl.BlockSpec((None,) + shape, lambda b, c, pt: (b, 0, 0))
    return pl.pallas_call(
        functools.partial(_sample_select_kernel, pages=pages, ps=ps, nc=nc, iheads=iheads, idim=idim, tq=tq, rq=rq,
                          topk=topk),
        grid_spec=pltpu.PrefetchScalarGridSpec(
            num_scalar_prefetch=1,
            grid=(db, nc + 1),
            in_specs=[per_b(qi_hm.shape[1:]), per_b(w_hm.shape[1:]), per_b((idim, ps))]
            + [page_spec(p) for p in range(pages)],
            out_specs=[pl.BlockSpec((tq, ch // LANES, LANES), lambda b, c, pt: (b, jnp.minimum(c, nc - 1), 0)),
                       pl.BlockSpec((db, rq, ps), lambda b, c, pt: (0, 0, 0)),
                       pl.BlockSpec((db * tq, SUBLANES, LANES), lambda b, c, pt: (0, 0, 0))],
            scratch_shapes=[pltpu.VMEM((db * rq, nc * ch + ps), I32), pltpu.VMEM((db * tq, nc * ch + ps), I32)],
        ),
        out_shape=[jax.ShapeDtypeStruct((db * tq, n_pages * ps // LANES, LANES), I32),
                   jax.ShapeDtypeStruct((db, rq, ps), I32),
                   jax.ShapeDtypeStruct((db * tq, SUBLANES, LANES), I32)],
        compiler_params=_cparams("arbitrary", "arbitrary"),
        name="sample_select",
    )(page_table, qi_hm, w_hm, kin_t, *([cache_kidx_t] * pages))


def _sc_gather_kernel(past_hbm, thr_hbm, pt_hbm, ck_hbm, cv_hbm, ksel_hbm, vsel_hbm, pos_hbm, cnt_hbm,
                      row_v, thr_v, pt_v, idx_v, phys_v, rows_v, cnt_v, sem,
                      *, nq, tq, topk, ps, n_cores, rows_per_copy):
    wid = lax.axis_index("s") * n_cores + lax.axis_index("c")

    @pl.when(wid < nq)
    def _():
        pltpu.sync_copy(past_hbm.at[wid], row_v)
        pltpu.sync_copy(thr_hbm.at[wid], thr_v)
        pltpu.sync_copy(pt_hbm.at[wid // tq], pt_v)
        thr = thr_v[0, pl.ds(0, SC_LANES)]
        budget = thr_v[1, pl.ds(0, SC_LANES)]
        lane = lax.iota(I32, SC_LANES)
        zero = jnp.zeros((SC_LANES,), I32)
        for t in range(idx_v.shape[0] // SC_LANES):
            idx_v[pl.ds(t * SC_LANES, SC_LANES)] = zero

        def compact(pred):
            def body(r, cnt):
                for t in range(LANES // SC_LANES):
                    x = row_v[r, pl.ds(t * SC_LANES, SC_LANES)]
                    m = pred(x, cnt)
                    rank = plsc.cumsum(jnp.where(m, 1, 0).astype(I32))
                    plsc.store_scatter(idx_v, [cnt + rank - 1], lane + (r * LANES + t * SC_LANES), mask=m)
                    cnt = cnt + plsc.all_reduce_population_count(m)
                return cnt
            return body

        cnt = lax.fori_loop(0, row_v.shape[0], compact(lambda x, cnt: x > thr), zero)
        cnt = lax.fori_loop(0, row_v.shape[0], compact(lambda x, cnt: (x == thr) & (cnt < budget)), cnt)
        cnt_v[...] = jnp.minimum(cnt, budget)
        pltpu.sync_copy(cnt_v, cnt_hbm.at[wid])
        pltpu.sync_copy(idx_v.at[pl.ds(0, topk)], pos_hbm.at[wid])

        shift = ps.bit_length() - 1
        for t in range(topk // SC_LANES):
            pos = idx_v[pl.ds(t * SC_LANES, SC_LANES)]
            page = plsc.load_gather(pt_v, [lax.shift_right_logical(pos, shift)])
            phys_v[pl.ds(t * SC_LANES, SC_LANES)] = page * ps + (pos & (ps - 1))
        for g in range(topk // rows_per_copy):
            sel = phys_v.at[pl.ds(g * rows_per_copy, rows_per_copy)]
            dst = pl.ds(wid * topk + g * rows_per_copy, rows_per_copy)
            for src_hbm, dst_hbm in ((ck_hbm, ksel_hbm), (cv_hbm, vsel_hbm)):
                pltpu.async_copy(src_hbm.at[sel], rows_v, sem).wait()
                pltpu.sync_copy(rows_v, dst_hbm.at[dst])


def _sc_gather(past_keys, thr, page_table, cache_k, cache_v, *, tq, topk, ps):
    nq, key_rows, _ = past_keys.shape
    _, heads, hd = cache_k.shape
    assert ps & (ps - 1) == 0 and topk % SC_LANES == 0
    rows_per_copy = 64
    assert topk % rows_per_copy == 0
    mesh = plsc.VectorSubcoreMesh(core_axis_name="c", subcore_axis_name="s", num_cores=V7X_SC_CORES,
                                  num_subcores=V7X_SC_SUBCORES)
    assert nq <= V7X_SC_CORES * V7X_SC_SUBCORES
    sel_shape = jax.ShapeDtypeStruct((nq * topk, heads, hd), F32)
    return pl.kernel(
        functools.partial(_sc_gather_kernel, nq=nq, tq=tq, topk=topk, ps=ps, n_cores=V7X_SC_CORES,
                          rows_per_copy=rows_per_copy),
        out_type=[sel_shape, sel_shape, jax.ShapeDtypeStruct((nq, topk), I32),
                  jax.ShapeDtypeStruct((nq, SC_LANES), I32)],
        mesh=mesh,
        scratch_types=[pltpu.VMEM((key_rows, LANES), I32), pltpu.VMEM((SUBLANES, LANES), I32),
                       pltpu.VMEM((page_table.shape[1],), I32), pltpu.VMEM((topk + SC_LANES,), I32),
                       pltpu.VMEM((topk,), I32), pltpu.VMEM((rows_per_copy, heads, hd), F32),
                       pltpu.VMEM((SC_LANES,), I32), pltpu.SemaphoreType.DMA],
        compiler_params=pltpu.CompilerParams(needs_layout_passes=False),
        name="sc_select_gather",
    )(past_keys, thr, page_table, cache_k, cache_v)


def _sample_attn_sel_kernel(cnt_ref, relbt_ref, q_ref, ksel_ref, vsel_ref, pos_ref, snew_ref, thr_ref, kn_ref, vn_ref,
                            o_ref, *, heads, tq, past, n_buckets):
    w = pl.program_id(0)
    j = w % tq
    q = q_ref[...]

    def head_bias(dist):
        bucket = _bucket(dist, n_buckets)
        acc = jnp.zeros(bucket.shape, F32)
        for bkt in range(n_buckets):
            acc = jnp.where(bucket == bkt, relbt_ref[:, bkt:bkt + 1], acc)
        return acc * LOG2E

    def logits(keys):
        lg = _dot_nt(q, keys)
        head = lax.broadcasted_iota(I32, lg.shape, 0)
        col = lax.broadcasted_iota(I32, lg.shape, 1)
        return lg, (col & (heads - 1)) == head, col

    lg, own, col = logits(ksel_ref[...].astype(BF16))
    keep = own & (col < cnt_ref[w, 0] * heads)
    lg = jnp.where(keep, lg + head_bias(jnp.broadcast_to(past + j - pos_ref[...], lg.shape)), NEG_BIG)
    lgn, own, col = logits(kn_ref[...])
    keep = own & (snew_ref[...] >= thr_ref[0:1, :])
    lgn = jnp.where(keep, lgn + head_bias(j - lax.shift_right_logical(col, heads.bit_length() - 1)), NEG_BIG)

    m = jnp.maximum(jnp.max(lg, axis=1, keepdims=True), jnp.max(lgn, axis=1, keepdims=True))
    p = jnp.exp2(lg - m)
    pn = jnp.exp2(lgn - m)
    denom = jnp.sum(p, axis=1, keepdims=True) + jnp.sum(pn, axis=1, keepdims=True)
    acc = _dot(p.astype(BF16), vsel_ref[...].astype(BF16)) + _dot(pn.astype(BF16), vn_ref[...])
    o_ref[...] = acc / denom


def _sample_attn_sel(cnt, rel_bias_t, q, ksel, vsel, pos, snew, thr, kn, vn, *, heads, hd, topk, tq, past):
    nq = q.shape[0]
    ps = kn.shape[1]
    assert ps == LANES and hd == LANES and heads & (heads - 1) == 0
    per_q = lambda shape: pl.BlockSpec((None,) + shape, lambda w, cnt: (w, 0, 0))
    per_b = lambda shape: pl.BlockSpec((None,) + shape, lambda w, cnt: (w // tq, 0, 0))
    sel_spec = pl.BlockSpec((topk * heads, hd), lambda w, cnt: (w, 0))
    return pl.pallas_call(
        functools.partial(_sample_attn_sel_kernel, heads=heads, tq=tq, past=past, n_buckets=rel_bias_t.shape[1]),
        grid_spec=pltpu.PrefetchScalarGridSpec(
            num_scalar_prefetch=1,
            grid=(nq,),
            in_specs=[pl.BlockSpec(rel_bias_t.shape, lambda w, cnt: (0, 0)), per_q((heads, hd)), sel_spec, sel_spec,
                      per_q((1, topk * heads)), per_q((1, ps)), per_q((SUBLANES, LANES)), per_b((ps, hd)),
                      per_b((ps, hd))],
            out_specs=per_q((heads, hd)),
        ),
        out_shape=jax.ShapeDtypeStruct((nq, heads, hd), F32),
        compiler_params=_cparams("arbitrary"),
        name="sample_attn_sel",
    )(cnt, rel_bias_t, q, ksel, vsel, pos, snew, thr, kn, vn)


def _mix_kernel(x_ref, xs_ref, g_ref, a_ref, as_ref, c_ref, cs_ref, w_hbm, wpa_ref, wpb_ref, wo_ref, o_ref, os_ref,
                wg_ref, *, d, row0):
    _stage_rows(w_hbm, wg_ref, row0)
    n = pl.num_programs(0) - 1

    def rows(x_ref, a_ref, c_ref, o_ref):
        x = x_ref[...]
        hb = _rms_bf16(x, g_ref[...])
        a = _dot(a_ref[...], wpa_ref[...])
        m = jax.nn.sigmoid(_dot_nt(hb, wg_ref[0:d, :])) * a
        c = _dot(c_ref[...], wpb_ref[...])
        m = m + jax.nn.sigmoid(_dot_nt(hb, wg_ref[d:2 * d, :])) * c
        o_ref[...] = x + _dot(m.astype(BF16), wo_ref[...])

    @pl.when(pl.program_id(0) < n)
    def _():
        rows(x_ref, a_ref, c_ref, o_ref)

    @pl.when(pl.program_id(0) == n)
    def _():
        rows(xs_ref, as_ref, cs_ref, os_ref)


def _mix(xp, xs, g, attn_p, attn_s, c_p, c_s, w_all, row0, w_pa, w_pb, w_o, *, tm):
    m, d = xp.shape
    ms = xs.shape[0]
    n = m // tm
    assert row0 % 16 == 0
    prow = lambda width: pl.BlockSpec((tm, width), lambda i: (jnp.minimum(i, n - 1), 0))
    srow = lambda width: pl.BlockSpec((ms, width), lambda i: (0, 0))
    return pl.pallas_call(
        functools.partial(_mix_kernel, d=d, row0=row0),
        grid=(n + 1,),
        in_specs=[prow(d), srow(d), _resident((1, d)), prow(attn_p.shape[1]), srow(attn_s.shape[1]),
                  prow(c_p.shape[1]), srow(c_s.shape[1]), pl.BlockSpec(memory_space=pl.ANY), _resident(w_pa.shape),
                  _resident(w_pb.shape), _resident(w_o.shape)],
        out_specs=[prow(d), srow(d)],
        out_shape=[jax.ShapeDtypeStruct((m, d), F32), jax.ShapeDtypeStruct((ms, d), F32)],
        scratch_shapes=[pltpu.VMEM((2 * d, d), BF16)],
        compiler_params=_cparams("arbitrary"),
        name="mix_out",
    )(xp, xs, g, attn_p, attn_s, c_p, c_s, w_all, w_pa, w_pb, w_o)


def _mlp_kernel(x_ref, xs_ref, g_ref, gf_ref, w1_ref, w2_ref, y_ref, ys_ref, h_ref, *, tm, final):
    f = pl.program_id(1)
    last_tile = pl.program_id(0) == pl.num_programs(0) - 1

    def start(x_ref, y_ref, h_rows):
        h_ref[h_rows, :] = _rms_bf16(x_ref[...], g_ref[...])
        y_ref[...] = jnp.zeros(y_ref.shape, F32)

    def finish(x_ref, y_ref):
        x2 = x_ref[...] + y_ref[...]
        if final:
            x2 = x2 * lax.rsqrt(jnp.mean(x2 * x2, axis=-1, keepdims=True) + EPS) * gf_ref[...]
        y_ref[...] = x2

    def ff(h):
        t = jnp.square(jnp.maximum(_dot(h, w1_ref[...]), 0.0))
        return _dot(t.astype(BF16), w2_ref[...])

    @pl.when(f == 0)
    def _():
        start(x_ref, y_ref, slice(0, tm))

    @pl.when((f == 0) & last_tile)
    def _():
        start(xs_ref, ys_ref, slice(tm, h_ref.shape[0]))

    @pl.when(jnp.logical_not(last_tile))
    def _():
        y_ref[...] += ff(h_ref[0:tm, :])

    @pl.when(last_tile)
    def _():
        r = ff(h_ref[...])
        y_ref[...] += r[0:tm]
        ys_ref[...] += r[tm:]

    @pl.when(f == pl.num_programs(1) - 1)
    def _():
        finish(x_ref, y_ref)

    @pl.when((f == pl.num_programs(1) - 1) & last_tile)
    def _():
        finish(xs_ref, ys_ref)


def _mlp(xp, xs, g, gf, w1, w2, *, tm, tf, final):
    m, d = xp.shape
    ms = xs.shape[0]
    ff = w1.shape[1]
    srow = pl.BlockSpec((ms, d), lambda i, f: (0, 0))
    return pl.pallas_call(
        functools.partial(_mlp_kernel, tm=tm, final=final),
        grid=(m // tm, ff // tf),
        in_specs=[pl.BlockSpec((tm, d), lambda i, f: (i, 0)), srow, _resident((1, d)), _resident((1, d)),
                  pl.BlockSpec((d, tf), lambda i, f: (0, f)), pl.BlockSpec((tf, d), lambda i, f: (f, 0))],
        out_specs=[pl.BlockSpec((tm, d), lambda i, f: (i, 0)), srow],
        out_shape=[jax.ShapeDtypeStruct((m, d), F32), jax.ShapeDtypeStruct((ms, d), F32)],
        scratch_shapes=[pltpu.VMEM((tm + ms, d), BF16)],
        compiler_params=_cparams("arbitrary", "arbitrary"),
        name="mlp",
    )(xp, xs, g, gf, w1, w2)


def _tile(m, cap):
    return min(m, cap)


def kernel(x_prompt, x_sample, cache_k, cache_v, cache_kidx, state_conv, page_table, rel_bias, norm_mix_g, w_in,
           conv_w, w_pa, w_pb, w_o, norm_mlp_g, w_mlp_in, w_mlp_out, norm_final_g):
    batch, seq, d = x_prompt.shape
    db, tq, _ = x_sample.shape
    depth, n_pool, ps, heads, hd = cache_k.shape
    idim = cache_kidx.shape[-1]
    cw = conv_w.shape[-1]
    aw = heads * hd
    n_in = w_in.shape[-1]
    iheads = (n_in - 3 * aw - idim - 3 * cw - 2 * d) // (idim + 1)
    iw = iheads * idim
    assert 3 * aw + iw + idim + iheads + 3 * cw + 2 * d == n_in and idim + iheads <= LANES
    n_pages = page_table.shape[1]
    past = n_pages * ps
    rq = SUBLANES
    assert tq <= rq

    mp, ms = batch * seq, db * tq
    xp = x_prompt.reshape(mp, d)
    xs = x_sample.reshape(ms, d)
    tb = _tile(seq, 256)
    o_small = 3 * aw + iw
    o_conv = o_small + idim + iheads
    o_gate = o_conv + 3 * cw
    gf = norm_final_g.reshape(1, d)
    bias_tiles, bias_max = _bias_tiles(rel_bias, tb=tb)

    outs = {k: [] for k in ("kp", "vp", "kip", "sp", "ks", "vs", "kis", "ss")}
    for l in range(depth):
        wl = jnp.swapaxes(w_in[l], 0, 1).astype(BF16)
        g_mix = norm_mix_g[l].reshape(1, d)
        g_mlp = norm_mlp_g[l].reshape(1, d)

        tm = _tile(seq, 512)
        q, k, v, kb, vb, qi, sm, smb, kn = _proj_attn(xp, g_mix, wl, aw=aw, iw=iw, n_small=idim + iheads, hd=hd, tm=tm)
        c_in, u_tail, wpa, wpb, wo = _proj_conv(xp, g_mix, wl, o_conv, conv_w[l], None, (w_pa[l], w_pb[l], w_o[l]),
                                                seq=seq, tm=_tile(seq, 512))
        attn, w1, w2 = _attn_prompt(bias_max, bias_tiles, qi, sm, smb, q, kb, vb, kn, (w_mlp_in[l], w_mlp_out[l]),
                                    batch=batch, seq=seq, tb=tb, heads=heads, hd=hd, iheads=iheads, idim=idim,
                                    topk=min(TOPK_MAX, seq // 4))
        attn_p, c_in_p = attn, c_in
        outs["kp"].append(k.reshape(batch, seq, heads, hd))
        outs["vp"].append(v.reshape(batch, seq, heads, hd))
        outs["kip"].append(sm[:, :idim].reshape(batch, seq, idim))
        outs["sp"].append(u_tail.reshape(batch, -1, SUBLANES, cw)[:, -1, SUBLANES - 2:])

        q, k, v, kb, vb, qi, sm, smb, _ = _proj_attn(xs, g_mix, wl, aw=aw, iw=iw, n_small=idim + iheads, hd=hd, tm=ms)
        st = state_conv[l]
        zero = jnp.zeros((db, tq - 1, cw), F32)
        prev1 = jnp.concatenate([st[:, 1:2], zero], axis=1).reshape(ms, cw)
        prev2 = jnp.concatenate([st, zero[:, 1:]], axis=1).reshape(ms, cw)
        c_in, u_all = _proj_conv(xs, g_mix, wl, o_conv, conv_w[l], (prev1, prev2), seq=tq, tm=ms)

        def pad_rows(a, n):
            return jnp.pad(a, ((0, 0), (0, n - a.shape[1])) + ((0, 0),) * (a.ndim - 2))

        rh = tq if rq % tq == 0 else rq
        qi_hm = pad_rows(qi.reshape(db, tq, iheads, idim).transpose(0, 2, 1, 3).reshape(db * iheads, tq, idim), rh)
        qi_hm = qi_hm.reshape(db, iheads * rh, idim)
        w_hm = pad_rows(sm[:, idim:idim + iheads].reshape(db, tq, iheads).transpose(0, 2, 1).reshape(db * iheads, tq), rh)
        w_hm = jnp.broadcast_to(w_hm.reshape(db, iheads * rh, 1), (db, iheads * rh, LANES))
        kin_t = jnp.swapaxes(pad_rows(smb[:, :idim].reshape(db, tq, idim), ps), 1, 2)
        topk_s = min(TOPK_MAX, (past + tq) // 4)
        past_keys, snew, thr = _sample_select(page_table, qi_hm, w_hm, kin_t, jnp.swapaxes(cache_kidx[l], 1, 2),
                                              pages=math.gcd(n_pages, 128), tq=tq, rq=rq, iheads=iheads, topk=topk_s)
        ksel, vsel, pos, cnt = _sc_gather(past_keys, thr, page_table, cache_k[l].reshape(n_pool * ps, heads, hd),
                                          cache_v[l].reshape(n_pool * ps, heads, hd), tq=tq, topk=topk_s, ps=ps)
        pos_rows = jnp.repeat(pos, heads, axis=1).reshape(ms, 1, topk_s * heads)
        snew_rows = jnp.repeat(snew[:, :tq, :ps // heads], heads, axis=2).reshape(ms, 1, ps)
        attn = _sample_attn_sel(cnt, rel_bias.T, q.reshape(ms, heads, hd), ksel.reshape(ms * topk_s * heads, hd),
                                vsel.reshape(ms * topk_s * heads, hd), pos_rows, snew_rows, thr,
                                pad_rows(kb.reshape(db, tq * heads, hd), ps), pad_rows(vb.reshape(db, tq * heads, hd), ps),
                                heads=heads, hd=hd, topk=topk_s, tq=tq, past=past)
        attn = attn.reshape(ms, aw).astype(BF16)

        x1p, x1s = _mix(xp, xs, g_mix, attn_p, attn, c_in_p, c_in, wl, o_gate, wpa, wpb, wo, tm=_tile(mp, 256))
        xp_next, xs_next = _mlp(x1p, x1s, g_mlp, gf, w1, w2, tm=_tile(mp, 1024), tf=_tile(w1.shape[1], 512),
                                final=l == depth - 1)
        outs["ks"].append(k.reshape(db, tq, heads, hd))
        outs["vs"].append(v.reshape(db, tq, heads, hd))
        outs["kis"].append(sm[:, :idim].reshape(db, tq, idim))
        outs["ss"].append(u_all.reshape(db, tq, cw)[:, tq - 2:])
        xp, xs = xp_next, xs_next

    st = {k: jnp.stack(v) for k, v in outs.items()}
    return (xp.reshape(batch, seq, d), xs.reshape(db, tq, d), st["kp"], st["vp"], st["kip"], st["sp"],
            st["ks"], st["vs"], st["kis"], st["ss"])
```

```python
import functools
import math

import jax
import jax.numpy as jnp
import numpy as np
from jax import lax
from jax.experimental import pallas as pl
from jax.experimental.pallas import tpu as pltpu
from jax.experimental.pallas import tpu_sc as plsc

F32 = jnp.float32
BF16 = jnp.bfloat16
I32 = jnp.int32
I16 = jnp.int16
HALF = 1 << 15

TOPK_MAX = 256
MAX_DISTANCE = 128
EPS = 1e-6

LANES = 128
SUBLANES = 8
V7X_SCOPED_VMEM_BYTES = 60000 * 1024
SC_LANES = 16
V7X_SC_CORES = 2
V7X_SC_SUBCORES = 16

LOG2E = math.log2(math.e)
INT_MIN = np.int32(-2 ** 31)
NEG_BIG = -1e30
SOFTMAX_MIN_MASS = 2.0 ** -60
PAGE_RING_SLOTS = 3


def _cparams(*sem):
    return pltpu.CompilerParams(dimension_semantics=sem, vmem_limit_bytes=V7X_SCOPED_VMEM_BYTES)


def _resident(shape):
    nd = len(shape)
    return pl.BlockSpec(shape, lambda *_: (0,) * nd, pipeline_mode=pl.Buffered(1))


def _rms_bf16(x, g):
    y = x * lax.rsqrt(jnp.mean(x * x, axis=-1, keepdims=True) + EPS)
    return (y * g).astype(BF16)


def _dot(a, b):
    return jnp.dot(a, b, preferred_element_type=F32)


def _dot_nt(a, b):
    return lax.dot_general(a, b, (((1,), (1,)), ((), ())), preferred_element_type=F32)


def _ordered_key(x):
    b = lax.bitcast_convert_type(x, I32)
    return b ^ ((b >> 31) & np.int32(0x7FFFFFFF))


def _bucket(n, n_buckets):
    n = jnp.maximum(n, 0)
    me = n_buckets // 2
    nf = jnp.maximum(n, me).astype(F32)
    large = me + (jnp.log(nf / me) / math.log(MAX_DISTANCE / me) * (n_buckets - me)).astype(I32)
    large = jnp.minimum(large, n_buckets - 1)
    return jnp.where(n < me, n, large)


def _bias_of_bucket(bucket, relb_ref, h, n_buckets):
    acc = jnp.zeros(bucket.shape, F32)
    for bkt in range(n_buckets):
        acc = jnp.where(bucket == bkt, relb_ref[bkt, h], acc)
    return acc


def _fold_lanes(x):
    acc = x[:, 0:LANES]
    for t in range(1, x.shape[1] // LANES):
        acc = acc + x[:, t * LANES:(t + 1) * LANES]
    return acc


def _fold_rows(x, op=jnp.add, rows=SUBLANES):
    parts = [x[t * rows:(t + 1) * rows, :] for t in range(x.shape[0] // rows)]
    while len(parts) > 1:
        parts = [op(parts[t], parts[t + 1]) for t in range(0, len(parts) - 1, 2)] + parts[len(parts) & ~1:]
    return parts[0]


def _bisect(count_ge, shape, axis, k, trips, bits, at_none):
    def bit_body(t, carry):
        u, at = carry
        cand = u | jnp.left_shift(np.int32(1), bits - 1 - t)
        cnt = jnp.sum(count_ge(cand), axis=axis, keepdims=True)
        return jnp.where(cnt >= k, cand, u), jnp.where(cnt >= k, jnp.broadcast_to(cnt, shape), at)

    return lax.fori_loop(0, trips, bit_body, (jnp.zeros(shape, I32), at_none))


def _kth_threshold(count_ge, shape, axis, k, trips):
    uthr, at_thr = _bisect(lambda u: count_ge(u ^ INT_MIN), shape, axis, k, trips, 32, jnp.full(shape, k, F32))
    return jnp.maximum(uthr ^ INT_MIN, INT_MIN + 1), at_thr - k


def _flash_update(lg, v, m_ref, l_ref, acc_ref, h, hd):
    reps = lg.shape[1] // LANES
    m_prev = m_ref[h]
    m_new = jnp.maximum(m_prev, jnp.max(lg, axis=1, keepdims=True))
    p = jnp.exp2(lg - jnp.tile(m_new, (1, reps)))
    alpha = jnp.exp2(m_prev - m_new)
    pv = _dot(p.astype(BF16), jnp.concatenate([v, jnp.ones((v.shape[0], LANES), BF16)], axis=1))
    l_ref[h] = alpha * l_ref[h] + pv[:, hd:]
    m_ref[h] = m_new
    hs = slice(h * hd, (h + 1) * hd)
    acc_ref[:, hs] = acc_ref[:, hs] * jnp.tile(alpha, (1, hd // LANES)) + pv[:, :hd]


def _stage_rows(w_hbm, w_ref, row0):
    @pl.when(pl.program_id(0) == 0)
    def _():
        pltpu.sync_copy(w_hbm.at[pl.ds(row0, w_ref.shape[0])], w_ref)


def _proj_attn_kernel(x_ref, g_ref, w_hbm, q_ref, k_ref, v_ref, kb_ref, vb_ref, qi_ref,
                      sm_ref, smb_ref, kn_ref, w_ref, ws_ref, *, aw, iw, hd, tm, n_small, qscale):
    heads = aw // hd
    _stage_rows(w_hbm, w_ref, 0)

    @pl.when(pl.program_id(0) == 0)
    def _():
        ws_ref[...] = jnp.zeros(ws_ref.shape, BF16)
        pltpu.sync_copy(w_hbm.at[pl.ds(w_ref.shape[0], n_small)], ws_ref.at[pl.ds(0, n_small)])

    hb = _rms_bf16(x_ref[...], g_ref[...])
    q_ref[...] = (_dot_nt(hb, w_ref[0:aw, :]) * qscale).astype(BF16)
    for o_ref, ob_ref, c0 in ((k_ref, kb_ref, aw), (v_ref, vb_ref, 2 * aw)):
        kv = _dot_nt(hb, w_ref[c0:c0 + aw, :])
        ob_ref[...] = kv.astype(BF16)
        if o_ref is k_ref:
            kf = kv.astype(BF16).astype(F32)
            lane = lax.broadcasted_iota(I32, (tm, LANES), 1)
            kn = jnp.zeros((tm, LANES), F32)
            for h in range(heads):
                kh = kf[:, h * hd:(h + 1) * hd]
                kn = jnp.where(lane == h, jnp.sum(kh * kh, axis=1, keepdims=True), kn)
            kn_ref[...] = kn
        for h in range(heads):
            o_ref[pl.ds(h, tm, stride=heads), :] = kv[:, h * hd:(h + 1) * hd]
    qi_ref[...] = _dot_nt(hb, w_ref[3 * aw:3 * aw + iw, :]).astype(BF16)
    sm = _dot_nt(hb, ws_ref[...])
    sm_ref[...] = sm
    smb_ref[...] = sm.astype(BF16)


def _proj_attn(x2d, g, w_all, *, aw, iw, n_small, hd, tm):
    m, d = x2d.shape
    wt = 3 * aw + iw
    heads = aw // hd
    assert hd == LANES and n_small <= LANES and wt % 16 == 0 and n_small % 16 == 0
    row = lambda width: pl.BlockSpec((tm, width), lambda i: (i, 0))
    kv_spec = pl.BlockSpec((tm * heads, hd), lambda i: (i, 0))
    kv_shape = jax.ShapeDtypeStruct((m * heads, hd), F32)
    outs = [(aw, BF16), None, None, (aw, BF16), (aw, BF16), (iw, BF16), (LANES, F32), (LANES, BF16), (LANES, F32)]
    return pl.pallas_call(
        functools.partial(_proj_attn_kernel, aw=aw, iw=iw, hd=hd, tm=tm, n_small=n_small,
                          qscale=hd ** -0.5 * LOG2E),
        grid=(m // tm,),
        in_specs=[row(d), _resident((1, d)), pl.BlockSpec(memory_space=pl.ANY)],
        out_specs=[kv_spec if o is None else row(o[0]) for o in outs],
        out_shape=[kv_shape if o is None else jax.ShapeDtypeStruct((m, o[0]), o[1]) for o in outs],
        scratch_shapes=[pltpu.VMEM((wt, d), BF16), pltpu.VMEM((LANES, d), BF16)],
        compiler_params=_cparams("arbitrary"),
        name="proj_attn",
    )(x2d, g, w_all)


def _proj_conv_kernel(*refs, cw, tm, seq, tiles_per_seq, tail, row0, n_cast):
    if tiles_per_seq:
        x_ref, g_ref, w_hbm, cwt_ref = refs[:4]
        c_ref, ut_ref = refs[4 + n_cast:6 + n_cast]
        w_ref, carry_ref = refs[6 + 2 * n_cast:]
        _cast_slabs(refs[4:4 + n_cast], refs[6 + n_cast:6 + 2 * n_cast])
    else:
        x_ref, g_ref, w_hbm, cwt_ref, p1_ref, p2_ref, c_ref, ut_ref, w_ref = refs
    _stage_rows(w_hbm, w_ref, row0)
    hb = _rms_bf16(x_ref[...], g_ref[...])
    cx = _dot_nt(hb, w_ref[0:cw, :])
    cb = _dot_nt(hb, w_ref[cw:2 * cw, :])
    cc = _dot_nt(hb, w_ref[2 * cw:3 * cw, :])
    u = cc * cx
    r = lax.broadcasted_iota(I32, (tm, cw), 0)
    um1 = pltpu.roll(u, 1, 0)
    um2 = pltpu.roll(u, 2, 0)
    if tiles_per_seq:
        @pl.when(pl.program_id(0) % tiles_per_seq == 0)
        def _():
            carry_ref[...] = jnp.zeros_like(carry_ref)
        prev1 = jnp.broadcast_to(carry_ref[SUBLANES - 1:SUBLANES, :], (tm, cw))
        prev2 = jnp.broadcast_to(carry_ref[SUBLANES - 2:SUBLANES - 1, :], (tm, cw))
        um1 = jnp.where(r == 0, prev1, um1)
        um2 = jnp.where(r == 0, prev2, jnp.where(r == 1, prev1, um2))
        carry_ref[...] = u[tm - SUBLANES:tm, :]
    else:
        assert seq & (seq - 1) == 0
        t = r & (seq - 1)
        um1 = jnp.where(t >= 1, um1, p1_ref[...])
        um2 = jnp.where(t >= 2, um2, p2_ref[...])
    y = cwt_ref[0:1, :] * um2 + cwt_ref[1:2, :] * um1 + cwt_ref[2:3, :] * u
    c_ref[...] = (cb * y).astype(BF16)
    ut_ref[...] = u[tm - tail:tm, :]


def _proj_conv(x2d, g, w_all, row0, conv_w, prev, to_cast=(), *, seq, tm):
    m, d = x2d.shape
    cw = conv_w.shape[1]
    assert conv_w.shape[0] == 3 and row0 % 16 == 0
    row = lambda width: pl.BlockSpec((tm, width), lambda i: (i, 0))
    in_specs = [row(d), _resident((1, d)), pl.BlockSpec(memory_space=pl.ANY), _resident((3, cw))]
    args = [x2d, g, w_all, conv_w]
    scratch = [pltpu.VMEM((3 * cw, d), BF16)]
    if prev is None:
        assert seq % tm == 0
        tiles_per_seq, tail = seq // tm, SUBLANES
        scratch += [pltpu.VMEM((SUBLANES, cw), F32)]
    else:
        assert tm % seq == 0 and seq >= 2
        tiles_per_seq, tail = 0, tm
        in_specs += [row(cw), row(cw)]
        args += list(prev)
    assert not (to_cast and prev is not None)
    c_in, c_out, c_shapes = _cast_specs(to_cast, m // tm, lambda i: i)
    return pl.pallas_call(
        functools.partial(_proj_conv_kernel, cw=cw, tm=tm, seq=seq, tiles_per_seq=tiles_per_seq, tail=tail,
                          row0=row0, n_cast=len(to_cast)),
        grid=(m // tm,),
        in_specs=in_specs + c_in,
        out_specs=[row(cw), pl.BlockSpec((tail, cw), lambda i: (i, 0))] + c_out,
        out_shape=[jax.ShapeDtypeStruct((m, cw), BF16), jax.ShapeDtypeStruct((m // tm * tail, cw), F32)] + c_shapes,
        scratch_shapes=scratch,
        compiler_params=_cparams("arbitrary"),
        name="proj_conv",
    )(*args, *to_cast)


def _bias_tiles_kernel(relb_ref, o_ref, bmax_ref, *, tb, n_buckets):
    kind = pl.program_id(0)
    h = pl.program_id(1)
    qry = lax.broadcasted_iota(I32, (tb, tb), 0)
    key = lax.broadcasted_iota(I32, (tb, tb), 1)
    bias = _bias_of_bucket(_bucket(kind * tb + qry - key, n_buckets), relb_ref, h, n_buckets)
    o_ref[...] = (bias - relb_ref[n_buckets - 1, h]) * LOG2E
    bmax = relb_ref[0, h]
    for bkt in range(1, n_buckets):
        bmax = jnp.maximum(bmax, relb_ref[bkt, h])
    bmax_ref[...] = jnp.full(bmax_ref.shape, (bmax - relb_ref[n_buckets - 1, h]) * LOG2E, F32)


def _bias_tiles(rel_bias, *, tb):
    n_buckets, heads = rel_bias.shape
    return pl.pallas_call(
        functools.partial(_bias_tiles_kernel, tb=tb, n_buckets=n_buckets),
        grid=(2, heads),
        in_specs=[pl.BlockSpec(memory_space=pltpu.SMEM)],
        out_specs=[pl.BlockSpec((None, None, tb, tb), lambda a, h: (a, h, 0, 0)),
                   pl.BlockSpec((None, None, SUBLANES, LANES), lambda a, h: (a, h, 0, 0))],
        out_shape=[jax.ShapeDtypeStruct((2, heads, tb, tb), F32),
                   jax.ShapeDtypeStruct((2, heads, SUBLANES, LANES), F32)],
        compiler_params=_cparams("arbitrary", "arbitrary"),
        name="bias_tiles",
    )(rel_bias)


def _cast_specs(weights, n_steps, step_of):
    in_specs, out_specs, out_shapes = [], [], []
    for w in weights:
        rows, cols = w.shape
        assert rows % n_steps == 0 and (rows // n_steps) % 16 == 0
        spec = pl.BlockSpec((rows // n_steps, cols), lambda *idx: (step_of(*idx), 0))
        in_specs.append(spec)
        out_specs.append(spec)
        out_shapes.append(jax.ShapeDtypeStruct((rows, cols), BF16))
    return in_specs, out_specs, out_shapes


def _cast_slabs(src_refs, dst_refs):
    for src, dst in zip(src_refs, dst_refs):
        dst[...] = src[...].astype(BF16)


def _attn_prompt_kernel(*refs, n_cast, tb, heads, hd, iheads, idim, topk):
    bmax_ref, qi_ref, sm_ref, smb_ref, q_ref, kb_ref, vb_ref, kn_ref, bias_ref = refs[:9]
    o_ref = refs[9 + n_cast]
    (skey_ref, hi_ref, lo_ref, thr_ref, wt_ref, madd_ref, kmax_ref, shift_ref, m_ref, l_ref,
     acc_ref) = refs[10 + 2 * n_cast:]
    _cast_slabs(refs[9:9 + n_cast], refs[10 + n_cast:10 + 2 * n_cast])
    i = pl.program_id(1)
    wscale = idim ** -0.5 * iheads ** -0.5

    l_ref[...] = jnp.zeros(l_ref.shape, F32)
    acc_ref[...] = jnp.zeros(acc_ref.shape, F32)
    wt_ref[...] = sm_ref[...].T * wscale

    def chunk(j):
        return pl.ds(pl.multiple_of(j * tb, tb), tb)

    def score_chunk(j, diag):
        kic = smb_ref[chunk(j), 0:idim]
        acc = jnp.zeros((tb, tb), F32)
        for h in range(iheads):
            s = _dot_nt(kic, qi_ref[:, h * idim:(h + 1) * idim])
            acc = acc + jnp.maximum(s, 0.0) * wt_ref[idim + h:idim + h + 1, :]
        key = _ordered_key(acc)
        if diag:
            kpos = lax.broadcasted_iota(I32, (tb, tb), 0)
            qpos = lax.broadcasted_iota(I32, (tb, tb), 1)
            key = jnp.where(kpos > qpos, INT_MIN, key)
        skey_ref[chunk(j), :] = key
        hi_ref[chunk(j), :] = (key >> 16).astype(I16)
        lo_ref[chunk(j), :] = ((key & 0xFFFF) - HALF).astype(I16)

    def score_body(j, carry):
        score_chunk(j, False)
        return carry

    lax.fori_loop(0, i, score_body, 0)
    score_chunk(i, True)

    def counter(half_ref):
        def count_ge(u):
            cand = (u - HALF).astype(I16)[0:1, :]

            def body(j, cnt):
                ge = jnp.where(half_ref[chunk(j), :] >= cand, jnp.int16(1), jnp.int16(0))
                return cnt + _fold_rows(ge, rows=2 * SUBLANES)

            return lax.fori_loop(0, i + 1, body, jnp.zeros((2 * SUBLANES, tb), I16)).astype(F32)

        return count_ge

    shape = (SUBLANES, tb)
    trips = jnp.where((i + 1) * tb <= topk, 0, 16)
    u_hi, n_ge_hi = _bisect(counter(hi_ref), shape, 0, float(topk), trips, 16, jnp.full(shape, topk, F32))
    t_hi = (u_hi - HALF).astype(I16)[0:1, :]
    n_gt_hi = jnp.where(u_hi == 2 * HALF - 1, 0.0,
                        jnp.sum(counter(hi_ref)(jnp.minimum(u_hi + 1, 2 * HALF - 1)), axis=0, keepdims=True))

    def keep_low_of_kth_high(j, carry):
        lo_ref[chunk(j), :] = jnp.where(hi_ref[chunk(j), :] == t_hi, lo_ref[chunk(j), :], jnp.int16(-HALF))
        return carry

    lax.fori_loop(0, i + 1, keep_low_of_kth_high, 0)
    u_lo, n_ge_lo = _bisect(counter(lo_ref), shape, 0, topk - n_gt_hi, trips, 16, n_ge_hi - n_gt_hi)
    thr = jnp.maximum(jnp.left_shift(u_hi - HALF, 16) | u_lo, INT_MIN + 1)
    surplus = n_gt_hi + n_ge_lo - topk
    thr_ref[...] = thr

    @pl.when(jnp.max(surplus) > 0.0)
    def _():
        thr_row = thr_ref[0:1, :]

        def gt_body(j, cnt):
            return cnt + _fold_rows(jnp.where(skey_ref[chunk(j), :] > thr_row, 1.0, 0.0))

        above = lax.fori_loop(0, i + 1, gt_body, jnp.zeros((SUBLANES, tb), F32))
        budget = topk - jnp.sum(above, axis=0, keepdims=True)
        kpos = lax.broadcasted_iota(I32, (tb, tb), 0)
        earlier = jnp.where(kpos > lax.broadcasted_iota(I32, (tb, tb), 1), 1.0, 0.0).astype(BF16)

        def fix_body(j, seen):
            sk = skey_ref[chunk(j), :]
            tie = jnp.where(sk == thr_row, 1.0, 0.0)
            rank = seen[0:1, :] + _dot(earlier, tie.astype(BF16))
            skey_ref[chunk(j), :] = jnp.where((tie > 0.0) & (rank >= budget), thr_row - 1, sk)
            return seen + jnp.sum(_fold_rows(tie), axis=0, keepdims=True)

        lax.fori_loop(0, i + 1, fix_body, jnp.zeros((SUBLANES, tb), F32))

    @pl.when(i == 0)
    def _():
        kmax_ref[...] = jnp.zeros(kmax_ref.shape, F32)

    kmax_ref[...] = jnp.maximum(kmax_ref[...], jnp.max(kn_ref[chunk(i), :], axis=0, keepdims=True))
    qsq = q_ref[...] * q_ref[...]
    for h in range(heads):
        qn2 = _dot(qsq[:, h * hd:(h + 1) * hd], jnp.ones((hd, LANES), BF16)) * (1.0 + 2.0 ** -6)
        shift_ref[h] = jnp.sqrt(qn2 * kmax_ref[0:1, h:h + 1]) + bmax_ref[0, h][0:1, :]

    def logits(j, h, kind):
        hs = slice(h * hd, (h + 1) * hd)
        lg = _dot_nt(q_ref[:, hs], kb_ref[chunk(j), hs]) + madd_ref[...]
        return lg if kind is None else lg + bias_ref[kind, h]

    def bounded_chunk(j, kind):
        for h in range(heads):
            hs = slice(h * hd, (h + 1) * hd)
            p = jnp.exp2(logits(j, h, kind) - jnp.tile(shift_ref[h], (1, tb // LANES)))
            v1 = jnp.concatenate([vb_ref[chunk(j), hs], jnp.ones((tb, LANES), BF16)], axis=1)
            pv = _dot(p.astype(BF16), v1)
            l_ref[h] += pv[:, hd:]
            acc_ref[:, hs] += pv[:, :hd]

    def running_max_chunk(j, kind):
        for h in range(heads):
            _flash_update(logits(j, h, kind), vb_ref[chunk(j), h * hd:(h + 1) * hd], m_ref, l_ref, acc_ref, h, hd)

    def attend(update):
        def one(j, kind):
            madd_ref[...] = jnp.where(skey_ref[chunk(j), :] >= thr_ref[0:1, :], 0.0, NEG_BIG).T
            update(j, kind)

        def far_body(j, carry):
            one(j, None)
            return carry

        lax.fori_loop(0, jnp.maximum(i - 1, 0), far_body, 0)

        @pl.when(i >= 1)
        def _():
            one(i - 1, 1)

        one(i, 0)

    attend(bounded_chunk)

    @pl.when(jnp.min(l_ref[...]) < SOFTMAX_MIN_MASS)
    def _():
        m_ref[...] = jnp.full(m_ref.shape, NEG_BIG, F32)
        l_ref[...] = jnp.zeros(l_ref.shape, F32)
        acc_ref[...] = jnp.zeros(acc_ref.shape, F32)
        attend(running_max_chunk)

    for h in range(heads):
        hs = slice(h * hd, (h + 1) * hd)
        o_ref[:, hs] = (acc_ref[:, hs] / jnp.tile(l_ref[h], (1, hd // LANES))).astype(BF16)


def _attn_prompt(bias_max, bias_tiles, qi, sm, smb, q, kb, vb, kn, to_cast, *, batch, seq, tb, heads, hd, iheads,
                 idim, topk):
    m, aw = q.shape
    iw = qi.shape[1]
    nq = seq // tb
    assert tb >= MAX_DISTANCE and tb % LANES == 0 and hd % LANES == 0
    qrow = lambda width: pl.BlockSpec((tb, width), lambda b, i: (b * nq + i, 0))
    seqblk = lambda width: pl.BlockSpec((seq, width), lambda b, i: (b, 0))
    c_in, c_out, c_shapes = _cast_specs(to_cast, batch * nq, lambda b, i: b * nq + i)
    return pl.pallas_call(
        functools.partial(_attn_prompt_kernel, n_cast=len(to_cast), tb=tb, heads=heads, hd=hd, iheads=iheads,
                          idim=idim, topk=topk),
        grid=(batch, nq),
        in_specs=[_resident(bias_max.shape), qrow(iw), qrow(LANES), seqblk(LANES), qrow(aw), seqblk(aw),
                  seqblk(aw), seqblk(LANES), _resident(bias_tiles.shape)] + c_in,
        out_specs=[qrow(aw)] + c_out,
        out_shape=[jax.ShapeDtypeStruct((m, aw), BF16)] + c_shapes,
        scratch_shapes=[pltpu.VMEM((seq, tb), I32), pltpu.VMEM((seq, tb), I16), pltpu.VMEM((seq, tb), I16),
                        pltpu.VMEM((SUBLANES, tb), I32),
                        pltpu.VMEM((LANES, tb), F32), pltpu.VMEM((tb, tb), F32),
                        pltpu.VMEM((SUBLANES, LANES), F32), pltpu.VMEM((heads, tb, LANES), F32),
                        pltpu.VMEM((heads, tb, LANES), F32), pltpu.VMEM((heads, tb, LANES), F32),
                        pltpu.VMEM((tb, aw), F32)],
        compiler_params=_cparams("arbitrary", "arbitrary"),
        name="attn_prompt",
    )(bias_max, qi, sm, smb, q, kb, vb, kn, bias_tiles, *to_cast)


def _sample_select_kernel(pt_ref, qi_ref, w_ref, kin_ref, *rest, pages, ps, nc, iheads, idim, tq, rq, topk):
    kidx_hbm, past_ref, new_ref, thr_ref, row_ref, real_ref, pg_ref, sem = rest
    n_buf = pg_ref.shape[0]
    b = pl.program_id(0)
    c = pl.program_id(1)
    db = pl.num_programs(0)
    ch = pages * ps
    rh = qi_ref.shape[0] // iheads
    wscale = idim ** -0.5 * iheads ** -0.5
    rows_b = pl.ds(pl.multiple_of(b * rq, rq), rq)

    def score(keys_t):
        s = _dot(qi_ref[...], keys_t)
        t = jnp.maximum(s, 0.0) * (w_ref[:, 0:1] * wscale)
        acc = _fold_rows(t, rows=rq)
        step = rh
        while step < rq:
            acc = acc + pltpu.roll(acc, step, 0)
            step *= 2
        return _ordered_key(acc)

    def page_copies(t, slot):
        bb, cc = t // nc, t % nc
        return [pltpu.make_async_copy(kidx_hbm.at[pt_ref[bb, cc * pages + p]], pg_ref.at[slot, p], sem.at[slot])
                for p in range(pages)]

    def start_pages(t):
        @pl.when(t < db * nc)
        def _():
            for p, cp in enumerate(page_copies(t, t % n_buf)):
                cp.start(priority=p % 2)

    @pl.when((b == 0) & (c == 0))
    def _():
        for t0 in range(n_buf - 1):
            start_pages(jnp.int32(t0))

    @pl.when(c < nc)
    def _():
        t_now = b * nc + c
        slot = t_now % n_buf
        start_pages(t_now + n_buf - 1)
        for cp in page_copies(t_now, slot):
            cp.wait()
        for p in range(pages):
            key = score(pg_ref[slot, p].astype(BF16))
            row_ref[rows_b, pl.ds(pl.multiple_of(c * ch + p * ps, LANES), ps)] = key
            for t in range(ps // LANES):
                for j in range(tq):
                    past_ref[j, p * (ps // LANES) + t:p * (ps // LANES) + t + 1, :] = \
                        key[j:j + 1, t * LANES:(t + 1) * LANES]

    @pl.when(c == nc)
    def _():
        key = score(kin_ref[...])
        j = lax.broadcasted_iota(I32, (rq, ps), 0)
        n = lax.broadcasted_iota(I32, (rq, ps), 1)
        key = jnp.where((n <= j) & (n < tq), key, INT_MIN)
        row_ref[rows_b, nc * ch:nc * ch + ps] = key

    @pl.when((c == nc) & (b == db - 1))
    def _():
        n_rows = real_ref.shape[0]
        for bb in range(n_rows // tq):
            real_ref[bb * tq:(bb + 1) * tq, :] = row_ref[bb * rq:bb * rq + tq, :]

        def count_ge(scand):
            sk = real_ref[...]
            return _fold_lanes(jnp.where(sk >= jnp.tile(scand, (1, sk.shape[1] // LANES)), 1.0, 0.0))

        thr, _ = _kth_threshold(count_ge, (n_rows, LANES), 1, float(topk), 32)
        count = lambda m: jnp.sum(_fold_lanes(jnp.where(m, 1.0, 0.0)), axis=1, keepdims=True)
        past_keys = real_ref[:, 0:nc * ch]
        new_keys = real_ref[:, nc * ch:nc * ch + ps]
        thr_new = jnp.tile(thr, (1, ps // LANES))
        gt_new = count(new_keys > thr_new)
        ties_allowed = topk - gt_new - count(past_keys > jnp.tile(thr, (1, nc * ch // LANES)))
        ties_past = count(past_keys == jnp.tile(thr, (1, nc * ch // LANES)))
        ties_new_allowed = ties_allowed - jnp.minimum(ties_past, ties_allowed)
        tie = jnp.where(new_keys == thr_new, 1.0, 0.0)
        lane = lax.broadcasted_iota(I32, tie.shape, 1)
        rank = jnp.zeros(tie.shape, F32)
        for s in range(1, tq):
            rank = rank + jnp.where(lane >= s, pltpu.roll(tie, s, 1), 0.0)
        new_keys = jnp.where((tie > 0.0) & (rank >= ties_new_allowed), thr_new - 1, new_keys)
        budget = jnp.broadcast_to(topk - gt_new, (n_rows, LANES)).astype(I32)
        thr_ref[...] = jnp.zeros(thr_ref.shape, I32)
        new_ref[...] = jnp.full(new_ref.shape, INT_MIN, I32)
        for bb in range(n_rows // tq):
            new_ref[bb, 0:tq, :] = new_keys[bb * tq:(bb + 1) * tq]
            for jj in range(tq):
                thr_ref[bb * tq + jj, 0:1, :] = thr[bb * tq + jj:bb * tq + jj + 1, :]
                thr_ref[bb * tq + jj, 1:2, :] = budget[bb * tq + jj:bb * tq + jj + 1, :]


def _sample_select(page_table, qi_hm, w_hm, kin_t, cache_kidx_t, *, pages, tq, rq, iheads, topk):
    db, n_pages = page_table.shape
    assert qi_hm.shape[1] % iheads == 0 and rq % (qi_hm.shape[1] // iheads) == 0 and qi_hm.shape[1] % 16 == 0
    _, idim, ps = cache_kidx_t.shape
    nc = n_pages // pages
    ch = pages * ps
    assert ps % LANES == 0 and (ch // LANES) % SUBLANES == 0 and (db * tq) % SUBLANES == 0
    per_b = lambda shape: pl.BlockSpec((None,) + shape, lambda b, c, pt: (b, 0, 0))
    return pl.pallas_call(
        functools.partial(_sample_select_kernel, pages=pages, ps=ps, nc=nc, iheads=iheads, idim=idim, tq=tq, rq=rq,
                          topk=topk),
        grid_spec=pltpu.PrefetchScalarGridSpec(
            num_scalar_prefetch=1,
            grid=(db, nc + 1),
            in_specs=[per_b(qi_hm.shape[1:]), per_b(w_hm.shape[1:]), per_b((idim, ps)),
                      pl.BlockSpec(memory_space=pl.ANY)],
            out_specs=[pl.BlockSpec((tq, ch // LANES, LANES), lambda b, c, pt: (b, jnp.minimum(c, nc - 1), 0)),
                       pl.BlockSpec((db, rq, ps), lambda b, c, pt: (0, 0, 0)),
                       pl.BlockSpec((db * tq, SUBLANES, LANES), lambda b, c, pt: (0, 0, 0))],
            scratch_shapes=[pltpu.VMEM((db * rq, nc * ch + ps), I32), pltpu.VMEM((db * tq, nc * ch + ps), I32),
                            pltpu.VMEM((PAGE_RING_SLOTS, pages, idim, ps), F32),
                            pltpu.SemaphoreType.DMA((PAGE_RING_SLOTS,))],
        ),
        out_shape=[jax.ShapeDtypeStruct((db * tq, n_pages * ps // LANES, LANES), I32),
                   jax.ShapeDtypeStruct((db, rq, ps), I32),
                   jax.ShapeDtypeStruct((db * tq, SUBLANES, LANES), I32)],
        compiler_params=_cparams("arbitrary", "arbitrary"),
        name="sample_select",
    )(page_table, qi_hm, w_hm, kin_t, cache_kidx_t)


def _sc_gather_kernel(past_hbm, thr_hbm, pt_hbm, ck_hbm, cv_hbm, ksel_hbm, vsel_hbm, pos_hbm, cnt_hbm,
                      row_v, thr_v, pt_v, idx_v, phys_v, rows_v, cnt_v, sem,
                      *, nq, tq, topk, ps, n_cores, rows_per_copy):
    wid = lax.axis_index("s") * n_cores + lax.axis_index("c")

    @pl.when(wid < nq)
    def _():
        pltpu.sync_copy(past_hbm.at[wid], row_v)
        pltpu.sync_copy(thr_hbm.at[wid], thr_v)
        pltpu.sync_copy(pt_hbm.at[wid // tq], pt_v)
        thr = thr_v[0, pl.ds(0, SC_LANES)]
        budget = thr_v[1, pl.ds(0, SC_LANES)]
        lane = lax.iota(I32, SC_LANES)
        zero = jnp.zeros((SC_LANES,), I32)
        for t in range(idx_v.shape[0] // SC_LANES):
            idx_v[pl.ds(t * SC_LANES, SC_LANES)] = zero

        def compact(pred):
            def body(r, cnt):
                for t in range(LANES // SC_LANES):
                    x = row_v[r, pl.ds(t * SC_LANES, SC_LANES)]
                    m = pred(x, cnt)
                    rank = plsc.cumsum(jnp.where(m, 1, 0).astype(I32))
                    plsc.store_scatter(idx_v, [cnt + rank - 1], lane + (r * LANES + t * SC_LANES), mask=m)
                    cnt = cnt + plsc.all_reduce_population_count(m)
                return cnt
            return body

        cnt = lax.fori_loop(0, row_v.shape[0], compact(lambda x, cnt: x > thr), zero)
        cnt = lax.fori_loop(0, row_v.shape[0], compact(lambda x, cnt: (x == thr) & (cnt < budget)), cnt)
        cnt_v[...] = jnp.minimum(cnt, budget)
        pltpu.sync_copy(cnt_v, cnt_hbm.at[wid])
        pltpu.sync_copy(idx_v.at[pl.ds(0, topk)], pos_hbm.at[wid])

        shift = ps.bit_length() - 1
        for t in range(topk // SC_LANES):
            pos = idx_v[pl.ds(t * SC_LANES, SC_LANES)]
            page = plsc.load_gather(pt_v, [lax.shift_right_logical(pos, shift)])
            phys_v[pl.ds(t * SC_LANES, SC_LANES)] = page * ps + (pos & (ps - 1))
        for g in range(topk // rows_per_copy):
            sel = phys_v.at[pl.ds(g * rows_per_copy, rows_per_copy)]
            dst = pl.ds(wid * topk + g * rows_per_copy, rows_per_copy)
            for src_hbm, dst_hbm in ((ck_hbm, ksel_hbm), (cv_hbm, vsel_hbm)):
                pltpu.async_copy(src_hbm.at[sel], rows_v, sem).wait()
                pltpu.sync_copy(rows_v, dst_hbm.at[dst])


def _sc_gather(past_keys, thr, page_table, cache_k, cache_v, *, tq, topk, ps):
    nq, key_rows, _ = past_keys.shape
    _, heads, hd = cache_k.shape
    assert ps & (ps - 1) == 0 and topk % SC_LANES == 0
    rows_per_copy = 64
    assert topk % rows_per_copy == 0
    mesh = plsc.VectorSubcoreMesh(core_axis_name="c", subcore_axis_name="s", num_cores=V7X_SC_CORES,
                                  num_subcores=V7X_SC_SUBCORES)
    assert nq <= V7X_SC_CORES * V7X_SC_SUBCORES
    sel_shape = jax.ShapeDtypeStruct((nq * topk, heads, hd), F32)
    return pl.kernel(
        functools.partial(_sc_gather_kernel, nq=nq, tq=tq, topk=topk, ps=ps, n_cores=V7X_SC_CORES,
                          rows_per_copy=rows_per_copy),
        out_type=[sel_shape, sel_shape, jax.ShapeDtypeStruct((nq, topk), I32),
                  jax.ShapeDtypeStruct((nq, SC_LANES), I32)],
        mesh=mesh,
        scratch_types=[pltpu.VMEM((key_rows, LANES), I32), pltpu.VMEM((SUBLANES, LANES), I32),
                       pltpu.VMEM((page_table.shape[1],), I32), pltpu.VMEM((topk + SC_LANES,), I32),
                       pltpu.VMEM((topk,), I32), pltpu.VMEM((rows_per_copy, heads, hd), F32),
                       pltpu.VMEM((SC_LANES,), I32), pltpu.SemaphoreType.DMA],
        compiler_params=pltpu.CompilerParams(needs_layout_passes=False),
        name="sc_select_gather",
    )(past_keys, thr, page_table, cache_k, cache_v)


def _sample_attn_sel_kernel(cnt_ref, relbt_ref, q_ref, ksel_ref, vsel_ref, pos_ref, snew_ref, thr_ref, kn_ref, vn_ref,
                            o_ref, *, heads, tq, past, n_buckets):
    w = pl.program_id(0)
    j = w % tq
    q = q_ref[...]

    def head_bias(dist):
        bucket = _bucket(dist, n_buckets)
        acc = jnp.zeros(bucket.shape, F32)
        for bkt in range(n_buckets):
            acc = jnp.where(bucket == bkt, relbt_ref[:, bkt:bkt + 1], acc)
        return acc * LOG2E

    def logits(keys):
        lg = _dot_nt(q, keys)
        head = lax.broadcasted_iota(I32, lg.shape, 0)
        col = lax.broadcasted_iota(I32, lg.shape, 1)
        return lg, (col & (heads - 1)) == head, col

    lg, own, col = logits(ksel_ref[...].astype(BF16))
    keep = own & (col < cnt_ref[w, 0] * heads)
    lg = jnp.where(keep, lg + head_bias(jnp.broadcast_to(past + j - pos_ref[...], lg.shape)), NEG_BIG)
    lgn, own, col = logits(kn_ref[...])
    keep = own & (snew_ref[...] >= thr_ref[0:1, :])
    lgn = jnp.where(keep, lgn + head_bias(j - lax.shift_right_logical(col, heads.bit_length() - 1)), NEG_BIG)

    m = jnp.maximum(jnp.max(lg, axis=1, keepdims=True), jnp.max(lgn, axis=1, keepdims=True))
    p = jnp.exp2(lg - m)
    pn = jnp.exp2(lgn - m)
    denom = jnp.sum(p, axis=1, keepdims=True) + jnp.sum(pn, axis=1, keepdims=True)
    acc = _dot(p.astype(BF16), vsel_ref[...].astype(BF16)) + _dot(pn.astype(BF16), vn_ref[...])
    o_ref[...] = acc / denom


def _sample_attn_sel(cnt, rel_bias_t, q, ksel, vsel, pos, snew, thr, kn, vn, *, heads, hd, topk, tq, past):
    nq = q.shape[0]
    ps = kn.shape[1]
    assert ps == LANES and hd == LANES and heads & (heads - 1) == 0
    per_q = lambda shape: pl.BlockSpec((None,) + shape, lambda w, cnt: (w, 0, 0))
    per_b = lambda shape: pl.BlockSpec((None,) + shape, lambda w, cnt: (w // tq, 0, 0))
    sel_spec = pl.BlockSpec((topk * heads, hd), lambda w, cnt: (w, 0))
    return pl.pallas_call(
        functools.partial(_sample_attn_sel_kernel, heads=heads, tq=tq, past=past, n_buckets=rel_bias_t.shape[1]),
        grid_spec=pltpu.PrefetchScalarGridSpec(
            num_scalar_prefetch=1,
            grid=(nq,),
            in_specs=[pl.BlockSpec(rel_bias_t.shape, lambda w, cnt: (0, 0)), per_q((heads, hd)), sel_spec, sel_spec,
                      per_q((1, topk * heads)), per_q((1, ps)), per_q((SUBLANES, LANES)), per_b((ps, hd)),
                      per_b((ps, hd))],
            out_specs=per_q((heads, hd)),
        ),
        out_shape=jax.ShapeDtypeStruct((nq, heads, hd), F32),
        compiler_params=_cparams("arbitrary"),
        name="sample_attn_sel",
    )(cnt, rel_bias_t, q, ksel, vsel, pos, snew, thr, kn, vn)


def _mix_kernel(x_ref, xs_ref, g_ref, a_ref, as_ref, c_ref, cs_ref, w_hbm, wpa_ref, wpb_ref, wo_ref, o_ref, os_ref,
                wg_ref, *, d, row0):
    _stage_rows(w_hbm, wg_ref, row0)
    n = pl.num_programs(0) - 1

    def rows(x_ref, a_ref, c_ref, o_ref):
        x = x_ref[...]
        hb = _rms_bf16(x, g_ref[...])
        a = _dot(a_ref[...], wpa_ref[...])
        m = jax.nn.sigmoid(_dot_nt(hb, wg_ref[0:d, :])) * a
        c = _dot(c_ref[...], wpb_ref[...])
        m = m + jax.nn.sigmoid(_dot_nt(hb, wg_ref[d:2 * d, :])) * c
        o_ref[...] = x + _dot(m.astype(BF16), wo_ref[...])

    @pl.when(pl.program_id(0) < n)
    def _():
        rows(x_ref, a_ref, c_ref, o_ref)

    @pl.when(pl.program_id(0) == n)
    def _():
        rows(xs_ref, as_ref, cs_ref, os_ref)


def _mix(xp, xs, g, attn_p, attn_s, c_p, c_s, w_all, row0, w_pa, w_pb, w_o, *, tm):
    m, d = xp.shape
    ms = xs.shape[0]
    n = m // tm
    assert row0 % 16 == 0
    prow = lambda width: pl.BlockSpec((tm, width), lambda i: (jnp.minimum(i, n - 1), 0))
    srow = lambda width: pl.BlockSpec((ms, width), lambda i: (0, 0))
    return pl.pallas_call(
        functools.partial(_mix_kernel, d=d, row0=row0),
        grid=(n + 1,),
        in_specs=[prow(d), srow(d), _resident((1, d)), prow(attn_p.shape[1]), srow(attn_s.shape[1]),
                  prow(c_p.shape[1]), srow(c_s.shape[1]), pl.BlockSpec(memory_space=pl.ANY), _resident(w_pa.shape),
                  _resident(w_pb.shape), _resident(w_o.shape)],
        out_specs=[prow(d), srow(d)],
        out_shape=[jax.ShapeDtypeStruct((m, d), F32), jax.ShapeDtypeStruct((ms, d), F32)],
        scratch_shapes=[pltpu.VMEM((2 * d, d), BF16)],
        compiler_params=_cparams("arbitrary"),
        name="mix_out",
    )(xp, xs, g, attn_p, attn_s, c_p, c_s, w_all, w_pa, w_pb, w_o)


def _mlp_kernel(x_ref, xs_ref, g_ref, gf_ref, w1_ref, w2_ref, y_ref, ys_ref, h_ref, *, tm, final):
    f = pl.program_id(1)
    last_tile = pl.program_id(0) == pl.num_programs(0) - 1

    def start(x_ref, y_ref, h_rows):
        h_ref[h_rows, :] = _rms_bf16(x_ref[...], g_ref[...])
        y_ref[...] = jnp.zeros(y_ref.shape, F32)

    def finish(x_ref, y_ref):
        x2 = x_ref[...] + y_ref[...]
        if final:
            x2 = x2 * lax.rsqrt(jnp.mean(x2 * x2, axis=-1, keepdims=True) + EPS) * gf_ref[...]
        y_ref[...] = x2

    def ff(h):
        t = jnp.square(jnp.maximum(_dot(h, w1_ref[...]), 0.0))
        return _dot(t.astype(BF16), w2_ref[...])

    @pl.when(f == 0)
    def _():
        start(x_ref, y_ref, slice(0, tm))

    @pl.when((f == 0) & last_tile)
    def _():
        start(xs_ref, ys_ref, slice(tm, h_ref.shape[0]))

    @pl.when(jnp.logical_not(last_tile))
    def _():
        y_ref[...] += ff(h_ref[0:tm, :])

    @pl.when(last_tile)
    def _():
        r = ff(h_ref[...])
        y_ref[...] += r[0:tm]
        ys_ref[...] += r[tm:]

    @pl.when(f == pl.num_programs(1) - 1)
    def _():
        finish(x_ref, y_ref)

    @pl.when((f == pl.num_programs(1) - 1) & last_tile)
    def _():
        finish(xs_ref, ys_ref)


def _mlp(xp, xs, g, gf, w1, w2, *, tm, tf, final):
    m, d = xp.shape
    ms = xs.shape[0]
    ff = w1.shape[1]
    srow = pl.BlockSpec((ms, d), lambda i, f: (0, 0))
    return pl.pallas_call(
        functools.partial(_mlp_kernel, tm=tm, final=final),
        grid=(m // tm, ff // tf),
        in_specs=[pl.BlockSpec((tm, d), lambda i, f: (i, 0)), srow, _resident((1, d)), _resident((1, d)),
                  pl.BlockSpec((d, tf), lambda i, f: (0, f)), pl.BlockSpec((tf, d), lambda i, f: (f, 0))],
        out_specs=[pl.BlockSpec((tm, d), lambda i, f: (i, 0)), srow],
        out_shape=[jax.ShapeDtypeStruct((m, d), F32), jax.ShapeDtypeStruct((ms, d), F32)],
        scratch_shapes=[pltpu.VMEM((tm + ms, d), BF16)],
        compiler_params=_cparams("arbitrary", "arbitrary"),
        name="mlp",
    )(xp, xs, g, gf, w1, w2)


def _tile(m, cap):
    return min(m, cap)


def kernel(x_prompt, x_sample, cache_k, cache_v, cache_kidx, state_conv, page_table, rel_bias, norm_mix_g, w_in,
           conv_w, w_pa, w_pb, w_o, norm_mlp_g, w_mlp_in, w_mlp_out, norm_final_g):
    batch, seq, d = x_prompt.shape
    db, tq, _ = x_sample.shape
    depth, n_pool, ps, heads, hd = cache_k.shape
    idim = cache_kidx.shape[-1]
    cw = conv_w.shape[-1]
    aw = heads * hd
    n_in = w_in.shape[-1]
    iheads = (n_in - 3 * aw - idim - 3 * cw - 2 * d) // (idim + 1)
    iw = iheads * idim
    assert 3 * aw + iw + idim + iheads + 3 * cw + 2 * d == n_in and idim + iheads <= LANES
    n_pages = page_table.shape[1]
    past = n_pages * ps
    rq = SUBLANES
    assert tq <= rq

    mp, ms = batch * seq, db * tq
    xp = x_prompt.reshape(mp, d)
    xs = x_sample.reshape(ms, d)
    tb = _tile(seq, 256)
    o_small = 3 * aw + iw
    o_conv = o_small + idim + iheads
    o_gate = o_conv + 3 * cw
    gf = norm_final_g.reshape(1, d)
    bias_tiles, bias_max = _bias_tiles(rel_bias, tb=tb)

    outs = {k: [] for k in ("kp", "vp", "kip", "sp", "ks", "vs", "kis", "ss")}
    for l in range(depth):
        wl = jnp.swapaxes(w_in[l], 0, 1).astype(BF16)
        g_mix = norm_mix_g[l].reshape(1, d)
        g_mlp = norm_mlp_g[l].reshape(1, d)

        tm = _tile(seq, 512)
        q, k, v, kb, vb, qi, sm, smb, kn = _proj_attn(xp, g_mix, wl, aw=aw, iw=iw, n_small=idim + iheads, hd=hd, tm=tm)
        c_in, u_tail, wpa, wpb, wo = _proj_conv(xp, g_mix, wl, o_conv, conv_w[l], None, (w_pa[l], w_pb[l], w_o[l]),
                                                seq=seq, tm=_tile(seq, 512))
        attn, w1, w2 = _attn_prompt(bias_max, bias_tiles, qi, sm, smb, q, kb, vb, kn, (w_mlp_in[l], w_mlp_out[l]),
                                    batch=batch, seq=seq, tb=tb, heads=heads, hd=hd, iheads=iheads, idim=idim,
                                    topk=min(TOPK_MAX, seq // 4))
        attn_p, c_in_p = attn, c_in
        outs["kp"].append(k.reshape(batch, seq, heads, hd))
        outs["vp"].append(v.reshape(batch, seq, heads, hd))
        outs["kip"].append(sm[:, :idim].reshape(batch, seq, idim))
        outs["sp"].append(u_tail.reshape(batch, -1, SUBLANES, cw)[:, -1, SUBLANES - 2:])

        q, k, v, kb, vb, qi, sm, smb, _ = _proj_attn(xs, g_mix, wl, aw=aw, iw=iw, n_small=idim + iheads, hd=hd, tm=ms)
        st = state_conv[l]
        zero = jnp.zeros((db, tq - 1, cw), F32)
        prev1 = jnp.concatenate([st[:, 1:2], zero], axis=1).reshape(ms, cw)
        prev2 = jnp.concatenate([st, zero[:, 1:]], axis=1).reshape(ms, cw)
        c_in, u_all = _proj_conv(xs, g_mix, wl, o_conv, conv_w[l], (prev1, prev2), seq=tq, tm=ms)

        def pad_rows(a, n):
            return jnp.pad(a, ((0, 0), (0, n - a.shape[1])) + ((0, 0),) * (a.ndim - 2))

        rh = tq if rq % tq == 0 else rq
        qi_hm = pad_rows(qi.reshape(db, tq, iheads, idim).transpose(0, 2, 1, 3).reshape(db * iheads, tq, idim), rh)
        qi_hm = qi_hm.reshape(db, iheads * rh, idim)
        w_hm = pad_rows(sm[:, idim:idim + iheads].reshape(db, tq, iheads).transpose(0, 2, 1).reshape(db * iheads, tq), rh)
        w_hm = jnp.broadcast_to(w_hm.reshape(db, iheads * rh, 1), (db, iheads * rh, LANES))
        kin_t = jnp.swapaxes(pad_rows(smb[:, :idim].reshape(db, tq, idim), ps), 1, 2)
        topk_s = min(TOPK_MAX, (past + tq) // 4)
        past_keys, snew, thr = _sample_select(page_table, qi_hm, w_hm, kin_t, jnp.swapaxes(cache_kidx[l], 1, 2),
                                              pages=math.gcd(n_pages, 32), tq=tq, rq=rq, iheads=iheads, topk=topk_s)
        ksel, vsel, pos, cnt = _sc_gather(past_keys, thr, page_table, cache_k[l].reshape(n_pool * ps, heads, hd),
                                          cache_v[l].reshape(n_pool * ps, heads, hd), tq=tq, topk=topk_s, ps=ps)
        pos_rows = jnp.repeat(pos, heads, axis=1).reshape(ms, 1, topk_s * heads)
        snew_rows = jnp.repeat(snew[:, :tq, :ps // heads], heads, axis=2).reshape(ms, 1, ps)
        attn = _sample_attn_sel(cnt, rel_bias.T, q.reshape(ms, heads, hd), ksel.reshape(ms * topk_s * heads, hd),
                                vsel.reshape(ms * topk_s * heads, hd), pos_rows, snew_rows, thr,
                                pad_rows(kb.reshape(db, tq * heads, hd), ps), pad_rows(vb.reshape(db, tq * heads, hd), ps),
                                heads=heads, hd=hd, topk=topk_s, tq=tq, past=past)
        attn = attn.reshape(ms, aw).astype(BF16)

        x1p, x1s = _mix(xp, xs, g_mix, attn_p, attn, c_in_p, c_in, wl, o_gate, wpa, wpb, wo, tm=_tile(mp, 256))
        xp_next, xs_next = _mlp(x1p, x1s, g_mlp, gf, w1, w2, tm=_tile(mp, 1024), tf=_tile(w1.shape[1], 512),
                                final=l == depth - 1)
        outs["ks"].append(k.reshape(db, tq, heads, hd))
        outs["vs"].append(v.reshape(db, tq, heads, hd))
        outs["kis"].append(sm[:, :idim].reshape(db, tq, idim))
        outs["ss"].append(u_all.reshape(db, tq, cw)[:, tq - 2:])
        xp, xs = xp_next, xs_next

    st = {k: jnp.stack(v) for k, v in outs.items()}
    return (xp.reshape(batch, seq, d), xs.reshape(db, tq, d), st["kp"], st["vp"], st["kip"], st["sp"],
            st["ks"], st["vs"], st["kis"], st["ss"])
```
